```python
import jax
import jax.numpy as jnp
from jax import lax
import numpy as np

D_MODEL = 2048
BATCH = 8
SEQ = 4096
DEPTH = 2

GRID_W = 64
CTX_LEN = 256
HEAD_DIM = 128
BLOCK = 128
NA_HEADS = 4
NA_WIN_R = 8
NA_WIN_C = 16
SWA_HEADS = 4
SWA_KV_HEADS = 2
SWA_WINDOW = 128
MLA_HEADS = 4
MLA_Q_LORA = 384
MLA_KV_LORA = 128
MLA_NOPE = 128
MLA_ROPE = 64
MLA_V = 128
GQA_HEADS = 4
GQA_KV_HEADS = 2
MIX_WIDTH = (NA_HEADS + SWA_HEADS + GQA_HEADS) * HEAD_DIM + MLA_HEADS * MLA_V
D_FF = 5632
CONV_W = 3
ROPE_THETA = 10000.0
EPS = 1e-6
NEG = -1e30
DEEPNORM_ALPHA = (2 * DEPTH) ** 0.25
DEEPNORM_BETA = (8 * DEPTH) ** -0.25
IN_SIZES = (
    NA_HEADS * HEAD_DIM, NA_HEADS * HEAD_DIM, NA_HEADS * HEAD_DIM,
    SWA_HEADS * HEAD_DIM, SWA_KV_HEADS * HEAD_DIM, SWA_KV_HEADS * HEAD_DIM,
    MLA_Q_LORA, MLA_KV_LORA, MLA_ROPE,
    GQA_HEADS * HEAD_DIM, GQA_KV_HEADS * HEAD_DIM, GQA_KV_HEADS * HEAD_DIM,
)
IN_COLS = sum(IN_SIZES)

kernel_name = 'hybrid_dit_parallel_heads_deepnorm'


def layer_norm(x, g=None, b=None):
    xf = x.astype(jnp.float32)
    mu = jnp.mean(xf, axis=-1, keepdims=True)
    var = jnp.mean(jnp.square(xf - mu), axis=-1, keepdims=True)
    y = (xf - mu) * lax.rsqrt(var + EPS)
    if g is not None:
        y = y * g + b
    return y.astype(x.dtype)


def rms_norm(x, g):
    xf = x.astype(jnp.float32)
    y = xf * lax.rsqrt(jnp.mean(xf * xf, axis=-1, keepdims=True) + EPS) * g
    return y.astype(x.dtype)


def modulate(x, shift, scale):
    return layer_norm(x) * (1 + scale) + shift


def rope_1d(x, pos):
    half = x.shape[-1] // 2
    inv_freq = ROPE_THETA ** (-jnp.arange(half, dtype=jnp.float32) / half)
    ang = pos.astype(jnp.float32)[:, None] * inv_freq[None, :]
    cos, sin = jnp.cos(ang), jnp.sin(ang)
    xf = x.astype(jnp.float32)
    x1, x2 = xf[..., :half], xf[..., half:]
    return jnp.concatenate([x1 * cos - x2 * sin, x2 * cos + x1 * sin], axis=-1).astype(x.dtype)


def rope_2d(x, row, col):
    h = x.shape[-1] // 2
    return jnp.concatenate([rope_1d(x[..., :h], row), rope_1d(x[..., h:], col)], axis=-1)


def split_cols(p):
    out, o = [], 0
    for n in IN_SIZES:
        out.append(p[..., o:o + n])
        o += n
    return out


def to_heads(t, n):
    B, S, _ = t.shape
    return t.reshape(B, S, n, -1).transpose(0, 2, 1, 3)


def from_heads(o):
    B, n, S, d = o.shape
    return o.transpose(0, 2, 1, 3).reshape(B, S, n * d)


def full_attention(q, k, v, scale, sink=None):
    s = jnp.einsum('bhgqd,bhkd->bhgqk', q, k, preferred_element_type=jnp.float32) * scale
    if sink is not None:
        s_sink = jnp.broadcast_to(sink.astype(jnp.float32)[None, :, :, None, None], s.shape[:-1] + (1,))
        s = jnp.concatenate([s, s_sink], axis=-1)
    p = jax.nn.softmax(s, axis=-1)[..., :k.shape[2]]
    return jnp.einsum('bhgqk,bhkd->bhgqd', p.astype(v.dtype), v)


def mixer_neighbourhood(px, pc, rpb, need_ctx):
    q, k, v = (to_heads(t, NA_HEADS) for t in px)
    kc, vc = to_heads(pc[1], NA_HEADS), to_heads(pc[2], NA_HEADS)
    scale = HEAD_DIM ** -0.5
    B, H, S, d = q.shape
    rows = S // GRID_W
    wr = min(NA_WIN_R, rows)
    qg = q.reshape(B, H, rows, GRID_W, d)
    kg = k.reshape(B, H, rows, GRID_W, d)
    vg = v.reshape(B, H, rows, GRID_W, d)
    r = jnp.arange(rows)
    r0 = jnp.clip(r - wr // 2, 0, rows - wr)
    krow = r0[:, None] + jnp.arange(wr)[None, :]
    k_band = kg[:, :, krow]
    v_band = vg[:, :, krow]
    cq = jnp.arange(GRID_W)
    c0 = jnp.clip(cq - NA_WIN_C // 2, 0, GRID_W - NA_WIN_C)
    col_in = (cq[None, :] >= c0[:, None]) & (cq[None, :] < c0[:, None] + NA_WIN_C)
    drow_idx = krow - r[:, None] + NA_WIN_R - 1
    dcol_idx = jnp.clip(cq[None, :] - cq[:, None] + NA_WIN_C - 1, 0, 2 * NA_WIN_C - 2)
    bias = rpb[:, drow_idx[:, None, :, None], dcol_idx[None, :, None, :]]
    s = jnp.einsum('bhrqd,bhrwkd->bhrqwk', qg, k_band, preferred_element_type=jnp.float32) * scale
    s = jnp.where(col_in[:, None, :], s + bias[None], NEG)
    nwin = wr * GRID_W
    s = s.reshape(B, H, rows, GRID_W, nwin)
    s_ctx = jnp.einsum('bhrqd,bhcd->bhrqc', qg, kc, preferred_element_type=jnp.float32) * scale
    p = jax.nn.softmax(jnp.concatenate([s, s_ctx], axis=-1), axis=-1)
    o = (jnp.einsum('bhrqn,bhrnd->bhrqd', p[..., :nwin].astype(v.dtype), v_band.reshape(B, H, rows, nwin, d))
         + jnp.einsum('bhrqc,bhcd->bhrqd', p[..., nwin:].astype(v.dtype), vc))
    out_x = from_heads(o.reshape(B, H, S, d))
    out_c = None
    if need_ctx:
        qc = to_heads(pc[0], NA_HEADS)
        out_c = from_heads(full_attention(qc[:, :, None], kc, vc, scale)[:, :, 0])
    return out_x, out_c


def mixer_sliding(px, pc, row, col, sink, need_ctx):
    G = SWA_HEADS // SWA_KV_HEADS
    q = rope_2d(to_heads(px[0], SWA_HEADS), row, col)
    k = rope_2d(to_heads(px[1], SWA_KV_HEADS), row, col)
    v = to_heads(px[2], SWA_KV_HEADS)
    kc, vc = to_heads(pc[1], SWA_KV_HEADS), to_heads(pc[2], SWA_KV_HEADS)
    sink = sink.reshape(SWA_KV_HEADS, G)
    scale = HEAD_DIM ** -0.5
    B, _, S, d = q.shape
    nb = S // BLOCK
    qb = q.reshape(B, SWA_KV_HEADS, G, nb, BLOCK, d)
    pad = ((0, 0), (0, 0), (BLOCK, BLOCK), (0, 0))
    idx = jnp.arange(nb)[:, None] * BLOCK + jnp.arange(3 * BLOCK)[None, :]
    kb = jnp.pad(k, pad)[:, :, idx]
    vb = jnp.pad(v, pad)[:, :, idx]
    kpos = idx - BLOCK
    qpos = jnp.arange(S).reshape(nb, BLOCK)
    valid = ((jnp.abs(kpos[:, None, :] - qpos[:, :, None]) <= SWA_WINDOW)
             & (kpos >= 0)[:, None, :] & (kpos < S)[:, None, :])
    s = jnp.einsum('bhgnqd,bhnkd->bhgnqk', qb, kb, preferred_element_type=jnp.float32) * scale
    s = jnp.where(valid, s, NEG)
    s_ctx = jnp.einsum('bhgnqd,bhcd->bhgnqc', qb, kc, preferred_element_type=jnp.float32) * scale
    s_sink = jnp.broadcast_to(sink.astype(jnp.float32)[None, :, :, None, None, None], s.shape[:-1] + (1,))
    p = jax.nn.softmax(jnp.concatenate([s, s_ctx, s_sink], axis=-1), axis=-1)
    nk, C = 3 * BLOCK, kc.shape[2]
    o = (jnp.einsum('bhgnqk,bhnkd->bhgnqd', p[..., :nk].astype(v.dtype), vb)
         + jnp.einsum('bhgnqc,bhcd->bhgnqd', p[..., nk:nk + C].astype(v.dtype), vc))
    out_x = from_heads(o.reshape(B, SWA_HEADS, S, d))
    out_c = None
    if need_ctx:
        qc = to_heads(pc[0], SWA_HEADS)
        qc = qc.reshape(B, SWA_KV_HEADS, G, qc.shape[2], d)
        oc = full_attention(qc, kc, vc, scale, sink)
        out_c = from_heads(oc.reshape(B, SWA_HEADS, qc.shape[3], d))
    return out_x, out_c


def mixer_mla(px, pc, row, col, q_norm, kv_norm, w_uq, w_ukv, need_ctx):
    def proj_q(cq):
        qh = to_heads(rms_norm(cq, q_norm) @ w_uq, MLA_HEADS)
        return qh[..., :MLA_NOPE], qh[..., MLA_NOPE:]

    def proj_kv(ckv):
        kvh = to_heads(rms_norm(ckv, kv_norm) @ w_ukv, MLA_HEADS)
        return kvh[..., :MLA_NOPE], kvh[..., MLA_NOPE:]

    scale = (MLA_NOPE + MLA_ROPE) ** -0.5

    def attend(qn, qpe, kn, kpe, v):
        s = (jnp.einsum('bhqd,bhkd->bhqk', qn, kn, preferred_element_type=jnp.float32)
             + jnp.einsum('bhqr,bkr->bhqk', qpe, kpe, preferred_element_type=jnp.float32)) * scale
        p = jax.nn.softmax(s, axis=-1)
        return jnp.einsum('bhqk,bhkd->bhqd', p.astype(v.dtype), v)

    qn, qpe = proj_q(px[0])
    qpe = rope_2d(qpe, row, col)
    kn, v = proj_kv(px[1])
    kpe = rope_2d(px[2], row, col)
    kn_c, v_c = proj_kv(pc[1])
    kpe_c = pc[2]
    kn_all = jnp.concatenate([kn, kn_c], axis=2)
    kpe_all = jnp.concatenate([kpe, kpe_c], axis=1)
    v_all = jnp.concatenate([v, v_c], axis=2)
    B, H, S, _ = qn.shape
    nb = S // BLOCK
    blocks = lambda t: jnp.moveaxis(t.reshape(B, H, nb, BLOCK, t.shape[-1]), 2, 0)
    o = lax.map(lambda qs: attend(qs[0], qs[1], kn_all, kpe_all, v_all), (blocks(qn), blocks(qpe)))
    out_x = from_heads(jnp.moveaxis(o, 0, 2).reshape(B, H, S, MLA_V))
    out_c = None
    if need_ctx:
        qn_c, qpe_c = proj_q(pc[0])
        out_c = from_heads(attend(qn_c, qpe_c, kn_c, kpe_c, v_c))
    return out_x, out_c


def mixer_gqa(px, pc, row, col, q_norm, k_norm, need_ctx):
    G = GQA_HEADS // GQA_KV_HEADS
    q = rope_2d(rms_norm(to_heads(px[0], GQA_HEADS), q_norm), row, col)
    k = rope_2d(rms_norm(to_heads(px[1], GQA_KV_HEADS), k_norm), row, col)
    v = to_heads(px[2], GQA_KV_HEADS)
    kc = rms_norm(to_heads(pc[1], GQA_KV_HEADS), k_norm)
    vc = to_heads(pc[2], GQA_KV_HEADS)
    k_all = jnp.concatenate([k, kc], axis=2)
    v_all = jnp.concatenate([v, vc], axis=2)
    scale = HEAD_DIM ** -0.5
    B, _, S, d = q.shape
    nb = S // BLOCK
    qb = jnp.moveaxis(q.reshape(B, GQA_KV_HEADS, G, nb, BLOCK, d), 3, 0)
    o = lax.map(lambda qi: full_attention(qi, k_all, v_all, scale), qb)
    out_x = from_heads(jnp.moveaxis(o, 0, 3).reshape(B, GQA_HEADS, S, d))
    out_c = None
    if need_ctx:
        qc = rms_norm(to_heads(pc[0], GQA_HEADS), q_norm)
        C = qc.shape[2]
        oc = full_attention(qc.reshape(B, GQA_KV_HEADS, G, C, d), kc, vc, scale)
        out_c = from_heads(oc.reshape(B, GQA_HEADS, C, d))
    return out_x, out_c


def depthwise_conv(h, w, b):
    S = h.shape[1]
    hp = jnp.pad(h, ((0, 0), (CONV_W // 2, CONV_W // 2), (0, 0)))
    return hp[:, 0:S] * w[0] + hp[:, 1:S + 1] * w[1] + hp[:, 2:S + 2] * w[2] + b


def conv_ffn(h, w_gate, w_up, conv_w, conv_b, w_down):
    a = depthwise_conv(h @ w_gate, conv_w, conv_b)
    return (jax.nn.silu(a) * (h @ w_up)) @ w_down


def _fwd_setup_inputs(seed: int = 0) -> dict:
    key = jax.random.key(seed)
    ks = jax.random.split(key, 25)
    L, D = DEPTH, D_MODEL

    def nrm(k, shape, s):
        return jax.random.normal(k, shape, jnp.float32) * s

    return {
        'x': nrm(ks[0], (BATCH, SEQ, D), 1.0),
        'c': nrm(ks[1], (BATCH, D), 1.0),
        'ctx': nrm(ks[2], (BATCH, CTX_LEN, D), 1.0),
        'c_ctx': nrm(ks[3], (D,), 1.0),
        'w_ada': nrm(ks[4], (L, D, 6 * D), D ** -0.5),
        'b_ada': nrm(ks[5], (L, 6 * D), 0.02),
        'w_in': nrm(ks[6], (L, D, IN_COLS), D ** -0.5),
        'na_rpb': nrm(ks[7], (L, NA_HEADS, 2 * NA_WIN_R - 1, 2 * NA_WIN_C - 1), 0.5),
        'swa_sink': nrm(ks[8], (L, SWA_HEADS), 0.5),
        'mla_q_norm': 1.0 + nrm(ks[9], (L, MLA_Q_LORA), 0.05),
        'mla_kv_norm': 1.0 + nrm(ks[10], (L, MLA_KV_LORA), 0.05),
        'mla_w_uq': nrm(ks[11], (L, MLA_Q_LORA, MLA_HEADS * (MLA_NOPE + MLA_ROPE)), MLA_Q_LORA ** -0.5),
        'mla_w_ukv': nrm(ks[12], (L, MLA_KV_LORA, MLA_HEADS * (MLA_NOPE + MLA_V)), MLA_KV_LORA ** -0.5),
        'gqa_q_norm': 1.0 + nrm(ks[13], (L, HEAD_DIM), 0.05),
        'gqa_k_norm': 1.0 + nrm(ks[14], (L, HEAD_DIM), 0.05),
        'w_out': nrm(ks[15], (L, MIX_WIDTH, D), DEEPNORM_BETA * MIX_WIDTH ** -0.5),
        'ln1_g': 1.0 + nrm(ks[16], (L, D), 0.05),
        'ln1_b': nrm(ks[17], (L, D), 0.02),
        'ffn_w_gate': nrm(ks[18], (L, D, D_FF), D ** -0.5),
        'ffn_w_up': nrm(ks[19], (L, D, D_FF), D ** -0.5),
        'ffn_conv_w': nrm(ks[20], (L, CONV_W, D_FF), CONV_W ** -0.5),
        'ffn_conv_b': nrm(ks[21], (L, D_FF), 0.02),
        'ffn_w_down': nrm(ks[22], (L, D_FF, D), DEEPNORM_BETA * D_FF ** -0.5),
        'ln2_g': 1.0 + nrm(ks[23], (L, D), 0.05),
        'ln2_b': nrm(ks[24], (L, D), 0.02),
    }


def _fwd_reference(x, c, ctx, c_ctx, w_ada, b_ada, w_in, na_rpb, swa_sink, mla_q_norm, mla_kv_norm,
              mla_w_uq, mla_w_ukv, gqa_q_norm, gqa_k_norm, w_out, ln1_g, ln1_b,
              ffn_w_gate, ffn_w_up, ffn_conv_w, ffn_conv_b, ffn_w_down, ln2_g, ln2_b):
    t = jnp.arange(x.shape[1])
    row, col = t // GRID_W, t % GRID_W
    for i in range(DEPTH):
        need_ctx = i < DEPTH - 1
        mod_x = jnp.split((jax.nn.silu(c) @ w_ada[i] + b_ada[i])[:, None, :], 6, axis=-1)
        mod_c = jnp.split(jax.nn.silu(c_ctx) @ w_ada[i] + b_ada[i], 6, axis=-1)

        px = split_cols(modulate(x, mod_x[0], mod_x[1]) @ w_in[i])
        pc = split_cols(modulate(ctx, mod_c[0], mod_c[1]) @ w_in[i])
        oa_x, oa_c = mixer_neighbourhood(px[0:3], pc[0:3], na_rpb[i], need_ctx)
        ob_x, ob_c = mixer_sliding(px[3:6], pc[3:6], row, col, swa_sink[i], need_ctx)
        oc_x, oc_c = mixer_mla(px[6:9], pc[6:9], row, col, mla_q_norm[i], mla_kv_norm[i],
                               mla_w_uq[i], mla_w_ukv[i], need_ctx)
        od_x, od_c = mixer_gqa(px[9:12], pc[9:12], row, col, gqa_q_norm[i], gqa_k_norm[i], need_ctx)
        mix_x = jnp.concatenate([oa_x, ob_x, oc_x, od_x], axis=-1)
        x = layer_norm(DEEPNORM_ALPHA * x + mod_x[2] * (mix_x @ w_out[i]), ln1_g[i], ln1_b[i])
        if need_ctx:
            mix_c = jnp.concatenate([oa_c, ob_c, oc_c, od_c], axis=-1)
            ctx = layer_norm(DEEPNORM_ALPHA * ctx + mod_c[2] * (mix_c @ w_out[i]), ln1_g[i], ln1_b[i])

        ffn_args = (ffn_w_gate[i], ffn_w_up[i], ffn_conv_w[i], ffn_conv_b[i], ffn_w_down[i])
        x = layer_norm(DEEPNORM_ALPHA * x + mod_x[5] * conv_ffn(modulate(x, mod_x[3], mod_x[4]), *ffn_args),
                       ln2_g[i], ln2_b[i])
        if need_ctx:
            ctx = layer_norm(DEEPNORM_ALPHA * ctx + mod_c[5] * conv_ffn(modulate(ctx, mod_c[3], mod_c[4]), *ffn_args),
                             ln2_g[i], ln2_b[i])
    return x


import jax as _jax
import jax.numpy as _jnp

TWIN_FORMAT = 'train_step'
FWD_PARAMS = ['x', 'c', 'ctx', 'c_ctx', 'w_ada', 'b_ada', 'w_in', 'na_rpb', 'swa_sink', 'mla_q_norm', 'mla_kv_norm', 'mla_w_uq', 'mla_w_ukv', 'gqa_q_norm', 'gqa_k_norm', 'w_out', 'ln1_g', 'ln1_b', 'ffn_w_gate', 'ffn_w_up', 'ffn_conv_w', 'ffn_conv_b', 'ffn_w_down', 'ln2_g', 'ln2_b']
TWIN_WEIGHTS = ['c_ctx', 'w_ada', 'b_ada', 'w_in', 'na_rpb', 'swa_sink', 'mla_q_norm', 'mla_kv_norm', 'mla_w_uq', 'mla_w_ukv', 'gqa_q_norm', 'gqa_k_norm', 'w_out', 'ln1_g', 'ln1_b', 'ffn_w_gate', 'ffn_w_up', 'ffn_conv_w', 'ffn_conv_b', 'ffn_w_down', 'ln2_g', 'ln2_b']
TWIN_DIFF_INPUT = 'x'
TWIN_INPUTS = ['x', 'c', 'ctx', 'c_ctx', 'w_ada', 'b_ada', 'w_in', 'na_rpb', 'swa_sink', 'mla_q_norm', 'mla_kv_norm', 'mla_w_uq', 'mla_w_ukv', 'gqa_q_norm', 'gqa_k_norm', 'w_out', 'ln1_g', 'ln1_b', 'ffn_w_gate', 'ffn_w_up', 'ffn_conv_w', 'ffn_conv_b', 'ffn_w_down', 'ln2_g', 'ln2_b', 'loss_target', 'm_c_ctx', 'm_w_ada', 'm_b_ada', 'm_w_in', 'm_na_rpb', 'm_swa_sink', 'm_mla_q_norm', 'm_mla_kv_norm', 'm_mla_w_uq', 'm_mla_w_ukv', 'm_gqa_q_norm', 'm_gqa_k_norm', 'm_w_out', 'm_ln1_g', 'm_ln1_b', 'm_ffn_w_gate', 'm_ffn_w_up', 'm_ffn_conv_w', 'm_ffn_conv_b', 'm_ffn_w_down', 'm_ln2_g', 'm_ln2_b', 'v_c_ctx', 'v_w_ada', 'v_b_ada', 'v_w_in', 'v_na_rpb', 'v_swa_sink', 'v_mla_q_norm', 'v_mla_kv_norm', 'v_mla_w_uq', 'v_mla_w_ukv', 'v_gqa_q_norm', 'v_gqa_k_norm', 'v_w_out', 'v_ln1_g', 'v_ln1_b', 'v_ffn_w_gate', 'v_ffn_w_up', 'v_ffn_conv_w', 'v_ffn_conv_b', 'v_ffn_w_down', 'v_ln2_g', 'v_ln2_b']
TWIN_OUTPUTS = ['loss', 'grad_x', 'grad_c_ctx', 'grad_w_ada', 'grad_b_ada', 'grad_w_in', 'grad_na_rpb', 'grad_swa_sink', 'grad_mla_q_norm', 'grad_mla_kv_norm', 'grad_mla_w_uq', 'grad_mla_w_ukv', 'grad_gqa_q_norm', 'grad_gqa_k_norm', 'grad_w_out', 'grad_ln1_g', 'grad_ln1_b', 'grad_ffn_w_gate', 'grad_ffn_w_up', 'grad_ffn_conv_w', 'grad_ffn_conv_b', 'grad_ffn_w_down', 'grad_ln2_g', 'grad_ln2_b', 'delta_c_ctx', 'delta_w_ada', 'delta_b_ada', 'delta_w_in', 'delta_na_rpb', 'delta_swa_sink', 'delta_mla_q_norm', 'delta_mla_kv_norm', 'delta_mla_w_uq', 'delta_mla_w_ukv', 'delta_gqa_q_norm', 'delta_gqa_k_norm', 'delta_w_out', 'delta_ln1_g', 'delta_ln1_b', 'delta_ffn_w_gate', 'delta_ffn_w_up', 'delta_ffn_conv_w', 'delta_ffn_conv_b', 'delta_ffn_w_down', 'delta_ln2_g', 'delta_ln2_b', 'new_m_c_ctx', 'new_m_w_ada', 'new_m_b_ada', 'new_m_w_in', 'new_m_na_rpb', 'new_m_swa_sink', 'new_m_mla_q_norm', 'new_m_mla_kv_norm', 'new_m_mla_w_uq', 'new_m_mla_w_ukv', 'new_m_gqa_q_norm', 'new_m_gqa_k_norm', 'new_m_w_out', 'new_m_ln1_g', 'new_m_ln1_b', 'new_m_ffn_w_gate', 'new_m_ffn_w_up', 'new_m_ffn_conv_w', 'new_m_ffn_conv_b', 'new_m_ffn_w_down', 'new_m_ln2_g', 'new_m_ln2_b', 'new_v_c_ctx', 'new_v_w_ada', 'new_v_b_ada', 'new_v_w_in', 'new_v_na_rpb', 'new_v_swa_sink', 'new_v_mla_q_norm', 'new_v_mla_kv_norm', 'new_v_mla_w_uq', 'new_v_mla_w_ukv', 'new_v_gqa_q_norm', 'new_v_gqa_k_norm', 'new_v_w_out', 'new_v_ln1_g', 'new_v_ln1_b', 'new_v_ffn_w_gate', 'new_v_ffn_w_up', 'new_v_ffn_conv_w', 'new_v_ffn_conv_b', 'new_v_ffn_w_down', 'new_v_ln2_g', 'new_v_ln2_b']
TWIN_LEAF_KINDS = {'loss': 'loss', 'grad_x': 'grad_x', 'grad_c_ctx': 'grad_w', 'grad_w_ada': 'grad_w', 'grad_b_ada': 'grad_w', 'grad_w_in': 'grad_w', 'grad_na_rpb': 'grad_w', 'grad_swa_sink': 'grad_w', 'grad_mla_q_norm': 'grad_w', 'grad_mla_kv_norm': 'grad_w', 'grad_mla_w_uq': 'grad_w', 'grad_mla_w_ukv': 'grad_w', 'grad_gqa_q_norm': 'grad_w', 'grad_gqa_k_norm': 'grad_w', 'grad_w_out': 'grad_w', 'grad_ln1_g': 'grad_w', 'grad_ln1_b': 'grad_w', 'grad_ffn_w_gate': 'grad_w', 'grad_ffn_w_up': 'grad_w', 'grad_ffn_conv_w': 'grad_w', 'grad_ffn_conv_b': 'grad_w', 'grad_ffn_w_down': 'grad_w', 'grad_ln2_g': 'grad_w', 'grad_ln2_b': 'grad_w', 'delta_c_ctx': 'delta_w', 'delta_w_ada': 'delta_w', 'delta_b_ada': 'delta_w', 'delta_w_in': 'delta_w', 'delta_na_rpb': 'delta_w', 'delta_swa_sink': 'delta_w', 'delta_mla_q_norm': 'delta_w', 'delta_mla_kv_norm': 'delta_w', 'delta_mla_w_uq': 'delta_w', 'delta_mla_w_ukv': 'delta_w', 'delta_gqa_q_norm': 'delta_w', 'delta_gqa_k_norm': 'delta_w', 'delta_w_out': 'delta_w', 'delta_ln1_g': 'delta_w', 'delta_ln1_b': 'delta_w', 'delta_ffn_w_gate': 'delta_w', 'delta_ffn_w_up': 'delta_w', 'delta_ffn_conv_w': 'delta_w', 'delta_ffn_conv_b': 'delta_w', 'delta_ffn_w_down': 'delta_w', 'delta_ln2_g': 'delta_w', 'delta_ln2_b': 'delta_w', 'new_m_c_ctx': 'new_m', 'new_m_w_ada': 'new_m', 'new_m_b_ada': 'new_m', 'new_m_w_in': 'new_m', 'new_m_na_rpb': 'new_m', 'new_m_swa_sink': 'new_m', 'new_m_mla_q_norm': 'new_m', 'new_m_mla_kv_norm': 'new_m', 'new_m_mla_w_uq': 'new_m', 'new_m_mla_w_ukv': 'new_m', 'new_m_gqa_q_norm': 'new_m', 'new_m_gqa_k_norm': 'new_m', 'new_m_w_out': 'new_m', 'new_m_ln1_g': 'new_m', 'new_m_ln1_b': 'new_m', 'new_m_ffn_w_gate': 'new_m', 'new_m_ffn_w_up': 'new_m', 'new_m_ffn_conv_w': 'new_m', 'new_m_ffn_conv_b': 'new_m', 'new_m_ffn_w_down': 'new_m', 'new_m_ln2_g': 'new_m', 'new_m_ln2_b': 'new_m', 'new_v_c_ctx': 'new_v', 'new_v_w_ada': 'new_v', 'new_v_b_ada': 'new_v', 'new_v_w_in': 'new_v', 'new_v_na_rpb': 'new_v', 'new_v_swa_sink': 'new_v', 'new_v_mla_q_norm': 'new_v', 'new_v_mla_kv_norm': 'new_v', 'new_v_mla_w_uq': 'new_v', 'new_v_mla_w_ukv': 'new_v', 'new_v_gqa_q_norm': 'new_v', 'new_v_gqa_k_norm': 'new_v', 'new_v_w_out': 'new_v', 'new_v_ln1_g': 'new_v', 'new_v_ln1_b': 'new_v', 'new_v_ffn_w_gate': 'new_v', 'new_v_ffn_w_up': 'new_v', 'new_v_ffn_conv_w': 'new_v', 'new_v_ffn_conv_b': 'new_v', 'new_v_ffn_w_down': 'new_v', 'new_v_ln2_g': 'new_v', 'new_v_ln2_b': 'new_v'}


def _forward(args):
    return _fwd_reference(*[args[k] for k in FWD_PARAMS])


def _output_shape():
    def fwd():
        inp = _fwd_setup_inputs(0)
        return _fwd_reference(*[inp[k] for k in FWD_PARAMS])
    out = _jax.eval_shape(fwd)
    return out.shape, out.dtype

N_MICROBATCH = 1
ADAM_LR = 0.001
ADAM_B1 = 0.9
ADAM_B2 = 0.999
ADAM_EPS = 1e-08
ADAM_WD = 0.01
ADAM_STEP = 10
PER_EXAMPLE_BATCH_AXIS = {'x': 0, 'c': 0, 'ctx': 0, 'loss_target': 0}
SHARED_INPUTS = []
_WEIGHT_DTYPES = {'c_ctx': _jnp.float32, 'w_ada': _jnp.float32, 'b_ada': _jnp.float32, 'w_in': _jnp.float32, 'na_rpb': _jnp.float32, 'swa_sink': _jnp.float32, 'mla_q_norm': _jnp.float32, 'mla_kv_norm': _jnp.float32, 'mla_w_uq': _jnp.float32, 'mla_w_ukv': _jnp.float32, 'gqa_q_norm': _jnp.float32, 'gqa_k_norm': _jnp.float32, 'w_out': _jnp.float32, 'ln1_g': _jnp.float32, 'ln1_b': _jnp.float32, 'ffn_w_gate': _jnp.float32, 'ffn_w_up': _jnp.float32, 'ffn_conv_w': _jnp.float32, 'ffn_conv_b': _jnp.float32, 'ffn_w_down': _jnp.float32, 'ln2_g': _jnp.float32, 'ln2_b': _jnp.float32}
MOMENT_SCALE = {'c_ctx': 1.239730e-02, 'w_ada': 1.718046e-02, 'b_ada': 3.340454e-02, 'w_in': 1.144900e-02, 'na_rpb': 1.589175e-03, 'swa_sink': 2.108414e-04, 'mla_q_norm': 2.899719e-03, 'mla_kv_norm': 2.394304e-02, 'mla_w_uq': 2.088544e-03, 'mla_w_ukv': 9.298962e-03, 'gqa_q_norm': 4.644424e-03, 'gqa_k_norm': 4.791755e-03, 'w_out': 3.109973e-02, 'ln1_g': 1.506237e+00, 'ln1_b': 2.821974e-01, 'ffn_w_gate': 1.278197e-02, 'ffn_w_up': 1.273809e-02, 'ffn_conv_w': 1.296154e-02, 'ffn_conv_b': 9.851307e-03, 'ffn_w_down': 4.260105e-02, 'ln2_g': 1.150288e+01, 'ln2_b': 7.991459e-01}


def _to_microbatches(a, axis):
    t = _jnp.moveaxis(a, axis, 0)
    t = t.reshape((N_MICROBATCH, t.shape[0] // N_MICROBATCH) + t.shape[1:])
    return _jnp.moveaxis(t, 1, axis + 1)


def setup_inputs(seed: int = 0) -> dict:
    inp = _fwd_setup_inputs(seed)
    key = _jax.random.fold_in(_jax.random.key(seed), 7919)
    shape, _ = _output_shape()
    out = dict(inp)
    out["loss_target"] = _jax.random.normal(_jax.random.fold_in(key, 0), shape, _jnp.float32)
    for i, name in enumerate(TWIN_WEIGHTS):
        w = inp[name].astype(_jnp.float32)
        if MOMENT_SCALE is None:
            s = _jnp.sqrt(_jnp.mean(_jnp.square(w)) + 1e-30)
        else:
            s = MOMENT_SCALE[name]
        km, kv = _jax.random.split(_jax.random.fold_in(key, i + 1))
        out[name] = w
        out["m_" + name] = s * _jax.random.normal(km, w.shape, _jnp.float32)
        out["v_" + name] = (s * s) * _jax.random.uniform(kv, w.shape, _jnp.float32, 0.5, 1.5)
    if N_MICROBATCH > 1:
        for name, axis in PER_EXAMPLE_BATCH_AXIS.items():
            out[name] = _to_microbatches(out[name], axis)
    return {'x': out['x'], 'c': out['c'], 'ctx': out['ctx'], 'c_ctx': out['c_ctx'], 'w_ada': out['w_ada'], 'b_ada': out['b_ada'], 'w_in': out['w_in'], 'na_rpb': out['na_rpb'], 'swa_sink': out['swa_sink'], 'mla_q_norm': out['mla_q_norm'], 'mla_kv_norm': out['mla_kv_norm'], 'mla_w_uq': out['mla_w_uq'], 'mla_w_ukv': out['mla_w_ukv'], 'gqa_q_norm': out['gqa_q_norm'], 'gqa_k_norm': out['gqa_k_norm'], 'w_out': out['w_out'], 'ln1_g': out['ln1_g'], 'ln1_b': out['ln1_b'], 'ffn_w_gate': out['ffn_w_gate'], 'ffn_w_up': out['ffn_w_up'], 'ffn_conv_w': out['ffn_conv_w'], 'ffn_conv_b': out['ffn_conv_b'], 'ffn_w_down': out['ffn_w_down'], 'ln2_g': out['ln2_g'], 'ln2_b': out['ln2_b'], 'loss_target': out['loss_target'], 'm_c_ctx': out['m_c_ctx'], 'm_w_ada': out['m_w_ada'], 'm_b_ada': out['m_b_ada'], 'm_w_in': out['m_w_in'], 'm_na_rpb': out['m_na_rpb'], 'm_swa_sink': out['m_swa_sink'], 'm_mla_q_norm': out['m_mla_q_norm'], 'm_mla_kv_norm': out['m_mla_kv_norm'], 'm_mla_w_uq': out['m_mla_w_uq'], 'm_mla_w_ukv': out['m_mla_w_ukv'], 'm_gqa_q_norm': out['m_gqa_q_norm'], 'm_gqa_k_norm': out['m_gqa_k_norm'], 'm_w_out': out['m_w_out'], 'm_ln1_g': out['m_ln1_g'], 'm_ln1_b': out['m_ln1_b'], 'm_ffn_w_gate': out['m_ffn_w_gate'], 'm_ffn_w_up': out['m_ffn_w_up'], 'm_ffn_conv_w': out['m_ffn_conv_w'], 'm_ffn_conv_b': out['m_ffn_conv_b'], 'm_ffn_w_down': out['m_ffn_w_down'], 'm_ln2_g': out['m_ln2_g'], 'm_ln2_b': out['m_ln2_b'], 'v_c_ctx': out['v_c_ctx'], 'v_w_ada': out['v_w_ada'], 'v_b_ada': out['v_b_ada'], 'v_w_in': out['v_w_in'], 'v_na_rpb': out['v_na_rpb'], 'v_swa_sink': out['v_swa_sink'], 'v_mla_q_norm': out['v_mla_q_norm'], 'v_mla_kv_norm': out['v_mla_kv_norm'], 'v_mla_w_uq': out['v_mla_w_uq'], 'v_mla_w_ukv': out['v_mla_w_ukv'], 'v_gqa_q_norm': out['v_gqa_q_norm'], 'v_gqa_k_norm': out['v_gqa_k_norm'], 'v_w_out': out['v_w_out'], 'v_ln1_g': out['v_ln1_g'], 'v_ln1_b': out['v_ln1_b'], 'v_ffn_w_gate': out['v_ffn_w_gate'], 'v_ffn_w_up': out['v_ffn_w_up'], 'v_ffn_conv_w': out['v_ffn_conv_w'], 'v_ffn_conv_b': out['v_ffn_conv_b'], 'v_ffn_w_down': out['v_ffn_w_down'], 'v_ln2_g': out['v_ln2_g'], 'v_ln2_b': out['v_ln2_b']}


def _loss(weights, diff, rest, loss_target):
    with _jax.named_scope("forward"):
        args = {**rest, TWIN_DIFF_INPUT: diff, **{k: w.astype(_WEIGHT_DTYPES[k]) for k, w in weights.items()}}
        y = _forward(args)
    with _jax.named_scope("loss_head"):
        err = _jnp.square(y.astype(_jnp.float32) - loss_target)
        return 0.5 * _jnp.sum(_jnp.mean(err, axis=-1)) if err.ndim else 0.5 * err


def _adamw(w, g, m, v):
    m = ADAM_B1 * m + (1.0 - ADAM_B1) * g
    v = ADAM_B2 * v + (1.0 - ADAM_B2) * _jnp.square(g)
    m_hat = m / (1.0 - ADAM_B1 ** ADAM_STEP)
    v_hat = v / (1.0 - ADAM_B2 ** ADAM_STEP)
    delta = -ADAM_LR * (m_hat / (_jnp.sqrt(v_hat) + ADAM_EPS) + ADAM_WD * w)
    return delta, m, v


def reference(x, c, ctx, c_ctx, w_ada, b_ada, w_in, na_rpb, swa_sink, mla_q_norm, mla_kv_norm, mla_w_uq, mla_w_ukv, gqa_q_norm, gqa_k_norm, w_out, ln1_g, ln1_b, ffn_w_gate, ffn_w_up, ffn_conv_w, ffn_conv_b, ffn_w_down, ln2_g, ln2_b, loss_target, m_c_ctx, m_w_ada, m_b_ada, m_w_in, m_na_rpb, m_swa_sink, m_mla_q_norm, m_mla_kv_norm, m_mla_w_uq, m_mla_w_ukv, m_gqa_q_norm, m_gqa_k_norm, m_w_out, m_ln1_g, m_ln1_b, m_ffn_w_gate, m_ffn_w_up, m_ffn_conv_w, m_ffn_conv_b, m_ffn_w_down, m_ln2_g, m_ln2_b, v_c_ctx, v_w_ada, v_b_ada, v_w_in, v_na_rpb, v_swa_sink, v_mla_q_norm, v_mla_kv_norm, v_mla_w_uq, v_mla_w_ukv, v_gqa_q_norm, v_gqa_k_norm, v_w_out, v_ln1_g, v_ln1_b, v_ffn_w_gate, v_ffn_w_up, v_ffn_conv_w, v_ffn_conv_b, v_ffn_w_down, v_ln2_g, v_ln2_b):
    given = dict(x=x, c=c, ctx=ctx, c_ctx=c_ctx, w_ada=w_ada, b_ada=b_ada, w_in=w_in, na_rpb=na_rpb, swa_sink=swa_sink, mla_q_norm=mla_q_norm, mla_kv_norm=mla_kv_norm, mla_w_uq=mla_w_uq, mla_w_ukv=mla_w_ukv, gqa_q_norm=gqa_q_norm, gqa_k_norm=gqa_k_norm, w_out=w_out, ln1_g=ln1_g, ln1_b=ln1_b, ffn_w_gate=ffn_w_gate, ffn_w_up=ffn_w_up, ffn_conv_w=ffn_conv_w, ffn_conv_b=ffn_conv_b, ffn_w_down=ffn_w_down, ln2_g=ln2_g, ln2_b=ln2_b, loss_target=loss_target, m_c_ctx=m_c_ctx, m_w_ada=m_w_ada, m_b_ada=m_b_ada, m_w_in=m_w_in, m_na_rpb=m_na_rpb, m_swa_sink=m_swa_sink, m_mla_q_norm=m_mla_q_norm, m_mla_kv_norm=m_mla_kv_norm, m_mla_w_uq=m_mla_w_uq, m_mla_w_ukv=m_mla_w_ukv, m_gqa_q_norm=m_gqa_q_norm, m_gqa_k_norm=m_gqa_k_norm, m_w_out=m_w_out, m_ln1_g=m_ln1_g, m_ln1_b=m_ln1_b, m_ffn_w_gate=m_ffn_w_gate, m_ffn_w_up=m_ffn_w_up, m_ffn_conv_w=m_ffn_conv_w, m_ffn_conv_b=m_ffn_conv_b, m_ffn_w_down=m_ffn_w_down, m_ln2_g=m_ln2_g, m_ln2_b=m_ln2_b, v_c_ctx=v_c_ctx, v_w_ada=v_w_ada, v_b_ada=v_b_ada, v_w_in=v_w_in, v_na_rpb=v_na_rpb, v_swa_sink=v_swa_sink, v_mla_q_norm=v_mla_q_norm, v_mla_kv_norm=v_mla_kv_norm, v_mla_w_uq=v_mla_w_uq, v_mla_w_ukv=v_mla_w_ukv, v_gqa_q_norm=v_gqa_q_norm, v_gqa_k_norm=v_gqa_k_norm, v_w_out=v_w_out, v_ln1_g=v_ln1_g, v_ln1_b=v_ln1_b, v_ffn_w_gate=v_ffn_w_gate, v_ffn_w_up=v_ffn_w_up, v_ffn_conv_w=v_ffn_conv_w, v_ffn_conv_b=v_ffn_conv_b, v_ffn_w_down=v_ffn_w_down, v_ln2_g=v_ln2_g, v_ln2_b=v_ln2_b)
    weights = {n: given[n] for n in TWIN_WEIGHTS}
    shared = {n: given[n] for n in SHARED_INPUTS}
    per_example = {n: given[n] for n in ['x', 'c', 'ctx']}
    grad_fn = _jax.value_and_grad(_loss, argnums=(0, 1))

    def one_microbatch(ex, loss_target):
        ex = dict(ex)
        diff = ex.pop(TWIN_DIFF_INPUT)
        return grad_fn(weights, diff, {**shared, **ex}, loss_target)

    if N_MICROBATCH == 1:
        loss, (grad_w, grad_x) = one_microbatch(per_example, given["loss_target"])
    else:
        def body(carry, xs):
            loss_sum, grad_sum = carry
            l_k, (gw_k, gx_k) = one_microbatch(xs[0], xs[1])
            with _jax.named_scope("update"):
                return (loss_sum + l_k, _jax.tree.map(_jnp.add, grad_sum, gw_k)), gx_k

        init = (_jnp.zeros((), _jnp.float32), _jax.tree.map(_jnp.zeros_like, weights))
        (loss, grad_w), grad_x = _jax.lax.scan(body, init, (per_example, given["loss_target"]))
    with _jax.named_scope("update"):
        delta_w, new_m, new_v = {}, {}, {}
        for n in TWIN_WEIGHTS:
            delta_w[n], new_m[n], new_v[n] = _adamw(weights[n], grad_w[n], given["m_" + n], given["v_" + n])
    return (loss, grad_x, *[grad_w[n] for n in TWIN_WEIGHTS], *[delta_w[n] for n in TWIN_WEIGHTS],
            *[new_m[n] for n in TWIN_WEIGHTS], *[new_v[n] for n in TWIN_WEIGHTS])
```

```python
import functools
import math

import numpy as np
import jax
import jax.numpy as jnp
from jax import lax
from jax.experimental import pallas as pl
from jax.experimental.pallas import tpu as pltpu

F32 = jnp.float32
BF16 = jnp.bfloat16
_MXU = jnp.bfloat16

N_DEV = 8
GRID_W = 64
HEAD_DIM = 128
NA_HEADS, NA_WIN_R, NA_WIN_C = 4, 8, 16
SWA_HEADS, SWA_KV_HEADS, SWA_WINDOW = 4, 2, 128
MLA_HEADS, MLA_Q_LORA, MLA_KV_LORA, MLA_NOPE, MLA_ROPE, MLA_V = 4, 384, 128, 128, 64, 128
GQA_HEADS, GQA_KV_HEADS = 4, 2
ROPE_THETA = 10000.0
EPS = 1e-6
NEG = -1e30
IN_SIZES = (512, 512, 512, 512, 256, 256, MLA_Q_LORA, MLA_KV_LORA, MLA_ROPE, 512, 256, 256)
IN_COLS = sum(IN_SIZES)
KPE_END = sum(IN_SIZES[:9])
IN_COLS_PAD = IN_COLS + 64
ADAM_LR, ADAM_B1, ADAM_B2, ADAM_EPS, ADAM_WD, ADAM_STEP = 0.001, 0.9, 0.999, 1e-08, 0.01, 10

LANE = 128
ROW_TILE = 256
ATT_BLOCK = 128
VMEM_LIMIT = 56 * 1024 * 1024
MM_BUDGET = 36 * 1024 * 1024
HBM_BPS = 3.0e12
STEP_S = 0.4e-6


def _cparams(n_axes):
    return pltpu.CompilerParams(dimension_semantics=("arbitrary",) * n_axes, vmem_limit_bytes=VMEM_LIMIT)


def _divisors(n, step, cap):
    out = [d for d in range(step, min(n, cap) + 1, step) if n % d == 0]
    if n <= cap and n not in out:
        out.append(n)
    return out


def _mm_tiles(M, N, K, sa, sb, m_step, k_step):
    best, best_cost = None, None
    for tm in _divisors(M, m_step, 2304):
        for tn in _divisors(N, LANE, 2304):
            for tk in _divisors(K, k_step, 2304):
                vm = 2 * (tm * tk * sa + tk * tn * sb) + 3 * tm * tn * 4
                if vm > MM_BUDGET:
                    continue
                traffic = M * K * sa * (N // tn) + K * N * sb * (M // tm) + M * N * 4
                steps = (M // tm) * (N // tn) * (K // tk)
                cost = traffic / HBM_BPS + steps * STEP_S
                if tm % 256 or tn % 256:
                    cost *= 1.05
                if best_cost is None or cost < best_cost:
                    best, best_cost = (tm, tn, tk), cost
    assert best is not None, (M, N, K)
    return best


def _matmul(a, b, mode, name):
    if mode == "nn":
        (M, K), (K2, N) = a.shape, b.shape
    elif mode == "nt":
        (M, K), (N, K2) = a.shape, b.shape
    else:
        (K, M), (K2, N) = a.shape, b.shape
    assert K == K2, (a.shape, b.shape, mode)
    tm, tn, tk = _mm_tiles(M, N, K, a.dtype.itemsize, b.dtype.itemsize,
                           LANE if mode == "tn" else 8, 8 if mode == "tn" else LANE)
    nk = K // tk
    dims = {"nn": (((1,), (0,)), ((), ())), "nt": (((1,), (1,)), ((), ())), "tn": (((0,), (0,)), ((), ()))}[mode]

    def body(a_ref, b_ref, o_ref, acc_ref):
        k = pl.program_id(2)

        @pl.when(k == 0)
        def _():
            acc_ref[...] = jnp.zeros_like(acc_ref)

        acc_ref[...] += lax.dot_general(a_ref[...].astype(_MXU), b_ref[...].astype(_MXU), dims,
                                        preferred_element_type=F32)

        @pl.when(k == nk - 1)
        def _():
            o_ref[...] = acc_ref[...]

    if mode == "nn":
        a_spec = pl.BlockSpec((tm, tk), lambda i, j, k: (i, k))
        b_spec = pl.BlockSpec((tk, tn), lambda i, j, k: (k, j))
    elif mode == "nt":
        a_spec = pl.BlockSpec((tm, tk), lambda i, j, k: (i, k))
        b_spec = pl.BlockSpec((tn, tk), lambda i, j, k: (j, k))
    else:
        a_spec = pl.BlockSpec((tk, tm), lambda i, j, k: (k, i))
        b_spec = pl.BlockSpec((tk, tn), lambda i, j, k: (k, j))
    return pl.pallas_call(
        body, name=name, grid=(M // tm, N // tn, nk),
        in_specs=[a_spec, b_spec], out_specs=pl.BlockSpec((tm, tn), lambda i, j, k: (i, j)),
        out_shape=jax.ShapeDtypeStruct((M, N), F32),
        scratch_shapes=[pltpu.VMEM((tm, tn), F32)], compiler_params=_cparams(3),
    )(a, b)


def _make_linear(tag):
    @jax.custom_vjp
    def linear(a, w, sink):
        return _matmul(a, w, "nn", tag + "_fwd")

    def fwd(a, w, sink):
        return _matmul(a, w, "nn", tag + "_fwd"), (a, w)

    def bwd(res, dy):
        a, w = res
        da = _matmul(dy, w, "nt", tag + "_bwd_da")
        dw = _matmul(a, dy, "tn", tag + "_bwd_dw")
        return da, jnp.zeros_like(w), dw

    linear.defvjp(fwd, bwd)
    return linear


def _seg_map(n_x_tiles):
    return lambda i: (jnp.where(i >= n_x_tiles, 1, 0), 0, 0)


def _ln_stats(x):
    mu = jnp.mean(x, axis=-1, keepdims=True)
    xc = x - mu
    var = jnp.mean(xc * xc, axis=-1, keepdims=True)
    rstd = lax.rsqrt(var + EPS)
    return xc * rstd, rstd


def _make_modulate(tag, n_x):
    def call_fwd(x, shift, scale):
        T, D = x.shape
        nxt = n_x // ROW_TILE

        def body(x_ref, sh_ref, sc_ref, o_ref):
            xhat, _ = _ln_stats(x_ref[...])
            o_ref[...] = xhat * (1.0 + sc_ref[0]) + sh_ref[0]

        row = pl.BlockSpec((ROW_TILE, D), lambda i: (i, 0))
        seg = pl.BlockSpec((1, 1, D), _seg_map(nxt))
        return pl.pallas_call(body, name=tag + "_fwd", grid=(T // ROW_TILE,), in_specs=[row, seg, seg],
                              out_specs=row, out_shape=jax.ShapeDtypeStruct((T, D), F32),
                              compiler_params=_cparams(1))(x, shift, scale)

    def call_bwd(x, scale, dh):
        T, D = x.shape
        nxt = n_x // ROW_TILE
        nseg = scale.shape[0]

        def body(x_ref, sc_ref, dh_ref, dx_ref, dsh_ref, dsc_ref):
            i = pl.program_id(0)

            @pl.when((i == 0) | (i == nxt))
            def _():
                dsh_ref[...] = jnp.zeros_like(dsh_ref)
                dsc_ref[...] = jnp.zeros_like(dsc_ref)

            xhat, rstd = _ln_stats(x_ref[...])
            dh = dh_ref[...]
            dsh_ref[0] += jnp.sum(dh, axis=0, keepdims=True)
            dsc_ref[0] += jnp.sum(dh * xhat, axis=0, keepdims=True)
            dxh = dh * (1.0 + sc_ref[0])
            m1 = jnp.mean(dxh, axis=-1, keepdims=True)
            m2 = jnp.mean(dxh * xhat, axis=-1, keepdims=True)
            dx_ref[...] = rstd * (dxh - m1 - xhat * m2)

        row = pl.BlockSpec((ROW_TILE, D), lambda i: (i, 0))
        seg = pl.BlockSpec((1, 1, D), _seg_map(nxt))
        segshape = jax.ShapeDtypeStruct((nseg, 1, D), F32)
        return pl.pallas_call(body, name=tag + "_bwd", grid=(T // ROW_TILE,), in_specs=[row, seg, row],
                              out_specs=[row, seg, seg],
                              out_shape=[jax.ShapeDtypeStruct((T, D), F32), segshape, segshape],
                              compiler_params=_cparams(1))(x, scale, dh)

    @jax.custom_vjp
    def modulate(x, shift, scale):
        return call_fwd(x, shift, scale)

    def fwd(x, shift, scale):
        return call_fwd(x, shift, scale), (x, scale)

    def bwd(res, dh):
        x, scale = res
        return tuple(call_bwd(x, scale, dh))

    modulate.defvjp(fwd, bwd)
    return modulate


def _make_resid_ln(tag, n_x, alpha):
    def call_fwd(x, y, gate, g, b):
        T, D = x.shape
        nxt = n_x // ROW_TILE

        def body(x_ref, y_ref, gt_ref, g_ref, b_ref, o_ref):
            u = alpha * x_ref[...] + gt_ref[0] * y_ref[...]
            uhat, _ = _ln_stats(u)
            o_ref[...] = uhat * g_ref[...] + b_ref[...]

        row = pl.BlockSpec((ROW_TILE, D), lambda i: (i, 0))
        seg = pl.BlockSpec((1, 1, D), _seg_map(nxt))
        vec = pl.BlockSpec((1, D), lambda i: (0, 0))
        return pl.pallas_call(body, name=tag + "_fwd", grid=(T // ROW_TILE,), in_specs=[row, row, seg, vec, vec],
                              out_specs=row, out_shape=jax.ShapeDtypeStruct((T, D), F32),
                              compiler_params=_cparams(1))(x, y, gate, g, b)

    def call_bwd(x, y, gate, g, do):
        T, D = x.shape
        nxt = n_x // ROW_TILE
        nseg = gate.shape[0]

        def body(x_ref, y_ref, gt_ref, g_ref, do_ref, dx_ref, dy_ref, dgt_ref, dg_ref, db_ref):
            i = pl.program_id(0)

            @pl.when(i == 0)
            def _():
                dg_ref[...] = jnp.zeros_like(dg_ref)
                db_ref[...] = jnp.zeros_like(db_ref)

            @pl.when((i == 0) | (i == nxt))
            def _():
                dgt_ref[...] = jnp.zeros_like(dgt_ref)

            y = y_ref[...]
            gate_v = gt_ref[0]
            uhat, rstd = _ln_stats(alpha * x_ref[...] + gate_v * y)
            do = do_ref[...]
            dg_ref[...] += jnp.sum(do * uhat, axis=0, keepdims=True)
            db_ref[...] += jnp.sum(do, axis=0, keepdims=True)
            duh = do * g_ref[...]
            m1 = jnp.mean(duh, axis=-1, keepdims=True)
            m2 = jnp.mean(duh * uhat, axis=-1, keepdims=True)
            du = rstd * (duh - m1 - uhat * m2)
            dx_ref[...] = alpha * du
            dy_ref[...] = gate_v * du
            dgt_ref[0] += jnp.sum(du * y, axis=0, keepdims=True)

        row = pl.BlockSpec((ROW_TILE, D), lambda i: (i, 0))
        seg = pl.BlockSpec((1, 1, D), _seg_map(nxt))
        vec = pl.BlockSpec((1, D), lambda i: (0, 0))
        rs = jax.ShapeDtypeStruct((T, D), F32)
        vs = jax.ShapeDtypeStruct((1, D), F32)
        return pl.pallas_call(body, name=tag + "_bwd", grid=(T // ROW_TILE,), in_specs=[row, row, seg, vec, row],
                              out_specs=[row, row, seg, vec, vec],
                              out_shape=[rs, rs, jax.ShapeDtypeStruct((nseg, 1, D), F32), vs, vs],
                              compiler_params=_cparams(1))(x, y, gate, g, do)

    @jax.custom_vjp
    def resid_ln(x, y, gate, g, b):
        return call_fwd(x, y, gate, g, b)

    def fwd(x, y, gate, g, b):
        return call_fwd(x, y, gate, g, b), (x, y, gate, g)

    def bwd(res, do):
        x, y, gate, g = res
        return tuple(call_bwd(x, y, gate, g, do))

    resid_ln.defvjp(fwd, bwd)
    return resid_ln


def _make_rmsnorm(tag):
    def call_fwd(x, g):
        T, W = x.shape
        gw = g.shape[1]

        def body(x_ref, g_ref, o_ref):
            x = x_ref[...]
            r = lax.rsqrt(jnp.mean(x * x, axis=-1, keepdims=True) + EPS)
            o_ref[...] = x * r * g_ref[...]

        blk = pl.BlockSpec((ROW_TILE, gw), lambda i, c: (i, c))
        vec = pl.BlockSpec((1, gw), lambda i, c: (0, 0))
        return pl.pallas_call(body, name=tag + "_fwd", grid=(T // ROW_TILE, W // gw), in_specs=[blk, vec],
                              out_specs=blk, out_shape=jax.ShapeDtypeStruct((T, W), F32),
                              compiler_params=_cparams(2))(x, g)

    def call_bwd(x, g, dy):
        T, W = x.shape
        gw = g.shape[1]

        def body(x_ref, g_ref, dy_ref, dx_ref, dg_ref):
            @pl.when((pl.program_id(0) == 0) & (pl.program_id(1) == 0))
            def _():
                dg_ref[...] = jnp.zeros_like(dg_ref)

            x = x_ref[...]
            r = lax.rsqrt(jnp.mean(x * x, axis=-1, keepdims=True) + EPS)
            xn = x * r
            dy = dy_ref[...]
            dg_ref[...] += jnp.sum(dy * xn, axis=0, keepdims=True)
            dxn = dy * g_ref[...]
            dx_ref[...] = r * (dxn - xn * jnp.mean(dxn * xn, axis=-1, keepdims=True))

        blk = pl.BlockSpec((ROW_TILE, gw), lambda i, c: (i, c))
        vec = pl.BlockSpec((1, gw), lambda i, c: (0, 0))
        return pl.pallas_call(body, name=tag + "_bwd", grid=(T // ROW_TILE, W // gw), in_specs=[blk, vec, blk],
                              out_specs=[blk, vec],
                              out_shape=[jax.ShapeDtypeStruct((T, W), F32), jax.ShapeDtypeStruct((1, gw), F32)],
                              compiler_params=_cparams(2))(x, g, dy)

    @jax.custom_vjp
    def rmsnorm(x, g):
        return call_fwd(x, g)

    def fwd(x, g):
        return call_fwd(x, g), (x, g)

    def bwd(res, dy):
        x, g = res
        return tuple(call_bwd(x, g, dy))

    rmsnorm.defvjp(fwd, bwd)
    return rmsnorm


def _rope_tables(n_x, n_all, width, half):
    t = np.arange(n_all)
    row, col = t // GRID_W, t % GRID_W
    lane = np.arange(LANE)
    inside = lane < width
    axis_is_col = (lane // (2 * half)) % 2 == 1
    f = (lane % (2 * half)) % half
    inv_freq = ROPE_THETA ** (-(f.astype(np.float64)) / half)
    pos = np.where(axis_is_col[None, :], col[:, None], row[:, None]).astype(np.float64)
    ang = (pos.astype(np.float32) * inv_freq.astype(np.float32)[None, :]).astype(np.float32)
    live = inside[None, :] & (t < n_x)[:, None]
    cos = np.where(live, np.cos(ang), 1.0).astype(np.float32)
    sin = np.where(live, np.sin(ang), 0.0).astype(np.float32)
    first = (lane % (2 * half)) < half
    s_lo = np.where(first[None, :], -sin, 0.0).astype(np.float32)
    s_hi = np.where(first[None, :], 0.0, sin).astype(np.float32)
    return jnp.asarray(cos), jnp.asarray(s_lo), jnp.asarray(s_hi)


def _make_rope(tag, half):
    def call(x, cos, s_lo, s_hi, transpose, name):
        T, W = x.shape

        def body(x_ref, c_ref, lo_ref, hi_ref, o_ref):
            x = x_ref[...]
            if not transpose:
                o_ref[...] = (x * c_ref[...] + pltpu.roll(x, LANE - half, 1) * lo_ref[...]
                              + pltpu.roll(x, half, 1) * hi_ref[...])
            else:
                o_ref[...] = (x * c_ref[...] + pltpu.roll(x * lo_ref[...], half, 1)
                              + pltpu.roll(x * hi_ref[...], LANE - half, 1))

        blk = pl.BlockSpec((ROW_TILE, LANE), lambda i, c: (i, c))
        tab = pl.BlockSpec((ROW_TILE, LANE), lambda i, c: (i, 0))
        return pl.pallas_call(body, name=name, grid=(T // ROW_TILE, W // LANE), in_specs=[blk, tab, tab, tab],
                              out_specs=blk, out_shape=jax.ShapeDtypeStruct((T, W), F32),
                              compiler_params=_cparams(2))(x, cos, s_lo, s_hi)

    @jax.custom_vjp
    def rope(x, cos, s_lo, s_hi):
        return call(x, cos, s_lo, s_hi, False, tag + "_fwd")

    def fwd(x, cos, s_lo, s_hi):
        return call(x, cos, s_lo, s_hi, False, tag + "_fwd"), (cos, s_lo, s_hi)

    def bwd(res, dy):
        cos, s_lo, s_hi = res
        return call(dy, cos, s_lo, s_hi, True, tag + "_bwd"), None, None, None

    rope.defvjp(fwd, bwd)
    return rope


def _make_conv_gate(tag, n_x):
    def shifted(v, T):
        t = lax.broadcasted_iota(jnp.int32, v.shape, 0)
        prev = jnp.where((t == 0) | (t == n_x), 0.0, pltpu.roll(v, 1, 0))
        nxt = jnp.where((t == n_x - 1) | (t == T - 1), 0.0, pltpu.roll(v, T - 1, 0))
        return prev, nxt

    def call_fwd(a, u, cw, cb):
        T, Fd = a.shape

        def body(a_ref, u_ref, cw_ref, cb_ref, z_ref):
            pre = a_ref[...]
            prev, nxt = shifted(pre, T)
            s = prev * cw_ref[0:1, :] + pre * cw_ref[1:2, :] + nxt * cw_ref[2:3, :] + cb_ref[...]
            z_ref[...] = s * (1.0 / (1.0 + jnp.exp(-s))) * u_ref[...]

        col = pl.BlockSpec((T, LANE), lambda j: (0, j))
        return pl.pallas_call(body, name=tag + "_fwd", grid=(Fd // LANE,),
                              in_specs=[col, col, pl.BlockSpec((3, LANE), lambda j: (0, j)),
                                        pl.BlockSpec((1, LANE), lambda j: (0, j))],
                              out_specs=col, out_shape=jax.ShapeDtypeStruct((T, Fd), F32),
                              compiler_params=_cparams(1))(a, u, cw, cb)

    def call_bwd(a, u, cw, cb, dz):
        T, Fd = a.shape

        def body(a_ref, u_ref, cw_ref, cb_ref, dz_ref, da_ref, du_ref, dcw_ref, dcb_ref):
            pre = a_ref[...]
            prev, nxt = shifted(pre, T)
            s = prev * cw_ref[0:1, :] + pre * cw_ref[1:2, :] + nxt * cw_ref[2:3, :] + cb_ref[...]
            sig = 1.0 / (1.0 + jnp.exp(-s))
            dz = dz_ref[...]
            du_ref[...] = dz * s * sig
            ds = dz * u_ref[...] * (sig * (1.0 + s * (1.0 - sig)))
            ds_prev, ds_next = shifted(ds, T)
            da_ref[...] = ds * cw_ref[1:2, :] + ds_next * cw_ref[0:1, :] + ds_prev * cw_ref[2:3, :]
            dcw_ref[0:1, :] = jnp.sum(ds * prev, axis=0, keepdims=True)
            dcw_ref[1:2, :] = jnp.sum(ds * pre, axis=0, keepdims=True)
            dcw_ref[2:3, :] = jnp.sum(ds * nxt, axis=0, keepdims=True)
            dcb_ref[...] = jnp.sum(ds, axis=0, keepdims=True)

        col = pl.BlockSpec((T, LANE), lambda j: (0, j))
        w3 = pl.BlockSpec((3, LANE), lambda j: (0, j))
        w1 = pl.BlockSpec((1, LANE), lambda j: (0, j))
        big = jax.ShapeDtypeStruct((T, Fd), F32)
        return pl.pallas_call(body, name=tag + "_bwd", grid=(Fd // LANE,), in_specs=[col, col, w3, w1, col],
                              out_specs=[col, col, w3, w1],
                              out_shape=[big, big, jax.ShapeDtypeStruct((3, Fd), F32),
                                         jax.ShapeDtypeStruct((1, Fd), F32)],
                              compiler_params=_cparams(1))(a, u, cw, cb, dz)

    @jax.custom_vjp
    def conv_gate(a, u, cw, cb):
        return call_fwd(a, u, cw, cb)

    def fwd(a, u, cw, cb):
        return call_fwd(a, u, cw, cb), (a, u, cw, cb)

    def bwd(res, dz):
        return tuple(call_bwd(*res, dz))

    conv_gate.defvjp(fwd, bwd)
    return conv_gate


def _make_loss_head(tag):
    def call(y, tgt):
        T, D = y.shape

        def body(y_ref, t_ref, l_ref, r_ref):
            @pl.when(pl.program_id(0) == 0)
            def _():
                l_ref[...] = jnp.zeros_like(l_ref)

            d = y_ref[...] - t_ref[...]
            r_ref[...] = d * (1.0 / D)
            l_ref[...] += jnp.sum(d * d) * (0.5 / D)

        row = pl.BlockSpec((ROW_TILE, D), lambda i: (i, 0))
        return pl.pallas_call(body, name=tag, grid=(T // ROW_TILE,), in_specs=[row, row],
                              out_specs=[pl.BlockSpec((8, LANE), lambda i: (0, 0)), row],
                              out_shape=[jax.ShapeDtypeStruct((8, LANE), F32), jax.ShapeDtypeStruct((T, D), F32)],
                              compiler_params=_cparams(1))(y, tgt)

    @jax.custom_vjp
    def loss_head(y, tgt):
        return call(y, tgt)[0][0, 0]

    def fwd(y, tgt):
        l, r = call(y, tgt)
        return l[0, 0], r

    def bwd(r, g):
        return r * g, None

    loss_head.defvjp(fwd, bwd)
    return loss_head


class _AttnCfg:
    def __init__(self, mode, H, Hkv, dqk, dv, scale, tq, tk, nx=0, nctx=0, has_sink=False):
        self.mode, self.H, self.Hkv, self.dqk, self.dv, self.scale = mode, H, Hkv, dqk, dv, scale
        self.tq, self.tk, self.nx, self.nctx, self.has_sink = tq, tk, nx, nctx, has_sink
        self.G = H // Hkv
        self.n_local = {"full": 0, "swa": 3, "na": 5}[mode]

    def nkv(self, Tk):
        return Tk // self.tk if self.mode == "full" else self.n_local + self.nctx

    def kv_block(self, i, j):
        if self.mode == "full":
            return j
        if self.mode == "swa":
            return jnp.where(j < 3, jnp.clip(i - 1 + j, 0, self.nx - 1), self.nx + j - 3)
        return jnp.where(j < 5, jnp.clip(i - 2, 0, self.nx - 5) + j, self.nx + j - 5)

    def local_active(self, i, j):
        if self.mode == "swa":
            return (i - 1 + j >= 0) & (i - 1 + j <= self.nx - 1)
        return True

    def local_scores(self, s, i, j, tab_ref):
        tq, tk = self.tq, self.tk
        sub = lax.broadcasted_iota(jnp.int32, (tq, tk), 0)
        lan = lax.broadcasted_iota(jnp.int32, (tq, tk), 1)
        kb = self.kv_block(i, j)
        if self.mode == "swa":
            diff = (i * tq + sub) - (kb * tk + lan)
            return jnp.where(jnp.abs(diff) <= SWA_WINDOW, s, -jnp.inf)
        half = GRID_W
        rows = 2 * self.nx
        qrow = 2 * i + jnp.where(sub >= half, 1, 0)
        krow = 2 * kb + jnp.where(lan >= half, 1, 0)
        r0 = jnp.clip(qrow - NA_WIN_R // 2, 0, rows - NA_WIN_R)
        ok = (krow >= r0) & (krow < r0 + NA_WIN_R)
        top = tab_ref[0, jnp.clip(2 * (kb - i) + 8, 0, 15)]
        bot = tab_ref[0, jnp.clip(2 * (kb - i) + 7, 0, 15)]
        return jnp.where(ok, s + jnp.concatenate([top, bot], axis=0), -jnp.inf)

    def na_bias_slots(self, i, j):
        kb = self.kv_block(i, j)
        return jnp.clip(2 * (kb - i) + 8, 0, 15), jnp.clip(2 * (kb - i) + 7, 0, 15)


def _attn_specs(cfg, Tk):
    G, tq, tk = cfg.G, cfg.tq, cfg.tk
    qmap = lambda kh, g, i, j: (i, kh * G + g)
    q_spec = pl.BlockSpec((tq, cfg.dqk), qmap)
    k_spec = pl.BlockSpec((tk, cfg.dqk), lambda kh, g, i, j: (cfg.kv_block(i, j), kh))
    v_spec = pl.BlockSpec((tk, cfg.dv), lambda kh, g, i, j: (cfg.kv_block(i, j), kh))
    o_spec = pl.BlockSpec((tq, cfg.dv), qmap)
    lse_spec = pl.BlockSpec((1, tq, LANE), lambda kh, g, i, j: (kh * G + g, i, 0))
    tab_spec = pl.BlockSpec((1, 16, GRID_W, LANE), lambda kh, g, i, j: (kh * G + g, 0, 0, 0))
    sink_spec = pl.BlockSpec((1, 8, LANE), lambda kh, g, i, j: (kh * G + g, 0, 0))
    return q_spec, k_spec, v_spec, o_spec, lse_spec, tab_spec, sink_spec


def _attn_fwd_call(cfg, name, q, k, v, tab, sink):
    Tq, Tk = q.shape[0], k.shape[0]
    nq, nkv = Tq // cfg.tq, cfg.nkv(Tk)
    dn = (((1,), (1,)), ((), ()))

    def body(*refs):
        q_ref, k_ref, v_ref = refs[:3]
        n = 3
        tab_ref = sink_ref = None
        if cfg.mode == "na":
            tab_ref, n = refs[n], n + 1
        if cfg.has_sink:
            sink_ref, n = refs[n], n + 1
        o_ref, lse_ref, m_scr, l_scr, acc_scr = refs[n:]
        i, j = pl.program_id(2), pl.program_id(3)

        @pl.when(j == 0)
        def _():
            m_scr[...] = jnp.full_like(m_scr, NEG)
            l_scr[...] = jnp.zeros_like(l_scr)
            acc_scr[...] = jnp.zeros_like(acc_scr)

        def raw_scores():
            return lax.dot_general(q_ref[...].astype(_MXU), k_ref[...].astype(_MXU), dn,
                                   preferred_element_type=F32) * cfg.scale

        def update(s):
            m_old = m_scr[...]
            m_new = jnp.maximum(m_old, jnp.max(s, axis=-1, keepdims=True))
            alpha = jnp.exp(m_old - m_new)
            p = jnp.exp(s - m_new)
            l_scr[...] = alpha * l_scr[...] + jnp.sum(p, axis=-1, keepdims=True)
            acc_scr[...] = alpha * acc_scr[...] + jnp.dot(p.astype(_MXU), v_ref[...].astype(_MXU),
                                                          preferred_element_type=F32)
            m_scr[...] = m_new

        if cfg.mode == "full":
            update(raw_scores())
        else:
            @pl.when((j < cfg.n_local) & cfg.local_active(i, j))
            def _():
                update(cfg.local_scores(raw_scores(), i, j, tab_ref))

            @pl.when(j >= cfg.n_local)
            def _():
                update(raw_scores())

        @pl.when(j == nkv - 1)
        def _():
            m, l = m_scr[...], l_scr[...]
            if cfg.has_sink:
                sk = sink_ref[0, 0:1, 0:1]
                m_new = jnp.maximum(m, sk)
                alpha = jnp.exp(m - m_new)
                l = alpha * l + jnp.exp(sk - m_new)
                acc = acc_scr[...] * alpha
                m = m_new
            else:
                acc = acc_scr[...]
            o_ref[...] = acc / l
            lse_ref[0] = jnp.broadcast_to(m + jnp.log(l), (cfg.tq, LANE))

    q_spec, k_spec, v_spec, o_spec, lse_spec, tab_spec, sink_spec = _attn_specs(cfg, Tk)
    in_specs, args = [q_spec, k_spec, v_spec], [q, k, v]
    if cfg.mode == "na":
        in_specs.append(tab_spec)
        args.append(tab)
    if cfg.has_sink:
        in_specs.append(sink_spec)
        args.append(sink)
    return pl.pallas_call(
        body, name=name, grid=(cfg.Hkv, cfg.G, nq, nkv), in_specs=in_specs, out_specs=[o_spec, lse_spec],
        out_shape=[jax.ShapeDtypeStruct((Tq, cfg.H * cfg.dv), F32), jax.ShapeDtypeStruct((cfg.H, Tq, LANE), F32)],
        scratch_shapes=[pltpu.VMEM((cfg.tq, 1), F32), pltpu.VMEM((cfg.tq, 1), F32), pltpu.VMEM((cfg.tq, cfg.dv), F32)],
        compiler_params=_cparams(4))(*args)


def _attn_bwd_call(cfg, name, q, k, v, tab, sink, o, lse, do):
    Tq, Tk = q.shape[0], k.shape[0]
    nq, nkv = Tq // cfg.tq, cfg.nkv(Tk)
    tq, tk = cfg.tq, cfg.tk
    dn_nt = (((1,), (1,)), ((), ()))
    dn_tn = (((0,), (0,)), ((), ()))

    def body(*refs):
        q_ref, k_ref, v_ref = refs[:3]
        n = 3
        tab_ref = sink_ref = dtab_ref = dsink_ref = None
        if cfg.mode == "na":
            tab_ref, n = refs[n], n + 1
        if cfg.has_sink:
            sink_ref, n = refs[n], n + 1
        o_ref, lse_ref, do_ref, dq_ref, dk_ref, dv_ref = refs[n:n + 6]
        n += 6
        if cfg.mode == "na":
            dtab_ref, n = refs[n], n + 1
        if cfg.has_sink:
            dsink_ref, n = refs[n], n + 1
        dq_scr, delta_scr = refs[n:]
        g, i, j = pl.program_id(1), pl.program_id(2), pl.program_id(3)

        @pl.when((g == 0) & (i == 0) & (j == 0))
        def _():
            dk_ref[...] = jnp.zeros_like(dk_ref)
            dv_ref[...] = jnp.zeros_like(dv_ref)

        if cfg.mode == "na":
            @pl.when((i == 0) & (j == 0))
            def _():
                dtab_ref[...] = jnp.zeros_like(dtab_ref)

        if cfg.has_sink:
            @pl.when((i == 0) & (j == 0))
            def _():
                dsink_ref[...] = jnp.zeros_like(dsink_ref)

        @pl.when(j == 0)
        def _():
            dq_scr[...] = jnp.zeros_like(dq_scr)
            delta = jnp.sum(do_ref[...] * o_ref[...], axis=-1, keepdims=True)
            delta_scr[...] = delta
            if cfg.has_sink:
                p_sink = jnp.exp(sink_ref[0, 0:1, 0:1] - lse_ref[0][:, 0:1])
                lane0 = ((lax.broadcasted_iota(jnp.int32, (8, LANE), 0) == 0)
                         & (lax.broadcasted_iota(jnp.int32, (8, LANE), 1) == 0))
                dsink_ref[0] += jnp.where(lane0, -jnp.sum(p_sink * delta), 0.0)

        def raw_scores():
            return lax.dot_general(q_ref[...].astype(_MXU), k_ref[...].astype(_MXU), dn_nt,
                                   preferred_element_type=F32) * cfg.scale

        def update(s, local):
            p = jnp.exp(s - lse_ref[0][:, 0:1])
            do_b = do_ref[...].astype(_MXU)
            dp = lax.dot_general(do_b, v_ref[...].astype(_MXU), dn_nt, preferred_element_type=F32)
            ds = p * (dp - delta_scr[...])
            if local and cfg.mode == "na":
                top, bot = cfg.na_bias_slots(i, j)
                dtab_ref[0, top] += ds[0:GRID_W, :]
                dtab_ref[0, bot] += ds[GRID_W:, :]
            dsb = (ds * cfg.scale).astype(_MXU)
            rows = pl.ds(pl.multiple_of(cfg.kv_block(i, j) * tk, tk), tk)
            dq_scr[...] += jnp.dot(dsb, k_ref[...].astype(_MXU), preferred_element_type=F32)
            dk_ref[rows, :] += lax.dot_general(dsb, q_ref[...].astype(_MXU), dn_tn, preferred_element_type=F32)
            dv_ref[rows, :] += lax.dot_general(p.astype(_MXU), do_b, dn_tn, preferred_element_type=F32)

        if cfg.mode == "full":
            update(raw_scores(), False)
        else:
            @pl.when((j < cfg.n_local) & cfg.local_active(i, j))
            def _():
                update(cfg.local_scores(raw_scores(), i, j, tab_ref), True)

            @pl.when(j >= cfg.n_local)
            def _():
                update(raw_scores(), False)

        @pl.when(j == nkv - 1)
        def _():
            dq_ref[...] = dq_scr[...]

    q_spec, k_spec, v_spec, o_spec, lse_spec, tab_spec, sink_spec = _attn_specs(cfg, Tk)
    in_specs, args = [q_spec, k_spec, v_spec], [q, k, v]
    if cfg.mode == "na":
        in_specs.append(tab_spec)
        args.append(tab)
    if cfg.has_sink:
        in_specs.append(sink_spec)
        args.append(sink)
    in_specs += [o_spec, lse_spec, o_spec]
    args += [o, lse, do]
    out_specs = [q_spec, pl.BlockSpec((Tk, cfg.dqk), lambda kh, g, i, j: (0, kh)),
                 pl.BlockSpec((Tk, cfg.dv), lambda kh, g, i, j: (0, kh))]
    out_shape = [jax.ShapeDtypeStruct(q.shape, F32), jax.ShapeDtypeStruct(k.shape, F32),
                 jax.ShapeDtypeStruct(v.shape, F32)]
    if cfg.mode == "na":
        out_specs.append(tab_spec)
        out_shape.append(jax.ShapeDtypeStruct(tab.shape, F32))
    if cfg.has_sink:
        out_specs.append(sink_spec)
        out_shape.append(jax.ShapeDtypeStruct(sink.shape, F32))
    return pl.pallas_call(
        body, name=name, grid=(cfg.Hkv, cfg.G, nq, nkv), in_specs=in_specs, out_specs=out_specs,
        out_shape=out_shape,
        scratch_shapes=[pltpu.VMEM((tq, cfg.dqk), F32), pltpu.VMEM((tq, 1), F32)],
        compiler_params=_cparams(4))(*args)


def _make_attention(tag, cfg):
    @jax.custom_vjp
    def attn(q, k, v, tab, sink):
        return _attn_fwd_call(cfg, tag + "_fwd", q, k, v, tab, sink)[0]

    def fwd(q, k, v, tab, sink):
        o, lse = _attn_fwd_call(cfg, tag + "_fwd", q, k, v, tab, sink)
        return o, (q, k, v, tab, sink, o, lse)

    def bwd(res, do):
        q, k, v, tab, sink, o, lse = res
        outs = list(_attn_bwd_call(cfg, tag + "_bwd", q, k, v, tab, sink, o, lse, do))
        dq, dk, dv = outs[:3]
        rest = outs[3:]
        dtab = rest.pop(0) if cfg.mode == "na" else None
        dsink = rest.pop(0) if cfg.has_sink else None
        return dq, dk, dv, dtab, dsink

    attn.defvjp(fwd, bwd)
    return attn


def _pick_block(n, prefs):
    for p in prefs:
        if n % p == 0:
            return p
    return n


def _na_table(rpb):
    cq = np.arange(GRID_W)
    dcol = np.clip(cq[None, :] - cq[:, None] + NA_WIN_C - 1, 0, 2 * NA_WIN_C - 2)
    onehot = (dcol[:, :, None] == np.arange(2 * NA_WIN_C - 1)[None, None, :]).astype(np.float32)
    c0 = np.clip(cq - NA_WIN_C // 2, 0, GRID_W - NA_WIN_C)
    col_in = (cq[None, :] >= c0[:, None]) & (cq[None, :] < c0[:, None] + NA_WIN_C)
    tz = jnp.einsum("hrj,qkj->hrqk", rpb, jnp.asarray(onehot), precision=lax.Precision.HIGHEST)
    tz = jnp.where(jnp.asarray(col_in)[None, None], tz, NEG)
    zero = jnp.zeros_like(tz[:, :1])
    tzp = jnp.concatenate([zero, tz, zero], axis=1)
    return jnp.concatenate([tzp[:, 0:16], tzp[:, 1:17]], axis=-1)


def _sink_block(sink):
    return jnp.broadcast_to(sink[:, None, None], (sink.shape[0], 8, LANE))


def _seg2(vx, vc, nseg):
    rows = [vx, vc][:nseg]
    return jnp.stack(rows)[:, None, :]


def _local_loss(x, ctx, tgt, modx, modc, p, wb, sinks):
    S, D = x.shape
    C = ctx.shape[0]
    depth = wb["w_in"].shape[0]
    alpha = (2 * depth) ** 0.25
    T0 = S + C
    nx, nctx = S // ATT_BLOCK, C // ATT_BLOCK
    sc = HEAD_DIM ** -0.5
    sc_mla = (MLA_NOPE + MLA_ROPE) ** -0.5
    tq_full = _pick_block(S, (256, 128))
    tk_all = _pick_block(T0, (2176, 640, 512, 256, 128))
    tctx = _pick_block(C, (256, 128))
    rope128 = _rope_tables(S, T0, 128, 32)
    rope64 = _rope_tables(S, T0, 64, 16)

    xs = jnp.concatenate([x, ctx], axis=0)
    for l in range(depth):
        last = l == depth - 1
        t = "l%d_" % l
        lin = lambda a, name: _make_linear(t + name)(a, wb[name][l], sinks[name][l])
        mx, mc = modx[l], modc[l]

        h = _make_modulate(t + "mod1", S)(xs, _seg2(mx[0], mc[0], 2), _seg2(mx[1], mc[1], 2))
        pj = lin(h, "w_in")
        qa, ka, va = pj[:, 0:512], pj[:, 512:1024], pj[:, 1024:1536]
        tab = _na_table(p["na_rpb"][l])
        cfg_a = _AttnCfg("na", NA_HEADS, NA_HEADS, 128, 128, sc, ATT_BLOCK, ATT_BLOCK, nx=nx, nctx=nctx)
        oa = _make_attention(t + "attn_a", cfg_a)(qa[:S], ka, va, tab, None)
        rope_b = _make_rope(t + "rope_b", 32)
        qb = rope_b(pj[:, 1536:2048], *rope128)
        kb = _make_rope(t + "rope_bk", 32)(pj[:, 2048:2304], *rope128)
        vb = pj[:, 2304:2560]
        snk = _sink_block(p["swa_sink"][l])
        cfg_b = _AttnCfg("swa", SWA_HEADS, SWA_KV_HEADS, 128, 128, sc, ATT_BLOCK, ATT_BLOCK, nx=nx, nctx=nctx,
                         has_sink=True)
        ob = _make_attention(t + "attn_b", cfg_b)(qb[:S], kb, vb, None, snk)
        cq = _make_rmsnorm(t + "rms_cq")(pj[:, 2560:2944], p["mla_q_norm"][l][None, :])
        ckv = _make_rmsnorm(t + "rms_ckv")(pj[:, 2944:3072], p["mla_kv_norm"][l][None, :])
        qh = lin(cq, "mla_w_uq").reshape(T0, MLA_HEADS, MLA_NOPE + MLA_ROPE)
        kvh = lin(ckv, "mla_w_ukv").reshape(T0, MLA_HEADS, MLA_NOPE + MLA_V)
        qpe = jnp.pad(qh[:, :, MLA_NOPE:], ((0, 0), (0, 0), (0, LANE - MLA_ROPE))).reshape(T0, MLA_HEADS * LANE)
        qpe = _make_rope(t + "rope_cq", 16)(qpe, *rope64).reshape(T0, MLA_HEADS, LANE)
        kpe = _make_rope(t + "rope_ck", 16)(pj[:, 3072:3200], *rope64)
        qc = jnp.concatenate([qh[:, :, :MLA_NOPE], qpe], axis=-1).reshape(T0, MLA_HEADS * 2 * LANE)
        kc = jnp.concatenate([kvh[:, :, :MLA_NOPE], jnp.broadcast_to(kpe[:, None, :], (T0, MLA_HEADS, LANE))],
                             axis=-1).reshape(T0, MLA_HEADS * 2 * LANE)
        vc = kvh[:, :, MLA_NOPE:].reshape(T0, MLA_HEADS * MLA_V)
        cfg_c = _AttnCfg("full", MLA_HEADS, MLA_HEADS, 2 * LANE, MLA_V, sc_mla, tq_full, tk_all)
        oc = _make_attention(t + "attn_c", cfg_c)(qc[:S], kc, vc, None, None)
        qd = _make_rmsnorm(t + "rms_dq")(pj[:, 3200:3712], p["gqa_q_norm"][l][None, :])
        kd = _make_rmsnorm(t + "rms_dk")(pj[:, 3712:3968], p["gqa_k_norm"][l][None, :])
        qd = _make_rope(t + "rope_dq", 32)(qd, *rope128)
        kd = _make_rope(t + "rope_dk", 32)(kd, *rope128)
        vd = pj[:, 3968:4224]
        cfg_d = _AttnCfg("full", GQA_HEADS, GQA_KV_HEADS, 128, 128, sc, tq_full, tk_all)
        od = _make_attention(t + "attn_d", cfg_d)(qd[:S], kd, vd, None, None)
        mix = jnp.concatenate([oa, ob, oc, od], axis=1)
        if not last:
            full_c = lambda H, Hkv, dqk, s, sink: _AttnCfg("full", H, Hkv, dqk, 128, s, tctx, tctx, has_sink=sink)
            oa_c = _make_attention(t + "ctx_a", full_c(4, 4, 128, sc, False))(qa[S:], ka[S:], va[S:], None, None)
            ob_c = _make_attention(t + "ctx_b", full_c(4, 2, 128, sc, True))(qb[S:], kb[S:], vb[S:], None, snk)
            oc_c = _make_attention(t + "ctx_c", full_c(4, 4, 256, sc_mla, False))(qc[S:], kc[S:], vc[S:], None, None)
            od_c = _make_attention(t + "ctx_d", full_c(4, 2, 128, sc, False))(qd[S:], kd[S:], vd[S:], None, None)
            mix = jnp.concatenate([mix, jnp.concatenate([oa_c, ob_c, oc_c, od_c], axis=1)], axis=0)
            res, nseg = xs, 2
        else:
            res, nseg = xs[:S], 1
        y = lin(mix, "w_out")
        x1 = _make_resid_ln(t + "ln1", S, alpha)(res, y, _seg2(mx[2], mc[2], nseg), p["ln1_g"][l][None, :],
                                                 p["ln1_b"][l][None, :])
        h2 = _make_modulate(t + "mod2", S)(x1, _seg2(mx[3], mc[3], nseg), _seg2(mx[4], mc[4], nseg))
        gp = lin(h2, "ffn_w_gate")
        up = lin(h2, "ffn_w_up")
        z = _make_conv_gate(t + "conv", S)(gp, up, p["ffn_conv_w"][l], p["ffn_conv_b"][l][None, :])
        f = lin(z, "ffn_w_down")
        xs = _make_resid_ln(t + "ln2", S, alpha)(x1, f, _seg2(mx[5], mc[5], nseg), p["ln2_g"][l][None, :],
                                                 p["ln2_b"][l][None, :])
    return _make_loss_head("loss_head")(xs, tgt)


def _exchange(inp, scatter, name):
    out_shape = inp.shape if scatter else (N_DEV,) + inp.shape

    def body(in_ref, out_ref, send_sems, recv_sems, local_sem):
        x, y, c = lax.axis_index("x"), lax.axis_index("y"), lax.axis_index("c")
        me = 4 * x + 2 * y + c
        own = pltpu.make_async_copy(in_ref.at[me] if scatter else in_ref, out_ref.at[me], local_sem)
        own.start()
        copies = []
        for k in range(1, N_DEV):
            px, py, pc = (x + (k >> 2)) % 2, (y + ((k >> 1) & 1)) % 2, (c + (k & 1)) % 2
            peer = 4 * px + 2 * py + pc
            cp = pltpu.make_async_remote_copy(
                src_ref=in_ref.at[peer] if scatter else in_ref, dst_ref=out_ref.at[me],
                send_sem=send_sems.at[k - 1], recv_sem=recv_sems.at[k - 1],
                device_id=(px, py, pc), device_id_type=pl.DeviceIdType.MESH)
            cp.start()
            copies.append(cp)
        for cp in copies:
            cp.wait()
        own.wait()

    hbm = pl.BlockSpec(memory_space=pl.ANY)
    return pl.pallas_call(
        body, name=name, in_specs=[hbm], out_specs=hbm, out_shape=jax.ShapeDtypeStruct(out_shape, inp.dtype),
        scratch_shapes=[pltpu.SemaphoreType.DMA((N_DEV - 1,)), pltpu.SemaphoreType.DMA((N_DEV - 1,)),
                        pltpu.SemaphoreType.DMA],
        compiler_params=pltpu.CompilerParams(has_side_effects=True),
    )(inp)


def _adamw(w, parts, m, v, name):
    R, C = w.shape
    P = parts.shape[0]
    c1 = 1.0 - ADAM_B1 ** ADAM_STEP
    c2 = 1.0 - ADAM_B2 ** ADAM_STEP
    per_row = C * (4 * 14 + 2 * P * parts.dtype.itemsize)
    cands = _divisors(R, 8, R)
    fitting = [d for d in cands if d * per_row <= 24 * 1024 * 1024]
    tr = max(fitting) if fitting else min(cands)

    def body(w_ref, p_ref, m_ref, v_ref, g_ref, d_ref, nm_ref, nv_ref):
        g = p_ref[0].astype(F32)
        for k in range(1, P):
            g = g + p_ref[k].astype(F32)
        mm = ADAM_B1 * m_ref[...] + (1.0 - ADAM_B1) * g
        vv = ADAM_B2 * v_ref[...] + (1.0 - ADAM_B2) * (g * g)
        g_ref[...] = g
        nm_ref[...] = mm
        nv_ref[...] = vv
        d_ref[...] = -ADAM_LR * ((mm / c1) / (jnp.sqrt(vv / c2) + ADAM_EPS) + ADAM_WD * w_ref[...])

    blk = pl.BlockSpec((tr, C), lambda i: (i, 0))
    pblk = pl.BlockSpec((P, tr, C), lambda i: (0, i, 0))
    sh = jax.ShapeDtypeStruct((R, C), F32)
    return pl.pallas_call(body, name=name, grid=(R // tr,), in_specs=[blk, pblk, blk, blk],
                          out_specs=[blk, blk, blk, blk], out_shape=[sh, sh, sh, sh],
                          compiler_params=_cparams(1))(w, parts, m, v)


def _adamw_nd(w, parts, m, v, name):
    shape = w.shape
    C = shape[-1]
    R = int(np.prod(shape[:-1])) if len(shape) > 1 else 1
    outs = _adamw(w.reshape(R, C), parts.reshape(parts.shape[0], R, C), m.reshape(R, C), v.reshape(R, C), name)
    return [o.reshape(shape) for o in outs]


def _silu(v):
    return v * (1.0 / (1.0 + jnp.exp(-v)))


def _ada_rows(c_all, c_ctx):
    return jnp.concatenate([c_all, jnp.broadcast_to(c_ctx[None, :], (N_DEV, c_ctx.shape[0]))], axis=0)


def _silu_rows(rows, name):
    def body(r_ref, o_ref):
        o_ref[...] = _silu(r_ref[...])

    return pl.pallas_call(body, name=name, out_shape=jax.ShapeDtypeStruct(rows.shape, F32))(rows)


_BIG_COL = ("w_in", "mla_w_uq", "mla_w_ukv", "ffn_w_gate", "ffn_w_up")
_BIG_ROW = ("w_out", "ffn_w_down")
_SMALL = ("c_ctx", "na_rpb", "swa_sink", "mla_q_norm", "mla_kv_norm", "gqa_q_norm", "gqa_k_norm",
          "ln1_g", "ln1_b", "ffn_conv_b", "ln2_g", "ln2_b")
_NAMES = ("c_ctx", "w_ada", "b_ada", "w_in", "na_rpb", "swa_sink", "mla_q_norm", "mla_kv_norm", "mla_w_uq",
          "mla_w_ukv", "gqa_q_norm", "gqa_k_norm", "w_out", "ln1_g", "ln1_b", "ffn_w_gate", "ffn_w_up",
          "ffn_conv_w", "ffn_conv_b", "ffn_w_down", "ln2_g", "ln2_b")


def _full_from_cols(g, pad_in=False):
    _, L, K, n = g.shape
    full = jnp.transpose(g, (1, 2, 0, 3)).reshape(L, K, N_DEV * n)
    if pad_in:
        full = jnp.concatenate([full[:, :, :KPE_END], jnp.zeros((L, K, 64), full.dtype), full[:, :, KPE_END:]], axis=-1)
    return full


def _cols_to_parts(dw, pad_in=False):
    dw = jnp.stack(dw)
    if pad_in:
        dw = jnp.concatenate([dw[:, :, :KPE_END], dw[:, :, KPE_END + 64:]], axis=-1)
    L, K, N = dw.shape
    return jnp.transpose(dw.reshape(L, K, N_DEV, N // N_DEV), (2, 0, 1, 3)).astype(BF16)


def _rows_to_parts(dw):
    dw = jnp.stack(dw)
    L, K, N = dw.shape
    return jnp.transpose(dw.reshape(L, N_DEV, K // N_DEV, N), (1, 0, 2, 3)).astype(BF16)


def _pack_small(tree, extra=None):
    flat = [tree[n].reshape(-1) for n in _SMALL]
    flat.append(jnp.zeros((1,), F32) if extra is None else extra.reshape(1))
    v = jnp.concatenate(flat)
    n = v.shape[0]
    padded = -(-n // 1024) * 1024
    return jnp.pad(v, (0, padded - n)).reshape(padded // LANE, LANE)


def _unpack_small(mat, like):
    v = mat.reshape(-1)
    out, o = {}, 0
    for n in _SMALL:
        k = int(np.prod(like[n].shape))
        out[n] = v[o:o + k].reshape(like[n].shape)
        o += k
    return out, v[o]


def kernel(x, c, ctx, c_ctx, w_ada, b_ada, w_in, na_rpb, swa_sink, mla_q_norm, mla_kv_norm, mla_w_uq, mla_w_ukv, gqa_q_norm, gqa_k_norm, w_out, ln1_g, ln1_b, ffn_w_gate, ffn_w_up, ffn_conv_w, ffn_conv_b, ffn_w_down, ln2_g, ln2_b, loss_target, m_c_ctx, m_w_ada, m_b_ada, m_w_in, m_na_rpb, m_swa_sink, m_mla_q_norm, m_mla_kv_norm, m_mla_w_uq, m_mla_w_ukv, m_gqa_q_norm, m_gqa_k_norm, m_w_out, m_ln1_g, m_ln1_b, m_ffn_w_gate, m_ffn_w_up, m_ffn_conv_w, m_ffn_conv_b, m_ffn_w_down, m_ln2_g, m_ln2_b, v_c_ctx, v_w_ada, v_b_ada, v_w_in, v_na_rpb, v_swa_sink, v_mla_q_norm, v_mla_kv_norm, v_mla_w_uq, v_mla_w_ukv, v_gqa_q_norm, v_gqa_k_norm, v_w_out, v_ln1_g, v_ln1_b, v_ffn_w_gate, v_ffn_w_up, v_ffn_conv_w, v_ffn_conv_b, v_ffn_w_down, v_ln2_g, v_ln2_b):
    W = dict(c_ctx=c_ctx, w_ada=w_ada, b_ada=b_ada, w_in=w_in, na_rpb=na_rpb, swa_sink=swa_sink,
             mla_q_norm=mla_q_norm, mla_kv_norm=mla_kv_norm, mla_w_uq=mla_w_uq, mla_w_ukv=mla_w_ukv,
             gqa_q_norm=gqa_q_norm, gqa_k_norm=gqa_k_norm, w_out=w_out, ln1_g=ln1_g, ln1_b=ln1_b,
             ffn_w_gate=ffn_w_gate, ffn_w_up=ffn_w_up, ffn_conv_w=ffn_conv_w, ffn_conv_b=ffn_conv_b,
             ffn_w_down=ffn_w_down, ln2_g=ln2_g, ln2_b=ln2_b)
    M = dict(c_ctx=m_c_ctx, w_ada=m_w_ada, b_ada=m_b_ada, w_in=m_w_in, na_rpb=m_na_rpb, swa_sink=m_swa_sink,
             mla_q_norm=m_mla_q_norm, mla_kv_norm=m_mla_kv_norm, mla_w_uq=m_mla_w_uq, mla_w_ukv=m_mla_w_ukv,
             gqa_q_norm=m_gqa_q_norm, gqa_k_norm=m_gqa_k_norm, w_out=m_w_out, ln1_g=m_ln1_g, ln1_b=m_ln1_b,
             ffn_w_gate=m_ffn_w_gate, ffn_w_up=m_ffn_w_up, ffn_conv_w=m_ffn_conv_w, ffn_conv_b=m_ffn_conv_b,
             ffn_w_down=m_ffn_w_down, ln2_g=m_ln2_g, ln2_b=m_ln2_b)
    V = dict(c_ctx=v_c_ctx, w_ada=v_w_ada, b_ada=v_b_ada, w_in=v_w_in, na_rpb=v_na_rpb, swa_sink=v_swa_sink,
             mla_q_norm=v_mla_q_norm, mla_kv_norm=v_mla_kv_norm, mla_w_uq=v_mla_w_uq, mla_w_ukv=v_mla_w_ukv,
             gqa_q_norm=v_gqa_q_norm, gqa_k_norm=v_gqa_k_norm, w_out=v_w_out, ln1_g=v_ln1_g, ln1_b=v_ln1_b,
             ffn_w_gate=v_ffn_w_gate, ffn_w_up=v_ffn_w_up, ffn_conv_w=v_ffn_conv_w, ffn_conv_b=v_ffn_conv_b,
             ffn_w_down=v_ffn_w_down, ln2_g=v_ln2_g, ln2_b=v_ln2_b)
    L, D, n_ada = w_ada.shape
    me = 4 * lax.axis_index("x") + 2 * lax.axis_index("y") + lax.axis_index("c")
    xs, ctxs, tgt = x[0], ctx[0], loss_target[0]

    c_all = _exchange(c, False, "gather_c").reshape(N_DEV, D)
    a_rows = _silu_rows(_ada_rows(c_all, c_ctx), "ada_silu")
    b_mine = lax.dynamic_slice(b_ada, (0, me * n_ada), (L, n_ada))
    mod_mine = jnp.stack([_matmul(a_rows, w_ada[l], "nn", "ada_fwd_l%d" % l) + b_mine[l][None, :] for l in range(L)])
    mod_all = _exchange(mod_mine, False, "gather_mod")
    mod_all = jnp.transpose(mod_all, (1, 2, 0, 3)).reshape(L, 2 * N_DEV, N_DEV * n_ada)
    modx = lax.dynamic_slice(mod_all, (0, me, 0), (L, 1, 6 * D)).reshape(L, 6, D)
    modc = mod_all[:, N_DEV].reshape(L, 6, D)

    wb = {}
    for n in _BIG_COL:
        wb[n] = _full_from_cols(_exchange(W[n].astype(BF16), False, "gather_" + n), pad_in=(n == "w_in"))
    for n in _BIG_ROW:
        g = _exchange(W[n].astype(BF16), False, "gather_" + n)
        wb[n] = jnp.transpose(g, (1, 0, 2, 3)).reshape(L, N_DEV * g.shape[2], g.shape[3])
    conv_w_full = _full_from_cols(_exchange(ffn_conv_w, False, "gather_conv_w"))

    sinks = {n: tuple(jnp.zeros(wb[n].shape[1:], F32) for _ in range(L)) for n in wb}
    small = {n: W[n] for n in _SMALL if n != "c_ctx"}
    small["ffn_conv_w"] = conv_w_full

    def loss_fn(xv, mxv, mcv, sm, sk):
        return _local_loss(xv, ctxs, tgt, mxv, mcv, sm, wb, sk)

    loss_local, (grad_x, dmodx, dmodc, dsmall, dbig) = jax.value_and_grad(loss_fn, argnums=(0, 1, 2, 3, 4))(
        xs, modx, modc, small, sinks)

    dmods = _exchange(jnp.stack([dmodx.reshape(L, 6 * D), dmodc.reshape(L, 6 * D)]), False, "gather_dmod")
    dm_rows = jnp.concatenate([dmods[:, 0], dmods[:, 1]], axis=0)
    dm_rows = jnp.transpose(dm_rows, (1, 0, 2))
    dm_mine = lax.dynamic_slice(dm_rows, (0, 0, me * n_ada), (L, 2 * N_DEV, n_ada))
    g_w_ada = jnp.stack([_matmul(a_rows, dm_mine[l], "tn", "ada_dw_l%d" % l) for l in range(L)])
    d_rows = sum(_matmul(dm_mine[l], w_ada[l], "nt", "ada_da_l%d" % l) for l in range(L))
    sig = 1.0 / (1.0 + jnp.exp(-c_ctx))
    d_c_ctx_part = jnp.sum(d_rows[N_DEV:], axis=0) * (sig * (1.0 + c_ctx * (1.0 - sig)))

    outs = {}
    for n in _BIG_COL:
        parts = _exchange(_cols_to_parts(dbig[n], pad_in=(n == "w_in")), True, "scatter_" + n)
        outs[n] = _adamw_nd(W[n], parts, M[n], V[n], "adamw_" + n)
    for n in _BIG_ROW:
        parts = _exchange(_rows_to_parts(dbig[n]), True, "scatter_" + n)
        outs[n] = _adamw_nd(W[n], parts, M[n], V[n], "adamw_" + n)
    dcw = dsmall.pop("ffn_conv_w")
    cw_parts = _exchange(jnp.transpose(dcw.reshape(L, 3, N_DEV, -1), (2, 0, 1, 3)), True, "scatter_conv_w")
    outs["ffn_conv_w"] = _adamw_nd(ffn_conv_w, cw_parts, M["ffn_conv_w"], V["ffn_conv_w"], "adamw_conv_w")
    outs["w_ada"] = _adamw_nd(w_ada, g_w_ada[None], M["w_ada"], V["w_ada"], "adamw_w_ada")
    outs["b_ada"] = _adamw_nd(b_ada, jnp.transpose(dm_rows, (1, 0, 2)), M["b_ada"], V["b_ada"], "adamw_b_ada")

    dsmall["c_ctx"] = d_c_ctx_part
    small_parts = _exchange(_pack_small(dsmall, extra=loss_local), False, "gather_small")
    w_small = _pack_small({n: W[n] for n in _SMALL})
    g_s, d_s, nm_s, nv_s = _adamw(w_small, small_parts, _pack_small({n: M[n] for n in _SMALL}),
                                  _pack_small({n: V[n] for n in _SMALL}), "adamw_small")
    like = {n: W[n] for n in _SMALL}
    g_small, loss = _unpack_small(g_s, like)
    unpacked = [g_small, _unpack_small(d_s, like)[0], _unpack_small(nm_s, like)[0], _unpack_small(nv_s, like)[0]]
    for n in _SMALL:
        outs[n] = [u[n] for u in unpacked]

    result = [loss, grad_x[None]]
    for k in range(4):
        result += [outs[n][k] for n in _NAMES]
    return tuple(result)
```

```python
import functools
import math

import numpy as np
import jax
import jax.numpy as jnp
from jax import lax
from jax.experimental import pallas as pl
from jax.experimental.pallas import tpu as pltpu

F32 = jnp.float32
BF16 = jnp.bfloat16
_MXU = jnp.bfloat16

N_DEV = 8
GRID_W = 64
HEAD_DIM = 128
NA_HEADS, NA_WIN_R, NA_WIN_C = 4, 8, 16
SWA_HEADS, SWA_KV_HEADS, SWA_WINDOW = 4, 2, 128
MLA_HEADS, MLA_Q_LORA, MLA_KV_LORA, MLA_NOPE, MLA_ROPE, MLA_V = 4, 384, 128, 128, 64, 128
GQA_HEADS, GQA_KV_HEADS = 4, 2
ROPE_THETA = 10000.0
EPS = 1e-6
NEG = -1e30
IN_SIZES = (512, 512, 512, 512, 256, 256, MLA_Q_LORA, MLA_KV_LORA, MLA_ROPE, 512, 256, 256)
IN_COLS = sum(IN_SIZES)
KPE_END = sum(IN_SIZES[:9])
IN_COLS_PAD = IN_COLS + 64
ADAM_LR, ADAM_B1, ADAM_B2, ADAM_EPS, ADAM_WD, ADAM_STEP = 0.001, 0.9, 0.999, 1e-08, 0.01, 10

LANE = 128
ROW_TILE = 256
ATT_BLOCK = 128
VMEM_LIMIT = 56 * 1024 * 1024
MM_BUDGET = 36 * 1024 * 1024
HBM_BPS = 3.0e12
STEP_S = 0.4e-6


def _cparams(n_axes):
    return pltpu.CompilerParams(dimension_semantics=("arbitrary",) * n_axes, vmem_limit_bytes=VMEM_LIMIT)


def _exchange_copies(in_ref, out_ref, send_sems, recv_sems, local_sem, scatter):
    x, y, c = lax.axis_index("x"), lax.axis_index("y"), lax.axis_index("c")
    me = 4 * x + 2 * y + c
    copies = [pltpu.make_async_copy(in_ref.at[me] if scatter else in_ref, out_ref.at[me], local_sem)]
    for k in range(1, N_DEV):
        px, py, pc = (x + (k >> 2)) % 2, (y + ((k >> 1) & 1)) % 2, (c + (k & 1)) % 2
        peer = 4 * px + 2 * py + pc
        copies.append(pltpu.make_async_remote_copy(
            src_ref=in_ref.at[peer] if scatter else in_ref, dst_ref=out_ref.at[me],
            send_sem=send_sems.at[k - 1], recv_sem=recv_sems.at[k - 1],
            device_id=(px, py, pc), device_id_type=pl.DeviceIdType.MESH))
    return copies


def _exchange_out_shape(arr, scatter):
    return jax.ShapeDtypeStruct(arr.shape if scatter else (N_DEV,) + arr.shape, arr.dtype)


_EXCHANGE_SEMS = (pltpu.SemaphoreType.DMA((N_DEV - 1,)), pltpu.SemaphoreType.DMA((N_DEV - 1,)),
                  pltpu.SemaphoreType.DMA)


def _exchange(inp, scatter, name):
    def body(in_ref, out_ref, send_sems, recv_sems, local_sem):
        copies = _exchange_copies(in_ref, out_ref, send_sems, recv_sems, local_sem, scatter)
        for cp in copies:
            cp.start()
        for cp in copies:
            cp.wait()

    hbm = pl.BlockSpec(memory_space=pl.ANY)
    return pl.pallas_call(
        body, name=name, in_specs=[hbm], out_specs=hbm, out_shape=_exchange_out_shape(inp, scatter),
        scratch_shapes=list(_EXCHANGE_SEMS), compiler_params=pltpu.CompilerParams(has_side_effects=True),
    )(inp)


def _pcall(body, *, name, grid, in_specs, out_specs, out_shape, args, scratch_shapes=(), riders=()):
    in_specs, out_specs, out_shape = list(in_specs), list(out_specs), list(out_shape)
    scratch, args = list(scratch_shapes), list(args)
    n_in, n_out, n_scr, n_r = len(in_specs), len(out_specs), len(scratch), len(riders)
    hbm = pl.BlockSpec(memory_space=pl.ANY)
    for arr, scatter in riders:
        in_specs.append(hbm)
        args.append(arr)
        out_specs.append(hbm)
        out_shape.append(_exchange_out_shape(arr, scatter))
        scratch += list(_EXCHANGE_SEMS)

    def wrapped(*refs):
        ins, r_in = refs[:n_in], refs[n_in:n_in + n_r]
        o0 = n_in + n_r
        outs, r_out = refs[o0:o0 + n_out], refs[o0 + n_out:o0 + n_out + n_r]
        s0 = o0 + n_out + n_r
        scr, sems = refs[s0:s0 + n_scr], refs[s0 + n_scr:]

        def copies():
            out = []
            for r, (_, scatter) in enumerate(riders):
                out += _exchange_copies(r_in[r], r_out[r], sems[3 * r], sems[3 * r + 1], sems[3 * r + 2], scatter)
            return out

        if n_r:
            ids = [pl.program_id(a) for a in range(len(grid))]
            first = functools.reduce(lambda u, w: u & w, [i == 0 for i in ids])
            last = functools.reduce(lambda u, w: u & w, [i == n - 1 for i, n in zip(ids, grid)])

            @pl.when(first)
            def _():
                for cp in copies():
                    cp.start()

        body(*ins, *outs, *scr)

        if n_r:
            @pl.when(last)
            def _():
                for cp in copies():
                    cp.wait()

    params = pltpu.CompilerParams(dimension_semantics=("arbitrary",) * len(grid), vmem_limit_bytes=VMEM_LIMIT,
                                  has_side_effects=bool(riders))
    return list(pl.pallas_call(wrapped, name=name, grid=grid, in_specs=in_specs, out_specs=out_specs,
                               out_shape=out_shape, scratch_shapes=scratch, compiler_params=params)(*args))


def _riders(arrays, scatter):
    return [(a, scatter) for a in arrays]


def _nones(arrays):
    return tuple(None for _ in arrays)


def _divisors(n, step, cap):
    out = [d for d in range(step, min(n, cap) + 1, step) if n % d == 0]
    if n <= cap and n not in out:
        out.append(n)
    return out


def _mm_tiles(M, N, K, sa, sb, m_step, k_step):
    best, best_cost = None, None
    for tm in _divisors(M, m_step, 2304):
        for tn in _divisors(N, LANE, 2304):
            for tk in _divisors(K, k_step, 2304):
                vm = 2 * (tm * tk * sa + tk * tn * sb) + 3 * tm * tn * 4
                if vm > MM_BUDGET:
                    continue
                traffic = M * K * sa * (N // tn) + K * N * sb * (M // tm) + M * N * 4
                steps = (M // tm) * (N // tn) * (K // tk)
                cost = traffic / HBM_BPS + steps * STEP_S
                if tm % 256 or tn % 256:
                    cost *= 1.05
                if best_cost is None or cost < best_cost:
                    best, best_cost = (tm, tn, tk), cost
    assert best is not None, (M, N, K)
    return best


def _matmul(a, b, mode, name, riders=()):
    if mode == "nn":
        (M, K), (K2, N) = a.shape, b.shape
    elif mode == "nt":
        (M, K), (N, K2) = a.shape, b.shape
    else:
        (K, M), (K2, N) = a.shape, b.shape
    assert K == K2, (a.shape, b.shape, mode)
    tm, tn, tk = _mm_tiles(M, N, K, a.dtype.itemsize, b.dtype.itemsize,
                           LANE if mode == "tn" else 8, 8 if mode == "tn" else LANE)
    nk = K // tk
    dims = {"nn": (((1,), (0,)), ((), ())), "nt": (((1,), (1,)), ((), ())), "tn": (((0,), (0,)), ((), ()))}[mode]

    def body(a_ref, b_ref, o_ref, acc_ref):
        k = pl.program_id(2)

        @pl.when(k == 0)
        def _():
            acc_ref[...] = jnp.zeros_like(acc_ref)

        acc_ref[...] += lax.dot_general(a_ref[...].astype(_MXU), b_ref[...].astype(_MXU), dims,
                                        preferred_element_type=F32)

        @pl.when(k == nk - 1)
        def _():
            o_ref[...] = acc_ref[...]

    if mode == "nn":
        a_spec = pl.BlockSpec((tm, tk), lambda i, j, k: (i, k))
        b_spec = pl.BlockSpec((tk, tn), lambda i, j, k: (k, j))
    elif mode == "nt":
        a_spec = pl.BlockSpec((tm, tk), lambda i, j, k: (i, k))
        b_spec = pl.BlockSpec((tn, tk), lambda i, j, k: (j, k))
    else:
        a_spec = pl.BlockSpec((tk, tm), lambda i, j, k: (k, i))
        b_spec = pl.BlockSpec((tk, tn), lambda i, j, k: (k, j))
    outs = _pcall(body, name=name, grid=(M // tm, N // tn, nk), in_specs=[a_spec, b_spec],
                  out_specs=[pl.BlockSpec((tm, tn), lambda i, j, k: (i, j))],
                  out_shape=[jax.ShapeDtypeStruct((M, N), F32)], args=[a, b],
                  scratch_shapes=[pltpu.VMEM((tm, tn), F32)], riders=riders)
    return outs if riders else outs[0]


def _make_linear(tag, to_parts):
    def run(a, w, gin, tok):
        if gin:
            y, *got = _matmul(a, w, "nn", tag + "_fwd", riders=_riders(gin, False))
        else:
            y, got = _matmul(a, w, "nn", tag + "_fwd"), []
        return y, tuple(got), tok

    @jax.custom_vjp
    def linear(a, w, sink, gin, tok):
        return run(a, w, gin, tok)

    def fwd(a, w, sink, gin, tok):
        return run(a, w, gin, tok), (a, w, gin)

    def bwd(res, cts):
        a, w, gin = res
        dy, _, dtok = cts
        dw = _matmul(a, dy, "tn", tag + "_bwd_dw")
        if dtok:
            da, *recv = _matmul(dy, w, "nt", tag + "_bwd_da", riders=_riders(dtok, True))
        else:
            da, recv = _matmul(dy, w, "nt", tag + "_bwd_da"), []
        return da, jnp.zeros_like(w), to_parts(dw), _nones(gin), tuple(recv)

    linear.defvjp(fwd, bwd)
    return linear


def _seg_map(n_x_tiles):
    return lambda i: (jnp.where(i >= n_x_tiles, 1, 0), 0, 0)


def _ln_stats(x):
    mu = jnp.mean(x, axis=-1, keepdims=True)
    xc = x - mu
    var = jnp.mean(xc * xc, axis=-1, keepdims=True)
    rstd = lax.rsqrt(var + EPS)
    return xc * rstd, rstd


def _make_modulate(tag, n_x):
    def call_fwd(x, shift, scale):
        T, D = x.shape
        nxt = n_x // ROW_TILE

        def body(x_ref, sh_ref, sc_ref, o_ref):
            xhat, _ = _ln_stats(x_ref[...])
            o_ref[...] = xhat * (1.0 + sc_ref[0]) + sh_ref[0]

        row = pl.BlockSpec((ROW_TILE, D), lambda i: (i, 0))
        seg = pl.BlockSpec((1, 1, D), _seg_map(nxt))
        return pl.pallas_call(body, name=tag + "_fwd", grid=(T // ROW_TILE,), in_specs=[row, seg, seg],
                              out_specs=row, out_shape=jax.ShapeDtypeStruct((T, D), F32),
                              compiler_params=_cparams(1))(x, shift, scale)

    def call_bwd(x, scale, dh):
        T, D = x.shape
        nxt = n_x // ROW_TILE
        nseg = scale.shape[0]

        def body(x_ref, sc_ref, dh_ref, dx_ref, dsh_ref, dsc_ref):
            i = pl.program_id(0)

            @pl.when((i == 0) | (i == nxt))
            def _():
                dsh_ref[...] = jnp.zeros_like(dsh_ref)
                dsc_ref[...] = jnp.zeros_like(dsc_ref)

            xhat, rstd = _ln_stats(x_ref[...])
            dh = dh_ref[...]
            dsh_ref[0] += jnp.sum(dh, axis=0, keepdims=True)
            dsc_ref[0] += jnp.sum(dh * xhat, axis=0, keepdims=True)
            dxh = dh * (1.0 + sc_ref[0])
            m1 = jnp.mean(dxh, axis=-1, keepdims=True)
            m2 = jnp.mean(dxh * xhat, axis=-1, keepdims=True)
            dx_ref[...] = rstd * (dxh - m1 - xhat * m2)

        row = pl.BlockSpec((ROW_TILE, D), lambda i: (i, 0))
        seg = pl.BlockSpec((1, 1, D), _seg_map(nxt))
        segshape = jax.ShapeDtypeStruct((nseg, 1, D), F32)
        return pl.pallas_call(body, name=tag + "_bwd", grid=(T // ROW_TILE,), in_specs=[row, seg, row],
                              out_specs=[row, seg, seg],
                              out_shape=[jax.ShapeDtypeStruct((T, D), F32), segshape, segshape],
                              compiler_params=_cparams(1))(x, scale, dh)

    @jax.custom_vjp
    def modulate(x, shift, scale):
        return call_fwd(x, shift, scale)

    def fwd(x, shift, scale):
        return call_fwd(x, shift, scale), (x, scale)

    def bwd(res, dh):
        x, scale = res
        return tuple(call_bwd(x, scale, dh))

    modulate.defvjp(fwd, bwd)
    return modulate


def _make_resid_ln(tag, n_x, alpha):
    def call_fwd(x, y, gate, g, b):
        T, D = x.shape
        nxt = n_x // ROW_TILE

        def body(x_ref, y_ref, gt_ref, g_ref, b_ref, o_ref):
            u = alpha * x_ref[...] + gt_ref[0] * y_ref[...]
            uhat, _ = _ln_stats(u)
            o_ref[...] = uhat * g_ref[...] + b_ref[...]

        row = pl.BlockSpec((ROW_TILE, D), lambda i: (i, 0))
        seg = pl.BlockSpec((1, 1, D), _seg_map(nxt))
        vec = pl.BlockSpec((1, D), lambda i: (0, 0))
        return pl.pallas_call(body, name=tag + "_fwd", grid=(T // ROW_TILE,), in_specs=[row, row, seg, vec, vec],
                              out_specs=row, out_shape=jax.ShapeDtypeStruct((T, D), F32),
                              compiler_params=_cparams(1))(x, y, gate, g, b)

    def call_bwd(x, y, gate, g, do):
        T, D = x.shape
        nxt = n_x // ROW_TILE
        nseg = gate.shape[0]

        def body(x_ref, y_ref, gt_ref, g_ref, do_ref, dx_ref, dy_ref, dgt_ref, dg_ref, db_ref):
            i = pl.program_id(0)

            @pl.when(i == 0)
            def _():
                dg_ref[...] = jnp.zeros_like(dg_ref)
                db_ref[...] = jnp.zeros_like(db_ref)

            @pl.when((i == 0) | (i == nxt))
            def _():
                dgt_ref[...] = jnp.zeros_like(dgt_ref)

            y = y_ref[...]
            gate_v = gt_ref[0]
            uhat, rstd = _ln_stats(alpha * x_ref[...] + gate_v * y)
            do = do_ref[...]
            dg_ref[...] += jnp.sum(do * uhat, axis=0, keepdims=True)
            db_ref[...] += jnp.sum(do, axis=0, keepdims=True)
            duh = do * g_ref[...]
            m1 = jnp.mean(duh, axis=-1, keepdims=True)
            m2 = jnp.mean(duh * uhat, axis=-1, keepdims=True)
            du = rstd * (duh - m1 - uhat * m2)
            dx_ref[...] = alpha * du
            dy_ref[...] = gate_v * du
            dgt_ref[0] += jnp.sum(du * y, axis=0, keepdims=True)

        row = pl.BlockSpec((ROW_TILE, D), lambda i: (i, 0))
        seg = pl.BlockSpec((1, 1, D), _seg_map(nxt))
        vec = pl.BlockSpec((1, D), lambda i: (0, 0))
        rs = jax.ShapeDtypeStruct((T, D), F32)
        vs = jax.ShapeDtypeStruct((1, D), F32)
        return pl.pallas_call(body, name=tag + "_bwd", grid=(T // ROW_TILE,), in_specs=[row, row, seg, vec, row],
                              out_specs=[row, row, seg, vec, vec],
                              out_shape=[rs, rs, jax.ShapeDtypeStruct((nseg, 1, D), F32), vs, vs],
                              compiler_params=_cparams(1))(x, y, gate, g, do)

    @jax.custom_vjp
    def resid_ln(x, y, gate, g, b):
        return call_fwd(x, y, gate, g, b)

    def fwd(x, y, gate, g, b):
        return call_fwd(x, y, gate, g, b), (x, y, gate, g)

    def bwd(res, do):
        x, y, gate, g = res
        return tuple(call_bwd(x, y, gate, g, do))

    resid_ln.defvjp(fwd, bwd)
    return resid_ln


def _make_rmsnorm(tag):
    def call_fwd(x, g):
        T, W = x.shape
        gw = g.shape[1]

        def body(x_ref, g_ref, o_ref):
            x = x_ref[...]
            r = lax.rsqrt(jnp.mean(x * x, axis=-1, keepdims=True) + EPS)
            o_ref[...] = x * r * g_ref[...]

        blk = pl.BlockSpec((ROW_TILE, gw), lambda i, c: (i, c))
        vec = pl.BlockSpec((1, gw), lambda i, c: (0, 0))
        return pl.pallas_call(body, name=tag + "_fwd", grid=(T // ROW_TILE, W // gw), in_specs=[blk, vec],
                              out_specs=blk, out_shape=jax.ShapeDtypeStruct((T, W), F32),
                              compiler_params=_cparams(2))(x, g)

    def call_bwd(x, g, dy):
        T, W = x.shape
        gw = g.shape[1]

        def body(x_ref, g_ref, dy_ref, dx_ref, dg_ref):
            @pl.when((pl.program_id(0) == 0) & (pl.program_id(1) == 0))
            def _():
                dg_ref[...] = jnp.zeros_like(dg_ref)

            x = x_ref[...]
            r = lax.rsqrt(jnp.mean(x * x, axis=-1, keepdims=True) + EPS)
            xn = x * r
            dy = dy_ref[...]
            dg_ref[...] += jnp.sum(dy * xn, axis=0, keepdims=True)
            dxn = dy * g_ref[...]
            dx_ref[...] = r * (dxn - xn * jnp.mean(dxn * xn, axis=-1, keepdims=True))

        blk = pl.BlockSpec((ROW_TILE, gw), lambda i, c: (i, c))
        vec = pl.BlockSpec((1, gw), lambda i, c: (0, 0))
        return pl.pallas_call(body, name=tag + "_bwd", grid=(T // ROW_TILE, W // gw), in_specs=[blk, vec, blk],
                              out_specs=[blk, vec],
                              out_shape=[jax.ShapeDtypeStruct((T, W), F32), jax.ShapeDtypeStruct((1, gw), F32)],
                              compiler_params=_cparams(2))(x, g, dy)

    @jax.custom_vjp
    def rmsnorm(x, g):
        return call_fwd(x, g)

    def fwd(x, g):
        return call_fwd(x, g), (x, g)

    def bwd(res, dy):
        x, g = res
        return tuple(call_bwd(x, g, dy))

    rmsnorm.defvjp(fwd, bwd)
    return rmsnorm


def _rope_tables(n_x, n_all, width, half):
    t = np.arange(n_all)
    row, col = t // GRID_W, t % GRID_W
    lane = np.arange(LANE)
    inside = lane < width
    axis_is_col = (lane // (2 * half)) % 2 == 1
    f = (lane % (2 * half)) % half
    inv_freq = ROPE_THETA ** (-(f.astype(np.float64)) / half)
    pos = np.where(axis_is_col[None, :], col[:, None], row[:, None]).astype(np.float64)
    ang = (pos.astype(np.float32) * inv_freq.astype(np.float32)[None, :]).astype(np.float32)
    live = inside[None, :] & (t < n_x)[:, None]
    cos = np.where(live, np.cos(ang), 1.0).astype(np.float32)
    sin = np.where(live, np.sin(ang), 0.0).astype(np.float32)
    first = (lane % (2 * half)) < half
    s_lo = np.where(first[None, :], -sin, 0.0).astype(np.float32)
    s_hi = np.where(first[None, :], 0.0, sin).astype(np.float32)
    return jnp.asarray(cos), jnp.asarray(s_lo), jnp.asarray(s_hi)


def _make_rope(tag, half):
    def call(x, cos, s_lo, s_hi, transpose, name):
        T, W = x.shape

        def body(x_ref, c_ref, lo_ref, hi_ref, o_ref):
            x = x_ref[...]
            if not transpose:
                o_ref[...] = (x * c_ref[...] + pltpu.roll(x, LANE - half, 1) * lo_ref[...]
                              + pltpu.roll(x, half, 1) * hi_ref[...])
            else:
                o_ref[...] = (x * c_ref[...] + pltpu.roll(x * lo_ref[...], half, 1)
                              + pltpu.roll(x * hi_ref[...], LANE - half, 1))

        blk = pl.BlockSpec((ROW_TILE, LANE), lambda i, c: (i, c))
        tab = pl.BlockSpec((ROW_TILE, LANE), lambda i, c: (i, 0))
        return pl.pallas_call(body, name=name, grid=(T // ROW_TILE, W // LANE), in_specs=[blk, tab, tab, tab],
                              out_specs=blk, out_shape=jax.ShapeDtypeStruct((T, W), F32),
                              compiler_params=_cparams(2))(x, cos, s_lo, s_hi)

    @jax.custom_vjp
    def rope(x, cos, s_lo, s_hi):
        return call(x, cos, s_lo, s_hi, False, tag + "_fwd")

    def fwd(x, cos, s_lo, s_hi):
        return call(x, cos, s_lo, s_hi, False, tag + "_fwd"), (cos, s_lo, s_hi)

    def bwd(res, dy):
        cos, s_lo, s_hi = res
        return call(dy, cos, s_lo, s_hi, True, tag + "_bwd"), None, None, None

    rope.defvjp(fwd, bwd)
    return rope


def _make_conv_gate(tag, n_x):
    def shifted(v, T):
        t = lax.broadcasted_iota(jnp.int32, v.shape, 0)
        prev = jnp.where((t == 0) | (t == n_x), 0.0, pltpu.roll(v, 1, 0))
        nxt = jnp.where((t == n_x - 1) | (t == T - 1), 0.0, pltpu.roll(v, T - 1, 0))
        return prev, nxt

    def call_fwd(a, u, cw, cb, riders):
        T, Fd = a.shape

        def body(a_ref, u_ref, cw_ref, cb_ref, z_ref):
            pre = a_ref[...]
            prev, nxt = shifted(pre, T)
            s = prev * cw_ref[0:1, :] + pre * cw_ref[1:2, :] + nxt * cw_ref[2:3, :] + cb_ref[...]
            z_ref[...] = s * (1.0 / (1.0 + jnp.exp(-s))) * u_ref[...]

        col = pl.BlockSpec((T, LANE), lambda j: (0, j))
        return _pcall(body, name=tag + "_fwd", grid=(Fd // LANE,),
                      in_specs=[col, col, pl.BlockSpec((3, LANE), lambda j: (0, j)),
                                pl.BlockSpec((1, LANE), lambda j: (0, j))],
                      out_specs=[col], out_shape=[jax.ShapeDtypeStruct((T, Fd), F32)], args=[a, u, cw, cb],
                      riders=riders)

    def call_bwd(a, u, cw, cb, dz, riders):
        T, Fd = a.shape

        def body(a_ref, u_ref, cw_ref, cb_ref, dz_ref, da_ref, du_ref, dcw_ref, dcb_ref):
            pre = a_ref[...]
            prev, nxt = shifted(pre, T)
            s = prev * cw_ref[0:1, :] + pre * cw_ref[1:2, :] + nxt * cw_ref[2:3, :] + cb_ref[...]
            sig = 1.0 / (1.0 + jnp.exp(-s))
            dz = dz_ref[...]
            du_ref[...] = dz * s * sig
            ds = dz * u_ref[...] * (sig * (1.0 + s * (1.0 - sig)))
            ds_prev, ds_next = shifted(ds, T)
            da_ref[...] = ds * cw_ref[1:2, :] + ds_next * cw_ref[0:1, :] + ds_prev * cw_ref[2:3, :]
            dcw_ref[0:1, :] = jnp.sum(ds * prev, axis=0, keepdims=True)
            dcw_ref[1:2, :] = jnp.sum(ds * pre, axis=0, keepdims=True)
            dcw_ref[2:3, :] = jnp.sum(ds * nxt, axis=0, keepdims=True)
            dcb_ref[...] = jnp.sum(ds, axis=0, keepdims=True)

        col = pl.BlockSpec((T, LANE), lambda j: (0, j))
        w3 = pl.BlockSpec((3, LANE), lambda j: (0, j))
        w1 = pl.BlockSpec((1, LANE), lambda j: (0, j))
        big = jax.ShapeDtypeStruct((T, Fd), F32)
        return _pcall(body, name=tag + "_bwd", grid=(Fd // LANE,), in_specs=[col, col, w3, w1, col],
                      out_specs=[col, col, w3, w1],
                      out_shape=[big, big, jax.ShapeDtypeStruct((3, Fd), F32), jax.ShapeDtypeStruct((1, Fd), F32)],
                      args=[a, u, cw, cb, dz], riders=riders)

    def run(a, u, cw, cb, gin, tok):
        z, *got = call_fwd(a, u, cw, cb, _riders(gin, False))
        return z, tuple(got), tok

    @jax.custom_vjp
    def conv_gate(a, u, cw, cb, gin, tok):
        return run(a, u, cw, cb, gin, tok)

    def fwd(a, u, cw, cb, gin, tok):
        return run(a, u, cw, cb, gin, tok), (a, u, cw, cb, gin)

    def bwd(res, cts):
        a, u, cw, cb, gin = res
        dz, _, dtok = cts
        da, du, dcw, dcb, *recv = call_bwd(a, u, cw, cb, dz, _riders(dtok, True))
        return da, du, dcw, dcb, _nones(gin), tuple(recv)

    conv_gate.defvjp(fwd, bwd)
    return conv_gate


def _make_loss_head(tag):
    def call(y, tgt):
        T, D = y.shape

        def body(y_ref, t_ref, l_ref, r_ref):
            @pl.when(pl.program_id(0) == 0)
            def _():
                l_ref[...] = jnp.zeros_like(l_ref)

            d = y_ref[...] - t_ref[...]
            r_ref[...] = d * (1.0 / D)
            l_ref[...] += jnp.sum(d * d) * (0.5 / D)

        row = pl.BlockSpec((ROW_TILE, D), lambda i: (i, 0))
        return pl.pallas_call(body, name=tag, grid=(T // ROW_TILE,), in_specs=[row, row],
                              out_specs=[pl.BlockSpec((8, LANE), lambda i: (0, 0)), row],
                              out_shape=[jax.ShapeDtypeStruct((8, LANE), F32), jax.ShapeDtypeStruct((T, D), F32)],
                              compiler_params=_cparams(1))(y, tgt)

    @jax.custom_vjp
    def loss_head(y, tgt):
        return call(y, tgt)[0][0, 0]

    def fwd(y, tgt):
        l, r = call(y, tgt)
        return l[0, 0], r

    def bwd(r, g):
        return r * g, None

    loss_head.defvjp(fwd, bwd)
    return loss_head


class _AttnCfg:
    def __init__(self, H, Hkv, dqk, dv, scale, tq, tk, has_sink=False):
        self.H, self.Hkv, self.dqk, self.dv, self.scale = H, Hkv, dqk, dv, scale
        self.tq, self.tk, self.has_sink = tq, tk, has_sink
        self.G = H // Hkv


class _WinCfg:
    def __init__(self, mode, H, Hkv, scale, nx, n_ctx):
        self.mode, self.H, self.Hkv, self.scale, self.nx, self.n_ctx = mode, H, Hkv, scale, nx, n_ctx
        self.G = H // Hkv
        self.has_sink = mode == "swa"
        self.nwin, self.back = (3, 1) if mode == "swa" else (5, 2)
        self.width = self.nwin * ATT_BLOCK

    def first_block(self, i):
        return jnp.clip(i - self.back, 0, self.nx - self.nwin)

    def slots(self, i, jj):
        d = 2 * (self.first_block(i) + jj - i)
        return jnp.clip(d + 8, 0, 15), jnp.clip(d + 7, 0, 15)

    def local_scores(self, s, i, tab_ref):
        sub = lax.broadcasted_iota(jnp.int32, s.shape, 0)
        lan = lax.broadcasted_iota(jnp.int32, s.shape, 1)
        first = self.first_block(i)
        if self.mode == "swa":
            diff = (i * ATT_BLOCK + sub) - (first * ATT_BLOCK + lan)
            return jnp.where(jnp.abs(diff) <= SWA_WINDOW, s, -jnp.inf)
        rows = 2 * self.nx
        qrow = 2 * i + jnp.where(sub >= GRID_W, 1, 0)
        krow = 2 * first + lan // GRID_W
        r0 = jnp.clip(qrow - NA_WIN_R // 2, 0, rows - NA_WIN_R)
        ok = (krow >= r0) & (krow < r0 + NA_WIN_R)
        tiles = []
        for jj in range(self.nwin):
            top, bot = self.slots(i, jj)
            tiles.append(jnp.concatenate([tab_ref[0, top], tab_ref[0, bot]], axis=0))
        return jnp.where(ok, s + jnp.concatenate(tiles, axis=1), -jnp.inf)


def _win_specs(cfg, Tk):
    G = cfg.G
    qmap = lambda kh, g, i: (i, kh * G + g)
    q_spec = pl.BlockSpec((ATT_BLOCK, HEAD_DIM), qmap)
    kv_spec = pl.BlockSpec((Tk, HEAD_DIM), lambda kh, g, i: (0, kh))
    lse_spec = pl.BlockSpec((1, ATT_BLOCK, LANE), lambda kh, g, i: (kh * G + g, i, 0))
    tab_spec = pl.BlockSpec((1, 16, GRID_W, LANE), lambda kh, g, i: (kh * G + g, 0, 0, 0))
    sink_spec = pl.BlockSpec((1, 8, LANE), lambda kh, g, i: (kh * G + g, 0, 0))
    return q_spec, kv_spec, lse_spec, tab_spec, sink_spec


def _win_scores(cfg, i, q_ref, k_ref, tab_ref):
    dn = (((1,), (1,)), ((), ()))
    n_x = cfg.nx * ATT_BLOCK
    win = pl.ds(pl.multiple_of(cfg.first_block(i) * ATT_BLOCK, ATT_BLOCK), cfg.width)
    ctx = pl.ds(n_x, cfg.n_ctx)
    qb = q_ref[...].astype(_MXU)
    s_loc = lax.dot_general(qb, k_ref[win, :].astype(_MXU), dn, preferred_element_type=F32) * cfg.scale
    s_ctx = lax.dot_general(qb, k_ref[ctx, :].astype(_MXU), dn, preferred_element_type=F32) * cfg.scale
    return cfg.local_scores(s_loc, i, tab_ref), s_ctx, win, ctx


def _win_fwd_call(cfg, name, q, k, v, tab, sink, riders=()):
    Tq, Tk = q.shape[0], k.shape[0]

    def body(*refs):
        q_ref, k_ref, v_ref = refs[:3]
        extra = refs[3] if (cfg.mode == "na" or cfg.has_sink) else None
        o_ref, lse_ref = refs[-2:]
        i = pl.program_id(2)
        s_loc, s_ctx, win, ctx = _win_scores(cfg, i, q_ref, k_ref, extra if cfg.mode == "na" else None)
        m = jnp.maximum(jnp.max(s_loc, axis=-1, keepdims=True), jnp.max(s_ctx, axis=-1, keepdims=True))
        if cfg.has_sink:
            m = jnp.maximum(m, extra[0, 0:1, 0:1])
        p_loc, p_ctx = jnp.exp(s_loc - m), jnp.exp(s_ctx - m)
        l = jnp.sum(p_loc, axis=-1, keepdims=True) + jnp.sum(p_ctx, axis=-1, keepdims=True)
        if cfg.has_sink:
            l = l + jnp.exp(extra[0, 0:1, 0:1] - m)
        acc = (jnp.dot(p_loc.astype(_MXU), v_ref[win, :].astype(_MXU), preferred_element_type=F32)
               + jnp.dot(p_ctx.astype(_MXU), v_ref[ctx, :].astype(_MXU), preferred_element_type=F32))
        o_ref[...] = acc / l
        lse_ref[0] = jnp.broadcast_to(m + jnp.log(l), (ATT_BLOCK, LANE))

    q_spec, kv_spec, lse_spec, tab_spec, sink_spec = _win_specs(cfg, Tk)
    in_specs, args = [q_spec, kv_spec, kv_spec], [q, k, v]
    if cfg.mode == "na":
        in_specs.append(tab_spec)
        args.append(tab)
    if cfg.has_sink:
        in_specs.append(sink_spec)
        args.append(sink)
    return _pcall(
        body, name=name, grid=(cfg.Hkv, cfg.G, Tq // ATT_BLOCK), in_specs=in_specs, out_specs=[q_spec, lse_spec],
        out_shape=[jax.ShapeDtypeStruct((Tq, cfg.H * HEAD_DIM), F32), jax.ShapeDtypeStruct((cfg.H, Tq, LANE), F32)],
        args=args, riders=riders)


def _win_bwd_call(cfg, name, q, k, v, tab, sink, o, lse, do, riders=()):
    Tq, Tk = q.shape[0], k.shape[0]
    dn_nt = (((1,), (1,)), ((), ()))
    dn_tn = (((0,), (0,)), ((), ()))

    def body(*refs):
        q_ref, k_ref, v_ref = refs[:3]
        has_extra = cfg.mode == "na" or cfg.has_sink
        extra = refs[3] if has_extra else None
        n = 4 if has_extra else 3
        o_ref, lse_ref, do_ref, dq_ref, dk_ref, dv_ref = refs[n:n + 6]
        dextra = refs[n + 6] if has_extra else None
        g, i = pl.program_id(1), pl.program_id(2)

        @pl.when((g == 0) & (i == 0))
        def _():
            dk_ref[...] = jnp.zeros_like(dk_ref)
            dv_ref[...] = jnp.zeros_like(dv_ref)

        if has_extra:
            @pl.when(i == 0)
            def _():
                dextra[...] = jnp.zeros_like(dextra)

        s_loc, s_ctx, win, ctx = _win_scores(cfg, i, q_ref, k_ref, extra if cfg.mode == "na" else None)
        lse = lse_ref[0][:, 0:1]
        p_loc, p_ctx = jnp.exp(s_loc - lse), jnp.exp(s_ctx - lse)
        do_f = do_ref[...]
        do_b = do_f.astype(_MXU)
        delta = jnp.sum(do_f * o_ref[...], axis=-1, keepdims=True)
        dp_loc = lax.dot_general(do_b, v_ref[win, :].astype(_MXU), dn_nt, preferred_element_type=F32)
        dp_ctx = lax.dot_general(do_b, v_ref[ctx, :].astype(_MXU), dn_nt, preferred_element_type=F32)
        ds_loc, ds_ctx = p_loc * (dp_loc - delta), p_ctx * (dp_ctx - delta)
        if cfg.mode == "na":
            for jj in range(cfg.nwin):
                top, bot = cfg.slots(i, jj)
                tile = ds_loc[:, jj * ATT_BLOCK:(jj + 1) * ATT_BLOCK]
                dextra[0, top] += tile[0:GRID_W, :]
                dextra[0, bot] += tile[GRID_W:, :]
        if cfg.has_sink:
            p_sink = jnp.exp(extra[0, 0:1, 0:1] - lse)
            lane0 = ((lax.broadcasted_iota(jnp.int32, (8, LANE), 0) == 0)
                     & (lax.broadcasted_iota(jnp.int32, (8, LANE), 1) == 0))
            dextra[0] += jnp.where(lane0, -jnp.sum(p_sink * delta), 0.0)
        dsb_loc, dsb_ctx = (ds_loc * cfg.scale).astype(_MXU), (ds_ctx * cfg.scale).astype(_MXU)
        qb = q_ref[...].astype(_MXU)
        dq_ref[...] = (jnp.dot(dsb_loc, k_ref[win, :].astype(_MXU), preferred_element_type=F32)
                       + jnp.dot(dsb_ctx, k_ref[ctx, :].astype(_MXU), preferred_element_type=F32))
        dk_ref[win, :] += lax.dot_general(dsb_loc, qb, dn_tn, preferred_element_type=F32)
        dk_ref[ctx, :] += lax.dot_general(dsb_ctx, qb, dn_tn, preferred_element_type=F32)
        dv_ref[win, :] += lax.dot_general(p_loc.astype(_MXU), do_b, dn_tn, preferred_element_type=F32)
        dv_ref[ctx, :] += lax.dot_general(p_ctx.astype(_MXU), do_b, dn_tn, preferred_element_type=F32)

    q_spec, kv_spec, lse_spec, tab_spec, sink_spec = _win_specs(cfg, Tk)
    in_specs, args = [q_spec, kv_spec, kv_spec], [q, k, v]
    out_specs = [q_spec, kv_spec, kv_spec]
    out_shape = [jax.ShapeDtypeStruct(q.shape, F32), jax.ShapeDtypeStruct(k.shape, F32),
                 jax.ShapeDtypeStruct(v.shape, F32)]
    if cfg.mode == "na":
        in_specs.append(tab_spec)
        args.append(tab)
        out_specs.append(tab_spec)
        out_shape.append(jax.ShapeDtypeStruct(tab.shape, F32))
    if cfg.has_sink:
        in_specs.append(sink_spec)
        args.append(sink)
        out_specs.append(sink_spec)
        out_shape.append(jax.ShapeDtypeStruct(sink.shape, F32))
    in_specs += [q_spec, lse_spec, q_spec]
    args += [o, lse, do]
    return _pcall(body, name=name, grid=(cfg.Hkv, cfg.G, Tq // ATT_BLOCK), in_specs=in_specs, out_specs=out_specs,
                  out_shape=out_shape, args=args, riders=riders)


def _attn_specs(cfg, Tk):
    G, tq, tk = cfg.G, cfg.tq, cfg.tk
    qmap = lambda kh, g, i, j: (i, kh * G + g)
    q_spec = pl.BlockSpec((tq, cfg.dqk), qmap)
    k_spec = pl.BlockSpec((tk, cfg.dqk), lambda kh, g, i, j: (j, kh))
    v_spec = pl.BlockSpec((tk, cfg.dv), lambda kh, g, i, j: (j, kh))
    o_spec = pl.BlockSpec((tq, cfg.dv), qmap)
    lse_spec = pl.BlockSpec((1, tq, LANE), lambda kh, g, i, j: (kh * G + g, i, 0))
    sink_spec = pl.BlockSpec((1, 8, LANE), lambda kh, g, i, j: (kh * G + g, 0, 0))
    return q_spec, k_spec, v_spec, o_spec, lse_spec, sink_spec


def _attn_fwd_call(cfg, name, q, k, v, sink, riders=()):
    Tq, Tk = q.shape[0], k.shape[0]
    nq, nkv = Tq // cfg.tq, Tk // cfg.tk
    dn = (((1,), (1,)), ((), ()))

    def body(*refs):
        q_ref, k_ref, v_ref = refs[:3]
        sink_ref = refs[3] if cfg.has_sink else None
        o_ref, lse_ref, m_scr, l_scr, acc_scr = refs[-5:]
        j = pl.program_id(3)

        @pl.when(j == 0)
        def _():
            m_scr[...] = jnp.full_like(m_scr, NEG)
            l_scr[...] = jnp.zeros_like(l_scr)
            acc_scr[...] = jnp.zeros_like(acc_scr)

        s = lax.dot_general(q_ref[...].astype(_MXU), k_ref[...].astype(_MXU), dn,
                            preferred_element_type=F32) * cfg.scale
        m_old = m_scr[...]
        m_new = jnp.maximum(m_old, jnp.max(s, axis=-1, keepdims=True))
        alpha = jnp.exp(m_old - m_new)
        p = jnp.exp(s - m_new)
        l_scr[...] = alpha * l_scr[...] + jnp.sum(p, axis=-1, keepdims=True)
        acc_scr[...] = alpha * acc_scr[...] + jnp.dot(p.astype(_MXU), v_ref[...].astype(_MXU),
                                                      preferred_element_type=F32)
        m_scr[...] = m_new

        @pl.when(j == nkv - 1)
        def _():
            m, l = m_scr[...], l_scr[...]
            if cfg.has_sink:
                sk = sink_ref[0, 0:1, 0:1]
                m_new = jnp.maximum(m, sk)
                alpha = jnp.exp(m - m_new)
                l = alpha * l + jnp.exp(sk - m_new)
                acc = acc_scr[...] * alpha
                m = m_new
            else:
                acc = acc_scr[...]
            o_ref[...] = acc / l
            lse_ref[0] = jnp.broadcast_to(m + jnp.log(l), (cfg.tq, LANE))

    q_spec, k_spec, v_spec, o_spec, lse_spec, sink_spec = _attn_specs(cfg, Tk)
    in_specs, args = [q_spec, k_spec, v_spec], [q, k, v]
    if cfg.has_sink:
        in_specs.append(sink_spec)
        args.append(sink)
    return _pcall(
        body, name=name, grid=(cfg.Hkv, cfg.G, nq, nkv), in_specs=in_specs, out_specs=[o_spec, lse_spec],
        out_shape=[jax.ShapeDtypeStruct((Tq, cfg.H * cfg.dv), F32), jax.ShapeDtypeStruct((cfg.H, Tq, LANE), F32)],
        scratch_shapes=[pltpu.VMEM((cfg.tq, 1), F32), pltpu.VMEM((cfg.tq, 1), F32), pltpu.VMEM((cfg.tq, cfg.dv), F32)],
        args=args, riders=riders)


def _attn_bwd_call(cfg, name, q, k, v, sink, o, lse, do, riders=()):
    Tq, Tk = q.shape[0], k.shape[0]
    nq, nkv = Tq // cfg.tq, Tk // cfg.tk
    tq, tk = cfg.tq, cfg.tk
    dn_nt = (((1,), (1,)), ((), ()))
    dn_tn = (((0,), (0,)), ((), ()))

    def body(*refs):
        q_ref, k_ref, v_ref = refs[:3]
        n = 3
        sink_ref = dsink_ref = None
        if cfg.has_sink:
            sink_ref, n = refs[n], n + 1
        o_ref, lse_ref, do_ref, dq_ref, dk_ref, dv_ref = refs[n:n + 6]
        n += 6
        if cfg.has_sink:
            dsink_ref, n = refs[n], n + 1
        dq_scr, delta_scr = refs[n:]
        g, i, j = pl.program_id(1), pl.program_id(2), pl.program_id(3)

        @pl.when((g == 0) & (i == 0) & (j == 0))
        def _():
            dk_ref[...] = jnp.zeros_like(dk_ref)
            dv_ref[...] = jnp.zeros_like(dv_ref)

        if cfg.has_sink:
            @pl.when((i == 0) & (j == 0))
            def _():
                dsink_ref[...] = jnp.zeros_like(dsink_ref)

        @pl.when(j == 0)
        def _():
            dq_scr[...] = jnp.zeros_like(dq_scr)
            delta = jnp.sum(do_ref[...] * o_ref[...], axis=-1, keepdims=True)
            delta_scr[...] = delta
            if cfg.has_sink:
                p_sink = jnp.exp(sink_ref[0, 0:1, 0:1] - lse_ref[0][:, 0:1])
                lane0 = ((lax.broadcasted_iota(jnp.int32, (8, LANE), 0) == 0)
                         & (lax.broadcasted_iota(jnp.int32, (8, LANE), 1) == 0))
                dsink_ref[0] += jnp.where(lane0, -jnp.sum(p_sink * delta), 0.0)

        s = lax.dot_general(q_ref[...].astype(_MXU), k_ref[...].astype(_MXU), dn_nt,
                            preferred_element_type=F32) * cfg.scale
        p = jnp.exp(s - lse_ref[0][:, 0:1])
        do_b = do_ref[...].astype(_MXU)
        dp = lax.dot_general(do_b, v_ref[...].astype(_MXU), dn_nt, preferred_element_type=F32)
        dsb = (p * (dp - delta_scr[...]) * cfg.scale).astype(_MXU)
        rows = pl.ds(pl.multiple_of(j * tk, tk), tk)
        dq_scr[...] += jnp.dot(dsb, k_ref[...].astype(_MXU), preferred_element_type=F32)
        dk_ref[rows, :] += lax.dot_general(dsb, q_ref[...].astype(_MXU), dn_tn, preferred_element_type=F32)
        dv_ref[rows, :] += lax.dot_general(p.astype(_MXU), do_b, dn_tn, preferred_element_type=F32)

        @pl.when(j == nkv - 1)
        def _():
            dq_ref[...] = dq_scr[...]

    q_spec, k_spec, v_spec, o_spec, lse_spec, sink_spec = _attn_specs(cfg, Tk)
    in_specs, args = [q_spec, k_spec, v_spec], [q, k, v]
    if cfg.has_sink:
        in_specs.append(sink_spec)
        args.append(sink)
    in_specs += [o_spec, lse_spec, o_spec]
    args += [o, lse, do]
    out_specs = [q_spec, pl.BlockSpec((Tk, cfg.dqk), lambda kh, g, i, j: (0, kh)),
                 pl.BlockSpec((Tk, cfg.dv), lambda kh, g, i, j: (0, kh))]
    out_shape = [jax.ShapeDtypeStruct(q.shape, F32), jax.ShapeDtypeStruct(k.shape, F32),
                 jax.ShapeDtypeStruct(v.shape, F32)]
    if cfg.has_sink:
        out_specs.append(sink_spec)
        out_shape.append(jax.ShapeDtypeStruct(sink.shape, F32))
    return _pcall(body, name=name, grid=(cfg.Hkv, cfg.G, nq, nkv), in_specs=in_specs, out_specs=out_specs,
                  out_shape=out_shape, scratch_shapes=[pltpu.VMEM((tq, cfg.dqk), F32), pltpu.VMEM((tq, 1), F32)],
                  args=args, riders=riders)


def _make_attention(tag, cfg):
    windowed = isinstance(cfg, _WinCfg)

    def run_fwd(q, k, v, tab, sink, gin):
        if windowed:
            return _win_fwd_call(cfg, tag + "_fwd", q, k, v, tab, sink, _riders(gin, False))
        return _attn_fwd_call(cfg, tag + "_fwd", q, k, v, sink, _riders(gin, False))

    @jax.custom_vjp
    def attn(q, k, v, tab, sink, gin, tok):
        o, _, *got = run_fwd(q, k, v, tab, sink, gin)
        return o, tuple(got), tok

    def fwd(q, k, v, tab, sink, gin, tok):
        o, lse, *got = run_fwd(q, k, v, tab, sink, gin)
        return (o, tuple(got), tok), (q, k, v, tab, sink, gin, o, lse)

    def bwd(res, cts):
        q, k, v, tab, sink, gin, o, lse = res
        do, _, dtok = cts
        if windowed:
            outs = _win_bwd_call(cfg, tag + "_bwd", q, k, v, tab, sink, o, lse, do, _riders(dtok, True))
        else:
            outs = _attn_bwd_call(cfg, tag + "_bwd", q, k, v, sink, o, lse, do, _riders(dtok, True))
        dq, dk, dv = outs[:3]
        rest = outs[3:]
        dtab = rest.pop(0) if tab is not None else None
        dsink = rest.pop(0) if sink is not None else None
        return dq, dk, dv, dtab, dsink, _nones(gin), tuple(rest)

    attn.defvjp(fwd, bwd)
    return attn


def _pick_block(n, prefs):
    for p in prefs:
        if n % p == 0:
            return p
    return n


def _na_table(rpb):
    cq = np.arange(GRID_W)
    dcol = np.clip(cq[None, :] - cq[:, None] + NA_WIN_C - 1, 0, 2 * NA_WIN_C - 2)
    onehot = (dcol[:, :, None] == np.arange(2 * NA_WIN_C - 1)[None, None, :]).astype(np.float32)
    c0 = np.clip(cq - NA_WIN_C // 2, 0, GRID_W - NA_WIN_C)
    col_in = (cq[None, :] >= c0[:, None]) & (cq[None, :] < c0[:, None] + NA_WIN_C)
    tz = jnp.einsum("hrj,qkj->hrqk", rpb, jnp.asarray(onehot), precision=lax.Precision.HIGHEST)
    tz = jnp.where(jnp.asarray(col_in)[None, None], tz, NEG)
    zero = jnp.zeros_like(tz[:, :1])
    tzp = jnp.concatenate([zero, tz, zero], axis=1)
    return jnp.concatenate([tzp[:, 0:16], tzp[:, 1:17]], axis=-1)


def _sink_block(sink):
    return jnp.broadcast_to(sink[:, None, None], (sink.shape[0], 8, LANE))


def _seg2(vx, vc, nseg):
    rows = [vx, vc][:nseg]
    return jnp.stack(rows)[:, None, :]


def _full_weight(name, g):
    if name in _BIG_ROW:
        return g.reshape(N_DEV * g.shape[1], g.shape[2])
    _, K, n = g.shape
    full = jnp.transpose(g, (1, 0, 2)).reshape(K, N_DEV * n)
    if name == "w_in":
        full = jnp.concatenate([full[:, :KPE_END], jnp.zeros((K, 64), full.dtype), full[:, KPE_END:]], axis=-1)
    return full


def _grad_parts(name):
    def to_parts(dw):
        if name in _BIG_ROW:
            return dw.reshape(N_DEV, dw.shape[0] // N_DEV, dw.shape[1]).astype(BF16)
        if name == "w_in":
            dw = jnp.concatenate([dw[:, :KPE_END], dw[:, KPE_END + 64:]], axis=-1)
        K, N = dw.shape
        return jnp.transpose(dw.reshape(K, N_DEV, N // N_DEV), (1, 0, 2)).astype(BF16)

    return to_parts


_IN_GROUP = ("w_in", "mla_w_uq", "mla_w_ukv")


def _local_loss(x, ctx, tgt, modx, modc, p, full, shards, toks):
    S, D = x.shape
    C = ctx.shape[0]
    depth = len(full)
    ride = shards is not None
    have = [dict(f) for f in full]
    sinks = {(l, n): toks[l][n] for l in range(depth) for n in toks[l]}
    alpha = (2 * depth) ** 0.25
    T0 = S + C
    nx = S // ATT_BLOCK

    def gin(l, names):
        return tuple(shards[l][n] for n in names) if ride else ()

    def tok(l, names):
        return tuple(sinks[(l, n)] for n in names) if ride else ()

    def landed(result, lw, names_w, lt, names_t):
        out, got, tk = result
        for n, g in zip(names_w, got):
            have[lw][n] = _full_weight(n, g)
        for n, s in zip(names_t, tk):
            sinks[(lt, n)] = s
        return out

    sc = HEAD_DIM ** -0.5
    sc_mla = (MLA_NOPE + MLA_ROPE) ** -0.5
    tq_full = _pick_block(S, (256, 128))
    tk_all = _pick_block(T0, (2176, 640, 512, 256, 128))
    tctx = _pick_block(C, (256, 128))
    rope128 = _rope_tables(S, T0, 128, 32)
    rope64 = _rope_tables(S, T0, 64, 16)

    xs = jnp.concatenate([x, ctx], axis=0)
    for l in range(depth):
        last = l == depth - 1
        t = "l%d_" % l
        nxt = () if last else _IN_GROUP
        lin = lambda a, name, g=(), tk=(): _make_linear(t + name, _grad_parts(name))(
            a, have[l][name], sinks[(l, name)], g, tk)
        mx, mc = modx[l], modc[l]

        h = _make_modulate(t + "mod1", S)(xs, _seg2(mx[0], mc[0], 2), _seg2(mx[1], mc[1], 2))
        pj = lin(h, "w_in")[0]
        qa, ka, va = pj[:, 0:512], pj[:, 512:1024], pj[:, 1024:1536]
        tab = _na_table(p["na_rpb"][l])
        cfg_a = _WinCfg("na", NA_HEADS, NA_HEADS, sc, nx, C)
        oa = landed(_make_attention(t + "attn_a", cfg_a)(qa[:S], ka, va, tab, None, gin(l, ("w_out",)),
                                                         tok(l, ("w_out",))), l, ("w_out",), l, ("w_out",))
        rope_b = _make_rope(t + "rope_b", 32)
        qb = rope_b(pj[:, 1536:2048], *rope128)
        kb = _make_rope(t + "rope_bk", 32)(pj[:, 2048:2304], *rope128)
        vb = pj[:, 2304:2560]
        snk = _sink_block(p["swa_sink"][l])
        cfg_b = _WinCfg("swa", SWA_HEADS, SWA_KV_HEADS, sc, nx, C)
        ob = _make_attention(t + "attn_b", cfg_b)(qb[:S], kb, vb, None, snk, (), ())[0]
        cq = _make_rmsnorm(t + "rms_cq")(pj[:, 2560:2944], p["mla_q_norm"][l][None, :])
        ckv = _make_rmsnorm(t + "rms_ckv")(pj[:, 2944:3072], p["mla_kv_norm"][l][None, :])
        qh = lin(cq, "mla_w_uq")[0].reshape(T0, MLA_HEADS, MLA_NOPE + MLA_ROPE)
        kvh = lin(ckv, "mla_w_ukv")[0].reshape(T0, MLA_HEADS, MLA_NOPE + MLA_V)
        qpe = jnp.pad(qh[:, :, MLA_NOPE:], ((0, 0), (0, 0), (0, LANE - MLA_ROPE))).reshape(T0, MLA_HEADS * LANE)
        qpe = _make_rope(t + "rope_cq", 16)(qpe, *rope64).reshape(T0, MLA_HEADS, LANE)
        kpe = _make_rope(t + "rope_ck", 16)(pj[:, 3072:3200], *rope64)
        qc = jnp.concatenate([qh[:, :, :MLA_NOPE], qpe], axis=-1).reshape(T0, MLA_HEADS * 2 * LANE)
        kc = jnp.concatenate([kvh[:, :, :MLA_NOPE], jnp.broadcast_to(kpe[:, None, :], (T0, MLA_HEADS, LANE))],
                             axis=-1).reshape(T0, MLA_HEADS * 2 * LANE)
        vc = kvh[:, :, MLA_NOPE:].reshape(T0, MLA_HEADS * MLA_V)
        cfg_c = _AttnCfg(MLA_HEADS, MLA_HEADS, 2 * LANE, MLA_V, sc_mla, tq_full, tk_all)
        oc = landed(_make_attention(t + "attn_c", cfg_c)(qc[:S], kc, vc, None, None, gin(l, ("ffn_w_gate",)),
                                                         tok(l, ("ffn_w_gate",))), l, ("ffn_w_gate",), l, ("ffn_w_gate",))
        qd = _make_rmsnorm(t + "rms_dq")(pj[:, 3200:3712], p["gqa_q_norm"][l][None, :])
        kd = _make_rmsnorm(t + "rms_dk")(pj[:, 3712:3968], p["gqa_k_norm"][l][None, :])
        qd = _make_rope(t + "rope_dq", 32)(qd, *rope128)
        kd = _make_rope(t + "rope_dk", 32)(kd, *rope128)
        vd = pj[:, 3968:4224]
        cfg_d = _AttnCfg(GQA_HEADS, GQA_KV_HEADS, 128, 128, sc, tq_full, tk_all)
        od = landed(_make_attention(t + "attn_d", cfg_d)(qd[:S], kd, vd, None, None, gin(l, ("ffn_w_up",)),
                                                         tok(l, ("ffn_w_up",))), l, ("ffn_w_up",), l, ("ffn_w_up",))
        mix = jnp.concatenate([oa, ob, oc, od], axis=1)
        if not last:
            full_c = lambda H, Hkv, dqk, s, sink: _AttnCfg(H, Hkv, dqk, 128, s, tctx, tctx, has_sink=sink)
            ctx_att = lambda name, cfg, q_, k_, v_, s_: _make_attention(t + name, cfg)(
                q_[S:], k_[S:], v_[S:], None, s_, (), ())[0]
            oa_c = ctx_att("ctx_a", full_c(4, 4, 128, sc, False), qa, ka, va, None)
            ob_c = ctx_att("ctx_b", full_c(4, 2, 128, sc, True), qb, kb, vb, snk)
            oc_c = ctx_att("ctx_c", full_c(4, 4, 256, sc_mla, False), qc, kc, vc, None)
            od_c = ctx_att("ctx_d", full_c(4, 2, 128, sc, False), qd, kd, vd, None)
            mix = jnp.concatenate([mix, jnp.concatenate([oa_c, ob_c, oc_c, od_c], axis=1)], axis=0)
            res, nseg = xs, 2
        else:
            res, nseg = xs[:S], 1
        y = lin(mix, "w_out")[0]
        x1 = _make_resid_ln(t + "ln1", S, alpha)(res, y, _seg2(mx[2], mc[2], nseg), p["ln1_g"][l][None, :],
                                                 p["ln1_b"][l][None, :])
        h2 = _make_modulate(t + "mod2", S)(x1, _seg2(mx[3], mc[3], nseg), _seg2(mx[4], mc[4], nseg))
        gp = landed(lin(h2, "ffn_w_gate", gin(l, ("ffn_w_down",))), l, ("ffn_w_down",), l, ())
        up = lin(h2, "ffn_w_up")[0]
        z = landed(_make_conv_gate(t + "conv", S)(gp, up, p["ffn_conv_w"][l], p["ffn_conv_b"][l][None, :],
                                                  gin(l + 1, nxt), tok(l, ("ffn_w_down",))),
                   l + 1, nxt, l, ("ffn_w_down",))
        f = landed(lin(z, "ffn_w_down", (), tok(l + 1, nxt)), l + 1, (), l + 1, nxt)
        xs = _make_resid_ln(t + "ln2", S, alpha)(x1, f, _seg2(mx[5], mc[5], nseg), p["ln2_g"][l][None, :],
                                                 p["ln2_b"][l][None, :])
    return _make_loss_head("loss_head")(xs, tgt)


def _adamw(w, parts, m, v, name):
    R, C = w.shape
    P = parts.shape[0]
    c1 = 1.0 - ADAM_B1 ** ADAM_STEP
    c2 = 1.0 - ADAM_B2 ** ADAM_STEP
    per_row = C * (4 * 14 + 2 * P * parts.dtype.itemsize)
    cands = _divisors(R, 8, R)
    fitting = [d for d in cands if d * per_row <= 24 * 1024 * 1024]
    tr = max(fitting) if fitting else min(cands)

    def body(w_ref, p_ref, m_ref, v_ref, g_ref, d_ref, nm_ref, nv_ref):
        g = p_ref[0].astype(F32)
        for k in range(1, P):
            g = g + p_ref[k].astype(F32)
        mm = ADAM_B1 * m_ref[...] + (1.0 - ADAM_B1) * g
        vv = ADAM_B2 * v_ref[...] + (1.0 - ADAM_B2) * (g * g)
        g_ref[...] = g
        nm_ref[...] = mm
        nv_ref[...] = vv
        d_ref[...] = -ADAM_LR * ((mm / c1) / (jnp.sqrt(vv / c2) + ADAM_EPS) + ADAM_WD * w_ref[...])

    blk = pl.BlockSpec((tr, C), lambda i: (i, 0))
    pblk = pl.BlockSpec((P, tr, C), lambda i: (0, i, 0))
    sh = jax.ShapeDtypeStruct((R, C), F32)
    return pl.pallas_call(body, name=name, grid=(R // tr,), in_specs=[blk, pblk, blk, blk],
                          out_specs=[blk, blk, blk, blk], out_shape=[sh, sh, sh, sh],
                          compiler_params=_cparams(1))(w, parts, m, v)


def _adamw_nd(w, parts, m, v, name):
    shape = w.shape
    C = shape[-1]
    R = int(np.prod(shape[:-1])) if len(shape) > 1 else 1
    outs = _adamw(w.reshape(R, C), parts.reshape(parts.shape[0], R, C), m.reshape(R, C), v.reshape(R, C), name)
    return [o.reshape(shape) for o in outs]


def _silu(v):
    return v * (1.0 / (1.0 + jnp.exp(-v)))


def _ada_rows(c_all, c_ctx):
    return jnp.concatenate([c_all, jnp.broadcast_to(c_ctx[None, :], (N_DEV, c_ctx.shape[0]))], axis=0)


def _silu_rows(rows, name):
    def body(r_ref, o_ref):
        o_ref[...] = _silu(r_ref[...])

    return pl.pallas_call(body, name=name, out_shape=jax.ShapeDtypeStruct(rows.shape, F32))(rows)


_BIG_COL = ("w_in", "mla_w_uq", "mla_w_ukv", "ffn_w_gate", "ffn_w_up")
_BIG_ROW = ("w_out", "ffn_w_down")
_SMALL = ("c_ctx", "na_rpb", "swa_sink", "mla_q_norm", "mla_kv_norm", "gqa_q_norm", "gqa_k_norm",
          "ln1_g", "ln1_b", "ffn_conv_b", "ln2_g", "ln2_b")
_NAMES = ("c_ctx", "w_ada", "b_ada", "w_in", "na_rpb", "swa_sink", "mla_q_norm", "mla_kv_norm", "mla_w_uq",
          "mla_w_ukv", "gqa_q_norm", "gqa_k_norm", "w_out", "ln1_g", "ln1_b", "ffn_w_gate", "ffn_w_up",
          "ffn_conv_w", "ffn_conv_b", "ffn_w_down", "ln2_g", "ln2_b")


def _full_from_cols(g, pad_in=False):
    _, L, K, n = g.shape
    full = jnp.transpose(g, (1, 2, 0, 3)).reshape(L, K, N_DEV * n)
    if pad_in:
        full = jnp.concatenate([full[:, :, :KPE_END], jnp.zeros((L, K, 64), full.dtype), full[:, :, KPE_END:]], axis=-1)
    return full


def _pack_small(tree, extra=None):
    flat = [tree[n].reshape(-1) for n in _SMALL]
    flat.append(jnp.zeros((1,), F32) if extra is None else extra.reshape(1))
    v = jnp.concatenate(flat)
    n = v.shape[0]
    padded = -(-n // 1024) * 1024
    return jnp.pad(v, (0, padded - n)).reshape(padded // LANE, LANE)


def _unpack_small(mat, like):
    v = mat.reshape(-1)
    out, o = {}, 0
    for n in _SMALL:
        k = int(np.prod(like[n].shape))
        out[n] = v[o:o + k].reshape(like[n].shape)
        o += k
    return out, v[o]


def kernel(x, c, ctx, c_ctx, w_ada, b_ada, w_in, na_rpb, swa_sink, mla_q_norm, mla_kv_norm, mla_w_uq, mla_w_ukv, gqa_q_norm, gqa_k_norm, w_out, ln1_g, ln1_b, ffn_w_gate, ffn_w_up, ffn_conv_w, ffn_conv_b, ffn_w_down, ln2_g, ln2_b, loss_target, m_c_ctx, m_w_ada, m_b_ada, m_w_in, m_na_rpb, m_swa_sink, m_mla_q_norm, m_mla_kv_norm, m_mla_w_uq, m_mla_w_ukv, m_gqa_q_norm, m_gqa_k_norm, m_w_out, m_ln1_g, m_ln1_b, m_ffn_w_gate, m_ffn_w_up, m_ffn_conv_w, m_ffn_conv_b, m_ffn_w_down, m_ln2_g, m_ln2_b, v_c_ctx, v_w_ada, v_b_ada, v_w_in, v_na_rpb, v_swa_sink, v_mla_q_norm, v_mla_kv_norm, v_mla_w_uq, v_mla_w_ukv, v_gqa_q_norm, v_gqa_k_norm, v_w_out, v_ln1_g, v_ln1_b, v_ffn_w_gate, v_ffn_w_up, v_ffn_conv_w, v_ffn_conv_b, v_ffn_w_down, v_ln2_g, v_ln2_b):
    W = dict(c_ctx=c_ctx, w_ada=w_ada, b_ada=b_ada, w_in=w_in, na_rpb=na_rpb, swa_sink=swa_sink,
             mla_q_norm=mla_q_norm, mla_kv_norm=mla_kv_norm, mla_w_uq=mla_w_uq, mla_w_ukv=mla_w_ukv,
             gqa_q_norm=gqa_q_norm, gqa_k_norm=gqa_k_norm, w_out=w_out, ln1_g=ln1_g, ln1_b=ln1_b,
             ffn_w_gate=ffn_w_gate, ffn_w_up=ffn_w_up, ffn_conv_w=ffn_conv_w, ffn_conv_b=ffn_conv_b,
             ffn_w_down=ffn_w_down, ln2_g=ln2_g, ln2_b=ln2_b)
    M = dict(c_ctx=m_c_ctx, w_ada=m_w_ada, b_ada=m_b_ada, w_in=m_w_in, na_rpb=m_na_rpb, swa_sink=m_swa_sink,
             mla_q_norm=m_mla_q_norm, mla_kv_norm=m_mla_kv_norm, mla_w_uq=m_mla_w_uq, mla_w_ukv=m_mla_w_ukv,
             gqa_q_norm=m_gqa_q_norm, gqa_k_norm=m_gqa_k_norm, w_out=m_w_out, ln1_g=m_ln1_g, ln1_b=m_ln1_b,
             ffn_w_gate=m_ffn_w_gate, ffn_w_up=m_ffn_w_up, ffn_conv_w=m_ffn_conv_w, ffn_conv_b=m_ffn_conv_b,
             ffn_w_down=m_ffn_w_down, ln2_g=m_ln2_g, ln2_b=m_ln2_b)
    V = dict(c_ctx=v_c_ctx, w_ada=v_w_ada, b_ada=v_b_ada, w_in=v_w_in, na_rpb=v_na_rpb, swa_sink=v_swa_sink,
             mla_q_norm=v_mla_q_norm, mla_kv_norm=v_mla_kv_norm, mla_w_uq=v_mla_w_uq, mla_w_ukv=v_mla_w_ukv,
             gqa_q_norm=v_gqa_q_norm, gqa_k_norm=v_gqa_k_norm, w_out=v_w_out, ln1_g=v_ln1_g, ln1_b=v_ln1_b,
             ffn_w_gate=v_ffn_w_gate, ffn_w_up=v_ffn_w_up, ffn_conv_w=v_ffn_conv_w, ffn_conv_b=v_ffn_conv_b,
             ffn_w_down=v_ffn_w_down, ln2_g=v_ln2_g, ln2_b=v_ln2_b)
    L, D, n_ada = w_ada.shape
    me = 4 * lax.axis_index("x") + 2 * lax.axis_index("y") + lax.axis_index("c")
    xs, ctxs, tgt = x[0], ctx[0], loss_target[0]

    c_all = _exchange(c, False, "gather_c").reshape(N_DEV, D)
    a_rows = _silu_rows(_ada_rows(c_all, c_ctx), "ada_silu")
    b_mine = lax.dynamic_slice(b_ada, (0, me * n_ada), (L, n_ada))
    mod_mine = jnp.stack([_matmul(a_rows, w_ada[l], "nn", "ada_fwd_l%d" % l) + b_mine[l][None, :] for l in range(L)])
    mod_all = _exchange(mod_mine, False, "gather_mod")
    mod_all = jnp.transpose(mod_all, (1, 2, 0, 3)).reshape(L, 2 * N_DEV, N_DEV * n_ada)
    modx = lax.dynamic_slice(mod_all, (0, me, 0), (L, 1, 6 * D)).reshape(L, 6, D)
    modc = mod_all[:, N_DEV].reshape(L, 6, D)

    big = _BIG_COL + _BIG_ROW
    shards = [{n: W[n][l].astype(BF16) for n in big} for l in range(L)]
    full = [{n: _full_weight(n, _exchange(shards[0][n], False, "gather_l0_" + n)) for n in _IN_GROUP}]
    full += [{} for _ in range(1, L)]
    conv_w_full = _full_from_cols(_exchange(ffn_conv_w, False, "gather_conv_w"))

    toks = [{n: jnp.zeros((N_DEV,) + shards[l][n].shape, BF16) for n in big} for l in range(L)]
    small = {n: W[n] for n in _SMALL if n != "c_ctx"}
    small["ffn_conv_w"] = conv_w_full

    def loss_fn(xv, mxv, mcv, sm, tk):
        return _local_loss(xv, ctxs, tgt, mxv, mcv, sm, full, shards, tk)

    loss_local, (grad_x, dmodx, dmodc, dsmall, dtoks) = jax.value_and_grad(loss_fn, argnums=(0, 1, 2, 3, 4))(
        xs, modx, modc, small, toks)

    dmods = _exchange(jnp.stack([dmodx.reshape(L, 6 * D), dmodc.reshape(L, 6 * D)]), False, "gather_dmod")
    dm_rows = jnp.concatenate([dmods[:, 0], dmods[:, 1]], axis=0)
    dm_rows = jnp.transpose(dm_rows, (1, 0, 2))
    dm_mine = lax.dynamic_slice(dm_rows, (0, 0, me * n_ada), (L, 2 * N_DEV, n_ada))
    g_w_ada = jnp.stack([_matmul(a_rows, dm_mine[l], "tn", "ada_dw_l%d" % l) for l in range(L)])
    d_rows = sum(_matmul(dm_mine[l], w_ada[l], "nt", "ada_da_l%d" % l) for l in range(L))
    sig = 1.0 / (1.0 + jnp.exp(-c_ctx))
    d_c_ctx_part = jnp.sum(d_rows[N_DEV:], axis=0) * (sig * (1.0 + c_ctx * (1.0 - sig)))

    outs = {}
    for n in _IN_GROUP:
        dtoks[0][n] = _exchange(dtoks[0][n], True, "scatter_l0_" + n)
    for n in big:
        parts = jnp.stack([dtoks[l][n] for l in range(L)], axis=1)
        outs[n] = _adamw_nd(W[n], parts, M[n], V[n], "adamw_" + n)
    dcw = dsmall.pop("ffn_conv_w")
    cw_parts = _exchange(jnp.transpose(dcw.reshape(L, 3, N_DEV, -1), (2, 0, 1, 3)), True, "scatter_conv_w")
    outs["ffn_conv_w"] = _adamw_nd(ffn_conv_w, cw_parts, M["ffn_conv_w"], V["ffn_conv_w"], "adamw_conv_w")
    outs["w_ada"] = _adamw_nd(w_ada, g_w_ada[None], M["w_ada"], V["w_ada"], "adamw_w_ada")
    outs["b_ada"] = _adamw_nd(b_ada, jnp.transpose(dm_rows, (1, 0, 2)), M["b_ada"], V["b_ada"], "adamw_b_ada")

    dsmall["c_ctx"] = d_c_ctx_part
    small_parts = _exchange(_pack_small(dsmall, extra=loss_local), False, "gather_small")
    w_small = _pack_small({n: W[n] for n in _SMALL})
    g_s, d_s, nm_s, nv_s = _adamw(w_small, small_parts, _pack_small({n: M[n] for n in _SMALL}),
                                  _pack_small({n: V[n] for n in _SMALL}), "adamw_small")
    like = {n: W[n] for n in _SMALL}
    g_small, loss = _unpack_small(g_s, like)
    unpacked = [g_small, _unpack_small(d_s, like)[0], _unpack_small(nm_s, like)[0], _unpack_small(nv_s, like)[0]]
    for n in _SMALL:
        outs[n] = [u[n] for u in unpacked]

    result = [loss, grad_x[None]]
    for k in range(4):
        result += [outs[n][k] for n in _NAMES]
    return tuple(result)
```

```python
import functools
import math

import numpy as np
import jax
import jax.numpy as jnp
from jax import lax
from jax.experimental import pallas as pl
from jax.experimental.pallas import tpu as pltpu

F32 = jnp.float32
BF16 = jnp.bfloat16
_MXU = jnp.bfloat16

N_DEV = 8
GRID_W = 64
HEAD_DIM = 128
NA_HEADS, NA_WIN_R, NA_WIN_C = 4, 8, 16
SWA_HEADS, SWA_KV_HEADS, SWA_WINDOW = 4, 2, 128
MLA_HEADS, MLA_Q_LORA, MLA_KV_LORA, MLA_NOPE, MLA_ROPE, MLA_V = 4, 384, 128, 128, 64, 128
GQA_HEADS, GQA_KV_HEADS = 4, 2
ROPE_THETA = 10000.0
EPS = 1e-6
NEG = -1e30
IN_SIZES = (512, 512, 512, 512, 256, 256, MLA_Q_LORA, MLA_KV_LORA, MLA_ROPE, 512, 256, 256)
IN_COLS = sum(IN_SIZES)
KPE_END = sum(IN_SIZES[:9])
IN_COLS_PAD = IN_COLS + 64
ADAM_LR, ADAM_B1, ADAM_B2, ADAM_EPS, ADAM_WD, ADAM_STEP = 0.001, 0.9, 0.999, 1e-08, 0.01, 10

LANE = 128
ROW_TILE = 256
ATT_BLOCK = 128
VMEM_LIMIT = 56 * 1024 * 1024
MM_BUDGET = 36 * 1024 * 1024
HBM_BPS = 3.0e12
MXU_FLOPS = 9.0e14
VMEM_BPS = 4.0e12
STEP_S = 0.4e-6


def _cparams(n_axes):
    return pltpu.CompilerParams(dimension_semantics=("arbitrary",) * n_axes, vmem_limit_bytes=VMEM_LIMIT)


def _exchange_copies(in_ref, out_ref, send_sems, recv_sems, local_sem, scatter):
    x, y, c = lax.axis_index("x"), lax.axis_index("y"), lax.axis_index("c")
    me = 4 * x + 2 * y + c
    copies = [pltpu.make_async_copy(in_ref.at[me] if scatter else in_ref, out_ref.at[me], local_sem)]
    for k in range(1, N_DEV):
        px, py, pc = (x + (k >> 2)) % 2, (y + ((k >> 1) & 1)) % 2, (c + (k & 1)) % 2
        peer = 4 * px + 2 * py + pc
        copies.append(pltpu.make_async_remote_copy(
            src_ref=in_ref.at[peer] if scatter else in_ref, dst_ref=out_ref.at[me],
            send_sem=send_sems.at[k - 1], recv_sem=recv_sems.at[k - 1],
            device_id=(px, py, pc), device_id_type=pl.DeviceIdType.MESH))
    return copies


def _exchange_out_shape(arr, scatter):
    return jax.ShapeDtypeStruct(arr.shape if scatter else (N_DEV,) + arr.shape, arr.dtype)


_EXCHANGE_SEMS = (pltpu.SemaphoreType.DMA((N_DEV - 1,)), pltpu.SemaphoreType.DMA((N_DEV - 1,)),
                  pltpu.SemaphoreType.DMA)


def _exchange(inp, scatter, name):
    def body(in_ref, out_ref, send_sems, recv_sems, local_sem):
        copies = _exchange_copies(in_ref, out_ref, send_sems, recv_sems, local_sem, scatter)
        for cp in copies:
            cp.start()
        for cp in copies:
            cp.wait()

    hbm = pl.BlockSpec(memory_space=pl.ANY)
    return pl.pallas_call(
        body, name=name, in_specs=[hbm], out_specs=hbm, out_shape=_exchange_out_shape(inp, scatter),
        scratch_shapes=list(_EXCHANGE_SEMS), compiler_params=pltpu.CompilerParams(has_side_effects=True),
    )(inp)


def _pcall(body, *, name, grid, in_specs, out_specs, out_shape, args, scratch_shapes=(), riders=()):
    in_specs, out_specs, out_shape = list(in_specs), list(out_specs), list(out_shape)
    scratch, args = list(scratch_shapes), list(args)
    n_in, n_out, n_scr, n_r = len(in_specs), len(out_specs), len(scratch), len(riders)
    hbm = pl.BlockSpec(memory_space=pl.ANY)
    for arr, scatter in riders:
        in_specs.append(hbm)
        args.append(arr)
        out_specs.append(hbm)
        out_shape.append(_exchange_out_shape(arr, scatter))
        scratch += list(_EXCHANGE_SEMS)

    def wrapped(*refs):
        ins, r_in = refs[:n_in], refs[n_in:n_in + n_r]
        o0 = n_in + n_r
        outs, r_out = refs[o0:o0 + n_out], refs[o0 + n_out:o0 + n_out + n_r]
        s0 = o0 + n_out + n_r
        scr, sems = refs[s0:s0 + n_scr], refs[s0 + n_scr:]

        def copies():
            out = []
            for r, (_, scatter) in enumerate(riders):
                out += _exchange_copies(r_in[r], r_out[r], sems[3 * r], sems[3 * r + 1], sems[3 * r + 2], scatter)
            return out

        if n_r:
            ids = [pl.program_id(a) for a in range(len(grid))]
            first = functools.reduce(lambda u, w: u & w, [i == 0 for i in ids])
            last = functools.reduce(lambda u, w: u & w, [i == n - 1 for i, n in zip(ids, grid)])

            @pl.when(first)
            def _():
                for cp in copies():
                    cp.start()

        body(*ins, *outs, *scr)

        if n_r:
            @pl.when(last)
            def _():
                for cp in copies():
                    cp.wait()

    params = pltpu.CompilerParams(dimension_semantics=("arbitrary",) * len(grid), vmem_limit_bytes=VMEM_LIMIT,
                                  has_side_effects=bool(riders))
    return list(pl.pallas_call(wrapped, name=name, grid=grid, in_specs=in_specs, out_specs=out_specs,
                               out_shape=out_shape, scratch_shapes=scratch, compiler_params=params)(*args))


def _riders(arrays, scatter):
    return [(a, scatter) for a in arrays]


def _nones(arrays):
    return tuple(None for _ in arrays)


def _divisors(n, step, cap):
    out = [d for d in range(step, min(n, cap) + 1, step) if n % d == 0]
    if n <= cap and n not in out:
        out.append(n)
    return out


def _mm_tiles(M, N, K, sa, sb, m_step, k_step):
    best, best_cost = None, None
    for tm in _divisors(M, m_step, 2304):
        for tn in _divisors(N, LANE, 2304):
            for tk in _divisors(K, k_step, 2304):
                vm = 2 * (tm * tk * sa + tk * tn * sb) + 3 * tm * tn * 4
                if vm > MM_BUDGET:
                    continue
                traffic = M * K * sa * (N // tn) + K * N * sb * (M // tm) + M * N * 4
                steps = (M // tm) * (N // tn) * (K // tk)
                passes = 1 if tk == K else 3
                busy = steps * (2 * tm * tn * tk / MXU_FLOPS + passes * tm * tn * 4 / VMEM_BPS)
                cost = max(traffic / HBM_BPS, busy) + steps * STEP_S
                if tm % 256 or tn % 256:
                    cost *= 1.05
                if best_cost is None or cost < best_cost:
                    best, best_cost = (tm, tn, tk), cost
    assert best is not None, (M, N, K)
    return best


def _matmul(a, b, mode, name, riders=()):
    if mode == "nn":
        (M, K), (K2, N) = a.shape, b.shape
    elif mode == "nt":
        (M, K), (N, K2) = a.shape, b.shape
    else:
        (K, M), (K2, N) = a.shape, b.shape
    assert K == K2, (a.shape, b.shape, mode)
    tm, tn, tk = _mm_tiles(M, N, K, a.dtype.itemsize, b.dtype.itemsize,
                           LANE if mode == "tn" else 8, 8 if mode == "tn" else LANE)
    nk = K // tk
    dims = {"nn": (((1,), (0,)), ((), ())), "nt": (((1,), (1,)), ((), ())), "tn": (((0,), (0,)), ((), ()))}[mode]

    def body(a_ref, b_ref, o_ref):
        prod = lax.dot_general(a_ref[...].astype(_MXU), b_ref[...].astype(_MXU), dims, preferred_element_type=F32)
        if nk == 1:
            o_ref[...] = prod
        else:
            k = pl.program_id(2)

            @pl.when(k == 0)
            def _():
                o_ref[...] = prod

            @pl.when(k > 0)
            def _():
                o_ref[...] += prod

    if mode == "nn":
        a_spec = pl.BlockSpec((tm, tk), lambda i, j, k: (i, k))
        b_spec = pl.BlockSpec((tk, tn), lambda i, j, k: (k, j))
    elif mode == "nt":
        a_spec = pl.BlockSpec((tm, tk), lambda i, j, k: (i, k))
        b_spec = pl.BlockSpec((tn, tk), lambda i, j, k: (j, k))
    else:
        a_spec = pl.BlockSpec((tk, tm), lambda i, j, k: (k, i))
        b_spec = pl.BlockSpec((tk, tn), lambda i, j, k: (k, j))
    outs = _pcall(body, name=name, grid=(M // tm, N // tn, nk), in_specs=[a_spec, b_spec],
                  out_specs=[pl.BlockSpec((tm, tn), lambda i, j, k: (i, j))],
                  out_shape=[jax.ShapeDtypeStruct((M, N), F32)], args=[a, b], riders=riders)
    return outs if riders else outs[0]


def _make_linear(tag, to_parts):
    def run(a, w, gin, tok):
        if gin:
            y, *got = _matmul(a, w, "nn", tag + "_fwd", riders=_riders(gin, False))
        else:
            y, got = _matmul(a, w, "nn", tag + "_fwd"), []
        return y, tuple(got), tok

    @jax.custom_vjp
    def linear(a, w, sink, gin, tok):
        return run(a, w, gin, tok)

    def fwd(a, w, sink, gin, tok):
        return run(a, w, gin, tok), (a, w, gin)

    def bwd(res, cts):
        a, w, gin = res
        dy, _, dtok = cts
        dw = _matmul(a, dy, "tn", tag + "_bwd_dw")
        if dtok:
            da, *recv = _matmul(dy, w, "nt", tag + "_bwd_da", riders=_riders(dtok, True))
        else:
            da, recv = _matmul(dy, w, "nt", tag + "_bwd_da"), []
        return da, jnp.zeros_like(w), to_parts(dw), _nones(gin), tuple(recv)

    linear.defvjp(fwd, bwd)
    return linear


def _seg_map(n_x_tiles):
    return lambda i: (jnp.where(i >= n_x_tiles, 1, 0), 0, 0)


def _ln_stats(x):
    mu = jnp.mean(x, axis=-1, keepdims=True)
    xc = x - mu
    var = jnp.mean(xc * xc, axis=-1, keepdims=True)
    rstd = lax.rsqrt(var + EPS)
    return xc * rstd, rstd


def _make_modulate(tag, n_x):
    def call_fwd(x, shift, scale):
        T, D = x.shape
        nxt = n_x // ROW_TILE

        def body(x_ref, sh_ref, sc_ref, o_ref):
            xhat, _ = _ln_stats(x_ref[...])
            o_ref[...] = xhat * (1.0 + sc_ref[0]) + sh_ref[0]

        row = pl.BlockSpec((ROW_TILE, D), lambda i: (i, 0))
        seg = pl.BlockSpec((1, 1, D), _seg_map(nxt))
        return pl.pallas_call(body, name=tag + "_fwd", grid=(T // ROW_TILE,), in_specs=[row, seg, seg],
                              out_specs=row, out_shape=jax.ShapeDtypeStruct((T, D), F32),
                              compiler_params=_cparams(1))(x, shift, scale)

    def call_bwd(x, scale, dh):
        T, D = x.shape
        nxt = n_x // ROW_TILE
        nseg = scale.shape[0]

        def body(x_ref, sc_ref, dh_ref, dx_ref, dsh_ref, dsc_ref):
            i = pl.program_id(0)

            @pl.when((i == 0) | (i == nxt))
            def _():
                dsh_ref[...] = jnp.zeros_like(dsh_ref)
                dsc_ref[...] = jnp.zeros_like(dsc_ref)

            xhat, rstd = _ln_stats(x_ref[...])
            dh = dh_ref[...]
            dsh_ref[0] += jnp.sum(dh, axis=0, keepdims=True)
            dsc_ref[0] += jnp.sum(dh * xhat, axis=0, keepdims=True)
            dxh = dh * (1.0 + sc_ref[0])
            m1 = jnp.mean(dxh, axis=-1, keepdims=True)
            m2 = jnp.mean(dxh * xhat, axis=-1, keepdims=True)
            dx_ref[...] = rstd * (dxh - m1 - xhat * m2)

        row = pl.BlockSpec((ROW_TILE, D), lambda i: (i, 0))
        seg = pl.BlockSpec((1, 1, D), _seg_map(nxt))
        segshape = jax.ShapeDtypeStruct((nseg, 1, D), F32)
        return pl.pallas_call(body, name=tag + "_bwd", grid=(T // ROW_TILE,), in_specs=[row, seg, row],
                              out_specs=[row, seg, seg],
                              out_shape=[jax.ShapeDtypeStruct((T, D), F32), segshape, segshape],
                              compiler_params=_cparams(1))(x, scale, dh)

    @jax.custom_vjp
    def modulate(x, shift, scale):
        return call_fwd(x, shift, scale)

    def fwd(x, shift, scale):
        return call_fwd(x, shift, scale), (x, scale)

    def bwd(res, dh):
        x, scale = res
        return tuple(call_bwd(x, scale, dh))

    modulate.defvjp(fwd, bwd)
    return modulate


def _make_resid_ln(tag, n_x, alpha):
    def call_fwd(x, y, gate, g, b):
        T, D = x.shape
        nxt = n_x // ROW_TILE

        def body(x_ref, y_ref, gt_ref, g_ref, b_ref, o_ref):
            u = alpha * x_ref[...] + gt_ref[0] * y_ref[...]
            uhat, _ = _ln_stats(u)
            o_ref[...] = uhat * g_ref[...] + b_ref[...]

        row = pl.BlockSpec((ROW_TILE, D), lambda i: (i, 0))
        seg = pl.BlockSpec((1, 1, D), _seg_map(nxt))
        vec = pl.BlockSpec((1, D), lambda i: (0, 0))
        return pl.pallas_call(body, name=tag + "_fwd", grid=(T // ROW_TILE,), in_specs=[row, row, seg, vec, vec],
                              out_specs=row, out_shape=jax.ShapeDtypeStruct((T, D), F32),
                              compiler_params=_cparams(1))(x, y, gate, g, b)

    def call_bwd(x, y, gate, g, do):
        T, D = x.shape
        nxt = n_x // ROW_TILE
        nseg = gate.shape[0]

        def body(x_ref, y_ref, gt_ref, g_ref, do_ref, dx_ref, dy_ref, dgt_ref, dg_ref, db_ref):
            i = pl.program_id(0)

            @pl.when(i == 0)
            def _():
                dg_ref[...] = jnp.zeros_like(dg_ref)
                db_ref[...] = jnp.zeros_like(db_ref)

            @pl.when((i == 0) | (i == nxt))
            def _():
                dgt_ref[...] = jnp.zeros_like(dgt_ref)

            y = y_ref[...]
            gate_v = gt_ref[0]
            uhat, rstd = _ln_stats(alpha * x_ref[...] + gate_v * y)
            do = do_ref[...]
            dg_ref[...] += jnp.sum(do * uhat, axis=0, keepdims=True)
            db_ref[...] += jnp.sum(do, axis=0, keepdims=True)
            duh = do * g_ref[...]
            m1 = jnp.mean(duh, axis=-1, keepdims=True)
            m2 = jnp.mean(duh * uhat, axis=-1, keepdims=True)
            du = rstd * (duh - m1 - uhat * m2)
            dx_ref[...] = alpha * du
            dy_ref[...] = gate_v * du
            dgt_ref[0] += jnp.sum(du * y, axis=0, keepdims=True)

        row = pl.BlockSpec((ROW_TILE, D), lambda i: (i, 0))
        seg = pl.BlockSpec((1, 1, D), _seg_map(nxt))
        vec = pl.BlockSpec((1, D), lambda i: (0, 0))
        rs = jax.ShapeDtypeStruct((T, D), F32)
        vs = jax.ShapeDtypeStruct((1, D), F32)
        return pl.pallas_call(body, name=tag + "_bwd", grid=(T // ROW_TILE,), in_specs=[row, row, seg, vec, row],
                              out_specs=[row, row, seg, vec, vec],
                              out_shape=[rs, rs, jax.ShapeDtypeStruct((nseg, 1, D), F32), vs, vs],
                              compiler_params=_cparams(1))(x, y, gate, g, do)

    @jax.custom_vjp
    def resid_ln(x, y, gate, g, b):
        return call_fwd(x, y, gate, g, b)

    def fwd(x, y, gate, g, b):
        return call_fwd(x, y, gate, g, b), (x, y, gate, g)

    def bwd(res, do):
        x, y, gate, g = res
        return tuple(call_bwd(x, y, gate, g, do))

    resid_ln.defvjp(fwd, bwd)
    return resid_ln


def _make_rmsnorm(tag):
    def call_fwd(x, g):
        T, W = x.shape
        gw = g.shape[1]

        def body(x_ref, g_ref, o_ref):
            x = x_ref[...]
            r = lax.rsqrt(jnp.mean(x * x, axis=-1, keepdims=True) + EPS)
            o_ref[...] = x * r * g_ref[...]

        blk = pl.BlockSpec((ROW_TILE, gw), lambda i, c: (i, c))
        vec = pl.BlockSpec((1, gw), lambda i, c: (0, 0))
        return pl.pallas_call(body, name=tag + "_fwd", grid=(T // ROW_TILE, W // gw), in_specs=[blk, vec],
                              out_specs=blk, out_shape=jax.ShapeDtypeStruct((T, W), F32),
                              compiler_params=_cparams(2))(x, g)

    def call_bwd(x, g, dy):
        T, W = x.shape
        gw = g.shape[1]

        def body(x_ref, g_ref, dy_ref, dx_ref, dg_ref):
            @pl.when((pl.program_id(0) == 0) & (pl.program_id(1) == 0))
            def _():
                dg_ref[...] = jnp.zeros_like(dg_ref)

            x = x_ref[...]
            r = lax.rsqrt(jnp.mean(x * x, axis=-1, keepdims=True) + EPS)
            xn = x * r
            dy = dy_ref[...]
            dg_ref[...] += jnp.sum(dy * xn, axis=0, keepdims=True)
            dxn = dy * g_ref[...]
            dx_ref[...] = r * (dxn - xn * jnp.mean(dxn * xn, axis=-1, keepdims=True))

        blk = pl.BlockSpec((ROW_TILE, gw), lambda i, c: (i, c))
        vec = pl.BlockSpec((1, gw), lambda i, c: (0, 0))
        return pl.pallas_call(body, name=tag + "_bwd", grid=(T // ROW_TILE, W // gw), in_specs=[blk, vec, blk],
                              out_specs=[blk, vec],
                              out_shape=[jax.ShapeDtypeStruct((T, W), F32), jax.ShapeDtypeStruct((1, gw), F32)],
                              compiler_params=_cparams(2))(x, g, dy)

    @jax.custom_vjp
    def rmsnorm(x, g):
        return call_fwd(x, g)

    def fwd(x, g):
        return call_fwd(x, g), (x, g)

    def bwd(res, dy):
        x, g = res
        return tuple(call_bwd(x, g, dy))

    rmsnorm.defvjp(fwd, bwd)
    return rmsnorm


def _rope_tables(n_x, n_all, width, half):
    t = np.arange(n_all)
    row, col = t // GRID_W, t % GRID_W
    lane = np.arange(LANE)
    inside = lane < width
    axis_is_col = (lane // (2 * half)) % 2 == 1
    f = (lane % (2 * half)) % half
    inv_freq = ROPE_THETA ** (-(f.astype(np.float64)) / half)
    pos = np.where(axis_is_col[None, :], col[:, None], row[:, None]).astype(np.float64)
    ang = (pos.astype(np.float32) * inv_freq.astype(np.float32)[None, :]).astype(np.float32)
    live = inside[None, :] & (t < n_x)[:, None]
    cos = np.where(live, np.cos(ang), 1.0).astype(np.float32)
    sin = np.where(live, np.sin(ang), 0.0).astype(np.float32)
    first = (lane % (2 * half)) < half
    s_lo = np.where(first[None, :], -sin, 0.0).astype(np.float32)
    s_hi = np.where(first[None, :], 0.0, sin).astype(np.float32)
    return jnp.asarray(cos), jnp.asarray(s_lo), jnp.asarray(s_hi)


def _make_rope(tag, half):
    def call(x, cos, s_lo, s_hi, transpose, name):
        T, W = x.shape

        def body(x_ref, c_ref, lo_ref, hi_ref, o_ref):
            x = x_ref[...]
            if not transpose:
                o_ref[...] = (x * c_ref[...] + pltpu.roll(x, LANE - half, 1) * lo_ref[...]
                              + pltpu.roll(x, half, 1) * hi_ref[...])
            else:
                o_ref[...] = (x * c_ref[...] + pltpu.roll(x * lo_ref[...], half, 1)
                              + pltpu.roll(x * hi_ref[...], LANE - half, 1))

        blk = pl.BlockSpec((ROW_TILE, LANE), lambda i, c: (i, c))
        tab = pl.BlockSpec((ROW_TILE, LANE), lambda i, c: (i, 0))
        return pl.pallas_call(body, name=name, grid=(T // ROW_TILE, W // LANE), in_specs=[blk, tab, tab, tab],
                              out_specs=blk, out_shape=jax.ShapeDtypeStruct((T, W), F32),
                              compiler_params=_cparams(2))(x, cos, s_lo, s_hi)

    @jax.custom_vjp
    def rope(x, cos, s_lo, s_hi):
        return call(x, cos, s_lo, s_hi, False, tag + "_fwd")

    def fwd(x, cos, s_lo, s_hi):
        return call(x, cos, s_lo, s_hi, False, tag + "_fwd"), (cos, s_lo, s_hi)

    def bwd(res, dy):
        cos, s_lo, s_hi = res
        return call(dy, cos, s_lo, s_hi, True, tag + "_bwd"), None, None, None

    rope.defvjp(fwd, bwd)
    return rope


def _make_conv_gate(tag, n_x):
    def shifted(v, T):
        t = lax.broadcasted_iota(jnp.int32, v.shape, 0)
        prev = jnp.where((t == 0) | (t == n_x), 0.0, pltpu.roll(v, 1, 0))
        nxt = jnp.where((t == n_x - 1) | (t == T - 1), 0.0, pltpu.roll(v, T - 1, 0))
        return prev, nxt

    def call_fwd(a, u, cw, cb, riders):
        T, Fd = a.shape

        def body(a_ref, u_ref, cw_ref, cb_ref, z_ref):
            pre = a_ref[...]
            prev, nxt = shifted(pre, T)
            s = prev * cw_ref[0:1, :] + pre * cw_ref[1:2, :] + nxt * cw_ref[2:3, :] + cb_ref[...]
            z_ref[...] = s * (1.0 / (1.0 + jnp.exp(-s))) * u_ref[...]

        col = pl.BlockSpec((T, LANE), lambda j: (0, j))
        return _pcall(body, name=tag + "_fwd", grid=(Fd // LANE,),
                      in_specs=[col, col, pl.BlockSpec((3, LANE), lambda j: (0, j)),
                                pl.BlockSpec((1, LANE), lambda j: (0, j))],
                      out_specs=[col], out_shape=[jax.ShapeDtypeStruct((T, Fd), F32)], args=[a, u, cw, cb],
                      riders=riders)

    def call_bwd(a, u, cw, cb, dz, riders):
        T, Fd = a.shape

        def body(a_ref, u_ref, cw_ref, cb_ref, dz_ref, da_ref, du_ref, dcw_ref, dcb_ref):
            pre = a_ref[...]
            prev, nxt = shifted(pre, T)
            s = prev * cw_ref[0:1, :] + pre * cw_ref[1:2, :] + nxt * cw_ref[2:3, :] + cb_ref[...]
            sig = 1.0 / (1.0 + jnp.exp(-s))
            dz = dz_ref[...]
            du_ref[...] = dz * s * sig
            ds = dz * u_ref[...] * (sig * (1.0 + s * (1.0 - sig)))
            ds_prev, ds_next = shifted(ds, T)
            da_ref[...] = ds * cw_ref[1:2, :] + ds_next * cw_ref[0:1, :] + ds_prev * cw_ref[2:3, :]
            dcw_ref[0:1, :] = jnp.sum(ds * prev, axis=0, keepdims=True)
            dcw_ref[1:2, :] = jnp.sum(ds * pre, axis=0, keepdims=True)
            dcw_ref[2:3, :] = jnp.sum(ds * nxt, axis=0, keepdims=True)
            dcb_ref[...] = jnp.sum(ds, axis=0, keepdims=True)

        col = pl.BlockSpec((T, LANE), lambda j: (0, j))
        w3 = pl.BlockSpec((3, LANE), lambda j: (0, j))
        w1 = pl.BlockSpec((1, LANE), lambda j: (0, j))
        big = jax.ShapeDtypeStruct((T, Fd), F32)
        return _pcall(body, name=tag + "_bwd", grid=(Fd // LANE,), in_specs=[col, col, w3, w1, col],
                      out_specs=[col, col, w3, w1],
                      out_shape=[big, big, jax.ShapeDtypeStruct((3, Fd), F32), jax.ShapeDtypeStruct((1, Fd), F32)],
                      args=[a, u, cw, cb, dz], riders=riders)

    def run(a, u, cw, cb, gin, tok):
        z, *got = call_fwd(a, u, cw, cb, _riders(gin, False))
        return z, tuple(got), tok

    @jax.custom_vjp
    def conv_gate(a, u, cw, cb, gin, tok):
        return run(a, u, cw, cb, gin, tok)

    def fwd(a, u, cw, cb, gin, tok):
        return run(a, u, cw, cb, gin, tok), (a, u, cw, cb, gin)

    def bwd(res, cts):
        a, u, cw, cb, gin = res
        dz, _, dtok = cts
        da, du, dcw, dcb, *recv = call_bwd(a, u, cw, cb, dz, _riders(dtok, True))
        return da, du, dcw, dcb, _nones(gin), tuple(recv)

    conv_gate.defvjp(fwd, bwd)
    return conv_gate


def _make_loss_head(tag):
    def call(y, tgt):
        T, D = y.shape

        def body(y_ref, t_ref, l_ref, r_ref):
            @pl.when(pl.program_id(0) == 0)
            def _():
                l_ref[...] = jnp.zeros_like(l_ref)

            d = y_ref[...] - t_ref[...]
            r_ref[...] = d * (1.0 / D)
            l_ref[...] += jnp.sum(d * d) * (0.5 / D)

        row = pl.BlockSpec((ROW_TILE, D), lambda i: (i, 0))
        return pl.pallas_call(body, name=tag, grid=(T // ROW_TILE,), in_specs=[row, row],
                              out_specs=[pl.BlockSpec((8, LANE), lambda i: (0, 0)), row],
                              out_shape=[jax.ShapeDtypeStruct((8, LANE), F32), jax.ShapeDtypeStruct((T, D), F32)],
                              compiler_params=_cparams(1))(y, tgt)

    @jax.custom_vjp
    def loss_head(y, tgt):
        return call(y, tgt)[0][0, 0]

    def fwd(y, tgt):
        l, r = call(y, tgt)
        return l[0, 0], r

    def bwd(r, g):
        return r * g, None

    loss_head.defvjp(fwd, bwd)
    return loss_head


class _AttnCfg:
    def __init__(self, H, Hkv, dqk, dv, scale, tq, tk, has_sink=False):
        self.H, self.Hkv, self.dqk, self.dv, self.scale = H, Hkv, dqk, dv, scale
        self.tq, self.tk, self.has_sink = tq, tk, has_sink
        self.G = H // Hkv


class _WinCfg:
    def __init__(self, mode, H, Hkv, scale, nx, n_ctx):
        self.mode, self.H, self.Hkv, self.scale, self.nx, self.n_ctx = mode, H, Hkv, scale, nx, n_ctx
        self.G = H // Hkv
        self.has_sink = mode == "swa"
        self.nwin, self.back = (3, 1) if mode == "swa" else (5, 2)
        self.width = self.nwin * ATT_BLOCK

    def first_block(self, i):
        return jnp.clip(i - self.back, 0, self.nx - self.nwin)

    def slots(self, i, jj):
        d = 2 * (self.first_block(i) + jj - i)
        return jnp.clip(d + 8, 0, 15), jnp.clip(d + 7, 0, 15)

    def local_scores(self, s, i, tab_ref):
        sub = lax.broadcasted_iota(jnp.int32, s.shape, 0)
        lan = lax.broadcasted_iota(jnp.int32, s.shape, 1)
        first = self.first_block(i)
        if self.mode == "swa":
            diff = (i * ATT_BLOCK + sub) - (first * ATT_BLOCK + lan)
            return jnp.where(jnp.abs(diff) <= SWA_WINDOW, s, -jnp.inf)
        rows = 2 * self.nx
        qrow = 2 * i + jnp.where(sub >= GRID_W, 1, 0)
        krow = 2 * first + lan // GRID_W
        r0 = jnp.clip(qrow - NA_WIN_R // 2, 0, rows - NA_WIN_R)
        ok = (krow >= r0) & (krow < r0 + NA_WIN_R)
        tiles = []
        for jj in range(self.nwin):
            top, bot = self.slots(i, jj)
            tiles.append(jnp.concatenate([tab_ref[0, top], tab_ref[0, bot]], axis=0))
        return jnp.where(ok, s + jnp.concatenate(tiles, axis=1), -jnp.inf)


def _win_specs(cfg, Tk):
    G = cfg.G
    qmap = lambda kh, g, i: (i, kh * G + g)
    q_spec = pl.BlockSpec((ATT_BLOCK, HEAD_DIM), qmap)
    kv_spec = pl.BlockSpec((Tk, HEAD_DIM), lambda kh, g, i: (0, kh))
    lse_spec = pl.BlockSpec((1, ATT_BLOCK, LANE), lambda kh, g, i: (kh * G + g, i, 0))
    tab_spec = pl.BlockSpec((1, 16, GRID_W, LANE), lambda kh, g, i: (kh * G + g, 0, 0, 0))
    sink_spec = pl.BlockSpec((1, 8, LANE), lambda kh, g, i: (kh * G + g, 0, 0))
    return q_spec, kv_spec, lse_spec, tab_spec, sink_spec


def _win_scores(cfg, i, q_ref, k_ref, tab_ref):
    dn = (((1,), (1,)), ((), ()))
    n_x = cfg.nx * ATT_BLOCK
    win = pl.ds(pl.multiple_of(cfg.first_block(i) * ATT_BLOCK, ATT_BLOCK), cfg.width)
    ctx = pl.ds(n_x, cfg.n_ctx)
    qb = q_ref[...].astype(_MXU)
    s_loc = lax.dot_general(qb, k_ref[win, :].astype(_MXU), dn, preferred_element_type=F32) * cfg.scale
    s_ctx = lax.dot_general(qb, k_ref[ctx, :].astype(_MXU), dn, preferred_element_type=F32) * cfg.scale
    return cfg.local_scores(s_loc, i, tab_ref), s_ctx, win, ctx


def _win_fwd_call(cfg, name, q, k, v, tab, sink, riders=()):
    Tq, Tk = q.shape[0], k.shape[0]

    def body(*refs):
        q_ref, k_ref, v_ref = refs[:3]
        extra = refs[3] if (cfg.mode == "na" or cfg.has_sink) else None
        o_ref, lse_ref = refs[-2:]
        i = pl.program_id(2)
        s_loc, s_ctx, win, ctx = _win_scores(cfg, i, q_ref, k_ref, extra if cfg.mode == "na" else None)
        m = jnp.maximum(jnp.max(s_loc, axis=-1, keepdims=True), jnp.max(s_ctx, axis=-1, keepdims=True))
        if cfg.has_sink:
            m = jnp.maximum(m, extra[0, 0:1, 0:1])
        p_loc, p_ctx = jnp.exp(s_loc - m), jnp.exp(s_ctx - m)
        l = jnp.sum(p_loc, axis=-1, keepdims=True) + jnp.sum(p_ctx, axis=-1, keepdims=True)
        if cfg.has_sink:
            l = l + jnp.exp(extra[0, 0:1, 0:1] - m)
        acc = (jnp.dot(p_loc.astype(_MXU), v_ref[win, :].astype(_MXU), preferred_element_type=F32)
               + jnp.dot(p_ctx.astype(_MXU), v_ref[ctx, :].astype(_MXU), preferred_element_type=F32))
        o_ref[...] = acc / l
        lse_ref[0] = jnp.broadcast_to(m + jnp.log(l), (ATT_BLOCK, LANE))

    q_spec, kv_spec, lse_spec, tab_spec, sink_spec = _win_specs(cfg, Tk)
    in_specs, args = [q_spec, kv_spec, kv_spec], [q, k, v]
    if cfg.mode == "na":
        in_specs.append(tab_spec)
        args.append(tab)
    if cfg.has_sink:
        in_specs.append(sink_spec)
        args.append(sink)
    return _pcall(
        body, name=name, grid=(cfg.Hkv, cfg.G, Tq // ATT_BLOCK), in_specs=in_specs, out_specs=[q_spec, lse_spec],
        out_shape=[jax.ShapeDtypeStruct((Tq, cfg.H * HEAD_DIM), F32), jax.ShapeDtypeStruct((cfg.H, Tq, LANE), F32)],
        args=args, riders=riders)


def _win_bwd_call(cfg, name, q, k, v, tab, sink, o, lse, do, riders=()):
    Tq, Tk = q.shape[0], k.shape[0]
    dn_nt = (((1,), (1,)), ((), ()))
    dn_tn = (((0,), (0,)), ((), ()))

    def body(*refs):
        q_ref, k_ref, v_ref = refs[:3]
        has_extra = cfg.mode == "na" or cfg.has_sink
        extra = refs[3] if has_extra else None
        n = 4 if has_extra else 3
        o_ref, lse_ref, do_ref, dq_ref, dk_ref, dv_ref = refs[n:n + 6]
        dextra = refs[n + 6] if has_extra else None
        g, i = pl.program_id(1), pl.program_id(2)

        @pl.when((g == 0) & (i == 0))
        def _():
            dk_ref[...] = jnp.zeros_like(dk_ref)
            dv_ref[...] = jnp.zeros_like(dv_ref)

        if has_extra:
            @pl.when(i == 0)
            def _():
                dextra[...] = jnp.zeros_like(dextra)

        s_loc, s_ctx, win, ctx = _win_scores(cfg, i, q_ref, k_ref, extra if cfg.mode == "na" else None)
        lse = lse_ref[0][:, 0:1]
        p_loc, p_ctx = jnp.exp(s_loc - lse), jnp.exp(s_ctx - lse)
        do_f = do_ref[...]
        do_b = do_f.astype(_MXU)
        delta = jnp.sum(do_f * o_ref[...], axis=-1, keepdims=True)
        dp_loc = lax.dot_general(do_b, v_ref[win, :].astype(_MXU), dn_nt, preferred_element_type=F32)
        dp_ctx = lax.dot_general(do_b, v_ref[ctx, :].astype(_MXU), dn_nt, preferred_element_type=F32)
        ds_loc, ds_ctx = p_loc * (dp_loc - delta), p_ctx * (dp_ctx - delta)
        if cfg.mode == "na":
            for jj in range(cfg.nwin):
                top, bot = cfg.slots(i, jj)
                tile = ds_loc[:, jj * ATT_BLOCK:(jj + 1) * ATT_BLOCK]
                dextra[0, top] += tile[0:GRID_W, :]
                dextra[0, bot] += tile[GRID_W:, :]
        if cfg.has_sink:
            p_sink = jnp.exp(extra[0, 0:1, 0:1] - lse)
            lane0 = ((lax.broadcasted_iota(jnp.int32, (8, LANE), 0) == 0)
                     & (lax.broadcasted_iota(jnp.int32, (8, LANE), 1) == 0))
            dextra[0] += jnp.where(lane0, -jnp.sum(p_sink * delta), 0.0)
        dsb_loc, dsb_ctx = (ds_loc * cfg.scale).astype(_MXU), (ds_ctx * cfg.scale).astype(_MXU)
        qb = q_ref[...].astype(_MXU)
        dq_ref[...] = (jnp.dot(dsb_loc, k_ref[win, :].astype(_MXU), preferred_element_type=F32)
                       + jnp.dot(dsb_ctx, k_ref[ctx, :].astype(_MXU), preferred_element_type=F32))
        dk_ref[win, :] += lax.dot_general(dsb_loc, qb, dn_tn, preferred_element_type=F32)
        dk_ref[ctx, :] += lax.dot_general(dsb_ctx, qb, dn_tn, preferred_element_type=F32)
        dv_ref[win, :] += lax.dot_general(p_loc.astype(_MXU), do_b, dn_tn, preferred_element_type=F32)
        dv_ref[ctx, :] += lax.dot_general(p_ctx.astype(_MXU), do_b, dn_tn, preferred_element_type=F32)

    q_spec, kv_spec, lse_spec, tab_spec, sink_spec = _win_specs(cfg, Tk)
    in_specs, args = [q_spec, kv_spec, kv_spec], [q, k, v]
    out_specs = [q_spec, kv_spec, kv_spec]
    out_shape = [jax.ShapeDtypeStruct(q.shape, F32), jax.ShapeDtypeStruct(k.shape, F32),
                 jax.ShapeDtypeStruct(v.shape, F32)]
    if cfg.mode == "na":
        in_specs.append(tab_spec)
        args.append(tab)
        out_specs.append(tab_spec)
        out_shape.append(jax.ShapeDtypeStruct(tab.shape, F32))
    if cfg.has_sink:
        in_specs.append(sink_spec)
        args.append(sink)
        out_specs.append(sink_spec)
        out_shape.append(jax.ShapeDtypeStruct(sink.shape, F32))
    in_specs += [q_spec, lse_spec, q_spec]
    args += [o, lse, do]
    return _pcall(body, name=name, grid=(cfg.Hkv, cfg.G, Tq // ATT_BLOCK), in_specs=in_specs, out_specs=out_specs,
                  out_shape=out_shape, args=args, riders=riders)


def _attn_specs(cfg, Tk):
    G, tq = cfg.G, cfg.tq
    qmap = lambda kh, g, i: (i, kh * G + g)
    q_spec = pl.BlockSpec((tq, cfg.dqk), qmap)
    k_spec = pl.BlockSpec((Tk, cfg.dqk), lambda kh, g, i: (0, kh))
    v_spec = pl.BlockSpec((Tk, cfg.dv), lambda kh, g, i: (0, kh))
    o_spec = pl.BlockSpec((tq, cfg.dv), qmap)
    lse_spec = pl.BlockSpec((1, tq, LANE), lambda kh, g, i: (kh * G + g, i, 0))
    sink_spec = pl.BlockSpec((1, 8, LANE), lambda kh, g, i: (kh * G + g, 0, 0))
    return q_spec, k_spec, v_spec, o_spec, lse_spec, sink_spec


def _attn_fwd_call(cfg, name, q, k, v, sink, riders=()):
    Tq, Tk = q.shape[0], k.shape[0]
    nq, nkv = Tq // cfg.tq, Tk // cfg.tk
    dn = (((1,), (1,)), ((), ()))

    def body(*refs):
        q_ref, k_ref, v_ref = refs[:3]
        sink_ref = refs[3] if cfg.has_sink else None
        o_ref, lse_ref = refs[-2:]
        qb = q_ref[...].astype(_MXU)
        m = jnp.full((cfg.tq, 1), NEG, F32)
        l = jnp.zeros((cfg.tq, 1), F32)
        acc = jnp.zeros((cfg.tq, cfg.dv), F32)
        for j in range(nkv):
            rows = pl.ds(j * cfg.tk, cfg.tk)
            s = lax.dot_general(qb, k_ref[rows, :].astype(_MXU), dn, preferred_element_type=F32) * cfg.scale
            m_new = jnp.maximum(m, jnp.max(s, axis=-1, keepdims=True))
            alpha = jnp.exp(m - m_new)
            p = jnp.exp(s - m_new)
            l = alpha * l + jnp.sum(p, axis=-1, keepdims=True)
            acc = alpha * acc + jnp.dot(p.astype(_MXU), v_ref[rows, :].astype(_MXU), preferred_element_type=F32)
            m = m_new
        if cfg.has_sink:
            sk = sink_ref[0, 0:1, 0:1]
            m_new = jnp.maximum(m, sk)
            alpha = jnp.exp(m - m_new)
            l = alpha * l + jnp.exp(sk - m_new)
            acc = acc * alpha
            m = m_new
        o_ref[...] = acc / l
        lse_ref[0] = jnp.broadcast_to(m + jnp.log(l), (cfg.tq, LANE))

    q_spec, k_spec, v_spec, o_spec, lse_spec, sink_spec = _attn_specs(cfg, Tk)
    in_specs, args = [q_spec, k_spec, v_spec], [q, k, v]
    if cfg.has_sink:
        in_specs.append(sink_spec)
        args.append(sink)
    return _pcall(
        body, name=name, grid=(cfg.Hkv, cfg.G, nq), in_specs=in_specs, out_specs=[o_spec, lse_spec],
        out_shape=[jax.ShapeDtypeStruct((Tq, cfg.H * cfg.dv), F32), jax.ShapeDtypeStruct((cfg.H, Tq, LANE), F32)],
        args=args, riders=riders)


def _attn_bwd_call(cfg, name, q, k, v, sink, o, lse, do, riders=()):
    Tq, Tk = q.shape[0], k.shape[0]
    nq, nkv = Tq // cfg.tq, Tk // cfg.tk
    tq, tk = cfg.tq, cfg.tk
    dn_nt = (((1,), (1,)), ((), ()))
    dn_tn = (((0,), (0,)), ((), ()))

    def body(*refs):
        q_ref, k_ref, v_ref = refs[:3]
        n = 3
        sink_ref = dsink_ref = None
        if cfg.has_sink:
            sink_ref, n = refs[n], n + 1
        o_ref, lse_ref, do_ref, dq_ref, dk_ref, dv_ref = refs[n:n + 6]
        n += 6
        if cfg.has_sink:
            dsink_ref, n = refs[n], n + 1
        g, i = pl.program_id(1), pl.program_id(2)

        @pl.when((g == 0) & (i == 0))
        def _():
            dk_ref[...] = jnp.zeros_like(dk_ref)
            dv_ref[...] = jnp.zeros_like(dv_ref)

        if cfg.has_sink:
            @pl.when(i == 0)
            def _():
                dsink_ref[...] = jnp.zeros_like(dsink_ref)

        do_f = do_ref[...]
        do_b = do_f.astype(_MXU)
        qb = q_ref[...].astype(_MXU)
        lse = lse_ref[0][:, 0:1]
        delta = jnp.sum(do_f * o_ref[...], axis=-1, keepdims=True)
        if cfg.has_sink:
            p_sink = jnp.exp(sink_ref[0, 0:1, 0:1] - lse)
            lane0 = ((lax.broadcasted_iota(jnp.int32, (8, LANE), 0) == 0)
                     & (lax.broadcasted_iota(jnp.int32, (8, LANE), 1) == 0))
            dsink_ref[0] += jnp.where(lane0, -jnp.sum(p_sink * delta), 0.0)
        dq = jnp.zeros((tq, cfg.dqk), F32)
        for j in range(nkv):
            rows = pl.ds(j * tk, tk)
            kb, vb = k_ref[rows, :].astype(_MXU), v_ref[rows, :].astype(_MXU)
            s = lax.dot_general(qb, kb, dn_nt, preferred_element_type=F32) * cfg.scale
            p = jnp.exp(s - lse)
            dp = lax.dot_general(do_b, vb, dn_nt, preferred_element_type=F32)
            dsb = (p * (dp - delta) * cfg.scale).astype(_MXU)
            dq = dq + jnp.dot(dsb, kb, preferred_element_type=F32)
            dk_ref[rows, :] += lax.dot_general(dsb, qb, dn_tn, preferred_element_type=F32)
            dv_ref[rows, :] += lax.dot_general(p.astype(_MXU), do_b, dn_tn, preferred_element_type=F32)
        dq_ref[...] = dq

    q_spec, k_spec, v_spec, o_spec, lse_spec, sink_spec = _attn_specs(cfg, Tk)
    in_specs, args = [q_spec, k_spec, v_spec], [q, k, v]
    if cfg.has_sink:
        in_specs.append(sink_spec)
        args.append(sink)
    in_specs += [o_spec, lse_spec, o_spec]
    args += [o, lse, do]
    out_specs = [q_spec, k_spec, v_spec]
    out_shape = [jax.ShapeDtypeStruct(q.shape, F32), jax.ShapeDtypeStruct(k.shape, F32),
                 jax.ShapeDtypeStruct(v.shape, F32)]
    if cfg.has_sink:
        out_specs.append(sink_spec)
        out_shape.append(jax.ShapeDtypeStruct(sink.shape, F32))
    return _pcall(body, name=name, grid=(cfg.Hkv, cfg.G, nq), in_specs=in_specs, out_specs=out_specs,
                  out_shape=out_shape, args=args, riders=riders)


def _make_attention(tag, cfg):
    windowed = isinstance(cfg, _WinCfg)

    def run_fwd(q, k, v, tab, sink, gin):
        if windowed:
            return _win_fwd_call(cfg, tag + "_fwd", q, k, v, tab, sink, _riders(gin, False))
        return _attn_fwd_call(cfg, tag + "_fwd", q, k, v, sink, _riders(gin, False))

    @jax.custom_vjp
    def attn(q, k, v, tab, sink, gin, tok):
        o, _, *got = run_fwd(q.astype(_MXU), k.astype(_MXU), v.astype(_MXU), tab, sink, gin)
        return o, tuple(got), tok

    def fwd(q, k, v, tab, sink, gin, tok):
        q, k, v = q.astype(_MXU), k.astype(_MXU), v.astype(_MXU)
        o, lse, *got = run_fwd(q, k, v, tab, sink, gin)
        return (o, tuple(got), tok), (q, k, v, tab, sink, gin, o, lse)

    def bwd(res, cts):
        q, k, v, tab, sink, gin, o, lse = res
        do, _, dtok = cts
        if windowed:
            outs = _win_bwd_call(cfg, tag + "_bwd", q, k, v, tab, sink, o, lse, do, _riders(dtok, True))
        else:
            outs = _attn_bwd_call(cfg, tag + "_bwd", q, k, v, sink, o, lse, do, _riders(dtok, True))
        dq, dk, dv = outs[:3]
        rest = outs[3:]
        dtab = rest.pop(0) if tab is not None else None
        dsink = rest.pop(0) if sink is not None else None
        return dq, dk, dv, dtab, dsink, _nones(gin), tuple(rest)

    attn.defvjp(fwd, bwd)
    return attn


def _pick_block(n, prefs):
    for p in prefs:
        if n % p == 0:
            return p
    return n


def _na_table(rpb):
    cq = np.arange(GRID_W)
    dcol = np.clip(cq[None, :] - cq[:, None] + NA_WIN_C - 1, 0, 2 * NA_WIN_C - 2)
    onehot = (dcol[:, :, None] == np.arange(2 * NA_WIN_C - 1)[None, None, :]).astype(np.float32)
    c0 = np.clip(cq - NA_WIN_C // 2, 0, GRID_W - NA_WIN_C)
    col_in = (cq[None, :] >= c0[:, None]) & (cq[None, :] < c0[:, None] + NA_WIN_C)
    tz = jnp.einsum("hrj,qkj->hrqk", rpb, jnp.asarray(onehot), precision=lax.Precision.HIGHEST)
    tz = jnp.where(jnp.asarray(col_in)[None, None], tz, NEG)
    zero = jnp.zeros_like(tz[:, :1])
    tzp = jnp.concatenate([zero, tz, zero], axis=1)
    return jnp.concatenate([tzp[:, 0:16], tzp[:, 1:17]], axis=-1)


def _sink_block(sink):
    return jnp.broadcast_to(sink[:, None, None], (sink.shape[0], 8, LANE))


def _make_split(bounds, axis):
    @jax.custom_vjp
    def split(t):
        return tuple(lax.slice_in_dim(t, a, b, axis=axis) for a, b in bounds)

    def fwd(t):
        return split(t), None

    def bwd(_, cts):
        return (jnp.concatenate(cts, axis=axis),)

    split.defvjp(fwd, bwd)
    return split


_IN_GROUPS = ((0, 512), (512, 1024), (1024, 1536), (1536, 2048), (2048, 2304), (2304, 2560), (2560, 2944),
              (2944, 3072), (3072, 3200), (3200, 3712), (3712, 3968), (3968, 4224))


def _seg2(vx, vc, nseg):
    rows = [vx, vc][:nseg]
    return jnp.stack(rows)[:, None, :]


def _full_weight(name, g):
    if name in _BIG_ROW:
        return g.reshape(N_DEV * g.shape[1], g.shape[2])
    _, K, n = g.shape
    full = jnp.transpose(g, (1, 0, 2)).reshape(K, N_DEV * n)
    if name == "w_in":
        full = jnp.concatenate([full[:, :KPE_END], jnp.zeros((K, 64), full.dtype), full[:, KPE_END:]], axis=-1)
    return full


def _grad_parts(name):
    def to_parts(dw):
        if name in _BIG_ROW:
            return dw.reshape(N_DEV, dw.shape[0] // N_DEV, dw.shape[1]).astype(BF16)
        if name == "w_in":
            dw = jnp.concatenate([dw[:, :KPE_END], dw[:, KPE_END + 64:]], axis=-1)
        K, N = dw.shape
        return jnp.transpose(dw.reshape(K, N_DEV, N // N_DEV), (1, 0, 2)).astype(BF16)

    return to_parts


_IN_GROUP = ("w_in", "mla_w_uq", "mla_w_ukv")


def _local_loss(x, ctx, tgt, modx, modc, p, full, shards, toks):
    S, D = x.shape
    C = ctx.shape[0]
    depth = len(full)
    ride = shards is not None
    have = [dict(f) for f in full]
    sinks = {(l, n): toks[l][n] for l in range(depth) for n in toks[l]}
    alpha = (2 * depth) ** 0.25
    T0 = S + C
    nx = S // ATT_BLOCK

    def gin(l, names):
        return tuple(shards[l][n] for n in names) if ride else ()

    def tok(l, names):
        return tuple(sinks[(l, n)] for n in names) if ride else ()

    def landed(result, lw, names_w, lt, names_t):
        out, got, tk = result
        for n, g in zip(names_w, got):
            have[lw][n] = _full_weight(n, g)
        for n, s in zip(names_t, tk):
            sinks[(lt, n)] = s
        return out

    sc = HEAD_DIM ** -0.5
    sc_mla = (MLA_NOPE + MLA_ROPE) ** -0.5
    tq_full = _pick_block(S, (256, 128))
    tk_all = _pick_block(T0, (2176, 640, 512, 256, 128))
    tctx = _pick_block(C, (256, 128))
    rope128 = _rope_tables(S, T0, 128, 32)
    rope64 = _rope_tables(S, T0, 64, 16)

    xs = jnp.concatenate([x, ctx], axis=0)
    for l in range(depth):
        last = l == depth - 1
        t = "l%d_" % l
        nxt = () if last else _IN_GROUP
        lin = lambda a, name, g=(), tk=(): _make_linear(t + name, _grad_parts(name))(
            a, have[l][name], sinks[(l, name)], g, tk)
        mx, mc = modx[l], modc[l]

        h = _make_modulate(t + "mod1", S)(xs, _seg2(mx[0], mc[0], 2), _seg2(mx[1], mc[1], 2))
        pj = lin(h, "w_in")[0]
        qa, ka, va, qb, kb, vb, cq, ckv, kpe, qd, kd, vd = _make_split(_IN_GROUPS, 1)(pj)
        rows = lambda v_: _make_split(((0, S), (S, T0)), 0)(v_)
        tab = _na_table(p["na_rpb"][l])
        cfg_a = _WinCfg("na", NA_HEADS, NA_HEADS, sc, nx, C)
        qa_x, qa_c = rows(qa)
        oa = landed(_make_attention(t + "attn_a", cfg_a)(qa_x, ka, va, tab, None, gin(l, ("w_out",)),
                                                         tok(l, ("w_out",))), l, ("w_out",), l, ("w_out",))
        qb = _make_rope(t + "rope_b", 32)(qb, *rope128)
        kb = _make_rope(t + "rope_bk", 32)(kb, *rope128)
        snk = _sink_block(p["swa_sink"][l])
        cfg_b = _WinCfg("swa", SWA_HEADS, SWA_KV_HEADS, sc, nx, C)
        qb_x, qb_c = rows(qb)
        ob = landed(_make_attention(t + "attn_b", cfg_b)(qb_x, kb, vb, None, snk, gin(l, ("ffn_w_down",)),
                                                         tok(l, ("ffn_w_down",))), l, ("ffn_w_down",), l, ("ffn_w_down",))
        cq = _make_rmsnorm(t + "rms_cq")(cq, p["mla_q_norm"][l][None, :])
        ckv = _make_rmsnorm(t + "rms_ckv")(ckv, p["mla_kv_norm"][l][None, :])
        qh = lin(cq, "mla_w_uq")[0].reshape(T0, MLA_HEADS, MLA_NOPE + MLA_ROPE)
        kvh = lin(ckv, "mla_w_ukv")[0].reshape(T0, MLA_HEADS, MLA_NOPE + MLA_V)
        qpe = jnp.pad(qh[:, :, MLA_NOPE:], ((0, 0), (0, 0), (0, LANE - MLA_ROPE))).reshape(T0, MLA_HEADS * LANE)
        qpe = _make_rope(t + "rope_cq", 16)(qpe, *rope64).reshape(T0, MLA_HEADS, LANE)
        kpe = _make_rope(t + "rope_ck", 16)(kpe, *rope64)
        qc = jnp.concatenate([qh[:, :, :MLA_NOPE], qpe], axis=-1).reshape(T0, MLA_HEADS * 2 * LANE)
        kc = jnp.concatenate([kvh[:, :, :MLA_NOPE], jnp.broadcast_to(kpe[:, None, :], (T0, MLA_HEADS, LANE))],
                             axis=-1).reshape(T0, MLA_HEADS * 2 * LANE)
        vc = kvh[:, :, MLA_NOPE:].reshape(T0, MLA_HEADS * MLA_V)
        cfg_c = _AttnCfg(MLA_HEADS, MLA_HEADS, 2 * LANE, MLA_V, sc_mla, tq_full, tk_all)
        qc_x, qc_c = rows(qc)
        oc = landed(_make_attention(t + "attn_c", cfg_c)(qc_x, kc, vc, None, None, gin(l, ("ffn_w_gate",)),
                                                         tok(l, ("ffn_w_gate",))), l, ("ffn_w_gate",), l, ("ffn_w_gate",))
        qd = _make_rmsnorm(t + "rms_dq")(qd, p["gqa_q_norm"][l][None, :])
        kd = _make_rmsnorm(t + "rms_dk")(kd, p["gqa_k_norm"][l][None, :])
        qd = _make_rope(t + "rope_dq", 32)(qd, *rope128)
        kd = _make_rope(t + "rope_dk", 32)(kd, *rope128)
        cfg_d = _AttnCfg(GQA_HEADS, GQA_KV_HEADS, 128, 128, sc, tq_full, tk_all)
        qd_x, qd_c = rows(qd)
        od = landed(_make_attention(t + "attn_d", cfg_d)(qd_x, kd, vd, None, None, gin(l, ("ffn_w_up",)),
                                                         tok(l, ("ffn_w_up",))), l, ("ffn_w_up",), l, ("ffn_w_up",))
        mix = jnp.concatenate([oa, ob, oc, od], axis=1)
        if not last:
            full_c = lambda H, Hkv, dqk, s, sink: _AttnCfg(H, Hkv, dqk, 128, s, tctx, tctx, has_sink=sink)
            ctx_att = lambda name, cfg, q_, k_, v_, s_: _make_attention(t + name, cfg)(
                q_, k_[S:], v_[S:], None, s_, (), ())[0]
            oa_c = ctx_att("ctx_a", full_c(4, 4, 128, sc, False), qa_c, ka, va, None)
            ob_c = ctx_att("ctx_b", full_c(4, 2, 128, sc, True), qb_c, kb, vb, snk)
            oc_c = ctx_att("ctx_c", full_c(4, 4, 256, sc_mla, False), qc_c, kc, vc, None)
            od_c = ctx_att("ctx_d", full_c(4, 2, 128, sc, False), qd_c, kd, vd, None)
            mix = jnp.concatenate([mix, jnp.concatenate([oa_c, ob_c, oc_c, od_c], axis=1)], axis=0)
            res, nseg = xs, 2
        else:
            res, nseg = xs[:S], 1
        y = lin(mix, "w_out")[0]
        x1 = _make_resid_ln(t + "ln1", S, alpha)(res, y, _seg2(mx[2], mc[2], nseg), p["ln1_g"][l][None, :],
                                                 p["ln1_b"][l][None, :])
        h2 = _make_modulate(t + "mod2", S)(x1, _seg2(mx[3], mc[3], nseg), _seg2(mx[4], mc[4], nseg))
        gp = lin(h2, "ffn_w_gate")[0]
        up = lin(h2, "ffn_w_up")[0]
        z = landed(_make_conv_gate(t + "conv", S)(gp, up, p["ffn_conv_w"][l], p["ffn_conv_b"][l][None, :],
                                                  gin(l + 1, nxt[1:]), tok(l + 1, nxt[1:])),
                   l + 1, nxt[1:], l + 1, nxt[1:])
        f = landed(lin(z, "ffn_w_down", gin(l + 1, nxt[:1]), tok(l + 1, nxt[:1])), l + 1, nxt[:1], l + 1, nxt[:1])
        xs = _make_resid_ln(t + "ln2", S, alpha)(x1, f, _seg2(mx[5], mc[5], nseg), p["ln2_g"][l][None, :],
                                                 p["ln2_b"][l][None, :])
    return _make_loss_head("loss_head")(xs, tgt)


def _adamw(w, parts, m, v, name):
    R, C = w.shape
    P = parts.shape[0]
    c1 = 1.0 - ADAM_B1 ** ADAM_STEP
    c2 = 1.0 - ADAM_B2 ** ADAM_STEP
    per_row = C * (4 * 14 + 2 * P * parts.dtype.itemsize)
    cands = _divisors(R, 8, R)
    fitting = [d for d in cands if d * per_row <= 24 * 1024 * 1024]
    tr = max(fitting) if fitting else min(cands)

    def body(w_ref, p_ref, m_ref, v_ref, g_ref, d_ref, nm_ref, nv_ref):
        g = p_ref[0].astype(F32)
        for k in range(1, P):
            g = g + p_ref[k].astype(F32)
        mm = ADAM_B1 * m_ref[...] + (1.0 - ADAM_B1) * g
        vv = ADAM_B2 * v_ref[...] + (1.0 - ADAM_B2) * (g * g)
        g_ref[...] = g
        nm_ref[...] = mm
        nv_ref[...] = vv
        d_ref[...] = -ADAM_LR * ((mm / c1) / (jnp.sqrt(vv / c2) + ADAM_EPS) + ADAM_WD * w_ref[...])

    blk = pl.BlockSpec((tr, C), lambda i: (i, 0))
    pblk = pl.BlockSpec((P, tr, C), lambda i: (0, i, 0))
    sh = jax.ShapeDtypeStruct((R, C), F32)
    return pl.pallas_call(body, name=name, grid=(R // tr,), in_specs=[blk, pblk, blk, blk],
                          out_specs=[blk, blk, blk, blk], out_shape=[sh, sh, sh, sh],
                          compiler_params=_cparams(1))(w, parts, m, v)


def _adamw_nd(w, parts, m, v, name):
    shape = w.shape
    C = shape[-1]
    R = int(np.prod(shape[:-1])) if len(shape) > 1 else 1
    outs = _adamw(w.reshape(R, C), parts.reshape(parts.shape[0], R, C), m.reshape(R, C), v.reshape(R, C), name)
    return [o.reshape(shape) for o in outs]


def _silu(v):
    return v * (1.0 / (1.0 + jnp.exp(-v)))


def _ada_rows(c_all, c_ctx):
    return jnp.concatenate([c_all, jnp.broadcast_to(c_ctx[None, :], (N_DEV, c_ctx.shape[0]))], axis=0)


def _silu_rows(rows, name):
    def body(r_ref, o_ref):
        o_ref[...] = _silu(r_ref[...])

    return pl.pallas_call(body, name=name, out_shape=jax.ShapeDtypeStruct(rows.shape, F32))(rows)


_BIG_COL = ("w_in", "mla_w_uq", "mla_w_ukv", "ffn_w_gate", "ffn_w_up")
_BIG_ROW = ("w_out", "ffn_w_down")
_SMALL = ("c_ctx", "na_rpb", "swa_sink", "mla_q_norm", "mla_kv_norm", "gqa_q_norm", "gqa_k_norm",
          "ln1_g", "ln1_b", "ffn_conv_b", "ln2_g", "ln2_b")
_NAMES = ("c_ctx", "w_ada", "b_ada", "w_in", "na_rpb", "swa_sink", "mla_q_norm", "mla_kv_norm", "mla_w_uq",
          "mla_w_ukv", "gqa_q_norm", "gqa_k_norm", "w_out", "ln1_g", "ln1_b", "ffn_w_gate", "ffn_w_up",
          "ffn_conv_w", "ffn_conv_b", "ffn_w_down", "ln2_g", "ln2_b")


def _full_from_cols(g, pad_in=False):
    _, L, K, n = g.shape
    full = jnp.transpose(g, (1, 2, 0, 3)).reshape(L, K, N_DEV * n)
    if pad_in:
        full = jnp.concatenate([full[:, :, :KPE_END], jnp.zeros((L, K, 64), full.dtype), full[:, :, KPE_END:]], axis=-1)
    return full


def _pack_small(tree, extra=None):
    flat = [tree[n].reshape(-1) for n in _SMALL]
    flat.append(jnp.zeros((1,), F32) if extra is None else extra.reshape(1))
    v = jnp.concatenate(flat)
    n = v.shape[0]
    padded = -(-n // 1024) * 1024
    return jnp.pad(v, (0, padded - n)).reshape(padded // LANE, LANE)


def _unpack_small(mat, like):
    v = mat.reshape(-1)
    out, o = {}, 0
    for n in _SMALL:
        k = int(np.prod(like[n].shape))
        out[n] = v[o:o + k].reshape(like[n].shape)
        o += k
    return out, v[o]


def kernel(x, c, ctx, c_ctx, w_ada, b_ada, w_in, na_rpb, swa_sink, mla_q_norm, mla_kv_norm, mla_w_uq, mla_w_ukv, gqa_q_norm, gqa_k_norm, w_out, ln1_g, ln1_b, ffn_w_gate, ffn_w_up, ffn_conv_w, ffn_conv_b, ffn_w_down, ln2_g, ln2_b, loss_target, m_c_ctx, m_w_ada, m_b_ada, m_w_in, m_na_rpb, m_swa_sink, m_mla_q_norm, m_mla_kv_norm, m_mla_w_uq, m_mla_w_ukv, m_gqa_q_norm, m_gqa_k_norm, m_w_out, m_ln1_g, m_ln1_b, m_ffn_w_gate, m_ffn_w_up, m_ffn_conv_w, m_ffn_conv_b, m_ffn_w_down, m_ln2_g, m_ln2_b, v_c_ctx, v_w_ada, v_b_ada, v_w_in, v_na_rpb, v_swa_sink, v_mla_q_norm, v_mla_kv_norm, v_mla_w_uq, v_mla_w_ukv, v_gqa_q_norm, v_gqa_k_norm, v_w_out, v_ln1_g, v_ln1_b, v_ffn_w_gate, v_ffn_w_up, v_ffn_conv_w, v_ffn_conv_b, v_ffn_w_down, v_ln2_g, v_ln2_b):
    W = dict(c_ctx=c_ctx, w_ada=w_ada, b_ada=b_ada, w_in=w_in, na_rpb=na_rpb, swa_sink=swa_sink,
             mla_q_norm=mla_q_norm, mla_kv_norm=mla_kv_norm, mla_w_uq=mla_w_uq, mla_w_ukv=mla_w_ukv,
             gqa_q_norm=gqa_q_norm, gqa_k_norm=gqa_k_norm, w_out=w_out, ln1_g=ln1_g, ln1_b=ln1_b,
             ffn_w_gate=ffn_w_gate, ffn_w_up=ffn_w_up, ffn_conv_w=ffn_conv_w, ffn_conv_b=ffn_conv_b,
             ffn_w_down=ffn_w_down, ln2_g=ln2_g, ln2_b=ln2_b)
    M = dict(c_ctx=m_c_ctx, w_ada=m_w_ada, b_ada=m_b_ada, w_in=m_w_in, na_rpb=m_na_rpb, swa_sink=m_swa_sink,
             mla_q_norm=m_mla_q_norm, mla_kv_norm=m_mla_kv_norm, mla_w_uq=m_mla_w_uq, mla_w_ukv=m_mla_w_ukv,
             gqa_q_norm=m_gqa_q_norm, gqa_k_norm=m_gqa_k_norm, w_out=m_w_out, ln1_g=m_ln1_g, ln1_b=m_ln1_b,
             ffn_w_gate=m_ffn_w_gate, ffn_w_up=m_ffn_w_up, ffn_conv_w=m_ffn_conv_w, ffn_conv_b=m_ffn_conv_b,
             ffn_w_down=m_ffn_w_down, ln2_g=m_ln2_g, ln2_b=m_ln2_b)
    V = dict(c_ctx=v_c_ctx, w_ada=v_w_ada, b_ada=v_b_ada, w_in=v_w_in, na_rpb=v_na_rpb, swa_sink=v_swa_sink,
             mla_q_norm=v_mla_q_norm, mla_kv_norm=v_mla_kv_norm, mla_w_uq=v_mla_w_uq, mla_w_ukv=v_mla_w_ukv,
             gqa_q_norm=v_gqa_q_norm, gqa_k_norm=v_gqa_k_norm, w_out=v_w_out, ln1_g=v_ln1_g, ln1_b=v_ln1_b,
             ffn_w_gate=v_ffn_w_gate, ffn_w_up=v_ffn_w_up, ffn_conv_w=v_ffn_conv_w, ffn_conv_b=v_ffn_conv_b,
             ffn_w_down=v_ffn_w_down, ln2_g=v_ln2_g, ln2_b=v_ln2_b)
    L, D, n_ada = w_ada.shape
    me = 4 * lax.axis_index("x") + 2 * lax.axis_index("y") + lax.axis_index("c")
    xs, ctxs, tgt = x[0], ctx[0], loss_target[0]

    c_all = _exchange(c, False, "gather_c").reshape(N_DEV, D)
    a_rows = _silu_rows(_ada_rows(c_all, c_ctx), "ada_silu")
    b_mine = lax.dynamic_slice(b_ada, (0, me * n_ada), (L, n_ada))
    mod_mine = jnp.stack([_matmul(a_rows, w_ada[l], "nn", "ada_fwd_l%d" % l) + b_mine[l][None, :] for l in range(L)])
    mod_all = _exchange(mod_mine, False, "gather_mod")
    mod_all = jnp.transpose(mod_all, (1, 2, 0, 3)).reshape(L, 2 * N_DEV, N_DEV * n_ada)
    modx = lax.dynamic_slice(mod_all, (0, me, 0), (L, 1, 6 * D)).reshape(L, 6, D)
    modc = mod_all[:, N_DEV].reshape(L, 6, D)

    big = _BIG_COL + _BIG_ROW
    shards = [{n: W[n][l].astype(BF16) for n in big} for l in range(L)]
    full = [{n: _full_weight(n, _exchange(shards[0][n], False, "gather_l0_" + n)) for n in _IN_GROUP}]
    full += [{} for _ in range(1, L)]
    conv_w_full = _full_from_cols(_exchange(ffn_conv_w, False, "gather_conv_w"))

    toks = [{n: jnp.zeros((N_DEV,) + shards[l][n].shape, BF16) for n in big} for l in range(L)]
    small = {n: W[n] for n in _SMALL if n != "c_ctx"}
    small["ffn_conv_w"] = conv_w_full

    def loss_fn(xv, mxv, mcv, sm, tk):
        return _local_loss(xv, ctxs, tgt, mxv, mcv, sm, full, shards, tk)

    loss_local, (grad_x, dmodx, dmodc, dsmall, dtoks) = jax.value_and_grad(loss_fn, argnums=(0, 1, 2, 3, 4))(
        xs, modx, modc, small, toks)

    dmods = _exchange(jnp.stack([dmodx.reshape(L, 6 * D), dmodc.reshape(L, 6 * D)]), False, "gather_dmod")
    dm_rows = jnp.concatenate([dmods[:, 0], dmods[:, 1]], axis=0)
    dm_rows = jnp.transpose(dm_rows, (1, 0, 2))
    dm_mine = lax.dynamic_slice(dm_rows, (0, 0, me * n_ada), (L, 2 * N_DEV, n_ada))
    g_w_ada = jnp.stack([_matmul(a_rows, dm_mine[l], "tn", "ada_dw_l%d" % l) for l in range(L)])
    d_rows = sum(_matmul(dm_mine[l], w_ada[l], "nt", "ada_da_l%d" % l) for l in range(L))
    sig = 1.0 / (1.0 + jnp.exp(-c_ctx))
    d_c_ctx_part = jnp.sum(d_rows[N_DEV:], axis=0) * (sig * (1.0 + c_ctx * (1.0 - sig)))

    outs = {}
    for n in _IN_GROUP:
        dtoks[0][n] = _exchange(dtoks[0][n], True, "scatter_l0_" + n)
    for n in big:
        parts = jnp.stack([dtoks[l][n] for l in range(L)], axis=1)
        outs[n] = _adamw_nd(W[n], parts, M[n], V[n], "adamw_" + n)
    dcw = dsmall.pop("ffn_conv_w")
    cw_parts = _exchange(jnp.transpose(dcw.reshape(L, 3, N_DEV, -1), (2, 0, 1, 3)), True, "scatter_conv_w")
    outs["ffn_conv_w"] = _adamw_nd(ffn_conv_w, cw_parts, M["ffn_conv_w"], V["ffn_conv_w"], "adamw_conv_w")
    outs["w_ada"] = _adamw_nd(w_ada, g_w_ada[None], M["w_ada"], V["w_ada"], "adamw_w_ada")
    outs["b_ada"] = _adamw_nd(b_ada, jnp.transpose(dm_rows, (1, 0, 2)), M["b_ada"], V["b_ada"], "adamw_b_ada")

    dsmall["c_ctx"] = d_c_ctx_part
    small_parts = _exchange(_pack_small(dsmall, extra=loss_local), False, "gather_small")
    w_small = _pack_small({n: W[n] for n in _SMALL})
    g_s, d_s, nm_s, nv_s = _adamw(w_small, small_parts, _pack_small({n: M[n] for n in _SMALL}),
                                  _pack_small({n: V[n] for n in _SMALL}), "adamw_small")
    like = {n: W[n] for n in _SMALL}
    g_small, loss = _unpack_small(g_s, like)
    unpacked = [g_small, _unpack_small(d_s, like)[0], _unpack_small(nm_s, like)[0], _unpack_small(nv_s, like)[0]]
    for n in _SMALL:
        outs[n] = [u[n] for u in unpacked]

    result = [loss, grad_x[None]]
    for k in range(4):
        result += [outs[n][k] for n in _NAMES]
    return tuple(result)
```

```python
import functools
import math

import numpy as np
import jax
import jax.numpy as jnp
from jax import lax
from jax.experimental import pallas as pl
from jax.experimental.pallas import tpu as pltpu

F32 = jnp.float32
BF16 = jnp.bfloat16
_MXU = jnp.bfloat16
_ACT = jnp.bfloat16

N_DEV = 8
GRID_W = 64
HEAD_DIM = 128
NA_HEADS, NA_WIN_R, NA_WIN_C = 4, 8, 16
SWA_HEADS, SWA_KV_HEADS, SWA_WINDOW = 4, 2, 128
MLA_HEADS, MLA_Q_LORA, MLA_KV_LORA, MLA_NOPE, MLA_ROPE, MLA_V = 4, 384, 128, 128, 64, 128
GQA_HEADS, GQA_KV_HEADS = 4, 2
ROPE_THETA = 10000.0
EPS = 1e-6
NEG = -1e30
IN_SIZES = (512, 512, 512, 512, 256, 256, MLA_Q_LORA, MLA_KV_LORA, MLA_ROPE, 512, 256, 256)
IN_COLS = sum(IN_SIZES)
KPE_END = sum(IN_SIZES[:9])
IN_COLS_PAD = IN_COLS + 64
ADAM_LR, ADAM_B1, ADAM_B2, ADAM_EPS, ADAM_WD, ADAM_STEP = 0.001, 0.9, 0.999, 1e-08, 0.01, 10

LANE = 128
ROW_TILE = 256
ATT_BLOCK = 128
VMEM_LIMIT = 56 * 1024 * 1024
MM_BUDGET = 36 * 1024 * 1024
HBM_BPS = 3.0e12
MXU_FLOPS = 9.0e14
VMEM_BPS = 4.0e12
STEP_S = 0.4e-6


def _cparams(n_axes):
    return pltpu.CompilerParams(dimension_semantics=("arbitrary",) * n_axes, vmem_limit_bytes=VMEM_LIMIT)


def _exchange_copies(in_ref, out_ref, send_sems, recv_sems, local_sem, scatter, rows=None):
    x, y, c = lax.axis_index("x"), lax.axis_index("y"), lax.axis_index("c")
    me = 4 * x + 2 * y + c
    cut = (lambda ref: ref) if rows is None else (lambda ref: ref.at[rows])
    copies = [pltpu.make_async_copy(cut(in_ref.at[me] if scatter else in_ref), cut(out_ref.at[me]), local_sem)]
    for k in range(1, N_DEV):
        px, py, pc = (x + (k >> 2)) % 2, (y + ((k >> 1) & 1)) % 2, (c + (k & 1)) % 2
        peer = 4 * px + 2 * py + pc
        copies.append(pltpu.make_async_remote_copy(
            src_ref=cut(in_ref.at[peer] if scatter else in_ref), dst_ref=cut(out_ref.at[me]),
            send_sem=send_sems.at[k - 1], recv_sem=recv_sems.at[k - 1],
            device_id=(px, py, pc), device_id_type=pl.DeviceIdType.MESH))
    return copies


PIECE_ROWS = 16
MAX_PIECES = 32


def _pieces(rows, steps):
    if rows % PIECE_ROWS:
        return 1
    units = rows // PIECE_ROWS
    return max(d for d in range(1, min(MAX_PIECES, steps) + 1) if units % d == 0)


def _exchange_out_shape(arr, scatter):
    return jax.ShapeDtypeStruct(arr.shape if scatter else (N_DEV,) + arr.shape, arr.dtype)


_EXCHANGE_SEMS = (pltpu.SemaphoreType.DMA((N_DEV - 1,)), pltpu.SemaphoreType.DMA((N_DEV - 1,)),
                  pltpu.SemaphoreType.DMA)


def _exchange(inp, scatter, name):
    def body(in_ref, out_ref, send_sems, recv_sems, local_sem):
        copies = _exchange_copies(in_ref, out_ref, send_sems, recv_sems, local_sem, scatter)
        for cp in copies:
            cp.start()
        for cp in copies:
            cp.wait()

    hbm = pl.BlockSpec(memory_space=pl.ANY)
    return pl.pallas_call(
        body, name=name, in_specs=[hbm], out_specs=hbm, out_shape=_exchange_out_shape(inp, scatter),
        scratch_shapes=list(_EXCHANGE_SEMS), compiler_params=pltpu.CompilerParams(has_side_effects=True),
    )(inp)


def _pcall(body, *, name, grid, in_specs, out_specs, out_shape, args, scratch_shapes=(), riders=()):
    in_specs, out_specs, out_shape = list(in_specs), list(out_specs), list(out_shape)
    scratch, args = list(scratch_shapes), list(args)
    n_in, n_out, n_scr, n_r = len(in_specs), len(out_specs), len(scratch), len(riders)
    hbm = pl.BlockSpec(memory_space=pl.ANY)
    for arr, scatter in riders:
        in_specs.append(hbm)
        args.append(arr)
        out_specs.append(hbm)
        out_shape.append(_exchange_out_shape(arr, scatter))
        scratch += list(_EXCHANGE_SEMS)

    def wrapped(*refs):
        ins, r_in = refs[:n_in], refs[n_in:n_in + n_r]
        o0 = n_in + n_r
        outs, r_out = refs[o0:o0 + n_out], refs[o0 + n_out:o0 + n_out + n_r]
        s0 = o0 + n_out + n_r
        scr, sems = refs[s0:s0 + n_scr], refs[s0 + n_scr:]

        def copies(r, rows=None):
            return _exchange_copies(r_in[r], r_out[r], sems[3 * r], sems[3 * r + 1], sems[3 * r + 2], riders[r][1],
                                    rows)

        if n_r:
            steps = int(np.prod(grid))
            step = functools.reduce(lambda u, w: u * w[1] + w[0],
                                    [(pl.program_id(a), n) for a, n in enumerate(grid)], 0)
            for r, (arr, scatter) in enumerate(riders):
                block_rows = arr.shape[1] if scatter else arr.shape[0]
                n_pieces = _pieces(block_rows, steps)
                every, piece = steps // n_pieces, block_rows // n_pieces

                @pl.when((step % every == 0) & (step // every < n_pieces))
                def _(r=r, every=every, piece=piece, whole=n_pieces == 1):
                    rows = None if whole else pl.ds(pl.multiple_of((step // every) * piece, PIECE_ROWS), piece)
                    for cp in copies(r, rows):
                        cp.start()

        body(*ins, *outs, *scr)

        if n_r:
            @pl.when(step == steps - 1)
            def _():
                for r in range(n_r):
                    for cp in copies(r):
                        cp.wait()

    params = pltpu.CompilerParams(dimension_semantics=("arbitrary",) * len(grid), vmem_limit_bytes=VMEM_LIMIT,
                                  has_side_effects=bool(riders))
    return list(pl.pallas_call(wrapped, name=name, grid=grid, in_specs=in_specs, out_specs=out_specs,
                               out_shape=out_shape, scratch_shapes=scratch, compiler_params=params)(*args))


def _riders(arrays, scatter):
    return [(a, scatter) for a in arrays]


def _nones(arrays):
    return tuple(None for _ in arrays)


def _divisors(n, step, cap):
    out = [d for d in range(step, min(n, cap) + 1, step) if n % d == 0]
    if n <= cap and n not in out:
        out.append(n)
    return out


def _mm_tiles(M, N, K, sa, sb, m_step, k_step):
    best, best_cost = None, None
    for tm in _divisors(M, m_step, 2304):
        for tn in _divisors(N, LANE, 2304):
            for tk in _divisors(K, k_step, 2304):
                vm = 2 * (tm * tk * sa + tk * tn * sb) + 3 * tm * tn * 4
                if vm > MM_BUDGET:
                    continue
                traffic = M * K * sa * (N // tn) + K * N * sb * (M // tm) + M * N * 4
                steps = (M // tm) * (N // tn) * (K // tk)
                passes = 1 if tk == K else 3
                busy = steps * (2 * tm * tn * tk / MXU_FLOPS + passes * tm * tn * 4 / VMEM_BPS)
                cost = max(traffic / HBM_BPS, busy) + steps * STEP_S
                if m_step == LANE:
                    cost = traffic / HBM_BPS + steps * STEP_S
                if tm % 256 or tn % 256:
                    cost *= 1.05
                if best_cost is None or cost < best_cost:
                    best, best_cost = (tm, tn, tk), cost
    assert best is not None, (M, N, K)
    return best


def _matmul(a, b, mode, name, riders=(), out_dtype=F32):
    if mode == "nn":
        (M, K), (K2, N) = a.shape, b.shape
    elif mode == "nt":
        (M, K), (N, K2) = a.shape, b.shape
    else:
        (K, M), (K2, N) = a.shape, b.shape
    assert K == K2, (a.shape, b.shape, mode)
    tm, tn, tk = _mm_tiles(M, N, K, a.dtype.itemsize, b.dtype.itemsize,
                           LANE if mode == "tn" else 8, 8 if mode == "tn" else LANE)
    nk = K // tk
    dims = {"nn": (((1,), (0,)), ((), ())), "nt": (((1,), (1,)), ((), ())), "tn": (((0,), (0,)), ((), ()))}[mode]

    own_acc = nk > 1 and out_dtype != F32

    def body(a_ref, b_ref, o_ref, *scr):
        prod = lax.dot_general(a_ref[...].astype(_MXU), b_ref[...].astype(_MXU), dims, preferred_element_type=F32)
        if nk == 1:
            o_ref[...] = prod.astype(out_dtype)
        else:
            acc_ref = scr[0] if own_acc else o_ref
            k = pl.program_id(2)

            @pl.when(k == 0)
            def _():
                acc_ref[...] = prod

            @pl.when(k > 0)
            def _():
                acc_ref[...] += prod

            if own_acc:
                @pl.when(k == nk - 1)
                def _():
                    o_ref[...] = acc_ref[...].astype(out_dtype)

    if mode == "nn":
        a_spec = pl.BlockSpec((tm, tk), lambda i, j, k: (i, k))
        b_spec = pl.BlockSpec((tk, tn), lambda i, j, k: (k, j))
    elif mode == "nt":
        a_spec = pl.BlockSpec((tm, tk), lambda i, j, k: (i, k))
        b_spec = pl.BlockSpec((tn, tk), lambda i, j, k: (j, k))
    else:
        a_spec = pl.BlockSpec((tk, tm), lambda i, j, k: (k, i))
        b_spec = pl.BlockSpec((tk, tn), lambda i, j, k: (k, j))
    outs = _pcall(body, name=name, grid=(M // tm, N // tn, nk), in_specs=[a_spec, b_spec],
                  out_specs=[pl.BlockSpec((tm, tn), lambda i, j, k: (i, j))],
                  out_shape=[jax.ShapeDtypeStruct((M, N), out_dtype)], args=[a, b],
                  scratch_shapes=[pltpu.VMEM((tm, tn), F32)] if own_acc else [], riders=riders)
    return outs if riders else outs[0]


def _make_linear(tag, to_parts, out_dtype=F32):
    def run(a, w, gin, tok):
        if gin:
            y, *got = _matmul(a, w, "nn", tag + "_fwd", riders=_riders(gin, False), out_dtype=out_dtype)
        else:
            y, got = _matmul(a, w, "nn", tag + "_fwd", out_dtype=out_dtype), []
        return y, tuple(got), tok

    @jax.custom_vjp
    def linear(a, w, sink, gin, tok):
        return run(a, w, gin, tok)

    def fwd(a, w, sink, gin, tok):
        return run(a, w, gin, tok), (a, w, gin)

    def bwd(res, cts):
        a, w, gin = res
        dy, _, dtok = cts
        dw = _matmul(a, dy, "tn", tag + "_bwd_dw")
        if dtok:
            da, *recv = _matmul(dy, w, "nt", tag + "_bwd_da", riders=_riders(dtok, True), out_dtype=a.dtype)
        else:
            da, recv = _matmul(dy, w, "nt", tag + "_bwd_da", out_dtype=a.dtype), []
        return da, jnp.zeros_like(w), to_parts(dw), _nones(gin), tuple(recv)

    linear.defvjp(fwd, bwd)
    return linear


def _seg_map(n_x_tiles):
    return lambda i: (jnp.where(i >= n_x_tiles, 1, 0), 0, 0)


def _ln_stats(x):
    mu = jnp.mean(x, axis=-1, keepdims=True)
    xc = x - mu
    var = jnp.mean(xc * xc, axis=-1, keepdims=True)
    rstd = lax.rsqrt(var + EPS)
    return xc * rstd, rstd


def _make_modulate(tag, n_x):
    def call_fwd(x, shift, scale):
        T, D = x.shape
        nxt = n_x // ROW_TILE

        def body(x_ref, sh_ref, sc_ref, o_ref):
            xhat, _ = _ln_stats(x_ref[...])
            o_ref[...] = xhat * (1.0 + sc_ref[0]) + sh_ref[0]

        row = pl.BlockSpec((ROW_TILE, D), lambda i: (i, 0))
        seg = pl.BlockSpec((1, 1, D), _seg_map(nxt))
        return pl.pallas_call(body, name=tag + "_fwd", grid=(T // ROW_TILE,), in_specs=[row, seg, seg],
                              out_specs=row, out_shape=jax.ShapeDtypeStruct((T, D), F32),
                              compiler_params=_cparams(1))(x, shift, scale)

    def call_bwd(x, scale, dh):
        T, D = x.shape
        nxt = n_x // ROW_TILE
        nseg = scale.shape[0]

        def body(x_ref, sc_ref, dh_ref, dx_ref, dsh_ref, dsc_ref):
            i = pl.program_id(0)

            @pl.when((i == 0) | (i == nxt))
            def _():
                dsh_ref[...] = jnp.zeros_like(dsh_ref)
                dsc_ref[...] = jnp.zeros_like(dsc_ref)

            xhat, rstd = _ln_stats(x_ref[...])
            dh = dh_ref[...]
            dsh_ref[0] += jnp.sum(dh, axis=0, keepdims=True)
            dsc_ref[0] += jnp.sum(dh * xhat, axis=0, keepdims=True)
            dxh = dh * (1.0 + sc_ref[0])
            m1 = jnp.mean(dxh, axis=-1, keepdims=True)
            m2 = jnp.mean(dxh * xhat, axis=-1, keepdims=True)
            dx_ref[...] = rstd * (dxh - m1 - xhat * m2)

        row = pl.BlockSpec((ROW_TILE, D), lambda i: (i, 0))
        seg = pl.BlockSpec((1, 1, D), _seg_map(nxt))
        segshape = jax.ShapeDtypeStruct((nseg, 1, D), F32)
        return pl.pallas_call(body, name=tag + "_bwd", grid=(T // ROW_TILE,), in_specs=[row, seg, row],
                              out_specs=[row, seg, seg],
                              out_shape=[jax.ShapeDtypeStruct((T, D), F32), segshape, segshape],
                              compiler_params=_cparams(1))(x, scale, dh)

    @jax.custom_vjp
    def modulate(x, shift, scale):
        return call_fwd(x, shift, scale)

    def fwd(x, shift, scale):
        return call_fwd(x, shift, scale), (x, scale)

    def bwd(res, dh):
        x, scale = res
        return tuple(call_bwd(x, scale, dh))

    modulate.defvjp(fwd, bwd)
    return modulate


def _make_resid_ln(tag, n_x, alpha):
    def call_fwd(x, y, gate, g, b):
        T, D = x.shape
        nxt = n_x // ROW_TILE

        def body(x_ref, y_ref, gt_ref, g_ref, b_ref, o_ref):
            u = alpha * x_ref[...] + gt_ref[0] * y_ref[...]
            uhat, _ = _ln_stats(u)
            o_ref[...] = uhat * g_ref[...] + b_ref[...]

        row = pl.BlockSpec((ROW_TILE, D), lambda i: (i, 0))
        seg = pl.BlockSpec((1, 1, D), _seg_map(nxt))
        vec = pl.BlockSpec((1, D), lambda i: (0, 0))
        return pl.pallas_call(body, name=tag + "_fwd", grid=(T // ROW_TILE,), in_specs=[row, row, seg, vec, vec],
                              out_specs=row, out_shape=jax.ShapeDtypeStruct((T, D), F32),
                              compiler_params=_cparams(1))(x, y, gate, g, b)

    def call_bwd(x, y, gate, g, do):
        T, D = x.shape
        nxt = n_x // ROW_TILE
        nseg = gate.shape[0]

        def body(x_ref, y_ref, gt_ref, g_ref, do_ref, dx_ref, dy_ref, dgt_ref, dg_ref, db_ref):
            i = pl.program_id(0)

            @pl.when(i == 0)
            def _():
                dg_ref[...] = jnp.zeros_like(dg_ref)
                db_ref[...] = jnp.zeros_like(db_ref)

            @pl.when((i == 0) | (i == nxt))
            def _():
                dgt_ref[...] = jnp.zeros_like(dgt_ref)

            y = y_ref[...]
            gate_v = gt_ref[0]
            uhat, rstd = _ln_stats(alpha * x_ref[...] + gate_v * y)
            do = do_ref[...]
            dg_ref[...] += jnp.sum(do * uhat, axis=0, keepdims=True)
            db_ref[...] += jnp.sum(do, axis=0, keepdims=True)
            duh = do * g_ref[...]
            m1 = jnp.mean(duh, axis=-1, keepdims=True)
            m2 = jnp.mean(duh * uhat, axis=-1, keepdims=True)
            du = rstd * (duh - m1 - uhat * m2)
            dx_ref[...] = alpha * du
            dy_ref[...] = gate_v * du
            dgt_ref[0] += jnp.sum(du * y, axis=0, keepdims=True)

        row = pl.BlockSpec((ROW_TILE, D), lambda i: (i, 0))
        seg = pl.BlockSpec((1, 1, D), _seg_map(nxt))
        vec = pl.BlockSpec((1, D), lambda i: (0, 0))
        rs = jax.ShapeDtypeStruct((T, D), F32)
        vs = jax.ShapeDtypeStruct((1, D), F32)
        return pl.pallas_call(body, name=tag + "_bwd", grid=(T // ROW_TILE,), in_specs=[row, row, seg, vec, row],
                              out_specs=[row, row, seg, vec, vec],
                              out_shape=[rs, rs, jax.ShapeDtypeStruct((nseg, 1, D), F32), vs, vs],
                              compiler_params=_cparams(1))(x, y, gate, g, do)

    @jax.custom_vjp
    def resid_ln(x, y, gate, g, b):
        return call_fwd(x, y, gate, g, b)

    def fwd(x, y, gate, g, b):
        return call_fwd(x, y, gate, g, b), (x, y, gate, g)

    def bwd(res, do):
        x, y, gate, g = res
        return tuple(call_bwd(x, y, gate, g, do))

    resid_ln.defvjp(fwd, bwd)
    return resid_ln


def _make_rmsnorm(tag):
    def call_fwd(x, g):
        T, W = x.shape
        gw = g.shape[1]

        def body(x_ref, g_ref, o_ref):
            x = x_ref[...]
            r = lax.rsqrt(jnp.mean(x * x, axis=-1, keepdims=True) + EPS)
            o_ref[...] = x * r * g_ref[...]

        blk = pl.BlockSpec((ROW_TILE, gw), lambda i, c: (i, c))
        vec = pl.BlockSpec((1, gw), lambda i, c: (0, 0))
        return pl.pallas_call(body, name=tag + "_fwd", grid=(T // ROW_TILE, W // gw), in_specs=[blk, vec],
                              out_specs=blk, out_shape=jax.ShapeDtypeStruct((T, W), F32),
                              compiler_params=_cparams(2))(x, g)

    def call_bwd(x, g, dy):
        T, W = x.shape
        gw = g.shape[1]

        def body(x_ref, g_ref, dy_ref, dx_ref, dg_ref):
            @pl.when((pl.program_id(0) == 0) & (pl.program_id(1) == 0))
            def _():
                dg_ref[...] = jnp.zeros_like(dg_ref)

            x = x_ref[...]
            r = lax.rsqrt(jnp.mean(x * x, axis=-1, keepdims=True) + EPS)
            xn = x * r
            dy = dy_ref[...]
            dg_ref[...] += jnp.sum(dy * xn, axis=0, keepdims=True)
            dxn = dy * g_ref[...]
            dx_ref[...] = r * (dxn - xn * jnp.mean(dxn * xn, axis=-1, keepdims=True))

        blk = pl.BlockSpec((ROW_TILE, gw), lambda i, c: (i, c))
        vec = pl.BlockSpec((1, gw), lambda i, c: (0, 0))
        return pl.pallas_call(body, name=tag + "_bwd", grid=(T // ROW_TILE, W // gw), in_specs=[blk, vec, blk],
                              out_specs=[blk, vec],
                              out_shape=[jax.ShapeDtypeStruct((T, W), F32), jax.ShapeDtypeStruct((1, gw), F32)],
                              compiler_params=_cparams(2))(x, g, dy)

    @jax.custom_vjp
    def rmsnorm(x, g):
        return call_fwd(x, g)

    def fwd(x, g):
        return call_fwd(x, g), (x, g)

    def bwd(res, dy):
        x, g = res
        return tuple(call_bwd(x, g, dy))

    rmsnorm.defvjp(fwd, bwd)
    return rmsnorm


def _rope_tables(n_x, n_all, width, half):
    t = np.arange(n_all)
    row, col = t // GRID_W, t % GRID_W
    lane = np.arange(LANE)
    inside = lane < width
    axis_is_col = (lane // (2 * half)) % 2 == 1
    f = (lane % (2 * half)) % half
    inv_freq = ROPE_THETA ** (-(f.astype(np.float64)) / half)
    pos = np.where(axis_is_col[None, :], col[:, None], row[:, None]).astype(np.float64)
    ang = (pos.astype(np.float32) * inv_freq.astype(np.float32)[None, :]).astype(np.float32)
    live = inside[None, :] & (t < n_x)[:, None]
    cos = np.where(live, np.cos(ang), 1.0).astype(np.float32)
    sin = np.where(live, np.sin(ang), 0.0).astype(np.float32)
    first = (lane % (2 * half)) < half
    s_lo = np.where(first[None, :], -sin, 0.0).astype(np.float32)
    s_hi = np.where(first[None, :], 0.0, sin).astype(np.float32)
    return jnp.asarray(cos), jnp.asarray(s_lo), jnp.asarray(s_hi)


def _make_rope(tag, half):
    def call(x, cos, s_lo, s_hi, transpose, name):
        T, W = x.shape

        def body(x_ref, c_ref, lo_ref, hi_ref, o_ref):
            x = x_ref[...]
            if not transpose:
                o_ref[...] = (x * c_ref[...] + pltpu.roll(x, LANE - half, 1) * lo_ref[...]
                              + pltpu.roll(x, half, 1) * hi_ref[...])
            else:
                o_ref[...] = (x * c_ref[...] + pltpu.roll(x * lo_ref[...], half, 1)
                              + pltpu.roll(x * hi_ref[...], LANE - half, 1))

        blk = pl.BlockSpec((ROW_TILE, LANE), lambda i, c: (i, c))
        tab = pl.BlockSpec((ROW_TILE, LANE), lambda i, c: (i, 0))
        return pl.pallas_call(body, name=name, grid=(T // ROW_TILE, W // LANE), in_specs=[blk, tab, tab, tab],
                              out_specs=blk, out_shape=jax.ShapeDtypeStruct((T, W), F32),
                              compiler_params=_cparams(2))(x, cos, s_lo, s_hi)

    @jax.custom_vjp
    def rope(x, cos, s_lo, s_hi):
        return call(x, cos, s_lo, s_hi, False, tag + "_fwd")

    def fwd(x, cos, s_lo, s_hi):
        return call(x, cos, s_lo, s_hi, False, tag + "_fwd"), (cos, s_lo, s_hi)

    def bwd(res, dy):
        cos, s_lo, s_hi = res
        return call(dy, cos, s_lo, s_hi, True, tag + "_bwd"), None, None, None

    rope.defvjp(fwd, bwd)
    return rope


def _make_conv_gate(tag, n_x):
    def shifted(v, T):
        t = lax.broadcasted_iota(jnp.int32, v.shape, 0)
        prev = jnp.where((t == 0) | (t == n_x), 0.0, pltpu.roll(v, 1, 0))
        nxt = jnp.where((t == n_x - 1) | (t == T - 1), 0.0, pltpu.roll(v, T - 1, 0))
        return prev, nxt

    def call_fwd(a, u, cw, cb, riders):
        T, Fd = a.shape

        def body(a_ref, u_ref, cw_ref, cb_ref, z_ref):
            pre = a_ref[...].astype(F32)
            prev, nxt = shifted(pre, T)
            s = prev * cw_ref[0:1, :] + pre * cw_ref[1:2, :] + nxt * cw_ref[2:3, :] + cb_ref[...]
            z_ref[...] = (s * (1.0 / (1.0 + jnp.exp(-s))) * u_ref[...].astype(F32)).astype(z_ref.dtype)

        col = pl.BlockSpec((T, LANE), lambda j: (0, j))
        return _pcall(body, name=tag + "_fwd", grid=(Fd // LANE,),
                      in_specs=[col, col, pl.BlockSpec((3, LANE), lambda j: (0, j)),
                                pl.BlockSpec((1, LANE), lambda j: (0, j))],
                      out_specs=[col], out_shape=[jax.ShapeDtypeStruct((T, Fd), a.dtype)], args=[a, u, cw, cb],
                      riders=riders)

    def call_bwd(a, u, cw, cb, dz, riders):
        T, Fd = a.shape

        def body(a_ref, u_ref, cw_ref, cb_ref, dz_ref, da_ref, du_ref, dcw_ref, dcb_ref):
            pre = a_ref[...].astype(F32)
            prev, nxt = shifted(pre, T)
            s = prev * cw_ref[0:1, :] + pre * cw_ref[1:2, :] + nxt * cw_ref[2:3, :] + cb_ref[...]
            sig = 1.0 / (1.0 + jnp.exp(-s))
            dz = dz_ref[...].astype(F32)
            du_ref[...] = (dz * s * sig).astype(du_ref.dtype)
            ds = dz * u_ref[...].astype(F32) * (sig * (1.0 + s * (1.0 - sig)))
            ds_prev, ds_next = shifted(ds, T)
            da_ref[...] = (ds * cw_ref[1:2, :] + ds_next * cw_ref[0:1, :] + ds_prev * cw_ref[2:3, :]).astype(
                da_ref.dtype)
            dcw_ref[0:1, :] = jnp.sum(ds * prev, axis=0, keepdims=True)
            dcw_ref[1:2, :] = jnp.sum(ds * pre, axis=0, keepdims=True)
            dcw_ref[2:3, :] = jnp.sum(ds * nxt, axis=0, keepdims=True)
            dcb_ref[...] = jnp.sum(ds, axis=0, keepdims=True)

        col = pl.BlockSpec((T, LANE), lambda j: (0, j))
        w3 = pl.BlockSpec((3, LANE), lambda j: (0, j))
        w1 = pl.BlockSpec((1, LANE), lambda j: (0, j))
        big = jax.ShapeDtypeStruct((T, Fd), a.dtype)
        return _pcall(body, name=tag + "_bwd", grid=(Fd // LANE,), in_specs=[col, col, w3, w1, col],
                      out_specs=[col, col, w3, w1],
                      out_shape=[big, big, jax.ShapeDtypeStruct((3, Fd), F32), jax.ShapeDtypeStruct((1, Fd), F32)],
                      args=[a, u, cw, cb, dz], riders=riders)

    def run(a, u, cw, cb, gin, tok):
        z, *got = call_fwd(a, u, cw, cb, _riders(gin, False))
        return z, tuple(got), tok

    @jax.custom_vjp
    def conv_gate(a, u, cw, cb, gin, tok):
        return run(a, u, cw, cb, gin, tok)

    def fwd(a, u, cw, cb, gin, tok):
        return run(a, u, cw, cb, gin, tok), (a, u, cw, cb, gin)

    def bwd(res, cts):
        a, u, cw, cb, gin = res
        dz, _, dtok = cts
        da, du, dcw, dcb, *recv = call_bwd(a, u, cw, cb, dz, _riders(dtok, True))
        return da, du, dcw, dcb, _nones(gin), tuple(recv)

    conv_gate.defvjp(fwd, bwd)
    return conv_gate


def _make_loss_head(tag):
    def call(y, tgt):
        T, D = y.shape

        def body(y_ref, t_ref, l_ref, r_ref):
            @pl.when(pl.program_id(0) == 0)
            def _():
                l_ref[...] = jnp.zeros_like(l_ref)

            d = y_ref[...] - t_ref[...]
            r_ref[...] = d * (1.0 / D)
            l_ref[...] += jnp.sum(d * d) * (0.5 / D)

        row = pl.BlockSpec((ROW_TILE, D), lambda i: (i, 0))
        return pl.pallas_call(body, name=tag, grid=(T // ROW_TILE,), in_specs=[row, row],
                              out_specs=[pl.BlockSpec((8, LANE), lambda i: (0, 0)), row],
                              out_shape=[jax.ShapeDtypeStruct((8, LANE), F32), jax.ShapeDtypeStruct((T, D), F32)],
                              compiler_params=_cparams(1))(y, tgt)

    @jax.custom_vjp
    def loss_head(y, tgt):
        return call(y, tgt)[0][0, 0]

    def fwd(y, tgt):
        l, r = call(y, tgt)
        return l[0, 0], r

    def bwd(r, g):
        return r * g, None

    loss_head.defvjp(fwd, bwd)
    return loss_head


class _AttnCfg:
    def __init__(self, H, Hkv, dqk, dv, scale, tq, tk, has_sink=False):
        self.H, self.Hkv, self.dqk, self.dv, self.scale = H, Hkv, dqk, dv, scale
        self.tq, self.tk, self.has_sink = tq, tk, has_sink
        self.G = H // Hkv


class _WinCfg:
    def __init__(self, mode, H, Hkv, scale, nx, n_ctx):
        self.mode, self.H, self.Hkv, self.scale, self.nx, self.n_ctx = mode, H, Hkv, scale, nx, n_ctx
        self.G = H // Hkv
        self.has_sink = mode == "swa"
        self.nwin, self.back = (3, 1) if mode == "swa" else (5, 2)
        self.width = self.nwin * ATT_BLOCK

    def first_block(self, i):
        return jnp.clip(i - self.back, 0, self.nx - self.nwin)

    def slots(self, i, jj):
        d = 2 * (self.first_block(i) + jj - i)
        return jnp.clip(d + 8, 0, 15), jnp.clip(d + 7, 0, 15)

    def local_scores(self, s, i, tab_ref):
        sub = lax.broadcasted_iota(jnp.int32, s.shape, 0)
        lan = lax.broadcasted_iota(jnp.int32, s.shape, 1)
        first = self.first_block(i)
        if self.mode == "swa":
            diff = (i * ATT_BLOCK + sub) - (first * ATT_BLOCK + lan)
            return jnp.where(jnp.abs(diff) <= SWA_WINDOW, s, -jnp.inf)
        rows = 2 * self.nx
        qrow = 2 * i + jnp.where(sub >= GRID_W, 1, 0)
        krow = 2 * first + lan // GRID_W
        r0 = jnp.clip(qrow - NA_WIN_R // 2, 0, rows - NA_WIN_R)
        ok = (krow >= r0) & (krow < r0 + NA_WIN_R)
        tiles = []
        for jj in range(self.nwin):
            top, bot = self.slots(i, jj)
            tiles.append(jnp.concatenate([tab_ref[0, top], tab_ref[0, bot]], axis=0))
        return jnp.where(ok, s + jnp.concatenate(tiles, axis=1), -jnp.inf)


def _win_specs(cfg, Tk):
    G = cfg.G
    qmap = lambda kh, g, i: (i, kh * G + g)
    q_spec = pl.BlockSpec((ATT_BLOCK, HEAD_DIM), qmap)
    kv_spec = pl.BlockSpec((Tk, HEAD_DIM), lambda kh, g, i: (0, kh))
    lse_spec = pl.BlockSpec((1, ATT_BLOCK, LANE), lambda kh, g, i: (kh * G + g, i, 0))
    tab_spec = pl.BlockSpec((1, 16, GRID_W, LANE), lambda kh, g, i: (kh * G + g, 0, 0, 0))
    sink_spec = pl.BlockSpec((1, 8, LANE), lambda kh, g, i: (kh * G + g, 0, 0))
    return q_spec, kv_spec, lse_spec, tab_spec, sink_spec


def _win_scores(cfg, i, q_ref, k_ref, tab_ref):
    dn = (((1,), (1,)), ((), ()))
    n_x = cfg.nx * ATT_BLOCK
    win = pl.ds(pl.multiple_of(cfg.first_block(i) * ATT_BLOCK, ATT_BLOCK), cfg.width)
    ctx = pl.ds(n_x, cfg.n_ctx)
    qb = q_ref[...].astype(_MXU)
    s_loc = lax.dot_general(qb, k_ref[win, :].astype(_MXU), dn, preferred_element_type=F32) * cfg.scale
    s_ctx = lax.dot_general(qb, k_ref[ctx, :].astype(_MXU), dn, preferred_element_type=F32) * cfg.scale
    return cfg.local_scores(s_loc, i, tab_ref), s_ctx, win, ctx


def _win_fwd_call(cfg, name, q, k, v, tab, sink, riders=()):
    Tq, Tk = q.shape[0], k.shape[0]

    def body(*refs):
        q_ref, k_ref, v_ref = refs[:3]
        extra = refs[3] if (cfg.mode == "na" or cfg.has_sink) else None
        o_ref, lse_ref = refs[-2:]
        i = pl.program_id(2)
        s_loc, s_ctx, win, ctx = _win_scores(cfg, i, q_ref, k_ref, extra if cfg.mode == "na" else None)
        m = jnp.maximum(jnp.max(s_loc, axis=-1, keepdims=True), jnp.max(s_ctx, axis=-1, keepdims=True))
        if cfg.has_sink:
            m = jnp.maximum(m, extra[0, 0:1, 0:1])
        p_loc, p_ctx = jnp.exp(s_loc - m), jnp.exp(s_ctx - m)
        l = jnp.sum(p_loc, axis=-1, keepdims=True) + jnp.sum(p_ctx, axis=-1, keepdims=True)
        if cfg.has_sink:
            l = l + jnp.exp(extra[0, 0:1, 0:1] - m)
        acc = (jnp.dot(p_loc.astype(_MXU), v_ref[win, :].astype(_MXU), preferred_element_type=F32)
               + jnp.dot(p_ctx.astype(_MXU), v_ref[ctx, :].astype(_MXU), preferred_element_type=F32))
        o_ref[...] = acc / l
        lse_ref[0] = jnp.broadcast_to(m + jnp.log(l), (ATT_BLOCK, LANE))

    q_spec, kv_spec, lse_spec, tab_spec, sink_spec = _win_specs(cfg, Tk)
    in_specs, args = [q_spec, kv_spec, kv_spec], [q, k, v]
    if cfg.mode == "na":
        in_specs.append(tab_spec)
        args.append(tab)
    if cfg.has_sink:
        in_specs.append(sink_spec)
        args.append(sink)
    return _pcall(
        body, name=name, grid=(cfg.Hkv, cfg.G, Tq // ATT_BLOCK), in_specs=in_specs, out_specs=[q_spec, lse_spec],
        out_shape=[jax.ShapeDtypeStruct((Tq, cfg.H * HEAD_DIM), F32), jax.ShapeDtypeStruct((cfg.H, Tq, LANE), F32)],
        args=args, riders=riders)


def _win_bwd_call(cfg, name, q, k, v, tab, sink, o, lse, do, riders=()):
    Tq, Tk = q.shape[0], k.shape[0]
    dn_nt = (((1,), (1,)), ((), ()))
    dn_tn = (((0,), (0,)), ((), ()))

    def body(*refs):
        q_ref, k_ref, v_ref = refs[:3]
        has_extra = cfg.mode == "na" or cfg.has_sink
        extra = refs[3] if has_extra else None
        n = 4 if has_extra else 3
        o_ref, lse_ref, do_ref, dq_ref, dk_ref, dv_ref = refs[n:n + 6]
        dextra = refs[n + 6] if has_extra else None
        g, i = pl.program_id(1), pl.program_id(2)

        @pl.when((g == 0) & (i == 0))
        def _():
            dk_ref[...] = jnp.zeros_like(dk_ref)
            dv_ref[...] = jnp.zeros_like(dv_ref)

        if has_extra:
            @pl.when(i == 0)
            def _():
                dextra[...] = jnp.zeros_like(dextra)

        s_loc, s_ctx, win, ctx = _win_scores(cfg, i, q_ref, k_ref, extra if cfg.mode == "na" else None)
        lse = lse_ref[0][:, 0:1]
        p_loc, p_ctx = jnp.exp(s_loc - lse), jnp.exp(s_ctx - lse)
        do_f = do_ref[...]
        do_b = do_f.astype(_MXU)
        delta = jnp.sum(do_f * o_ref[...], axis=-1, keepdims=True)
        dp_loc = lax.dot_general(do_b, v_ref[win, :].astype(_MXU), dn_nt, preferred_element_type=F32)
        dp_ctx = lax.dot_general(do_b, v_ref[ctx, :].astype(_MXU), dn_nt, preferred_element_type=F32)
        ds_loc, ds_ctx = p_loc * (dp_loc - delta), p_ctx * (dp_ctx - delta)
        if cfg.mode == "na":
            for jj in range(cfg.nwin):
                top, bot = cfg.slots(i, jj)
                tile = ds_loc[:, jj * ATT_BLOCK:(jj + 1) * ATT_BLOCK]
                dextra[0, top] += tile[0:GRID_W, :]
                dextra[0, bot] += tile[GRID_W:, :]
        if cfg.has_sink:
            p_sink = jnp.exp(extra[0, 0:1, 0:1] - lse)
            lane0 = ((lax.broadcasted_iota(jnp.int32, (8, LANE), 0) == 0)
                     & (lax.broadcasted_iota(jnp.int32, (8, LANE), 1) == 0))
            dextra[0] += jnp.where(lane0, -jnp.sum(p_sink * delta), 0.0)
        dsb_loc, dsb_ctx = (ds_loc * cfg.scale).astype(_MXU), (ds_ctx * cfg.scale).astype(_MXU)
        qb = q_ref[...].astype(_MXU)
        dq_ref[...] = (jnp.dot(dsb_loc, k_ref[win, :].astype(_MXU), preferred_element_type=F32)
                       + jnp.dot(dsb_ctx, k_ref[ctx, :].astype(_MXU), preferred_element_type=F32))
        dk_ref[win, :] += lax.dot_general(dsb_loc, qb, dn_tn, preferred_element_type=F32)
        dk_ref[ctx, :] += lax.dot_general(dsb_ctx, qb, dn_tn, preferred_element_type=F32)
        dv_ref[win, :] += lax.dot_general(p_loc.astype(_MXU), do_b, dn_tn, preferred_element_type=F32)
        dv_ref[ctx, :] += lax.dot_general(p_ctx.astype(_MXU), do_b, dn_tn, preferred_element_type=F32)

    q_spec, kv_spec, lse_spec, tab_spec, sink_spec = _win_specs(cfg, Tk)
    in_specs, args = [q_spec, kv_spec, kv_spec], [q, k, v]
    out_specs = [q_spec, kv_spec, kv_spec]
    out_shape = [jax.ShapeDtypeStruct(q.shape, F32), jax.ShapeDtypeStruct(k.shape, F32),
                 jax.ShapeDtypeStruct(v.shape, F32)]
    if cfg.mode == "na":
        in_specs.append(tab_spec)
        args.append(tab)
        out_specs.append(tab_spec)
        out_shape.append(jax.ShapeDtypeStruct(tab.shape, F32))
    if cfg.has_sink:
        in_specs.append(sink_spec)
        args.append(sink)
        out_specs.append(sink_spec)
        out_shape.append(jax.ShapeDtypeStruct(sink.shape, F32))
    in_specs += [q_spec, lse_spec, q_spec]
    args += [o, lse, do]
    return _pcall(body, name=name, grid=(cfg.Hkv, cfg.G, Tq // ATT_BLOCK), in_specs=in_specs, out_specs=out_specs,
                  out_shape=out_shape, args=args, riders=riders)


def _attn_specs(cfg, Tk):
    G, tq = cfg.G, cfg.tq
    qmap = lambda kh, g, i: (i, kh * G + g)
    q_spec = pl.BlockSpec((tq, cfg.dqk), qmap)
    k_spec = pl.BlockSpec((Tk, cfg.dqk), lambda kh, g, i: (0, kh))
    v_spec = pl.BlockSpec((Tk, cfg.dv), lambda kh, g, i: (0, kh))
    o_spec = pl.BlockSpec((tq, cfg.dv), qmap)
    lse_spec = pl.BlockSpec((1, tq, LANE), lambda kh, g, i: (kh * G + g, i, 0))
    sink_spec = pl.BlockSpec((1, 8, LANE), lambda kh, g, i: (kh * G + g, 0, 0))
    return q_spec, k_spec, v_spec, o_spec, lse_spec, sink_spec


def _attn_fwd_call(cfg, name, q, k, v, sink, riders=()):
    Tq, Tk = q.shape[0], k.shape[0]
    nq, nkv = Tq // cfg.tq, Tk // cfg.tk
    dn = (((1,), (1,)), ((), ()))

    def body(*refs):
        q_ref, k_ref, v_ref = refs[:3]
        sink_ref = refs[3] if cfg.has_sink else None
        o_ref, lse_ref = refs[-2:]
        qb = q_ref[...].astype(_MXU)
        m = jnp.full((cfg.tq, 1), NEG, F32)
        l = jnp.zeros((cfg.tq, 1), F32)
        acc = jnp.zeros((cfg.tq, cfg.dv), F32)
        for j in range(nkv):
            rows = pl.ds(j * cfg.tk, cfg.tk)
            s = lax.dot_general(qb, k_ref[rows, :].astype(_MXU), dn, preferred_element_type=F32) * cfg.scale
            m_new = jnp.maximum(m, jnp.max(s, axis=-1, keepdims=True))
            alpha = jnp.exp(m - m_new)
            p = jnp.exp(s - m_new)
            l = alpha * l + jnp.sum(p, axis=-1, keepdims=True)
            acc = alpha * acc + jnp.dot(p.astype(_MXU), v_ref[rows, :].astype(_MXU), preferred_element_type=F32)
            m = m_new
        if cfg.has_sink:
            sk = sink_ref[0, 0:1, 0:1]
            m_new = jnp.maximum(m, sk)
            alpha = jnp.exp(m - m_new)
            l = alpha * l + jnp.exp(sk - m_new)
            acc = acc * alpha
            m = m_new
        o_ref[...] = acc / l
        lse_ref[0] = jnp.broadcast_to(m + jnp.log(l), (cfg.tq, LANE))

    q_spec, k_spec, v_spec, o_spec, lse_spec, sink_spec = _attn_specs(cfg, Tk)
    in_specs, args = [q_spec, k_spec, v_spec], [q, k, v]
    if cfg.has_sink:
        in_specs.append(sink_spec)
        args.append(sink)
    return _pcall(
        body, name=name, grid=(cfg.Hkv, cfg.G, nq), in_specs=in_specs, out_specs=[o_spec, lse_spec],
        out_shape=[jax.ShapeDtypeStruct((Tq, cfg.H * cfg.dv), F32), jax.ShapeDtypeStruct((cfg.H, Tq, LANE), F32)],
        args=args, riders=riders)


def _attn_bwd_call(cfg, name, q, k, v, sink, o, lse, do, riders=()):
    Tq, Tk = q.shape[0], k.shape[0]
    nq, nkv = Tq // cfg.tq, Tk // cfg.tk
    tq, tk = cfg.tq, cfg.tk
    dn_nt = (((1,), (1,)), ((), ()))
    dn_tn = (((0,), (0,)), ((), ()))

    def body(*refs):
        q_ref, k_ref, v_ref = refs[:3]
        n = 3
        sink_ref = dsink_ref = None
        if cfg.has_sink:
            sink_ref, n = refs[n], n + 1
        o_ref, lse_ref, do_ref, dq_ref, dk_ref, dv_ref = refs[n:n + 6]
        n += 6
        if cfg.has_sink:
            dsink_ref, n = refs[n], n + 1
        g, i = pl.program_id(1), pl.program_id(2)

        @pl.when((g == 0) & (i == 0))
        def _():
            dk_ref[...] = jnp.zeros_like(dk_ref)
            dv_ref[...] = jnp.zeros_like(dv_ref)

        if cfg.has_sink:
            @pl.when(i == 0)
            def _():
                dsink_ref[...] = jnp.zeros_like(dsink_ref)

        do_f = do_ref[...]
        do_b = do_f.astype(_MXU)
        qb = q_ref[...].astype(_MXU)
        lse = lse_ref[0][:, 0:1]
        delta = jnp.sum(do_f * o_ref[...], axis=-1, keepdims=True)
        if cfg.has_sink:
            p_sink = jnp.exp(sink_ref[0, 0:1, 0:1] - lse)
            lane0 = ((lax.broadcasted_iota(jnp.int32, (8, LANE), 0) == 0)
                     & (lax.broadcasted_iota(jnp.int32, (8, LANE), 1) == 0))
            dsink_ref[0] += jnp.where(lane0, -jnp.sum(p_sink * delta), 0.0)
        dq = jnp.zeros((tq, cfg.dqk), F32)
        for j in range(nkv):
            rows = pl.ds(j * tk, tk)
            kb, vb = k_ref[rows, :].astype(_MXU), v_ref[rows, :].astype(_MXU)
            s = lax.dot_general(qb, kb, dn_nt, preferred_element_type=F32) * cfg.scale
            p = jnp.exp(s - lse)
            dp = lax.dot_general(do_b, vb, dn_nt, preferred_element_type=F32)
            dsb = (p * (dp - delta) * cfg.scale).astype(_MXU)
            dq = dq + jnp.dot(dsb, kb, preferred_element_type=F32)
            dk_ref[rows, :] += lax.dot_general(dsb, qb, dn_tn, preferred_element_type=F32)
            dv_ref[rows, :] += lax.dot_general(p.astype(_MXU), do_b, dn_tn, preferred_element_type=F32)
        dq_ref[...] = dq

    q_spec, k_spec, v_spec, o_spec, lse_spec, sink_spec = _attn_specs(cfg, Tk)
    in_specs, args = [q_spec, k_spec, v_spec], [q, k, v]
    if cfg.has_sink:
        in_specs.append(sink_spec)
        args.append(sink)
    in_specs += [o_spec, lse_spec, o_spec]
    args += [o, lse, do]
    out_specs = [q_spec, k_spec, v_spec]
    out_shape = [jax.ShapeDtypeStruct(q.shape, F32), jax.ShapeDtypeStruct(k.shape, F32),
                 jax.ShapeDtypeStruct(v.shape, F32)]
    if cfg.has_sink:
        out_specs.append(sink_spec)
        out_shape.append(jax.ShapeDtypeStruct(sink.shape, F32))
    return _pcall(body, name=name, grid=(cfg.Hkv, cfg.G, nq), in_specs=in_specs, out_specs=out_specs,
                  out_shape=out_shape, args=args, riders=riders)


def _make_attention(tag, cfg):
    windowed = isinstance(cfg, _WinCfg)

    def run_fwd(q, k, v, tab, sink, gin):
        if windowed:
            return _win_fwd_call(cfg, tag + "_fwd", q, k, v, tab, sink, _riders(gin, False))
        return _attn_fwd_call(cfg, tag + "_fwd", q, k, v, sink, _riders(gin, False))

    @jax.custom_vjp
    def attn(q, k, v, tab, sink, gin, tok):
        o, _, *got = run_fwd(q.astype(_MXU), k.astype(_MXU), v.astype(_MXU), tab, sink, gin)
        return o, tuple(got), tok

    def fwd(q, k, v, tab, sink, gin, tok):
        q, k, v = q.astype(_MXU), k.astype(_MXU), v.astype(_MXU)
        o, lse, *got = run_fwd(q, k, v, tab, sink, gin)
        return (o, tuple(got), tok), (q, k, v, tab, sink, gin, o, lse)

    def bwd(res, cts):
        q, k, v, tab, sink, gin, o, lse = res
        do, _, dtok = cts
        if windowed:
            outs = _win_bwd_call(cfg, tag + "_bwd", q, k, v, tab, sink, o, lse, do, _riders(dtok, True))
        else:
            outs = _attn_bwd_call(cfg, tag + "_bwd", q, k, v, sink, o, lse, do, _riders(dtok, True))
        dq, dk, dv = outs[:3]
        rest = outs[3:]
        dtab = rest.pop(0) if tab is not None else None
        dsink = rest.pop(0) if sink is not None else None
        return dq, dk, dv, dtab, dsink, _nones(gin), tuple(rest)

    attn.defvjp(fwd, bwd)
    return attn


def _pick_block(n, prefs):
    for p in prefs:
        if n % p == 0:
            return p
    return n


def _na_table(rpb):
    cq = np.arange(GRID_W)
    dcol = np.clip(cq[None, :] - cq[:, None] + NA_WIN_C - 1, 0, 2 * NA_WIN_C - 2)
    onehot = (dcol[:, :, None] == np.arange(2 * NA_WIN_C - 1)[None, None, :]).astype(np.float32)
    c0 = np.clip(cq - NA_WIN_C // 2, 0, GRID_W - NA_WIN_C)
    col_in = (cq[None, :] >= c0[:, None]) & (cq[None, :] < c0[:, None] + NA_WIN_C)
    tz = jnp.einsum("hrj,qkj->hrqk", rpb, jnp.asarray(onehot), precision=lax.Precision.HIGHEST)
    tz = jnp.where(jnp.asarray(col_in)[None, None], tz, NEG)
    zero = jnp.zeros_like(tz[:, :1])
    tzp = jnp.concatenate([zero, tz, zero], axis=1)
    return jnp.concatenate([tzp[:, 0:16], tzp[:, 1:17]], axis=-1)


def _sink_block(sink):
    return jnp.broadcast_to(sink[:, None, None], (sink.shape[0], 8, LANE))


def _make_split(bounds, axis):
    @jax.custom_vjp
    def split(t):
        return tuple(lax.slice_in_dim(t, a, b, axis=axis) for a, b in bounds)

    def fwd(t):
        return split(t), None

    def bwd(_, cts):
        return (jnp.concatenate(cts, axis=axis),)

    split.defvjp(fwd, bwd)
    return split


_IN_GROUPS = ((0, 512), (512, 1024), (1024, 1536), (1536, 2048), (2048, 2304), (2304, 2560), (2560, 2944),
              (2944, 3072), (3072, 3200), (3200, 3712), (3712, 3968), (3968, 4224))


def _seg2(vx, vc, nseg):
    rows = [vx, vc][:nseg]
    return jnp.stack(rows)[:, None, :]


def _full_weight(name, g):
    if name in _BIG_ROW:
        return g.reshape(N_DEV * g.shape[1], g.shape[2])
    _, K, n = g.shape
    full = jnp.transpose(g, (1, 0, 2)).reshape(K, N_DEV * n)
    if name == "w_in":
        full = jnp.concatenate([full[:, :KPE_END], jnp.zeros((K, 64), full.dtype), full[:, KPE_END:]], axis=-1)
    return full


def _grad_parts(name):
    def to_parts(dw):
        if name in _BIG_ROW:
            return dw.reshape(N_DEV, dw.shape[0] // N_DEV, dw.shape[1]).astype(BF16)
        if name == "w_in":
            dw = jnp.concatenate([dw[:, :KPE_END], dw[:, KPE_END + 64:]], axis=-1)
        K, N = dw.shape
        return jnp.transpose(dw.reshape(K, N_DEV, N // N_DEV), (1, 0, 2)).astype(BF16)

    return to_parts


_IN_GROUP = ("w_in", "mla_w_uq", "mla_w_ukv")


def _local_loss(x, ctx, tgt, modx, modc, p, full, shards, toks):
    S, D = x.shape
    C = ctx.shape[0]
    depth = len(full)
    ride = shards is not None
    have = [dict(f) for f in full]
    sinks = {(l, n): toks[l][n] for l in range(depth) for n in toks[l]}
    alpha = (2 * depth) ** 0.25
    T0 = S + C
    nx = S // ATT_BLOCK

    def gin(l, names):
        return tuple(shards[l][n] for n in names) if ride else ()

    def tok(l, names):
        return tuple(sinks[(l, n)] for n in names) if ride else ()

    def landed(result, lw, names_w, lt, names_t):
        out, got, tk = result
        for n, g in zip(names_w, got):
            have[lw][n] = _full_weight(n, g)
        for n, s in zip(names_t, tk):
            sinks[(lt, n)] = s
        return out

    sc = HEAD_DIM ** -0.5
    sc_mla = (MLA_NOPE + MLA_ROPE) ** -0.5
    tq_full = _pick_block(S, (256, 128))
    tk_all = _pick_block(T0, (2176, 640, 512, 256, 128))
    tctx = _pick_block(C, (256, 128))
    rope128 = _rope_tables(S, T0, 128, 32)
    rope64 = _rope_tables(S, T0, 64, 16)

    xs = jnp.concatenate([x, ctx], axis=0)
    for l in range(depth):
        last = l == depth - 1
        t = "l%d_" % l
        nxt = () if last else _IN_GROUP
        lin = lambda a, name, g=(), tk=(), out=F32: _make_linear(t + name, _grad_parts(name), out)(
            a, have[l][name], sinks[(l, name)], g, tk)
        mx, mc = modx[l], modc[l]

        h = _make_modulate(t + "mod1", S)(xs, _seg2(mx[0], mc[0], 2), _seg2(mx[1], mc[1], 2))
        pj = lin(h, "w_in")[0]
        qa, ka, va, qb, kb, vb, cq, ckv, kpe, qd, kd, vd = _make_split(_IN_GROUPS, 1)(pj)
        rows = lambda v_: _make_split(((0, S), (S, T0)), 0)(v_)
        tab = _na_table(p["na_rpb"][l])
        cfg_a = _WinCfg("na", NA_HEADS, NA_HEADS, sc, nx, C)
        qa_x, qa_c = rows(qa)
        oa = landed(_make_attention(t + "attn_a", cfg_a)(qa_x, ka, va, tab, None, gin(l, ("w_out",)),
                                                         tok(l, ("w_out",))), l, ("w_out",), l, ("w_out",))
        qb = _make_rope(t + "rope_b", 32)(qb, *rope128)
        kb = _make_rope(t + "rope_bk", 32)(kb, *rope128)
        snk = _sink_block(p["swa_sink"][l])
        cfg_b = _WinCfg("swa", SWA_HEADS, SWA_KV_HEADS, sc, nx, C)
        qb_x, qb_c = rows(qb)
        ob = landed(_make_attention(t + "attn_b", cfg_b)(qb_x, kb, vb, None, snk, gin(l, ("ffn_w_down",)),
                                                         tok(l, ("ffn_w_down",))), l, ("ffn_w_down",), l, ("ffn_w_down",))
        cq = _make_rmsnorm(t + "rms_cq")(cq, p["mla_q_norm"][l][None, :])
        ckv = _make_rmsnorm(t + "rms_ckv")(ckv, p["mla_kv_norm"][l][None, :])
        qh = lin(cq, "mla_w_uq")[0].reshape(T0, MLA_HEADS, MLA_NOPE + MLA_ROPE)
        kvh = lin(ckv, "mla_w_ukv")[0].reshape(T0, MLA_HEADS, MLA_NOPE + MLA_V)
        qpe = jnp.pad(qh[:, :, MLA_NOPE:], ((0, 0), (0, 0), (0, LANE - MLA_ROPE))).reshape(T0, MLA_HEADS * LANE)
        qpe = _make_rope(t + "rope_cq", 16)(qpe, *rope64).reshape(T0, MLA_HEADS, LANE)
        kpe = _make_rope(t + "rope_ck", 16)(kpe, *rope64)
        qc = jnp.concatenate([qh[:, :, :MLA_NOPE], qpe], axis=-1).reshape(T0, MLA_HEADS * 2 * LANE)
        kc = jnp.concatenate([kvh[:, :, :MLA_NOPE], jnp.broadcast_to(kpe[:, None, :], (T0, MLA_HEADS, LANE))],
                             axis=-1).reshape(T0, MLA_HEADS * 2 * LANE)
        vc = kvh[:, :, MLA_NOPE:].reshape(T0, MLA_HEADS * MLA_V)
        cfg_c = _AttnCfg(MLA_HEADS, MLA_HEADS, 2 * LANE, MLA_V, sc_mla, tq_full, tk_all)
        qc_x, qc_c = rows(qc)
        oc = landed(_make_attention(t + "attn_c", cfg_c)(qc_x, kc, vc, None, None, gin(l, ("ffn_w_gate",)),
                                                         tok(l, ("ffn_w_gate",))), l, ("ffn_w_gate",), l, ("ffn_w_gate",))
        qd = _make_rmsnorm(t + "rms_dq")(qd, p["gqa_q_norm"][l][None, :])
        kd = _make_rmsnorm(t + "rms_dk")(kd, p["gqa_k_norm"][l][None, :])
        qd = _make_rope(t + "rope_dq", 32)(qd, *rope128)
        kd = _make_rope(t + "rope_dk", 32)(kd, *rope128)
        cfg_d = _AttnCfg(GQA_HEADS, GQA_KV_HEADS, 128, 128, sc, tq_full, tk_all)
        qd_x, qd_c = rows(qd)
        od = landed(_make_attention(t + "attn_d", cfg_d)(qd_x, kd, vd, None, None, gin(l, ("ffn_w_up",)),
                                                         tok(l, ("ffn_w_up",))), l, ("ffn_w_up",), l, ("ffn_w_up",))
        mix = jnp.concatenate([oa, ob, oc, od], axis=1)
        if not last:
            full_c = lambda H, Hkv, dqk, s, sink: _AttnCfg(H, Hkv, dqk, 128, s, tctx, tctx, has_sink=sink)
            ctx_att = lambda name, cfg, q_, k_, v_, s_: _make_attention(t + name, cfg)(
                q_, k_[S:], v_[S:], None, s_, (), ())[0]
            oa_c = ctx_att("ctx_a", full_c(4, 4, 128, sc, False), qa_c, ka, va, None)
            ob_c = ctx_att("ctx_b", full_c(4, 2, 128, sc, True), qb_c, kb, vb, snk)
            oc_c = ctx_att("ctx_c", full_c(4, 4, 256, sc_mla, False), qc_c, kc, vc, None)
            od_c = ctx_att("ctx_d", full_c(4, 2, 128, sc, False), qd_c, kd, vd, None)
            mix = jnp.concatenate([mix, jnp.concatenate([oa_c, ob_c, oc_c, od_c], axis=1)], axis=0)
            res, nseg = xs, 2
        else:
            res, nseg = xs[:S], 1
        y = lin(mix, "w_out")[0]
        x1 = _make_resid_ln(t + "ln1", S, alpha)(res, y, _seg2(mx[2], mc[2], nseg), p["ln1_g"][l][None, :],
                                                 p["ln1_b"][l][None, :])
        h2 = _make_modulate(t + "mod2", S)(x1, _seg2(mx[3], mc[3], nseg), _seg2(mx[4], mc[4], nseg))
        gp = lin(h2, "ffn_w_gate", out=_ACT)[0]
        up = lin(h2, "ffn_w_up", out=_ACT)[0]
        z = landed(_make_conv_gate(t + "conv", S)(gp, up, p["ffn_conv_w"][l], p["ffn_conv_b"][l][None, :],
                                                  gin(l + 1, nxt[1:]), tok(l + 1, nxt[1:])),
                   l + 1, nxt[1:], l + 1, nxt[1:])
        f = landed(lin(z, "ffn_w_down", gin(l + 1, nxt[:1]), tok(l + 1, nxt[:1])), l + 1, nxt[:1], l + 1, nxt[:1])
        xs = _make_resid_ln(t + "ln2", S, alpha)(x1, f, _seg2(mx[5], mc[5], nseg), p["ln2_g"][l][None, :],
                                                 p["ln2_b"][l][None, :])
    return _make_loss_head("loss_head")(xs, tgt)


def _adamw(w, parts, m, v, name):
    R, C = w.shape
    P = parts.shape[0]
    c1 = 1.0 - ADAM_B1 ** ADAM_STEP
    c2 = 1.0 - ADAM_B2 ** ADAM_STEP
    per_row = C * (4 * 14 + 2 * P * parts.dtype.itemsize)
    cands = _divisors(R, 8, R)
    fitting = [d for d in cands if d * per_row <= 24 * 1024 * 1024]
    tr = max(fitting) if fitting else min(cands)

    def body(w_ref, p_ref, m_ref, v_ref, g_ref, d_ref, nm_ref, nv_ref):
        g = p_ref[0].astype(F32)
        for k in range(1, P):
            g = g + p_ref[k].astype(F32)
        mm = ADAM_B1 * m_ref[...] + (1.0 - ADAM_B1) * g
        vv = ADAM_B2 * v_ref[...] + (1.0 - ADAM_B2) * (g * g)
        g_ref[...] = g
        nm_ref[...] = mm
        nv_ref[...] = vv
        d_ref[...] = -ADAM_LR * ((mm / c1) / (jnp.sqrt(vv / c2) + ADAM_EPS) + ADAM_WD * w_ref[...])

    blk = pl.BlockSpec((tr, C), lambda i: (i, 0))
    pblk = pl.BlockSpec((P, tr, C), lambda i: (0, i, 0))
    sh = jax.ShapeDtypeStruct((R, C), F32)
    return pl.pallas_call(body, name=name, grid=(R // tr,), in_specs=[blk, pblk, blk, blk],
                          out_specs=[blk, blk, blk, blk], out_shape=[sh, sh, sh, sh],
                          compiler_params=_cparams(1))(w, parts, m, v)


def _adamw_nd(w, parts, m, v, name):
    shape = w.shape
    C = shape[-1]
    R = int(np.prod(shape[:-1])) if len(shape) > 1 else 1
    outs = _adamw(w.reshape(R, C), parts.reshape(parts.shape[0], R, C), m.reshape(R, C), v.reshape(R, C), name)
    return [o.reshape(shape) for o in outs]


def _adamw_layers(w, parts, m, v, name):
    L, R, C = w.shape
    P = parts[0].shape[0]
    c1 = 1.0 - ADAM_B1 ** ADAM_STEP
    c2 = 1.0 - ADAM_B2 ** ADAM_STEP
    per_row = C * (4 * 14 + 2 * L * P * parts[0].dtype.itemsize)
    cands = _divisors(R, 8, R)
    fitting = [d for d in cands if d * per_row <= 24 * 1024 * 1024]
    tr = max(fitting) if fitting else min(cands)
    nt = R // tr

    def body(*refs):
        w_ref, p_refs, (m_ref, v_ref, g_ref, d_ref, nm_ref, nv_ref) = refs[0], refs[1:1 + L], refs[1 + L:]
        layer = pl.program_id(0)
        for k in range(L):
            @pl.when(layer == k)
            def _(p_ref=p_refs[k]):
                g = p_ref[0].astype(F32)
                for j in range(1, P):
                    g = g + p_ref[j].astype(F32)
                mm = ADAM_B1 * m_ref[0] + (1.0 - ADAM_B1) * g
                vv = ADAM_B2 * v_ref[0] + (1.0 - ADAM_B2) * (g * g)
                g_ref[0] = g
                nm_ref[0] = mm
                nv_ref[0] = vv
                d_ref[0] = -ADAM_LR * ((mm / c1) / (jnp.sqrt(vv / c2) + ADAM_EPS) + ADAM_WD * w_ref[0])

    blk = pl.BlockSpec((1, tr, C), lambda l, i: (l, i, 0))
    pblk = [pl.BlockSpec((P, tr, C), lambda l, i, k=k: (0, jnp.where(l == k, i, jnp.where(l < k, 0, nt - 1)), 0))
            for k in range(L)]
    sh = jax.ShapeDtypeStruct((L, R, C), F32)
    return pl.pallas_call(body, name=name, grid=(L, nt), in_specs=[blk] + pblk + [blk, blk],
                          out_specs=[blk, blk, blk, blk], out_shape=[sh, sh, sh, sh],
                          compiler_params=_cparams(2))(w, *parts, m, v)


def _silu(v):
    return v * (1.0 / (1.0 + jnp.exp(-v)))


def _ada_rows(c_all, c_ctx):
    return jnp.concatenate([c_all, jnp.broadcast_to(c_ctx[None, :], (N_DEV, c_ctx.shape[0]))], axis=0)


def _silu_rows(rows, name):
    def body(r_ref, o_ref):
        o_ref[...] = _silu(r_ref[...])

    return pl.pallas_call(body, name=name, out_shape=jax.ShapeDtypeStruct(rows.shape, F32))(rows)


_BIG_COL = ("w_in", "mla_w_uq", "mla_w_ukv", "ffn_w_gate", "ffn_w_up")
_BIG_ROW = ("w_out", "ffn_w_down")
_SMALL = ("c_ctx", "na_rpb", "swa_sink", "mla_q_norm", "mla_kv_norm", "gqa_q_norm", "gqa_k_norm",
          "ln1_g", "ln1_b", "ffn_conv_b", "ln2_g", "ln2_b")
_NAMES = ("c_ctx", "w_ada", "b_ada", "w_in", "na_rpb", "swa_sink", "mla_q_norm", "mla_kv_norm", "mla_w_uq",
          "mla_w_ukv", "gqa_q_norm", "gqa_k_norm", "w_out", "ln1_g", "ln1_b", "ffn_w_gate", "ffn_w_up",
          "ffn_conv_w", "ffn_conv_b", "ffn_w_down", "ln2_g", "ln2_b")


def _full_from_cols(g, pad_in=False):
    _, L, K, n = g.shape
    full = jnp.transpose(g, (1, 2, 0, 3)).reshape(L, K, N_DEV * n)
    if pad_in:
        full = jnp.concatenate([full[:, :, :KPE_END], jnp.zeros((L, K, 64), full.dtype), full[:, :, KPE_END:]], axis=-1)
    return full


def _pack_small(tree, extra=None):
    flat = [tree[n].reshape(-1) for n in _SMALL]
    flat.append(jnp.zeros((1,), F32) if extra is None else extra.reshape(1))
    v = jnp.concatenate(flat)
    n = v.shape[0]
    padded = -(-n // 1024) * 1024
    return jnp.pad(v, (0, padded - n)).reshape(padded // LANE, LANE)


def _unpack_small(mat, like):
    v = mat.reshape(-1)
    out, o = {}, 0
    for n in _SMALL:
        k = int(np.prod(like[n].shape))
        out[n] = v[o:o + k].reshape(like[n].shape)
        o += k
    return out, v[o]


def kernel(x, c, ctx, c_ctx, w_ada, b_ada, w_in, na_rpb, swa_sink, mla_q_norm, mla_kv_norm, mla_w_uq, mla_w_ukv, gqa_q_norm, gqa_k_norm, w_out, ln1_g, ln1_b, ffn_w_gate, ffn_w_up, ffn_conv_w, ffn_conv_b, ffn_w_down, ln2_g, ln2_b, loss_target, m_c_ctx, m_w_ada, m_b_ada, m_w_in, m_na_rpb, m_swa_sink, m_mla_q_norm, m_mla_kv_norm, m_mla_w_uq, m_mla_w_ukv, m_gqa_q_norm, m_gqa_k_norm, m_w_out, m_ln1_g, m_ln1_b, m_ffn_w_gate, m_ffn_w_up, m_ffn_conv_w, m_ffn_conv_b, m_ffn_w_down, m_ln2_g, m_ln2_b, v_c_ctx, v_w_ada, v_b_ada, v_w_in, v_na_rpb, v_swa_sink, v_mla_q_norm, v_mla_kv_norm, v_mla_w_uq, v_mla_w_ukv, v_gqa_q_norm, v_gqa_k_norm, v_w_out, v_ln1_g, v_ln1_b, v_ffn_w_gate, v_ffn_w_up, v_ffn_conv_w, v_ffn_conv_b, v_ffn_w_down, v_ln2_g, v_ln2_b):
    W = dict(c_ctx=c_ctx, w_ada=w_ada, b_ada=b_ada, w_in=w_in, na_rpb=na_rpb, swa_sink=swa_sink,
             mla_q_norm=mla_q_norm, mla_kv_norm=mla_kv_norm, mla_w_uq=mla_w_uq, mla_w_ukv=mla_w_ukv,
             gqa_q_norm=gqa_q_norm, gqa_k_norm=gqa_k_norm, w_out=w_out, ln1_g=ln1_g, ln1_b=ln1_b,
             ffn_w_gate=ffn_w_gate, ffn_w_up=ffn_w_up, ffn_conv_w=ffn_conv_w, ffn_conv_b=ffn_conv_b,
             ffn_w_down=ffn_w_down, ln2_g=ln2_g, ln2_b=ln2_b)
    M = dict(c_ctx=m_c_ctx, w_ada=m_w_ada, b_ada=m_b_ada, w_in=m_w_in, na_rpb=m_na_rpb, swa_sink=m_swa_sink,
             mla_q_norm=m_mla_q_norm, mla_kv_norm=m_mla_kv_norm, mla_w_uq=m_mla_w_uq, mla_w_ukv=m_mla_w_ukv,
             gqa_q_norm=m_gqa_q_norm, gqa_k_norm=m_gqa_k_norm, w_out=m_w_out, ln1_g=m_ln1_g, ln1_b=m_ln1_b,
             ffn_w_gate=m_ffn_w_gate, ffn_w_up=m_ffn_w_up, ffn_conv_w=m_ffn_conv_w, ffn_conv_b=m_ffn_conv_b,
             ffn_w_down=m_ffn_w_down, ln2_g=m_ln2_g, ln2_b=m_ln2_b)
    V = dict(c_ctx=v_c_ctx, w_ada=v_w_ada, b_ada=v_b_ada, w_in=v_w_in, na_rpb=v_na_rpb, swa_sink=v_swa_sink,
             mla_q_norm=v_mla_q_norm, mla_kv_norm=v_mla_kv_norm, mla_w_uq=v_mla_w_uq, mla_w_ukv=v_mla_w_ukv,
             gqa_q_norm=v_gqa_q_norm, gqa_k_norm=v_gqa_k_norm, w_out=v_w_out, ln1_g=v_ln1_g, ln1_b=v_ln1_b,
             ffn_w_gate=v_ffn_w_gate, ffn_w_up=v_ffn_w_up, ffn_conv_w=v_ffn_conv_w, ffn_conv_b=v_ffn_conv_b,
             ffn_w_down=v_ffn_w_down, ln2_g=v_ln2_g, ln2_b=v_ln2_b)
    L, D, n_ada = w_ada.shape
    me = 4 * lax.axis_index("x") + 2 * lax.axis_index("y") + lax.axis_index("c")
    xs, ctxs, tgt = x[0], ctx[0], loss_target[0]

    c_all = _exchange(c, False, "gather_c").reshape(N_DEV, D)
    a_rows = _silu_rows(_ada_rows(c_all, c_ctx), "ada_silu")
    b_mine = lax.dynamic_slice(b_ada, (0, me * n_ada), (L, n_ada))
    mod_mine = jnp.stack([_matmul(a_rows, w_ada[l], "nn", "ada_fwd_l%d" % l) + b_mine[l][None, :] for l in range(L)])
    mod_all = _exchange(mod_mine, False, "gather_mod")
    mod_all = jnp.transpose(mod_all, (1, 2, 0, 3)).reshape(L, 2 * N_DEV, N_DEV * n_ada)
    modx = lax.dynamic_slice(mod_all, (0, me, 0), (L, 1, 6 * D)).reshape(L, 6, D)
    modc = mod_all[:, N_DEV].reshape(L, 6, D)

    big = _BIG_COL + _BIG_ROW
    shards = [{n: W[n][l].astype(BF16) for n in big} for l in range(L)]
    full = [{n: _full_weight(n, _exchange(shards[0][n], False, "gather_l0_" + n)) for n in _IN_GROUP}]
    full += [{} for _ in range(1, L)]
    conv_w_full = _full_from_cols(_exchange(ffn_conv_w, False, "gather_conv_w"))

    toks = [{n: jnp.zeros((N_DEV,) + shards[l][n].shape, BF16) for n in big} for l in range(L)]
    small = {n: W[n] for n in _SMALL if n != "c_ctx"}
    small["ffn_conv_w"] = conv_w_full

    def loss_fn(xv, mxv, mcv, sm, tk):
        return _local_loss(xv, ctxs, tgt, mxv, mcv, sm, full, shards, tk)

    loss_local, (grad_x, dmodx, dmodc, dsmall, dtoks) = jax.value_and_grad(loss_fn, argnums=(0, 1, 2, 3, 4))(
        xs, modx, modc, small, toks)

    dmods = _exchange(jnp.stack([dmodx.reshape(L, 6 * D), dmodc.reshape(L, 6 * D)]), False, "gather_dmod")
    dm_rows = jnp.concatenate([dmods[:, 0], dmods[:, 1]], axis=0)
    dm_rows = jnp.transpose(dm_rows, (1, 0, 2))
    dm_mine = lax.dynamic_slice(dm_rows, (0, 0, me * n_ada), (L, 2 * N_DEV, n_ada))
    g_w_ada = [_matmul(a_rows, dm_mine[l], "tn", "ada_dw_l%d" % l) for l in range(L)]
    d_rows = sum(_matmul(dm_mine[l], w_ada[l], "nt", "ada_da_l%d" % l) for l in range(L))
    sig = 1.0 / (1.0 + jnp.exp(-c_ctx))
    d_c_ctx_part = jnp.sum(d_rows[N_DEV:], axis=0) * (sig * (1.0 + c_ctx * (1.0 - sig)))

    outs = {}
    for n in _IN_GROUP:
        dtoks[0][n] = _exchange(dtoks[0][n], True, "scatter_l0_" + n)
    for n in big:
        outs[n] = _adamw_layers(W[n], [dtoks[l][n] for l in range(L)], M[n], V[n], "adamw_" + n)
    dcw = dsmall.pop("ffn_conv_w")
    cw_parts = _exchange(jnp.transpose(dcw.reshape(L, 3, N_DEV, -1), (2, 0, 1, 3)), True, "scatter_conv_w")
    outs["ffn_conv_w"] = _adamw_layers(ffn_conv_w, [cw_parts[:, l] for l in range(L)], M["ffn_conv_w"],
                                       V["ffn_conv_w"], "adamw_conv_w")
    outs["w_ada"] = _adamw_layers(w_ada, [g[None] for g in g_w_ada], M["w_ada"], V["w_ada"], "adamw_w_ada")
    outs["b_ada"] = _adamw_nd(b_ada, jnp.transpose(dm_rows, (1, 0, 2)), M["b_ada"], V["b_ada"], "adamw_b_ada")

    dsmall["c_ctx"] = d_c_ctx_part
    small_parts = _exchange(_pack_small(dsmall, extra=loss_local), False, "gather_small")
    w_small = _pack_small({n: W[n] for n in _SMALL})
    g_s, d_s, nm_s, nv_s = _adamw(w_small, small_parts, _pack_small({n: M[n] for n in _SMALL}),
                                  _pack_small({n: V[n] for n in _SMALL}), "adamw_small")
    like = {n: W[n] for n in _SMALL}
    g_small, loss = _unpack_small(g_s, like)
    unpacked = [g_small, _unpack_small(d_s, like)[0], _unpack_small(nm_s, like)[0], _unpack_small(nv_s, like)[0]]
    for n in _SMALL:
        outs[n] = [u[n] for u in unpacked]

    result = [loss, grad_x[None]]
    for k in range(4):
        result += [outs[n][k] for n in _NAMES]
    return tuple(result)
```

```python
import functools
import math

import numpy as np
import jax
import jax.numpy as jnp
from jax import lax
from jax.experimental import pallas as pl
from jax.experimental.pallas import tpu as pltpu

F32 = jnp.float32
BF16 = jnp.bfloat16
_MXU = jnp.bfloat16
_ACT = jnp.bfloat16

N_DEV = 8
GRID_W = 64
HEAD_DIM = 128
NA_HEADS, NA_WIN_R, NA_WIN_C = 4, 8, 16
SWA_HEADS, SWA_KV_HEADS, SWA_WINDOW = 4, 2, 128
MLA_HEADS, MLA_Q_LORA, MLA_KV_LORA, MLA_NOPE, MLA_ROPE, MLA_V = 4, 384, 128, 128, 64, 128
GQA_HEADS, GQA_KV_HEADS = 4, 2
ROPE_THETA = 10000.0
EPS = 1e-6
NEG = -1e30
IN_SIZES = (512, 512, 512, 512, 256, 256, MLA_Q_LORA, MLA_KV_LORA, MLA_ROPE, 512, 256, 256)
IN_COLS = sum(IN_SIZES)
KPE_END = sum(IN_SIZES[:9])
IN_COLS_PAD = IN_COLS + 64
ADAM_LR, ADAM_B1, ADAM_B2, ADAM_EPS, ADAM_WD, ADAM_STEP = 0.001, 0.9, 0.999, 1e-08, 0.01, 10

LANE = 128
ROW_TILE = 256
ATT_BLOCK = 128
VMEM_LIMIT = 56 * 1024 * 1024
MM_BUDGET = 36 * 1024 * 1024
HBM_BPS = 3.0e12
MXU_FLOPS = 9.0e14
VMEM_BPS = 4.0e12
STEP_S = 0.4e-6


def _cparams(n_axes):
    return pltpu.CompilerParams(dimension_semantics=("arbitrary",) * n_axes, vmem_limit_bytes=VMEM_LIMIT)


def _exchange_copies(in_ref, out_ref, send_sems, recv_sems, local_sem, scatter, rows=None):
    x, y, c = lax.axis_index("x"), lax.axis_index("y"), lax.axis_index("c")
    me = 4 * x + 2 * y + c
    cut = (lambda ref: ref) if rows is None else (lambda ref: ref.at[rows])
    copies = [pltpu.make_async_copy(cut(in_ref.at[me] if scatter else in_ref), cut(out_ref.at[me]), local_sem)]
    for k in range(1, N_DEV):
        px, py, pc = (x + (k >> 2)) % 2, (y + ((k >> 1) & 1)) % 2, (c + (k & 1)) % 2
        peer = 4 * px + 2 * py + pc
        copies.append(pltpu.make_async_remote_copy(
            src_ref=cut(in_ref.at[peer] if scatter else in_ref), dst_ref=cut(out_ref.at[me]),
            send_sem=send_sems.at[k - 1], recv_sem=recv_sems.at[k - 1],
            device_id=(px, py, pc), device_id_type=pl.DeviceIdType.MESH))
    return copies


PIECE_ROWS = 16
MAX_PIECES = 32


def _pieces(rows, steps):
    if rows % PIECE_ROWS:
        return 1
    units = rows // PIECE_ROWS
    return max(d for d in range(1, min(MAX_PIECES, steps) + 1) if units % d == 0)


def _exchange_out_shape(arr, scatter):
    return jax.ShapeDtypeStruct(arr.shape if scatter else (N_DEV,) + arr.shape, arr.dtype)


_EXCHANGE_SEMS = (pltpu.SemaphoreType.DMA((N_DEV - 1,)), pltpu.SemaphoreType.DMA((N_DEV - 1,)),
                  pltpu.SemaphoreType.DMA)


def _exchange(inp, scatter, name):
    def body(in_ref, out_ref, send_sems, recv_sems, local_sem):
        copies = _exchange_copies(in_ref, out_ref, send_sems, recv_sems, local_sem, scatter)
        for cp in copies:
            cp.start()
        for cp in copies:
            cp.wait()

    hbm = pl.BlockSpec(memory_space=pl.ANY)
    return pl.pallas_call(
        body, name=name, in_specs=[hbm], out_specs=hbm, out_shape=_exchange_out_shape(inp, scatter),
        scratch_shapes=list(_EXCHANGE_SEMS), compiler_params=pltpu.CompilerParams(has_side_effects=True),
    )(inp)


def _pcall(body, *, name, grid, in_specs, out_specs, out_shape, args, scratch_shapes=(), riders=()):
    in_specs, out_specs, out_shape = list(in_specs), list(out_specs), list(out_shape)
    scratch, args = list(scratch_shapes), list(args)
    n_in, n_out, n_scr, n_r = len(in_specs), len(out_specs), len(scratch), len(riders)
    hbm = pl.BlockSpec(memory_space=pl.ANY)
    for arr, scatter in riders:
        in_specs.append(hbm)
        args.append(arr)
        out_specs.append(hbm)
        out_shape.append(_exchange_out_shape(arr, scatter))
        scratch += list(_EXCHANGE_SEMS)

    def wrapped(*refs):
        ins, r_in = refs[:n_in], refs[n_in:n_in + n_r]
        o0 = n_in + n_r
        outs, r_out = refs[o0:o0 + n_out], refs[o0 + n_out:o0 + n_out + n_r]
        s0 = o0 + n_out + n_r
        scr, sems = refs[s0:s0 + n_scr], refs[s0 + n_scr:]

        def copies(r, rows=None):
            return _exchange_copies(r_in[r], r_out[r], sems[3 * r], sems[3 * r + 1], sems[3 * r + 2], riders[r][1],
                                    rows)

        if n_r:
            steps = int(np.prod(grid))
            step = functools.reduce(lambda u, w: u * w[1] + w[0],
                                    [(pl.program_id(a), n) for a, n in enumerate(grid)], 0)
            for r, (arr, scatter) in enumerate(riders):
                block_rows = arr.shape[1] if scatter else arr.shape[0]
                n_pieces = _pieces(block_rows, steps)
                every, piece = steps // n_pieces, block_rows // n_pieces

                @pl.when((step % every == 0) & (step // every < n_pieces))
                def _(r=r, every=every, piece=piece, whole=n_pieces == 1):
                    rows = None if whole else pl.ds(pl.multiple_of((step // every) * piece, PIECE_ROWS), piece)
                    for cp in copies(r, rows):
                        cp.start()

        body(*ins, *outs, *scr)

        if n_r:
            @pl.when(step == steps - 1)
            def _():
                for r in range(n_r):
                    for cp in copies(r):
                        cp.wait()

    params = pltpu.CompilerParams(dimension_semantics=("arbitrary",) * len(grid), vmem_limit_bytes=VMEM_LIMIT,
                                  has_side_effects=bool(riders))
    return list(pl.pallas_call(wrapped, name=name, grid=grid, in_specs=in_specs, out_specs=out_specs,
                               out_shape=out_shape, scratch_shapes=scratch, compiler_params=params)(*args))


def _riders(arrays, scatter):
    return [(a, scatter) for a in arrays]


def _nones(arrays):
    return tuple(None for _ in arrays)


def _divisors(n, step, cap):
    out = [d for d in range(step, min(n, cap) + 1, step) if n % d == 0]
    if n <= cap and n not in out:
        out.append(n)
    return out


def _mm_tiles(M, N, K, sa, sb, m_step, k_step):
    best, best_cost = None, None
    for tm in _divisors(M, m_step, 2304):
        for tn in _divisors(N, LANE, 2304):
            for tk in _divisors(K, k_step, 2304):
                vm = 2 * (tm * tk * sa + tk * tn * sb) + 3 * tm * tn * 4
                if vm > MM_BUDGET:
                    continue
                traffic = M * K * sa * (N // tn) + K * N * sb * (M // tm) + M * N * 4
                steps = (M // tm) * (N // tn) * (K // tk)
                passes = 1 if tk == K else 3
                busy = steps * (2 * tm * tn * tk / MXU_FLOPS + passes * tm * tn * 4 / VMEM_BPS)
                cost = max(traffic / HBM_BPS, busy) + steps * STEP_S
                if m_step == LANE:
                    cost = traffic / HBM_BPS + steps * STEP_S
                if tm % 256 or tn % 256:
                    cost *= 1.05
                if best_cost is None or cost < best_cost:
                    best, best_cost = (tm, tn, tk), cost
    assert best is not None, (M, N, K)
    return best


def _matmul(a, b, mode, name, riders=(), out_dtype=F32):
    if mode == "nn":
        (M, K), (K2, N) = a.shape, b.shape
    elif mode == "nt":
        (M, K), (N, K2) = a.shape, b.shape
    else:
        (K, M), (K2, N) = a.shape, b.shape
    assert K == K2, (a.shape, b.shape, mode)
    tm, tn, tk = _mm_tiles(M, N, K, a.dtype.itemsize, b.dtype.itemsize,
                           LANE if mode == "tn" else 8, 8 if mode == "tn" else LANE)
    nk = K // tk
    dims = {"nn": (((1,), (0,)), ((), ())), "nt": (((1,), (1,)), ((), ())), "tn": (((0,), (0,)), ((), ()))}[mode]

    own_acc = nk > 1 and out_dtype != F32

    def body(a_ref, b_ref, o_ref, *scr):
        prod = lax.dot_general(a_ref[...].astype(_MXU), b_ref[...].astype(_MXU), dims, preferred_element_type=F32)
        if nk == 1:
            o_ref[...] = prod.astype(out_dtype)
        else:
            acc_ref = scr[0] if own_acc else o_ref
            k = pl.program_id(2)

            @pl.when(k == 0)
            def _():
                acc_ref[...] = prod

            @pl.when(k > 0)
            def _():
                acc_ref[...] += prod

            if own_acc:
                @pl.when(k == nk - 1)
                def _():
                    o_ref[...] = acc_ref[...].astype(out_dtype)

    if mode == "nn":
        a_spec = pl.BlockSpec((tm, tk), lambda i, j, k: (i, k))
        b_spec = pl.BlockSpec((tk, tn), lambda i, j, k: (k, j))
    elif mode == "nt":
        a_spec = pl.BlockSpec((tm, tk), lambda i, j, k: (i, k))
        b_spec = pl.BlockSpec((tn, tk), lambda i, j, k: (j, k))
    else:
        a_spec = pl.BlockSpec((tk, tm), lambda i, j, k: (k, i))
        b_spec = pl.BlockSpec((tk, tn), lambda i, j, k: (k, j))
    outs = _pcall(body, name=name, grid=(M // tm, N // tn, nk), in_specs=[a_spec, b_spec],
                  out_specs=[pl.BlockSpec((tm, tn), lambda i, j, k: (i, j))],
                  out_shape=[jax.ShapeDtypeStruct((M, N), out_dtype)], args=[a, b],
                  scratch_shapes=[pltpu.VMEM((tm, tn), F32)] if own_acc else [], riders=riders)
    return outs if riders else outs[0]


def _make_linear(tag, to_parts, out_dtype=F32):
    def run(a, w, gin, tok):
        if gin:
            y, *got = _matmul(a, w, "nn", tag + "_fwd", riders=_riders(gin, False), out_dtype=out_dtype)
        else:
            y, got = _matmul(a, w, "nn", tag + "_fwd", out_dtype=out_dtype), []
        return y, tuple(got), tok

    @jax.custom_vjp
    def linear(a, w, sink, gin, tok):
        return run(a, w, gin, tok)

    def fwd(a, w, sink, gin, tok):
        return run(a, w, gin, tok), (a, w, gin)

    def bwd(res, cts):
        a, w, gin = res
        dy, _, dtok = cts
        dw = _matmul(a, dy, "tn", tag + "_bwd_dw")
        if dtok:
            da, *recv = _matmul(dy, w, "nt", tag + "_bwd_da", riders=_riders(dtok, True), out_dtype=a.dtype)
        else:
            da, recv = _matmul(dy, w, "nt", tag + "_bwd_da", out_dtype=a.dtype), []
        return da, jnp.zeros_like(w), to_parts(dw), _nones(gin), tuple(recv)

    linear.defvjp(fwd, bwd)
    return linear


def _seg_map(n_x_tiles):
    return lambda i: (jnp.where(i >= n_x_tiles, 1, 0), 0, 0)


def _ln_stats(x):
    mu = jnp.mean(x, axis=-1, keepdims=True)
    xc = x - mu
    var = jnp.mean(xc * xc, axis=-1, keepdims=True)
    rstd = lax.rsqrt(var + EPS)
    return xc * rstd, rstd


def _make_modulate(tag, n_x):
    def call_fwd(x, shift, scale):
        T, D = x.shape
        nxt = n_x // ROW_TILE

        def body(x_ref, sh_ref, sc_ref, o_ref):
            xhat, _ = _ln_stats(x_ref[...])
            o_ref[...] = xhat * (1.0 + sc_ref[0]) + sh_ref[0]

        row = pl.BlockSpec((ROW_TILE, D), lambda i: (i, 0))
        seg = pl.BlockSpec((1, 1, D), _seg_map(nxt))
        return pl.pallas_call(body, name=tag + "_fwd", grid=(T // ROW_TILE,), in_specs=[row, seg, seg],
                              out_specs=row, out_shape=jax.ShapeDtypeStruct((T, D), F32),
                              compiler_params=_cparams(1))(x, shift, scale)

    def call_bwd(x, scale, dh):
        T, D = x.shape
        nxt = n_x // ROW_TILE
        nseg = scale.shape[0]

        def body(x_ref, sc_ref, dh_ref, dx_ref, dsh_ref, dsc_ref):
            i = pl.program_id(0)

            @pl.when((i == 0) | (i == nxt))
            def _():
                dsh_ref[...] = jnp.zeros_like(dsh_ref)
                dsc_ref[...] = jnp.zeros_like(dsc_ref)

            xhat, rstd = _ln_stats(x_ref[...])
            dh = dh_ref[...]
            dsh_ref[0] += jnp.sum(dh, axis=0, keepdims=True)
            dsc_ref[0] += jnp.sum(dh * xhat, axis=0, keepdims=True)
            dxh = dh * (1.0 + sc_ref[0])
            m1 = jnp.mean(dxh, axis=-1, keepdims=True)
            m2 = jnp.mean(dxh * xhat, axis=-1, keepdims=True)
            dx_ref[...] = rstd * (dxh - m1 - xhat * m2)

        row = pl.BlockSpec((ROW_TILE, D), lambda i: (i, 0))
        seg = pl.BlockSpec((1, 1, D), _seg_map(nxt))
        segshape = jax.ShapeDtypeStruct((nseg, 1, D), F32)
        return pl.pallas_call(body, name=tag + "_bwd", grid=(T // ROW_TILE,), in_specs=[row, seg, row],
                              out_specs=[row, seg, seg],
                              out_shape=[jax.ShapeDtypeStruct((T, D), F32), segshape, segshape],
                              compiler_params=_cparams(1))(x, scale, dh)

    @jax.custom_vjp
    def modulate(x, shift, scale):
        return call_fwd(x, shift, scale)

    def fwd(x, shift, scale):
        return call_fwd(x, shift, scale), (x, scale)

    def bwd(res, dh):
        x, scale = res
        return tuple(call_bwd(x, scale, dh))

    modulate.defvjp(fwd, bwd)
    return modulate


def _make_resid_ln(tag, n_x, alpha):
    def call_fwd(x, y, gate, g, b):
        T, D = x.shape
        nxt = n_x // ROW_TILE

        def body(x_ref, y_ref, gt_ref, g_ref, b_ref, o_ref):
            u = alpha * x_ref[...] + gt_ref[0] * y_ref[...]
            uhat, _ = _ln_stats(u)
            o_ref[...] = uhat * g_ref[...] + b_ref[...]

        row = pl.BlockSpec((ROW_TILE, D), lambda i: (i, 0))
        seg = pl.BlockSpec((1, 1, D), _seg_map(nxt))
        vec = pl.BlockSpec((1, D), lambda i: (0, 0))
        return pl.pallas_call(body, name=tag + "_fwd", grid=(T // ROW_TILE,), in_specs=[row, row, seg, vec, vec],
                              out_specs=row, out_shape=jax.ShapeDtypeStruct((T, D), F32),
                              compiler_params=_cparams(1))(x, y, gate, g, b)

    def call_bwd(x, y, gate, g, do):
        T, D = x.shape
        nxt = n_x // ROW_TILE
        nseg = gate.shape[0]

        def body(x_ref, y_ref, gt_ref, g_ref, do_ref, dx_ref, dy_ref, dgt_ref, dg_ref, db_ref):
            i = pl.program_id(0)

            @pl.when(i == 0)
            def _():
                dg_ref[...] = jnp.zeros_like(dg_ref)
                db_ref[...] = jnp.zeros_like(db_ref)

            @pl.when((i == 0) | (i == nxt))
            def _():
                dgt_ref[...] = jnp.zeros_like(dgt_ref)

            y = y_ref[...]
            gate_v = gt_ref[0]
            uhat, rstd = _ln_stats(alpha * x_ref[...] + gate_v * y)
            do = do_ref[...]
            dg_ref[...] += jnp.sum(do * uhat, axis=0, keepdims=True)
            db_ref[...] += jnp.sum(do, axis=0, keepdims=True)
            duh = do * g_ref[...]
            m1 = jnp.mean(duh, axis=-1, keepdims=True)
            m2 = jnp.mean(duh * uhat, axis=-1, keepdims=True)
            du = rstd * (duh - m1 - uhat * m2)
            dx_ref[...] = alpha * du
            dy_ref[...] = gate_v * du
            dgt_ref[0] += jnp.sum(du * y, axis=0, keepdims=True)

        row = pl.BlockSpec((ROW_TILE, D), lambda i: (i, 0))
        seg = pl.BlockSpec((1, 1, D), _seg_map(nxt))
        vec = pl.BlockSpec((1, D), lambda i: (0, 0))
        rs = jax.ShapeDtypeStruct((T, D), F32)
        vs = jax.ShapeDtypeStruct((1, D), F32)
        return pl.pallas_call(body, name=tag + "_bwd", grid=(T // ROW_TILE,), in_specs=[row, row, seg, vec, row],
                              out_specs=[row, row, seg, vec, vec],
                              out_shape=[rs, rs, jax.ShapeDtypeStruct((nseg, 1, D), F32), vs, vs],
                              compiler_params=_cparams(1))(x, y, gate, g, do)

    @jax.custom_vjp
    def resid_ln(x, y, gate, g, b):
        return call_fwd(x, y, gate, g, b)

    def fwd(x, y, gate, g, b):
        return call_fwd(x, y, gate, g, b), (x, y, gate, g)

    def bwd(res, do):
        x, y, gate, g = res
        return tuple(call_bwd(x, y, gate, g, do))

    resid_ln.defvjp(fwd, bwd)
    return resid_ln


def _make_rmsnorm(tag):
    def call_fwd(x, g):
        T, W = x.shape
        gw = g.shape[1]

        def body(x_ref, g_ref, o_ref):
            x = x_ref[...]
            r = lax.rsqrt(jnp.mean(x * x, axis=-1, keepdims=True) + EPS)
            o_ref[...] = x * r * g_ref[...]

        blk = pl.BlockSpec((ROW_TILE, gw), lambda i, c: (i, c))
        vec = pl.BlockSpec((1, gw), lambda i, c: (0, 0))
        return pl.pallas_call(body, name=tag + "_fwd", grid=(T // ROW_TILE, W // gw), in_specs=[blk, vec],
                              out_specs=blk, out_shape=jax.ShapeDtypeStruct((T, W), F32),
                              compiler_params=_cparams(2))(x, g)

    def call_bwd(x, g, dy):
        T, W = x.shape
        gw = g.shape[1]

        def body(x_ref, g_ref, dy_ref, dx_ref, dg_ref):
            @pl.when((pl.program_id(0) == 0) & (pl.program_id(1) == 0))
            def _():
                dg_ref[...] = jnp.zeros_like(dg_ref)

            x = x_ref[...]
            r = lax.rsqrt(jnp.mean(x * x, axis=-1, keepdims=True) + EPS)
            xn = x * r
            dy = dy_ref[...]
            dg_ref[...] += jnp.sum(dy * xn, axis=0, keepdims=True)
            dxn = dy * g_ref[...]
            dx_ref[...] = r * (dxn - xn * jnp.mean(dxn * xn, axis=-1, keepdims=True))

        blk = pl.BlockSpec((ROW_TILE, gw), lambda i, c: (i, c))
        vec = pl.BlockSpec((1, gw), lambda i, c: (0, 0))
        return pl.pallas_call(body, name=tag + "_bwd", grid=(T // ROW_TILE, W // gw), in_specs=[blk, vec, blk],
                              out_specs=[blk, vec],
                              out_shape=[jax.ShapeDtypeStruct((T, W), F32), jax.ShapeDtypeStruct((1, gw), F32)],
                              compiler_params=_cparams(2))(x, g, dy)

    @jax.custom_vjp
    def rmsnorm(x, g):
        return call_fwd(x, g)

    def fwd(x, g):
        return call_fwd(x, g), (x, g)

    def bwd(res, dy):
        x, g = res
        return tuple(call_bwd(x, g, dy))

    rmsnorm.defvjp(fwd, bwd)
    return rmsnorm


def _rope_tables(n_x, n_all, width, half):
    t = np.arange(n_all)
    row, col = t // GRID_W, t % GRID_W
    lane = np.arange(LANE)
    inside = lane < width
    axis_is_col = (lane // (2 * half)) % 2 == 1
    f = (lane % (2 * half)) % half
    inv_freq = ROPE_THETA ** (-(f.astype(np.float64)) / half)
    pos = np.where(axis_is_col[None, :], col[:, None], row[:, None]).astype(np.float64)
    ang = (pos.astype(np.float32) * inv_freq.astype(np.float32)[None, :]).astype(np.float32)
    live = inside[None, :] & (t < n_x)[:, None]
    cos = np.where(live, np.cos(ang), 1.0).astype(np.float32)
    sin = np.where(live, np.sin(ang), 0.0).astype(np.float32)
    first = (lane % (2 * half)) < half
    s_lo = np.where(first[None, :], -sin, 0.0).astype(np.float32)
    s_hi = np.where(first[None, :], 0.0, sin).astype(np.float32)
    return jnp.asarray(cos), jnp.asarray(s_lo), jnp.asarray(s_hi)


def _make_rope(tag, half):
    def call(x, cos, s_lo, s_hi, transpose, name):
        T, W = x.shape

        def body(x_ref, c_ref, lo_ref, hi_ref, o_ref):
            x = x_ref[...]
            if not transpose:
                o_ref[...] = (x * c_ref[...] + pltpu.roll(x, LANE - half, 1) * lo_ref[...]
                              + pltpu.roll(x, half, 1) * hi_ref[...])
            else:
                o_ref[...] = (x * c_ref[...] + pltpu.roll(x * lo_ref[...], half, 1)
                              + pltpu.roll(x * hi_ref[...], LANE - half, 1))

        blk = pl.BlockSpec((ROW_TILE, LANE), lambda i, c: (i, c))
        tab = pl.BlockSpec((ROW_TILE, LANE), lambda i, c: (i, 0))
        return pl.pallas_call(body, name=name, grid=(T // ROW_TILE, W // LANE), in_specs=[blk, tab, tab, tab],
                              out_specs=blk, out_shape=jax.ShapeDtypeStruct((T, W), F32),
                              compiler_params=_cparams(2))(x, cos, s_lo, s_hi)

    @jax.custom_vjp
    def rope(x, cos, s_lo, s_hi):
        return call(x, cos, s_lo, s_hi, False, tag + "_fwd")

    def fwd(x, cos, s_lo, s_hi):
        return call(x, cos, s_lo, s_hi, False, tag + "_fwd"), (cos, s_lo, s_hi)

    def bwd(res, dy):
        cos, s_lo, s_hi = res
        return call(dy, cos, s_lo, s_hi, True, tag + "_bwd"), None, None, None

    rope.defvjp(fwd, bwd)
    return rope


def _make_conv_gate(tag, n_x):
    def shifted(v, T):
        t = lax.broadcasted_iota(jnp.int32, v.shape, 0)
        prev = jnp.where((t == 0) | (t == n_x), 0.0, pltpu.roll(v, 1, 0))
        nxt = jnp.where((t == n_x - 1) | (t == T - 1), 0.0, pltpu.roll(v, T - 1, 0))
        return prev, nxt

    def call_fwd(a, u, cw, cb, riders):
        T, Fd = a.shape

        def body(a_ref, u_ref, cw_ref, cb_ref, z_ref):
            pre = a_ref[...].astype(F32)
            prev, nxt = shifted(pre, T)
            s = prev * cw_ref[0:1, :] + pre * cw_ref[1:2, :] + nxt * cw_ref[2:3, :] + cb_ref[...]
            z_ref[...] = (s * (1.0 / (1.0 + jnp.exp(-s))) * u_ref[...].astype(F32)).astype(z_ref.dtype)

        col = pl.BlockSpec((T, LANE), lambda j: (0, j))
        return _pcall(body, name=tag + "_fwd", grid=(Fd // LANE,),
                      in_specs=[col, col, pl.BlockSpec((3, LANE), lambda j: (0, j)),
                                pl.BlockSpec((1, LANE), lambda j: (0, j))],
                      out_specs=[col], out_shape=[jax.ShapeDtypeStruct((T, Fd), a.dtype)], args=[a, u, cw, cb],
                      riders=riders)

    def call_bwd(a, u, cw, cb, dz, riders):
        T, Fd = a.shape

        def body(a_ref, u_ref, cw_ref, cb_ref, dz_ref, da_ref, du_ref, dcw_ref, dcb_ref):
            pre = a_ref[...].astype(F32)
            prev, nxt = shifted(pre, T)
            s = prev * cw_ref[0:1, :] + pre * cw_ref[1:2, :] + nxt * cw_ref[2:3, :] + cb_ref[...]
            sig = 1.0 / (1.0 + jnp.exp(-s))
            dz = dz_ref[...].astype(F32)
            du_ref[...] = (dz * s * sig).astype(du_ref.dtype)
            ds = dz * u_ref[...].astype(F32) * (sig * (1.0 + s * (1.0 - sig)))
            ds_prev, ds_next = shifted(ds, T)
            da_ref[...] = (ds * cw_ref[1:2, :] + ds_next * cw_ref[0:1, :] + ds_prev * cw_ref[2:3, :]).astype(
                da_ref.dtype)
            dcw_ref[0:1, :] = jnp.sum(ds * prev, axis=0, keepdims=True)
            dcw_ref[1:2, :] = jnp.sum(ds * pre, axis=0, keepdims=True)
            dcw_ref[2:3, :] = jnp.sum(ds * nxt, axis=0, keepdims=True)
            dcb_ref[...] = jnp.sum(ds, axis=0, keepdims=True)

        col = pl.BlockSpec((T, LANE), lambda j: (0, j))
        w3 = pl.BlockSpec((3, LANE), lambda j: (0, j))
        w1 = pl.BlockSpec((1, LANE), lambda j: (0, j))
        big = jax.ShapeDtypeStruct((T, Fd), a.dtype)
        return _pcall(body, name=tag + "_bwd", grid=(Fd // LANE,), in_specs=[col, col, w3, w1, col],
                      out_specs=[col, col, w3, w1],
                      out_shape=[big, big, jax.ShapeDtypeStruct((3, Fd), F32), jax.ShapeDtypeStruct((1, Fd), F32)],
                      args=[a, u, cw, cb, dz], riders=riders)

    def run(a, u, cw, cb, gin, tok):
        z, *got = call_fwd(a, u, cw, cb, _riders(gin, False))
        return z, tuple(got), tok

    @jax.custom_vjp
    def conv_gate(a, u, cw, cb, gin, tok):
        return run(a, u, cw, cb, gin, tok)

    def fwd(a, u, cw, cb, gin, tok):
        return run(a, u, cw, cb, gin, tok), (a, u, cw, cb, gin)

    def bwd(res, cts):
        a, u, cw, cb, gin = res
        dz, _, dtok = cts
        da, du, dcw, dcb, *recv = call_bwd(a, u, cw, cb, dz, _riders(dtok, True))
        return da, du, dcw, dcb, _nones(gin), tuple(recv)

    conv_gate.defvjp(fwd, bwd)
    return conv_gate


def _make_loss_head(tag):
    def call(y, tgt):
        T, D = y.shape

        def body(y_ref, t_ref, l_ref, r_ref):
            @pl.when(pl.program_id(0) == 0)
            def _():
                l_ref[...] = jnp.zeros_like(l_ref)

            d = y_ref[...] - t_ref[...]
            r_ref[...] = d * (1.0 / D)
            l_ref[...] += jnp.sum(d * d) * (0.5 / D)

        row = pl.BlockSpec((ROW_TILE, D), lambda i: (i, 0))
        return pl.pallas_call(body, name=tag, grid=(T // ROW_TILE,), in_specs=[row, row],
                              out_specs=[pl.BlockSpec((8, LANE), lambda i: (0, 0)), row],
                              out_shape=[jax.ShapeDtypeStruct((8, LANE), F32), jax.ShapeDtypeStruct((T, D), F32)],
                              compiler_params=_cparams(1))(y, tgt)

    @jax.custom_vjp
    def loss_head(y, tgt):
        return call(y, tgt)[0][0, 0]

    def fwd(y, tgt):
        l, r = call(y, tgt)
        return l[0, 0], r

    def bwd(r, g):
        return r * g, None

    loss_head.defvjp(fwd, bwd)
    return loss_head


class _AttnCfg:
    def __init__(self, H, Hkv, dqk, dv, scale, tq, tk, has_sink=False):
        self.H, self.Hkv, self.dqk, self.dv, self.scale = H, Hkv, dqk, dv, scale
        self.tq, self.tk, self.has_sink = tq, tk, has_sink
        self.G = H // Hkv


class _WinCfg:
    def __init__(self, mode, H, Hkv, scale, nx, n_ctx):
        self.mode, self.H, self.Hkv, self.scale, self.nx, self.n_ctx = mode, H, Hkv, scale, nx, n_ctx
        self.G = H // Hkv
        self.has_sink = mode == "swa"
        self.nwin, self.back = (3, 1) if mode == "swa" else (5, 2)
        self.width = self.nwin * ATT_BLOCK

    def first_block(self, i):
        return jnp.clip(i - self.back, 0, self.nx - self.nwin)

    def slots(self, i, jj):
        d = 2 * (self.first_block(i) + jj - i)
        return jnp.clip(d + 8, 0, 15), jnp.clip(d + 7, 0, 15)

    def local_scores(self, s, i, tab_ref):
        sub = lax.broadcasted_iota(jnp.int32, s.shape, 0)
        lan = lax.broadcasted_iota(jnp.int32, s.shape, 1)
        first = self.first_block(i)
        if self.mode == "swa":
            diff = (i * ATT_BLOCK + sub) - (first * ATT_BLOCK + lan)
            return jnp.where(jnp.abs(diff) <= SWA_WINDOW, s, -jnp.inf)
        rows = 2 * self.nx
        qrow = 2 * i + jnp.where(sub >= GRID_W, 1, 0)
        krow = 2 * first + lan // GRID_W
        r0 = jnp.clip(qrow - NA_WIN_R // 2, 0, rows - NA_WIN_R)
        ok = (krow >= r0) & (krow < r0 + NA_WIN_R)
        tiles = []
        for jj in range(self.nwin):
            top, bot = self.slots(i, jj)
            tiles.append(jnp.concatenate([tab_ref[0, top], tab_ref[0, bot]], axis=0))
        return jnp.where(ok, s + jnp.concatenate(tiles, axis=1), -jnp.inf)


def _win_specs(cfg, Tk):
    G = cfg.G
    qmap = lambda kh, g, i: (i, kh * G + g)
    q_spec = pl.BlockSpec((ATT_BLOCK, HEAD_DIM), qmap)
    kv_spec = pl.BlockSpec((Tk, HEAD_DIM), lambda kh, g, i: (0, kh))
    lse_spec = pl.BlockSpec((1, ATT_BLOCK, LANE), lambda kh, g, i: (kh * G + g, i, 0))
    tab_spec = pl.BlockSpec((1, 16, GRID_W, LANE), lambda kh, g, i: (kh * G + g, 0, 0, 0))
    sink_spec = pl.BlockSpec((1, 8, LANE), lambda kh, g, i: (kh * G + g, 0, 0))
    return q_spec, kv_spec, lse_spec, tab_spec, sink_spec


def _win_scores(cfg, i, q_ref, k_ref, tab_ref):
    dn = (((1,), (1,)), ((), ()))
    n_x = cfg.nx * ATT_BLOCK
    win = pl.ds(pl.multiple_of(cfg.first_block(i) * ATT_BLOCK, ATT_BLOCK), cfg.width)
    ctx = pl.ds(n_x, cfg.n_ctx)
    qb = q_ref[...].astype(_MXU)
    s_loc = lax.dot_general(qb, k_ref[win, :].astype(_MXU), dn, preferred_element_type=F32) * cfg.scale
    s_ctx = lax.dot_general(qb, k_ref[ctx, :].astype(_MXU), dn, preferred_element_type=F32) * cfg.scale
    return cfg.local_scores(s_loc, i, tab_ref), s_ctx, win, ctx


def _win_fwd_call(cfg, name, q, k, v, tab, sink, riders=()):
    Tq, Tk = q.shape[0], k.shape[0]

    def body(*refs):
        q_ref, k_ref, v_ref = refs[:3]
        extra = refs[3] if (cfg.mode == "na" or cfg.has_sink) else None
        o_ref, lse_ref = refs[-2:]
        i = pl.program_id(2)
        s_loc, s_ctx, win, ctx = _win_scores(cfg, i, q_ref, k_ref, extra if cfg.mode == "na" else None)
        m = jnp.maximum(jnp.max(s_loc, axis=-1, keepdims=True), jnp.max(s_ctx, axis=-1, keepdims=True))
        if cfg.has_sink:
            m = jnp.maximum(m, extra[0, 0:1, 0:1])
        p_loc, p_ctx = jnp.exp(s_loc - m), jnp.exp(s_ctx - m)
        l = jnp.sum(p_loc, axis=-1, keepdims=True) + jnp.sum(p_ctx, axis=-1, keepdims=True)
        if cfg.has_sink:
            l = l + jnp.exp(extra[0, 0:1, 0:1] - m)
        acc = (jnp.dot(p_loc.astype(_MXU), v_ref[win, :].astype(_MXU), preferred_element_type=F32)
               + jnp.dot(p_ctx.astype(_MXU), v_ref[ctx, :].astype(_MXU), preferred_element_type=F32))
        o_ref[...] = acc / l
        lse_ref[0] = jnp.broadcast_to(m + jnp.log(l), (ATT_BLOCK, LANE))

    q_spec, kv_spec, lse_spec, tab_spec, sink_spec = _win_specs(cfg, Tk)
    in_specs, args = [q_spec, kv_spec, kv_spec], [q, k, v]
    if cfg.mode == "na":
        in_specs.append(tab_spec)
        args.append(tab)
    if cfg.has_sink:
        in_specs.append(sink_spec)
        args.append(sink)
    return _pcall(
        body, name=name, grid=(cfg.Hkv, cfg.G, Tq // ATT_BLOCK), in_specs=in_specs, out_specs=[q_spec, lse_spec],
        out_shape=[jax.ShapeDtypeStruct((Tq, cfg.H * HEAD_DIM), F32), jax.ShapeDtypeStruct((cfg.H, Tq, LANE), F32)],
        args=args, riders=riders)


def _win_bwd_call(cfg, name, q, k, v, tab, sink, o, lse, do, riders=()):
    Tq, Tk = q.shape[0], k.shape[0]
    dn_nt = (((1,), (1,)), ((), ()))
    dn_tn = (((0,), (0,)), ((), ()))

    def body(*refs):
        q_ref, k_ref, v_ref = refs[:3]
        has_extra = cfg.mode == "na" or cfg.has_sink
        extra = refs[3] if has_extra else None
        n = 4 if has_extra else 3
        o_ref, lse_ref, do_ref, dq_ref, dk_ref, dv_ref = refs[n:n + 6]
        dextra = refs[n + 6] if has_extra else None
        g, i = pl.program_id(1), pl.program_id(2)

        @pl.when((g == 0) & (i == 0))
        def _():
            dk_ref[...] = jnp.zeros_like(dk_ref)
            dv_ref[...] = jnp.zeros_like(dv_ref)

        if has_extra:
            @pl.when(i == 0)
            def _():
                dextra[...] = jnp.zeros_like(dextra)

        s_loc, s_ctx, win, ctx = _win_scores(cfg, i, q_ref, k_ref, extra if cfg.mode == "na" else None)
        lse = lse_ref[0][:, 0:1]
        p_loc, p_ctx = jnp.exp(s_loc - lse), jnp.exp(s_ctx - lse)
        do_f = do_ref[...]
        do_b = do_f.astype(_MXU)
        delta = jnp.sum(do_f * o_ref[...], axis=-1, keepdims=True)
        dp_loc = lax.dot_general(do_b, v_ref[win, :].astype(_MXU), dn_nt, preferred_element_type=F32)
        dp_ctx = lax.dot_general(do_b, v_ref[ctx, :].astype(_MXU), dn_nt, preferred_element_type=F32)
        ds_loc, ds_ctx = p_loc * (dp_loc - delta), p_ctx * (dp_ctx - delta)
        if cfg.mode == "na":
            for jj in range(cfg.nwin):
                top, bot = cfg.slots(i, jj)
                tile = ds_loc[:, jj * ATT_BLOCK:(jj + 1) * ATT_BLOCK]
                dextra[0, top] += tile[0:GRID_W, :]
                dextra[0, bot] += tile[GRID_W:, :]
        if cfg.has_sink:
            p_sink = jnp.exp(extra[0, 0:1, 0:1] - lse)
            lane0 = ((lax.broadcasted_iota(jnp.int32, (8, LANE), 0) == 0)
                     & (lax.broadcasted_iota(jnp.int32, (8, LANE), 1) == 0))
            dextra[0] += jnp.where(lane0, -jnp.sum(p_sink * delta), 0.0)
        dsb_loc, dsb_ctx = (ds_loc * cfg.scale).astype(_MXU), (ds_ctx * cfg.scale).astype(_MXU)
        qb = q_ref[...].astype(_MXU)
        dq_ref[...] = (jnp.dot(dsb_loc, k_ref[win, :].astype(_MXU), preferred_element_type=F32)
                       + jnp.dot(dsb_ctx, k_ref[ctx, :].astype(_MXU), preferred_element_type=F32))
        dk_ref[win, :] += lax.dot_general(dsb_loc, qb, dn_tn, preferred_element_type=F32)
        dk_ref[ctx, :] += lax.dot_general(dsb_ctx, qb, dn_tn, preferred_element_type=F32)
        dv_ref[win, :] += lax.dot_general(p_loc.astype(_MXU), do_b, dn_tn, preferred_element_type=F32)
        dv_ref[ctx, :] += lax.dot_general(p_ctx.astype(_MXU), do_b, dn_tn, preferred_element_type=F32)

    q_spec, kv_spec, lse_spec, tab_spec, sink_spec = _win_specs(cfg, Tk)
    in_specs, args = [q_spec, kv_spec, kv_spec], [q, k, v]
    out_specs = [q_spec, kv_spec, kv_spec]
    out_shape = [jax.ShapeDtypeStruct(q.shape, F32), jax.ShapeDtypeStruct(k.shape, F32),
                 jax.ShapeDtypeStruct(v.shape, F32)]
    if cfg.mode == "na":
        in_specs.append(tab_spec)
        args.append(tab)
        out_specs.append(tab_spec)
        out_shape.append(jax.ShapeDtypeStruct(tab.shape, F32))
    if cfg.has_sink:
        in_specs.append(sink_spec)
        args.append(sink)
        out_specs.append(sink_spec)
        out_shape.append(jax.ShapeDtypeStruct(sink.shape, F32))
    in_specs += [q_spec, lse_spec, q_spec]
    args += [o, lse, do]
    return _pcall(body, name=name, grid=(cfg.Hkv, cfg.G, Tq // ATT_BLOCK), in_specs=in_specs, out_specs=out_specs,
                  out_shape=out_shape, args=args, riders=riders)


def _attn_specs(cfg, Tk):
    G, tq = cfg.G, cfg.tq
    qmap = lambda kh, g, i: (i, kh * G + g)
    q_spec = pl.BlockSpec((tq, cfg.dqk), qmap)
    k_spec = pl.BlockSpec((Tk, cfg.dqk), lambda kh, g, i: (0, kh))
    v_spec = pl.BlockSpec((Tk, cfg.dv), lambda kh, g, i: (0, kh))
    o_spec = pl.BlockSpec((tq, cfg.dv), qmap)
    lse_spec = pl.BlockSpec((1, tq, LANE), lambda kh, g, i: (kh * G + g, i, 0))
    sink_spec = pl.BlockSpec((1, 8, LANE), lambda kh, g, i: (kh * G + g, 0, 0))
    return q_spec, k_spec, v_spec, o_spec, lse_spec, sink_spec


def _attn_fwd_call(cfg, name, q, k, v, sink, riders=()):
    Tq, Tk = q.shape[0], k.shape[0]
    nq, nkv = Tq // cfg.tq, Tk // cfg.tk
    dn = (((1,), (1,)), ((), ()))

    def body(*refs):
        q_ref, k_ref, v_ref = refs[:3]
        sink_ref = refs[3] if cfg.has_sink else None
        o_ref, lse_ref = refs[-2:]
        qb = q_ref[...].astype(_MXU)
        m = jnp.full((cfg.tq, 1), NEG, F32)
        l = jnp.zeros((cfg.tq, 1), F32)
        acc = jnp.zeros((cfg.tq, cfg.dv), F32)
        for j in range(nkv):
            rows = pl.ds(j * cfg.tk, cfg.tk)
            s = lax.dot_general(qb, k_ref[rows, :].astype(_MXU), dn, preferred_element_type=F32) * cfg.scale
            m_new = jnp.maximum(m, jnp.max(s, axis=-1, keepdims=True))
            alpha = jnp.exp(m - m_new)
            p = jnp.exp(s - m_new)
            l = alpha * l + jnp.sum(p, axis=-1, keepdims=True)
            acc = alpha * acc + jnp.dot(p.astype(_MXU), v_ref[rows, :].astype(_MXU), preferred_element_type=F32)
            m = m_new
        if cfg.has_sink:
            sk = sink_ref[0, 0:1, 0:1]
            m_new = jnp.maximum(m, sk)
            alpha = jnp.exp(m - m_new)
            l = alpha * l + jnp.exp(sk - m_new)
            acc = acc * alpha
            m = m_new
        o_ref[...] = acc / l
        lse_ref[0] = jnp.broadcast_to(m + jnp.log(l), (cfg.tq, LANE))

    q_spec, k_spec, v_spec, o_spec, lse_spec, sink_spec = _attn_specs(cfg, Tk)
    in_specs, args = [q_spec, k_spec, v_spec], [q, k, v]
    if cfg.has_sink:
        in_specs.append(sink_spec)
        args.append(sink)
    return _pcall(
        body, name=name, grid=(cfg.Hkv, cfg.G, nq), in_specs=in_specs, out_specs=[o_spec, lse_spec],
        out_shape=[jax.ShapeDtypeStruct((Tq, cfg.H * cfg.dv), F32), jax.ShapeDtypeStruct((cfg.H, Tq, LANE), F32)],
        args=args, riders=riders)


def _attn_bwd_call(cfg, name, q, k, v, sink, o, lse, do, riders=()):
    Tq, Tk = q.shape[0], k.shape[0]
    nq, nkv = Tq // cfg.tq, Tk // cfg.tk
    tq, tk = cfg.tq, cfg.tk
    dn_nt = (((1,), (1,)), ((), ()))
    dn_tn = (((0,), (0,)), ((), ()))

    def body(*refs):
        q_ref, k_ref, v_ref = refs[:3]
        n = 3
        sink_ref = dsink_ref = None
        if cfg.has_sink:
            sink_ref, n = refs[n], n + 1
        o_ref, lse_ref, do_ref, dq_ref, dk_ref, dv_ref = refs[n:n + 6]
        n += 6
        if cfg.has_sink:
            dsink_ref, n = refs[n], n + 1
        g, i = pl.program_id(1), pl.program_id(2)

        @pl.when((g == 0) & (i == 0))
        def _():
            dk_ref[...] = jnp.zeros_like(dk_ref)
            dv_ref[...] = jnp.zeros_like(dv_ref)

        if cfg.has_sink:
            @pl.when(i == 0)
            def _():
                dsink_ref[...] = jnp.zeros_like(dsink_ref)

        do_f = do_ref[...]
        do_b = do_f.astype(_MXU)
        qb = q_ref[...].astype(_MXU)
        lse = lse_ref[0][:, 0:1]
        delta = jnp.sum(do_f * o_ref[...], axis=-1, keepdims=True)
        if cfg.has_sink:
            p_sink = jnp.exp(sink_ref[0, 0:1, 0:1] - lse)
            lane0 = ((lax.broadcasted_iota(jnp.int32, (8, LANE), 0) == 0)
                     & (lax.broadcasted_iota(jnp.int32, (8, LANE), 1) == 0))
            dsink_ref[0] += jnp.where(lane0, -jnp.sum(p_sink * delta), 0.0)
        dq = jnp.zeros((tq, cfg.dqk), F32)
        for j in range(nkv):
            rows = pl.ds(j * tk, tk)
            kb, vb = k_ref[rows, :].astype(_MXU), v_ref[rows, :].astype(_MXU)
            s = lax.dot_general(qb, kb, dn_nt, preferred_element_type=F32) * cfg.scale
            p = jnp.exp(s - lse)
            dp = lax.dot_general(do_b, vb, dn_nt, preferred_element_type=F32)
            dsb = (p * (dp - delta) * cfg.scale).astype(_MXU)
            dq = dq + jnp.dot(dsb, kb, preferred_element_type=F32)
            dk_ref[rows, :] += lax.dot_general(dsb, qb, dn_tn, preferred_element_type=F32)
            dv_ref[rows, :] += lax.dot_general(p.astype(_MXU), do_b, dn_tn, preferred_element_type=F32)
        dq_ref[...] = dq

    q_spec, k_spec, v_spec, o_spec, lse_spec, sink_spec = _attn_specs(cfg, Tk)
    in_specs, args = [q_spec, k_spec, v_spec], [q, k, v]
    if cfg.has_sink:
        in_specs.append(sink_spec)
        args.append(sink)
    in_specs += [o_spec, lse_spec, o_spec]
    args += [o, lse, do]
    out_specs = [q_spec, k_spec, v_spec]
    out_shape = [jax.ShapeDtypeStruct(q.shape, F32), jax.ShapeDtypeStruct(k.shape, F32),
                 jax.ShapeDtypeStruct(v.shape, F32)]
    if cfg.has_sink:
        out_specs.append(sink_spec)
        out_shape.append(jax.ShapeDtypeStruct(sink.shape, F32))
    return _pcall(body, name=name, grid=(cfg.Hkv, cfg.G, nq), in_specs=in_specs, out_specs=out_specs,
                  out_shape=out_shape, args=args, riders=riders)


def _make_attention(tag, cfg):
    windowed = isinstance(cfg, _WinCfg)

    def run_fwd(q, k, v, tab, sink, gin):
        if windowed:
            return _win_fwd_call(cfg, tag + "_fwd", q, k, v, tab, sink, _riders(gin, False))
        return _attn_fwd_call(cfg, tag + "_fwd", q, k, v, sink, _riders(gin, False))

    @jax.custom_vjp
    def attn(q, k, v, tab, sink, gin, tok):
        o, _, *got = run_fwd(q.astype(_MXU), k.astype(_MXU), v.astype(_MXU), tab, sink, gin)
        return o, tuple(got), tok

    def fwd(q, k, v, tab, sink, gin, tok):
        q, k, v = q.astype(_MXU), k.astype(_MXU), v.astype(_MXU)
        o, lse, *got = run_fwd(q, k, v, tab, sink, gin)
        return (o, tuple(got), tok), (q, k, v, tab, sink, gin, o, lse)

    def bwd(res, cts):
        q, k, v, tab, sink, gin, o, lse = res
        do, _, dtok = cts
        if windowed:
            outs = _win_bwd_call(cfg, tag + "_bwd", q, k, v, tab, sink, o, lse, do, _riders(dtok, True))
        else:
            outs = _attn_bwd_call(cfg, tag + "_bwd", q, k, v, sink, o, lse, do, _riders(dtok, True))
        dq, dk, dv = outs[:3]
        rest = outs[3:]
        dtab = rest.pop(0) if tab is not None else None
        dsink = rest.pop(0) if sink is not None else None
        return dq, dk, dv, dtab, dsink, _nones(gin), tuple(rest)

    attn.defvjp(fwd, bwd)
    return attn


def _pick_block(n, prefs):
    for p in prefs:
        if n % p == 0:
            return p
    return n


def _na_table(rpb):
    cq = np.arange(GRID_W)
    dcol = np.clip(cq[None, :] - cq[:, None] + NA_WIN_C - 1, 0, 2 * NA_WIN_C - 2)
    onehot = (dcol[:, :, None] == np.arange(2 * NA_WIN_C - 1)[None, None, :]).astype(np.float32)
    c0 = np.clip(cq - NA_WIN_C // 2, 0, GRID_W - NA_WIN_C)
    col_in = (cq[None, :] >= c0[:, None]) & (cq[None, :] < c0[:, None] + NA_WIN_C)
    tz = jnp.einsum("hrj,qkj->hrqk", rpb, jnp.asarray(onehot), precision=lax.Precision.HIGHEST)
    tz = jnp.where(jnp.asarray(col_in)[None, None], tz, NEG)
    zero = jnp.zeros_like(tz[:, :1])
    tzp = jnp.concatenate([zero, tz, zero], axis=1)
    return jnp.concatenate([tzp[:, 0:16], tzp[:, 1:17]], axis=-1)


def _sink_block(sink):
    return jnp.broadcast_to(sink[:, None, None], (sink.shape[0], 8, LANE))


def _make_split(bounds, axis):
    @jax.custom_vjp
    def split(t):
        return tuple(lax.slice_in_dim(t, a, b, axis=axis) for a, b in bounds)

    def fwd(t):
        return split(t), None

    def bwd(_, cts):
        return (jnp.concatenate(cts, axis=axis),)

    split.defvjp(fwd, bwd)
    return split


_IN_GROUPS = ((0, 512), (512, 1024), (1024, 1536), (1536, 2048), (2048, 2304), (2304, 2560), (2560, 2944),
              (2944, 3072), (3072, 3200), (3200, 3712), (3712, 3968), (3968, 4224))


def _seg2(vx, vc, nseg):
    rows = [vx, vc][:nseg]
    return jnp.stack(rows)[:, None, :]


_BIG_COL = ("w_in", "mla_w_uq", "mla_w_ukv", "ffn_w_gate", "ffn_w_up")
_BIG_ROW = ("w_out", "ffn_w_down")
_IN_GROUP = ("w_in", "mla_w_uq", "mla_w_ukv")
_PIECES = {"w_in": 4, "ffn_w_down": 2}


def _n_pieces(name):
    return _PIECES.get(name, 1)


def _shard_view(name, w):
    return jnp.swapaxes(w, 1, 2) if name in _BIG_COL else w


def _cut(name, a):
    return tuple(jnp.split(a, _n_pieces(name), axis=-1))


def _full_weight(name, pieces):
    g = jnp.concatenate(pieces, axis=-1) if len(pieces) > 1 else pieces[0]
    full = g.reshape(N_DEV * g.shape[1], g.shape[2])
    if name in _BIG_ROW:
        return full
    full = full.T
    if name == "w_in":
        K = full.shape[0]
        full = jnp.concatenate([full[:, :KPE_END], jnp.zeros((K, 64), full.dtype), full[:, KPE_END:]], axis=-1)
    return full


def _grad_parts(name):
    def to_parts(dw):
        if name == "w_in":
            dw = jnp.concatenate([dw[:, :KPE_END], dw[:, KPE_END + 64:]], axis=-1)
        if name in _BIG_COL:
            dw = dw.T
        return _cut(name, dw.reshape(N_DEV, dw.shape[0] // N_DEV, dw.shape[1]).astype(BF16))

    return to_parts


_RIDE_FWD = {
    "attn_a": ((0, "w_out", 0),), "attn_b": ((1, "w_in", 0), (1, "w_in", 1)), "attn_c": ((0, "ffn_w_gate", 0),),
    "attn_d": ((0, "ffn_w_up", 0),), "gate": ((0, "ffn_w_down", 0),), "up": ((0, "ffn_w_down", 1),),
    "conv": ((1, "mla_w_uq", 0), (1, "mla_w_ukv", 0)), "down": ((1, "w_in", 2), (1, "w_in", 3)),
}
_RIDE_BWD = {
    "attn_a": ((0, "w_out", 0),), "attn_b": ((1, "mla_w_uq", 0), (1, "mla_w_ukv", 0)),
    "attn_c": ((0, "ffn_w_gate", 0),), "attn_d": ((0, "ffn_w_up", 0),), "gate": ((0, "ffn_w_down", 0),),
    "up": ((0, "ffn_w_down", 1),), "conv": ((1, "w_in", 0), (1, "w_in", 1)), "down": ((1, "w_in", 2), (1, "w_in", 3)),
}


def _local_loss(x, ctx, tgt, modx, modc, p, full, shards, toks):
    S, D = x.shape
    C = ctx.shape[0]
    depth = len(full)
    ride = shards is not None
    have = [dict(f) for f in full]
    landing = {}
    sinks = {(l, n): list(toks[l][n]) for l in range(depth) for n in toks[l]}
    alpha = (2 * depth) ** 0.25
    T0 = S + C
    nx = S // ATT_BLOCK

    def riding(table, unit, l):
        return [(l + dl, n, k) for dl, n, k in table[unit] if l + dl < depth] if ride else []

    def gin(unit, l):
        return tuple(shards[ll][n][k] for ll, n, k in riding(_RIDE_FWD, unit, l))

    def tok(unit, l):
        return tuple(sinks[(ll, n)][k] for ll, n, k in riding(_RIDE_BWD, unit, l))

    def landed(result, unit, l):
        out, got, tk = result
        for (ll, n, k), g in zip(riding(_RIDE_FWD, unit, l), got):
            landing.setdefault((ll, n), {})[k] = g
            if len(landing[(ll, n)]) == _n_pieces(n):
                have[ll][n] = _full_weight(n, [landing[(ll, n)][j] for j in range(_n_pieces(n))])
        for (ll, n, k), s in zip(riding(_RIDE_BWD, unit, l), tk):
            sinks[(ll, n)][k] = s
        return out

    sc = HEAD_DIM ** -0.5
    sc_mla = (MLA_NOPE + MLA_ROPE) ** -0.5
    tq_full = _pick_block(S, (256, 128))
    tk_all = _pick_block(T0, (2176, 640, 512, 256, 128))
    tctx = _pick_block(C, (256, 128))
    rope128 = _rope_tables(S, T0, 128, 32)
    rope64 = _rope_tables(S, T0, 64, 16)

    xs = jnp.concatenate([x, ctx], axis=0)
    for l in range(depth):
        last = l == depth - 1
        t = "l%d_" % l
        lin = lambda a, name, unit=None, out=F32: _make_linear(t + name, _grad_parts(name), out)(
            a, have[l][name], tuple(sinks[(l, name)]), gin(unit, l) if unit else (), tok(unit, l) if unit else ())
        mx, mc = modx[l], modc[l]

        h = _make_modulate(t + "mod1", S)(xs, _seg2(mx[0], mc[0], 2), _seg2(mx[1], mc[1], 2))
        pj = lin(h, "w_in")[0]
        qa, ka, va, qb, kb, vb, cq, ckv, kpe, qd, kd, vd = _make_split(_IN_GROUPS, 1)(pj)
        rows = lambda v_: _make_split(((0, S), (S, T0)), 0)(v_)
        att = lambda unit, cfg, q_, k_, v_, tab_, s_: landed(
            _make_attention(t + unit, cfg)(q_, k_, v_, tab_, s_, gin(unit, l), tok(unit, l)), unit, l)
        tab = _na_table(p["na_rpb"][l])
        cfg_a = _WinCfg("na", NA_HEADS, NA_HEADS, sc, nx, C)
        qa_x, qa_c = rows(qa)
        oa = att("attn_a", cfg_a, qa_x, ka, va, tab, None)
        qb = _make_rope(t + "rope_b", 32)(qb, *rope128)
        kb = _make_rope(t + "rope_bk", 32)(kb, *rope128)
        snk = _sink_block(p["swa_sink"][l])
        cfg_b = _WinCfg("swa", SWA_HEADS, SWA_KV_HEADS, sc, nx, C)
        qb_x, qb_c = rows(qb)
        ob = att("attn_b", cfg_b, qb_x, kb, vb, None, snk)
        cq = _make_rmsnorm(t + "rms_cq")(cq, p["mla_q_norm"][l][None, :])
        ckv = _make_rmsnorm(t + "rms_ckv")(ckv, p["mla_kv_norm"][l][None, :])
        qh = lin(cq, "mla_w_uq")[0].reshape(T0, MLA_HEADS, MLA_NOPE + MLA_ROPE)
        kvh = lin(ckv, "mla_w_ukv")[0].reshape(T0, MLA_HEADS, MLA_NOPE + MLA_V)
        qpe = jnp.pad(qh[:, :, MLA_NOPE:], ((0, 0), (0, 0), (0, LANE - MLA_ROPE))).reshape(T0, MLA_HEADS * LANE)
        qpe = _make_rope(t + "rope_cq", 16)(qpe, *rope64).reshape(T0, MLA_HEADS, LANE)
        kpe = _make_rope(t + "rope_ck", 16)(kpe, *rope64)
        qc = jnp.concatenate([qh[:, :, :MLA_NOPE], qpe], axis=-1).reshape(T0, MLA_HEADS * 2 * LANE)
        kc = jnp.concatenate([kvh[:, :, :MLA_NOPE], jnp.broadcast_to(kpe[:, None, :], (T0, MLA_HEADS, LANE))],
                             axis=-1).reshape(T0, MLA_HEADS * 2 * LANE)
        vc = kvh[:, :, MLA_NOPE:].reshape(T0, MLA_HEADS * MLA_V)
        cfg_c = _AttnCfg(MLA_HEADS, MLA_HEADS, 2 * LANE, MLA_V, sc_mla, tq_full, tk_all)
        qc_x, qc_c = rows(qc)
        oc = att("attn_c", cfg_c, qc_x, kc, vc, None, None)
        qd = _make_rmsnorm(t + "rms_dq")(qd, p["gqa_q_norm"][l][None, :])
        kd = _make_rmsnorm(t + "rms_dk")(kd, p["gqa_k_norm"][l][None, :])
        qd = _make_rope(t + "rope_dq", 32)(qd, *rope128)
        kd = _make_rope(t + "rope_dk", 32)(kd, *rope128)
        cfg_d = _AttnCfg(GQA_HEADS, GQA_KV_HEADS, 128, 128, sc, tq_full, tk_all)
        qd_x, qd_c = rows(qd)
        od = att("attn_d", cfg_d, qd_x, kd, vd, None, None)
        mix = jnp.concatenate([oa, ob, oc, od], axis=1)
        if not last:
            full_c = lambda H, Hkv, dqk, s, sink: _AttnCfg(H, Hkv, dqk, 128, s, tctx, tctx, has_sink=sink)
            ctx_att = lambda name, cfg, q_, k_, v_, s_: _make_attention(t + name, cfg)(
                q_, k_[S:], v_[S:], None, s_, (), ())[0]
            oa_c = ctx_att("ctx_a", full_c(4, 4, 128, sc, False), qa_c, ka, va, None)
            ob_c = ctx_att("ctx_b", full_c(4, 2, 128, sc, True), qb_c, kb, vb, snk)
            oc_c = ctx_att("ctx_c", full_c(4, 4, 256, sc_mla, False), qc_c, kc, vc, None)
            od_c = ctx_att("ctx_d", full_c(4, 2, 128, sc, False), qd_c, kd, vd, None)
            mix = jnp.concatenate([mix, jnp.concatenate([oa_c, ob_c, oc_c, od_c], axis=1)], axis=0)
            res, nseg = xs, 2
        else:
            res, nseg = xs[:S], 1
        y = lin(mix, "w_out")[0]
        x1 = _make_resid_ln(t + "ln1", S, alpha)(res, y, _seg2(mx[2], mc[2], nseg), p["ln1_g"][l][None, :],
                                                 p["ln1_b"][l][None, :])
        h2 = _make_modulate(t + "mod2", S)(x1, _seg2(mx[3], mc[3], nseg), _seg2(mx[4], mc[4], nseg))
        gp = landed(lin(h2, "ffn_w_gate", "gate", _ACT), "gate", l)
        up = landed(lin(h2, "ffn_w_up", "up", _ACT), "up", l)
        z = landed(_make_conv_gate(t + "conv", S)(gp, up, p["ffn_conv_w"][l], p["ffn_conv_b"][l][None, :],
                                                  gin("conv", l), tok("conv", l)), "conv", l)
        f = landed(lin(z, "ffn_w_down", "down"), "down", l)
        xs = _make_resid_ln(t + "ln2", S, alpha)(x1, f, _seg2(mx[5], mc[5], nseg), p["ln2_g"][l][None, :],
                                                 p["ln2_b"][l][None, :])
    return _make_loss_head("loss_head")(xs, tgt)


def _adamw(w, parts, m, v, name):
    R, C = w.shape
    P = parts.shape[0]
    c1 = 1.0 - ADAM_B1 ** ADAM_STEP
    c2 = 1.0 - ADAM_B2 ** ADAM_STEP
    per_row = C * (4 * 14 + 2 * P * parts.dtype.itemsize)
    cands = _divisors(R, 8, R)
    fitting = [d for d in cands if d * per_row <= 24 * 1024 * 1024]
    tr = max(fitting) if fitting else min(cands)

    def body(w_ref, p_ref, m_ref, v_ref, g_ref, d_ref, nm_ref, nv_ref):
        g = p_ref[0].astype(F32)
        for k in range(1, P):
            g = g + p_ref[k].astype(F32)
        mm = ADAM_B1 * m_ref[...] + (1.0 - ADAM_B1) * g
        vv = ADAM_B2 * v_ref[...] + (1.0 - ADAM_B2) * (g * g)
        g_ref[...] = g
        nm_ref[...] = mm
        nv_ref[...] = vv
        d_ref[...] = -ADAM_LR * ((mm / c1) / (jnp.sqrt(vv / c2) + ADAM_EPS) + ADAM_WD * w_ref[...])

    blk = pl.BlockSpec((tr, C), lambda i: (i, 0))
    pblk = pl.BlockSpec((P, tr, C), lambda i: (0, i, 0))
    sh = jax.ShapeDtypeStruct((R, C), F32)
    return pl.pallas_call(body, name=name, grid=(R // tr,), in_specs=[blk, pblk, blk, blk],
                          out_specs=[blk, blk, blk, blk], out_shape=[sh, sh, sh, sh],
                          compiler_params=_cparams(1))(w, parts, m, v)


def _adamw_nd(w, parts, m, v, name):
    shape = w.shape
    C = shape[-1]
    R = int(np.prod(shape[:-1])) if len(shape) > 1 else 1
    outs = _adamw(w.reshape(R, C), parts.reshape(parts.shape[0], R, C), m.reshape(R, C), v.reshape(R, C), name)
    return [o.reshape(shape) for o in outs]


def _adamw_layers(w, parts, m, v, name, riders=()):
    L, R, C = w.shape
    P = parts[0].shape[0]
    c1 = 1.0 - ADAM_B1 ** ADAM_STEP
    c2 = 1.0 - ADAM_B2 ** ADAM_STEP
    per_row = C * (4 * 14 + 2 * L * P * parts[0].dtype.itemsize)
    cands = _divisors(R, 8, R)
    fitting = [d for d in cands if d * per_row <= 24 * 1024 * 1024]
    tr = max(fitting) if fitting else min(cands)
    nt = R // tr

    def body(*refs):
        w_ref, p_refs, (m_ref, v_ref, g_ref, d_ref, nm_ref, nv_ref) = refs[0], refs[1:1 + L], refs[1 + L:]
        layer = pl.program_id(0)
        for k in range(L):
            @pl.when(layer == k)
            def _(p_ref=p_refs[k]):
                g = p_ref[0].astype(F32)
                for j in range(1, P):
                    g = g + p_ref[j].astype(F32)
                mm = ADAM_B1 * m_ref[0] + (1.0 - ADAM_B1) * g
                vv = ADAM_B2 * v_ref[0] + (1.0 - ADAM_B2) * (g * g)
                g_ref[0] = g
                nm_ref[0] = mm
                nv_ref[0] = vv
                d_ref[0] = -ADAM_LR * ((mm / c1) / (jnp.sqrt(vv / c2) + ADAM_EPS) + ADAM_WD * w_ref[0])

    blk = pl.BlockSpec((1, tr, C), lambda l, i: (l, i, 0))
    pblk = [pl.BlockSpec((P, tr, C), lambda l, i, k=k: (0, jnp.where(l == k, i, jnp.where(l < k, 0, nt - 1)), 0))
            for k in range(L)]
    sh = jax.ShapeDtypeStruct((L, R, C), F32)
    return _pcall(body, name=name, grid=(L, nt), in_specs=[blk] + pblk + [blk, blk],
                  out_specs=[blk, blk, blk, blk], out_shape=[sh, sh, sh, sh], args=[w, *parts, m, v], riders=riders)


def _silu(v):
    return v * (1.0 / (1.0 + jnp.exp(-v)))


def _ada_rows(c_all, c_ctx):
    return jnp.concatenate([c_all, jnp.broadcast_to(c_ctx[None, :], (N_DEV, c_ctx.shape[0]))], axis=0)


def _silu_rows(rows, name):
    def body(r_ref, o_ref):
        o_ref[...] = _silu(r_ref[...])

    return pl.pallas_call(body, name=name, out_shape=jax.ShapeDtypeStruct(rows.shape, F32))(rows)


_SMALL = ("c_ctx", "na_rpb", "swa_sink", "mla_q_norm", "mla_kv_norm", "gqa_q_norm", "gqa_k_norm",
          "ln1_g", "ln1_b", "ffn_conv_b", "ln2_g", "ln2_b")
_NAMES = ("c_ctx", "w_ada", "b_ada", "w_in", "na_rpb", "swa_sink", "mla_q_norm", "mla_kv_norm", "mla_w_uq",
          "mla_w_ukv", "gqa_q_norm", "gqa_k_norm", "w_out", "ln1_g", "ln1_b", "ffn_w_gate", "ffn_w_up",
          "ffn_conv_w", "ffn_conv_b", "ffn_w_down", "ln2_g", "ln2_b")


def _full_from_cols(g, pad_in=False):
    _, L, K, n = g.shape
    full = jnp.transpose(g, (1, 2, 0, 3)).reshape(L, K, N_DEV * n)
    if pad_in:
        full = jnp.concatenate([full[:, :, :KPE_END], jnp.zeros((L, K, 64), full.dtype), full[:, :, KPE_END:]], axis=-1)
    return full


def _pack_small(tree, extra=None):
    flat = [tree[n].reshape(-1) for n in _SMALL]
    flat.append(jnp.zeros((1,), F32) if extra is None else extra.reshape(1))
    v = jnp.concatenate(flat)
    n = v.shape[0]
    padded = -(-n // 1024) * 1024
    return jnp.pad(v, (0, padded - n)).reshape(padded // LANE, LANE)


def _unpack_small(mat, like):
    v = mat.reshape(-1)
    out, o = {}, 0
    for n in _SMALL:
        k = int(np.prod(like[n].shape))
        out[n] = v[o:o + k].reshape(like[n].shape)
        o += k
    return out, v[o]


def kernel(x, c, ctx, c_ctx, w_ada, b_ada, w_in, na_rpb, swa_sink, mla_q_norm, mla_kv_norm, mla_w_uq, mla_w_ukv, gqa_q_norm, gqa_k_norm, w_out, ln1_g, ln1_b, ffn_w_gate, ffn_w_up, ffn_conv_w, ffn_conv_b, ffn_w_down, ln2_g, ln2_b, loss_target, m_c_ctx, m_w_ada, m_b_ada, m_w_in, m_na_rpb, m_swa_sink, m_mla_q_norm, m_mla_kv_norm, m_mla_w_uq, m_mla_w_ukv, m_gqa_q_norm, m_gqa_k_norm, m_w_out, m_ln1_g, m_ln1_b, m_ffn_w_gate, m_ffn_w_up, m_ffn_conv_w, m_ffn_conv_b, m_ffn_w_down, m_ln2_g, m_ln2_b, v_c_ctx, v_w_ada, v_b_ada, v_w_in, v_na_rpb, v_swa_sink, v_mla_q_norm, v_mla_kv_norm, v_mla_w_uq, v_mla_w_ukv, v_gqa_q_norm, v_gqa_k_norm, v_w_out, v_ln1_g, v_ln1_b, v_ffn_w_gate, v_ffn_w_up, v_ffn_conv_w, v_ffn_conv_b, v_ffn_w_down, v_ln2_g, v_ln2_b):
    W = dict(c_ctx=c_ctx, w_ada=w_ada, b_ada=b_ada, w_in=w_in, na_rpb=na_rpb, swa_sink=swa_sink,
             mla_q_norm=mla_q_norm, mla_kv_norm=mla_kv_norm, mla_w_uq=mla_w_uq, mla_w_ukv=mla_w_ukv,
             gqa_q_norm=gqa_q_norm, gqa_k_norm=gqa_k_norm, w_out=w_out, ln1_g=ln1_g, ln1_b=ln1_b,
             ffn_w_gate=ffn_w_gate, ffn_w_up=ffn_w_up, ffn_conv_w=ffn_conv_w, ffn_conv_b=ffn_conv_b,
             ffn_w_down=ffn_w_down, ln2_g=ln2_g, ln2_b=ln2_b)
    M = dict(c_ctx=m_c_ctx, w_ada=m_w_ada, b_ada=m_b_ada, w_in=m_w_in, na_rpb=m_na_rpb, swa_sink=m_swa_sink,
             mla_q_norm=m_mla_q_norm, mla_kv_norm=m_mla_kv_norm, mla_w_uq=m_mla_w_uq, mla_w_ukv=m_mla_w_ukv,
             gqa_q_norm=m_gqa_q_norm, gqa_k_norm=m_gqa_k_norm, w_out=m_w_out, ln1_g=m_ln1_g, ln1_b=m_ln1_b,
             ffn_w_gate=m_ffn_w_gate, ffn_w_up=m_ffn_w_up, ffn_conv_w=m_ffn_conv_w, ffn_conv_b=m_ffn_conv_b,
             ffn_w_down=m_ffn_w_down, ln2_g=m_ln2_g, ln2_b=m_ln2_b)
    V = dict(c_ctx=v_c_ctx, w_ada=v_w_ada, b_ada=v_b_ada, w_in=v_w_in, na_rpb=v_na_rpb, swa_sink=v_swa_sink,
             mla_q_norm=v_mla_q_norm, mla_kv_norm=v_mla_kv_norm, mla_w_uq=v_mla_w_uq, mla_w_ukv=v_mla_w_ukv,
             gqa_q_norm=v_gqa_q_norm, gqa_k_norm=v_gqa_k_norm, w_out=v_w_out, ln1_g=v_ln1_g, ln1_b=v_ln1_b,
             ffn_w_gate=v_ffn_w_gate, ffn_w_up=v_ffn_w_up, ffn_conv_w=v_ffn_conv_w, ffn_conv_b=v_ffn_conv_b,
             ffn_w_down=v_ffn_w_down, ln2_g=v_ln2_g, ln2_b=v_ln2_b)
    L, D, n_ada = w_ada.shape
    me = 4 * lax.axis_index("x") + 2 * lax.axis_index("y") + lax.axis_index("c")
    xs, ctxs, tgt = x[0], ctx[0], loss_target[0]

    c_all = _exchange(c, False, "gather_c").reshape(N_DEV, D)
    a_rows = _silu_rows(_ada_rows(c_all, c_ctx), "ada_silu")
    b_mine = lax.dynamic_slice(b_ada, (0, me * n_ada), (L, n_ada))
    mod_mine = jnp.stack([_matmul(a_rows, w_ada[l], "nn", "ada_fwd_l%d" % l) + b_mine[l][None, :] for l in range(L)])
    mod_all = _exchange(mod_mine, False, "gather_mod")
    mod_all = jnp.transpose(mod_all, (1, 2, 0, 3)).reshape(L, 2 * N_DEV, N_DEV * n_ada)
    modx = lax.dynamic_slice(mod_all, (0, me, 0), (L, 1, 6 * D)).reshape(L, 6, D)
    modc = mod_all[:, N_DEV].reshape(L, 6, D)

    big = _BIG_COL + _BIG_ROW
    views = {n: _shard_view(n, W[n]) for n in big}
    shards = [{n: _cut(n, views[n][l].astype(BF16)) for n in big} for l in range(L)]
    full = [{n: _full_weight(n, [_exchange(views[n][0].astype(BF16), False, "gather_l0_" + n)]) for n in _IN_GROUP}]
    full += [{} for _ in range(1, L)]
    conv_w_full = _full_from_cols(_exchange(ffn_conv_w, False, "gather_conv_w"))

    toks = [{n: tuple(jnp.zeros((N_DEV,) + s.shape, BF16) for s in shards[l][n]) for n in big} for l in range(L)]
    small = {n: W[n] for n in _SMALL if n != "c_ctx"}
    small["ffn_conv_w"] = conv_w_full

    def loss_fn(xv, mxv, mcv, sm, tk):
        return _local_loss(xv, ctxs, tgt, mxv, mcv, sm, full, shards, tk)

    loss_local, (grad_x, dmodx, dmodc, dsmall, dtoks) = jax.value_and_grad(loss_fn, argnums=(0, 1, 2, 3, 4))(
        xs, modx, modc, small, toks)

    dmods = _exchange(jnp.stack([dmodx.reshape(L, 6 * D), dmodc.reshape(L, 6 * D)]), False, "gather_dmod")
    dm_rows = jnp.concatenate([dmods[:, 0], dmods[:, 1]], axis=0)
    dm_rows = jnp.transpose(dm_rows, (1, 0, 2))
    dm_mine = lax.dynamic_slice(dm_rows, (0, 0, me * n_ada), (L, 2 * N_DEV, n_ada))
    g_w_ada = [_matmul(a_rows, dm_mine[l], "tn", "ada_dw_l%d" % l) for l in range(L)]
    d_rows = sum(_matmul(dm_mine[l], w_ada[l], "nt", "ada_da_l%d" % l) for l in range(L))
    sig = 1.0 / (1.0 + jnp.exp(-c_ctx))
    d_c_ctx_part = jnp.sum(d_rows[N_DEV:], axis=0) * (sig * (1.0 + c_ctx * (1.0 - sig)))

    outs = {}
    pieces = [{n: list(dtoks[l][n]) for n in big} for l in range(L)]
    late = {"ffn_w_gate": (("w_in", 0),), "ffn_w_up": (("w_in", 1),), "ffn_w_down": (("w_in", 2),),
            "w_ada": (("w_in", 3),), "w_out": (("mla_w_uq", 0), ("mla_w_ukv", 0))}

    def update(n, w, g_parts, m, v):
        res = _adamw_layers(w, g_parts, m, v, "adamw_" + n, _riders([pieces[0][a][k] for a, k in late.get(n, ())], True))
        for (a, k), got in zip(late.get(n, ()), res[4:]):
            pieces[0][a][k] = got
        return res[:4]

    def whole(l, n):
        return jnp.concatenate(pieces[l][n], axis=-1) if len(pieces[l][n]) > 1 else pieces[l][n][0]

    outs["w_ada"] = update("w_ada", w_ada, [g[None] for g in g_w_ada], M["w_ada"], V["w_ada"])
    for n in [n for n in big if n not in _IN_GROUP] + list(_IN_GROUP):
        res = update(n, views[n], [whole(l, n) for l in range(L)], _shard_view(n, M[n]), _shard_view(n, V[n]))
        outs[n] = [_shard_view(n, r) for r in res]
    dcw = dsmall.pop("ffn_conv_w")
    cw_parts = _exchange(jnp.transpose(dcw.reshape(L, 3, N_DEV, -1), (2, 0, 1, 3)), True, "scatter_conv_w")
    outs["ffn_conv_w"] = _adamw_layers(ffn_conv_w, [cw_parts[:, l] for l in range(L)], M["ffn_conv_w"],
                                       V["ffn_conv_w"], "adamw_conv_w")
    outs["b_ada"] = _adamw_nd(b_ada, jnp.transpose(dm_rows, (1, 0, 2)), M["b_ada"], V["b_ada"], "adamw_b_ada")

    dsmall["c_ctx"] = d_c_ctx_part
    small_parts = _exchange(_pack_small(dsmall, extra=loss_local), False, "gather_small")
    w_small = _pack_small({n: W[n] for n in _SMALL})
    g_s, d_s, nm_s, nv_s = _adamw(w_small, small_parts, _pack_small({n: M[n] for n in _SMALL}),
                                  _pack_small({n: V[n] for n in _SMALL}), "adamw_small")
    like = {n: W[n] for n in _SMALL}
    g_small, loss = _unpack_small(g_s, like)
    unpacked = [g_small, _unpack_small(d_s, like)[0], _unpack_small(nm_s, like)[0], _unpack_small(nv_s, like)[0]]
    for n in _SMALL:
        outs[n] = [u[n] for u in unpacked]

    result = [loss, grad_x[None]]
    for k in range(4):
        result += [outs[n][k] for n in _NAMES]
    return tuple(result)
```

```python
import functools
import math

import numpy as np
import jax
import jax.numpy as jnp
from jax import lax
from jax.experimental import pallas as pl
from jax.experimental.pallas import tpu as pltpu

F32 = jnp.float32
BF16 = jnp.bfloat16
_MXU = jnp.bfloat16
_ACT = jnp.bfloat16

N_DEV = 8
GRID_W = 64
HEAD_DIM = 128
NA_HEADS, NA_WIN_R, NA_WIN_C = 4, 8, 16
SWA_HEADS, SWA_KV_HEADS, SWA_WINDOW = 4, 2, 128
MLA_HEADS, MLA_Q_LORA, MLA_KV_LORA, MLA_NOPE, MLA_ROPE, MLA_V = 4, 384, 128, 128, 64, 128
GQA_HEADS, GQA_KV_HEADS = 4, 2
ROPE_THETA = 10000.0
EPS = 1e-6
NEG = -1e30
LOG2E = 1.4426950408889634
IN_SIZES = (512, 512, 512, 512, 256, 256, MLA_Q_LORA, MLA_KV_LORA, MLA_ROPE, 512, 256, 256)
IN_COLS = sum(IN_SIZES)
KPE_END = sum(IN_SIZES[:9])
IN_COLS_PAD = IN_COLS + 64
ADAM_LR, ADAM_B1, ADAM_B2, ADAM_EPS, ADAM_WD, ADAM_STEP = 0.001, 0.9, 0.999, 1e-08, 0.01, 10

LANE = 128
ROW_TILE = 256
ATT_BLOCK = 128
VMEM_LIMIT = 56 * 1024 * 1024
MM_BUDGET = 42 * 1024 * 1024
HBM_BPS = 3.0e12
MXU_FLOPS = 9.0e14
VMEM_BPS = 4.0e12
STEP_S = 0.4e-6


def _cparams(n_axes):
    return pltpu.CompilerParams(dimension_semantics=("arbitrary",) * n_axes, vmem_limit_bytes=VMEM_LIMIT)


def _exchange_copies(in_ref, out_ref, send_sems, recv_sems, local_sem, scatter, rows=None):
    x, y, c = lax.axis_index("x"), lax.axis_index("y"), lax.axis_index("c")
    me = 4 * x + 2 * y + c
    cut = (lambda ref: ref) if rows is None else (lambda ref: ref.at[rows])
    copies = [pltpu.make_async_copy(cut(in_ref.at[me] if scatter else in_ref), cut(out_ref.at[me]), local_sem)]
    for k in range(1, N_DEV):
        px, py, pc = (x + (k >> 2)) % 2, (y + ((k >> 1) & 1)) % 2, (c + (k & 1)) % 2
        peer = 4 * px + 2 * py + pc
        copies.append(pltpu.make_async_remote_copy(
            src_ref=cut(in_ref.at[peer] if scatter else in_ref), dst_ref=cut(out_ref.at[me]),
            send_sem=send_sems.at[k - 1], recv_sem=recv_sems.at[k - 1],
            device_id=(px, py, pc), device_id_type=pl.DeviceIdType.MESH))
    return copies


PIECE_ROWS = 16
MAX_PIECES = 32


def _pieces(rows, steps):
    if rows % PIECE_ROWS:
        return 1
    units = rows // PIECE_ROWS
    return max(d for d in range(1, min(MAX_PIECES, steps) + 1) if units % d == 0)


def _exchange_out_shape(arr, scatter):
    return jax.ShapeDtypeStruct(arr.shape if scatter else (N_DEV,) + arr.shape, arr.dtype)


_EXCHANGE_SEMS = (pltpu.SemaphoreType.DMA((N_DEV - 1,)), pltpu.SemaphoreType.DMA((N_DEV - 1,)),
                  pltpu.SemaphoreType.DMA)


def _exchange(inp, scatter, name):
    def body(in_ref, out_ref, send_sems, recv_sems, local_sem):
        copies = _exchange_copies(in_ref, out_ref, send_sems, recv_sems, local_sem, scatter)
        for cp in copies:
            cp.start()
        for cp in copies:
            cp.wait()

    hbm = pl.BlockSpec(memory_space=pl.ANY)
    return pl.pallas_call(
        body, name=name, in_specs=[hbm], out_specs=hbm, out_shape=_exchange_out_shape(inp, scatter),
        scratch_shapes=list(_EXCHANGE_SEMS), compiler_params=pltpu.CompilerParams(has_side_effects=True),
    )(inp)


def _pcall(body, *, name, grid, in_specs, out_specs, out_shape, args, scratch_shapes=(), riders=()):
    in_specs, out_specs, out_shape = list(in_specs), list(out_specs), list(out_shape)
    scratch, args = list(scratch_shapes), list(args)
    n_in, n_out, n_scr, n_r = len(in_specs), len(out_specs), len(scratch), len(riders)
    hbm = pl.BlockSpec(memory_space=pl.ANY)
    for arr, scatter in riders:
        in_specs.append(hbm)
        args.append(arr)
        out_specs.append(hbm)
        out_shape.append(_exchange_out_shape(arr, scatter))
        scratch += list(_EXCHANGE_SEMS)

    def wrapped(*refs):
        ins, r_in = refs[:n_in], refs[n_in:n_in + n_r]
        o0 = n_in + n_r
        outs, r_out = refs[o0:o0 + n_out], refs[o0 + n_out:o0 + n_out + n_r]
        s0 = o0 + n_out + n_r
        scr, sems = refs[s0:s0 + n_scr], refs[s0 + n_scr:]

        def copies(r, rows=None):
            return _exchange_copies(r_in[r], r_out[r], sems[3 * r], sems[3 * r + 1], sems[3 * r + 2], riders[r][1],
                                    rows)

        if n_r:
            steps = int(np.prod(grid))
            step = functools.reduce(lambda u, w: u * w[1] + w[0],
                                    [(pl.program_id(a), n) for a, n in enumerate(grid)], 0)
            for r, (arr, scatter) in enumerate(riders):
                block_rows = arr.shape[1] if scatter else arr.shape[0]
                n_pieces = _pieces(block_rows, steps)
                every, piece = steps // n_pieces, block_rows // n_pieces

                @pl.when((step % every == 0) & (step // every < n_pieces))
                def _(r=r, every=every, piece=piece, whole=n_pieces == 1):
                    rows = None if whole else pl.ds(pl.multiple_of((step // every) * piece, PIECE_ROWS), piece)
                    for cp in copies(r, rows):
                        cp.start()

        body(*ins, *outs, *scr)

        if n_r:
            @pl.when(step == steps - 1)
            def _():
                for r in range(n_r):
                    for cp in copies(r):
                        cp.wait()

    params = pltpu.CompilerParams(dimension_semantics=("arbitrary",) * len(grid), vmem_limit_bytes=VMEM_LIMIT,
                                  has_side_effects=bool(riders))
    return list(pl.pallas_call(wrapped, name=name, grid=grid, in_specs=in_specs, out_specs=out_specs,
                               out_shape=out_shape, scratch_shapes=scratch, compiler_params=params)(*args))


def _riders(arrays, scatter):
    return [(a, scatter) for a in arrays]


def _nones(arrays):
    return tuple(None for _ in arrays)


def _divisors(n, step, cap):
    out = [d for d in range(step, min(n, cap) + 1, step) if n % d == 0]
    if n <= cap and n not in out:
        out.append(n)
    return out


def _mm_tiles(M, N, K, sa, sb, m_step, k_step):
    best, best_cost = None, None
    for tm in _divisors(M, m_step, 2304):
        for tn in _divisors(N, LANE, 2304):
            for tk in _divisors(K, k_step, 2304):
                vm = 2 * (tm * tk * sa + tk * tn * sb) + 3 * tm * tn * 4
                if vm > MM_BUDGET:
                    continue
                traffic = M * K * sa * (N // tn) + K * N * sb * (M // tm) + M * N * 4
                steps = (M // tm) * (N // tn) * (K // tk)
                passes = 1 if tk == K else 3
                busy = steps * (2 * tm * tn * tk / MXU_FLOPS + passes * tm * tn * 4 / VMEM_BPS)
                cost = max(traffic / HBM_BPS, busy) + steps * STEP_S
                if m_step == LANE:
                    cost = traffic / HBM_BPS + steps * STEP_S
                if tm % 256 or tn % 256:
                    cost *= 1.05
                if best_cost is None or cost < best_cost:
                    best, best_cost = (tm, tn, tk), cost
    assert best is not None, (M, N, K)
    return best


def _matmul(a, b, mode, name, riders=(), out_dtype=F32):
    if mode == "nn":
        (M, K), (K2, N) = a.shape, b.shape
    elif mode == "nt":
        (M, K), (N, K2) = a.shape, b.shape
    else:
        (K, M), (K2, N) = a.shape, b.shape
    assert K == K2, (a.shape, b.shape, mode)
    tm, tn, tk = _mm_tiles(M, N, K, a.dtype.itemsize, b.dtype.itemsize,
                           LANE if mode == "tn" else 8, 8 if mode == "tn" else LANE)
    nk = K // tk
    dims = {"nn": (((1,), (0,)), ((), ())), "nt": (((1,), (1,)), ((), ())), "tn": (((0,), (0,)), ((), ()))}[mode]

    own_acc = nk > 1 and out_dtype != F32

    def body(a_ref, b_ref, o_ref, *scr):
        prod = lax.dot_general(a_ref[...].astype(_MXU), b_ref[...].astype(_MXU), dims, preferred_element_type=F32)
        if nk == 1:
            o_ref[...] = prod.astype(out_dtype)
        else:
            acc_ref = scr[0] if own_acc else o_ref
            k = pl.program_id(2)

            @pl.when(k == 0)
            def _():
                acc_ref[...] = prod

            @pl.when(k > 0)
            def _():
                acc_ref[...] += prod

            if own_acc:
                @pl.when(k == nk - 1)
                def _():
                    o_ref[...] = acc_ref[...].astype(out_dtype)

    if mode == "nn":
        a_spec = pl.BlockSpec((tm, tk), lambda i, j, k: (i, k))
        b_spec = pl.BlockSpec((tk, tn), lambda i, j, k: (k, j))
    elif mode == "nt":
        a_spec = pl.BlockSpec((tm, tk), lambda i, j, k: (i, k))
        b_spec = pl.BlockSpec((tn, tk), lambda i, j, k: (j, k))
    else:
        a_spec = pl.BlockSpec((tk, tm), lambda i, j, k: (k, i))
        b_spec = pl.BlockSpec((tk, tn), lambda i, j, k: (k, j))
    outs = _pcall(body, name=name, grid=(M // tm, N // tn, nk), in_specs=[a_spec, b_spec],
                  out_specs=[pl.BlockSpec((tm, tn), lambda i, j, k: (i, j))],
                  out_shape=[jax.ShapeDtypeStruct((M, N), out_dtype)], args=[a, b],
                  scratch_shapes=[pltpu.VMEM((tm, tn), F32)] if own_acc else [], riders=riders)
    return outs if riders else outs[0]


def _make_linear(tag, to_parts, out_dtype=F32):
    def run(a, w, gin, tok):
        if gin:
            y, *got = _matmul(a, w, "nn", tag + "_fwd", riders=_riders(gin, False), out_dtype=out_dtype)
        else:
            y, got = _matmul(a, w, "nn", tag + "_fwd", out_dtype=out_dtype), []
        return y, tuple(got), tok

    @jax.custom_vjp
    def linear(a, w, sink, gin, tok):
        return run(a.astype(_MXU), w, gin, tok)

    def fwd(a, w, sink, gin, tok):
        ab = a.astype(_MXU)
        return run(ab, w, gin, tok), (ab, w, gin, jnp.zeros((), a.dtype))

    def bwd(res, cts):
        a, w, gin, like_a = res
        dy, _, dtok = cts
        dy = dy.astype(_MXU)
        dw = _matmul(a, dy, "tn", tag + "_bwd_dw")
        if dtok:
            da, *recv = _matmul(dy, w, "nt", tag + "_bwd_da", riders=_riders(dtok, True), out_dtype=like_a.dtype)
        else:
            da, recv = _matmul(dy, w, "nt", tag + "_bwd_da", out_dtype=like_a.dtype), []
        return da, jnp.zeros_like(w), to_parts(dw), _nones(gin), tuple(recv)

    linear.defvjp(fwd, bwd)
    return linear


def _seg_map(n_x_tiles):
    return lambda i: (jnp.where(i >= n_x_tiles, 1, 0), 0, 0)


def _ln_stats(x):
    mu = jnp.mean(x, axis=-1, keepdims=True)
    xc = x - mu
    var = jnp.mean(xc * xc, axis=-1, keepdims=True)
    rstd = lax.rsqrt(var + EPS)
    return xc * rstd, rstd


def _make_modulate(tag, n_x):
    def call_fwd(x, shift, scale):
        T, D = x.shape
        nxt = n_x // ROW_TILE

        def body(x_ref, sh_ref, sc_ref, o_ref):
            xhat, _ = _ln_stats(x_ref[...])
            o_ref[...] = xhat * (1.0 + sc_ref[0]) + sh_ref[0]

        row = pl.BlockSpec((ROW_TILE, D), lambda i: (i, 0))
        seg = pl.BlockSpec((1, 1, D), _seg_map(nxt))
        return pl.pallas_call(body, name=tag + "_fwd", grid=(T // ROW_TILE,), in_specs=[row, seg, seg],
                              out_specs=row, out_shape=jax.ShapeDtypeStruct((T, D), F32),
                              compiler_params=_cparams(1))(x, shift, scale)

    def call_bwd(x, scale, dh):
        T, D = x.shape
        nxt = n_x // ROW_TILE
        nseg = scale.shape[0]

        def body(x_ref, sc_ref, dh_ref, dx_ref, dsh_ref, dsc_ref):
            i = pl.program_id(0)

            @pl.when((i == 0) | (i == nxt))
            def _():
                dsh_ref[...] = jnp.zeros_like(dsh_ref)
                dsc_ref[...] = jnp.zeros_like(dsc_ref)

            xhat, rstd = _ln_stats(x_ref[...])
            dh = dh_ref[...]
            dsh_ref[0] += jnp.sum(dh, axis=0, keepdims=True)
            dsc_ref[0] += jnp.sum(dh * xhat, axis=0, keepdims=True)
            dxh = dh * (1.0 + sc_ref[0])
            m1 = jnp.mean(dxh, axis=-1, keepdims=True)
            m2 = jnp.mean(dxh * xhat, axis=-1, keepdims=True)
            dx_ref[...] = rstd * (dxh - m1 - xhat * m2)

        row = pl.BlockSpec((ROW_TILE, D), lambda i: (i, 0))
        seg = pl.BlockSpec((1, 1, D), _seg_map(nxt))
        segshape = jax.ShapeDtypeStruct((nseg, 1, D), F32)
        return pl.pallas_call(body, name=tag + "_bwd", grid=(T // ROW_TILE,), in_specs=[row, seg, row],
                              out_specs=[row, seg, seg],
                              out_shape=[jax.ShapeDtypeStruct((T, D), F32), segshape, segshape],
                              compiler_params=_cparams(1))(x, scale, dh)

    @jax.custom_vjp
    def modulate(x, shift, scale):
        return call_fwd(x, shift, scale)

    def fwd(x, shift, scale):
        return call_fwd(x, shift, scale), (x, scale)

    def bwd(res, dh):
        x, scale = res
        return tuple(call_bwd(x, scale, dh))

    modulate.defvjp(fwd, bwd)
    return modulate


def _make_resid_ln(tag, n_x, alpha):
    def call_fwd(x, y, gate, g, b):
        T, D = x.shape
        nxt = n_x // ROW_TILE

        def body(x_ref, y_ref, gt_ref, g_ref, b_ref, o_ref):
            u = alpha * x_ref[...] + gt_ref[0] * y_ref[...]
            uhat, _ = _ln_stats(u)
            o_ref[...] = uhat * g_ref[...] + b_ref[...]

        row = pl.BlockSpec((ROW_TILE, D), lambda i: (i, 0))
        seg = pl.BlockSpec((1, 1, D), _seg_map(nxt))
        vec = pl.BlockSpec((1, D), lambda i: (0, 0))
        return pl.pallas_call(body, name=tag + "_fwd", grid=(T // ROW_TILE,), in_specs=[row, row, seg, vec, vec],
                              out_specs=row, out_shape=jax.ShapeDtypeStruct((T, D), F32),
                              compiler_params=_cparams(1))(x, y, gate, g, b)

    def call_bwd(x, y, gate, g, do):
        T, D = x.shape
        nxt = n_x // ROW_TILE
        nseg = gate.shape[0]

        def body(x_ref, y_ref, gt_ref, g_ref, do_ref, dx_ref, dy_ref, dgt_ref, dg_ref, db_ref):
            i = pl.program_id(0)

            @pl.when(i == 0)
            def _():
                dg_ref[...] = jnp.zeros_like(dg_ref)
                db_ref[...] = jnp.zeros_like(db_ref)

            @pl.when((i == 0) | (i == nxt))
            def _():
                dgt_ref[...] = jnp.zeros_like(dgt_ref)

            y = y_ref[...]
            gate_v = gt_ref[0]
            uhat, rstd = _ln_stats(alpha * x_ref[...] + gate_v * y)
            do = do_ref[...]
            dg_ref[...] += jnp.sum(do * uhat, axis=0, keepdims=True)
            db_ref[...] += jnp.sum(do, axis=0, keepdims=True)
            duh = do * g_ref[...]
            m1 = jnp.mean(duh, axis=-1, keepdims=True)
            m2 = jnp.mean(duh * uhat, axis=-1, keepdims=True)
            du = rstd * (duh - m1 - uhat * m2)
            dx_ref[...] = alpha * du
            dy_ref[...] = gate_v * du
            dgt_ref[0] += jnp.sum(du * y, axis=0, keepdims=True)

        row = pl.BlockSpec((ROW_TILE, D), lambda i: (i, 0))
        seg = pl.BlockSpec((1, 1, D), _seg_map(nxt))
        vec = pl.BlockSpec((1, D), lambda i: (0, 0))
        rs = jax.ShapeDtypeStruct((T, D), F32)
        vs = jax.ShapeDtypeStruct((1, D), F32)
        return pl.pallas_call(body, name=tag + "_bwd", grid=(T // ROW_TILE,), in_specs=[row, row, seg, vec, row],
                              out_specs=[row, row, seg, vec, vec],
                              out_shape=[rs, rs, jax.ShapeDtypeStruct((nseg, 1, D), F32), vs, vs],
                              compiler_params=_cparams(1))(x, y, gate, g, do)

    @jax.custom_vjp
    def resid_ln(x, y, gate, g, b):
        return call_fwd(x, y, gate, g, b)

    def fwd(x, y, gate, g, b):
        return call_fwd(x, y, gate, g, b), (x, y, gate, g)

    def bwd(res, do):
        x, y, gate, g = res
        return tuple(call_bwd(x, y, gate, g, do))

    resid_ln.defvjp(fwd, bwd)
    return resid_ln


def _make_rmsnorm(tag):
    def call_fwd(x, g):
        T, W = x.shape
        gw = g.shape[1]

        def body(x_ref, g_ref, o_ref):
            x = x_ref[...]
            r = lax.rsqrt(jnp.mean(x * x, axis=-1, keepdims=True) + EPS)
            o_ref[...] = x * r * g_ref[...]

        blk = pl.BlockSpec((ROW_TILE, gw), lambda i, c: (i, c))
        vec = pl.BlockSpec((1, gw), lambda i, c: (0, 0))
        return pl.pallas_call(body, name=tag + "_fwd", grid=(T // ROW_TILE, W // gw), in_specs=[blk, vec],
                              out_specs=blk, out_shape=jax.ShapeDtypeStruct((T, W), F32),
                              compiler_params=_cparams(2))(x, g)

    def call_bwd(x, g, dy):
        T, W = x.shape
        gw = g.shape[1]

        def body(x_ref, g_ref, dy_ref, dx_ref, dg_ref):
            @pl.when((pl.program_id(0) == 0) & (pl.program_id(1) == 0))
            def _():
                dg_ref[...] = jnp.zeros_like(dg_ref)

            x = x_ref[...]
            r = lax.rsqrt(jnp.mean(x * x, axis=-1, keepdims=True) + EPS)
            xn = x * r
            dy = dy_ref[...]
            dg_ref[...] += jnp.sum(dy * xn, axis=0, keepdims=True)
            dxn = dy * g_ref[...]
            dx_ref[...] = r * (dxn - xn * jnp.mean(dxn * xn, axis=-1, keepdims=True))

        blk = pl.BlockSpec((ROW_TILE, gw), lambda i, c: (i, c))
        vec = pl.BlockSpec((1, gw), lambda i, c: (0, 0))
        return pl.pallas_call(body, name=tag + "_bwd", grid=(T // ROW_TILE, W // gw), in_specs=[blk, vec, blk],
                              out_specs=[blk, vec],
                              out_shape=[jax.ShapeDtypeStruct((T, W), F32), jax.ShapeDtypeStruct((1, gw), F32)],
                              compiler_params=_cparams(2))(x, g, dy)

    @jax.custom_vjp
    def rmsnorm(x, g):
        return call_fwd(x, g)

    def fwd(x, g):
        return call_fwd(x, g), (x, g)

    def bwd(res, dy):
        x, g = res
        return tuple(call_bwd(x, g, dy))

    rmsnorm.defvjp(fwd, bwd)
    return rmsnorm


def _rope_tables(n_x, n_all, width, half):
    t = np.arange(n_all)
    row, col = t // GRID_W, t % GRID_W
    lane = np.arange(LANE)
    inside = lane < width
    axis_is_col = (lane // (2 * half)) % 2 == 1
    f = (lane % (2 * half)) % half
    inv_freq = ROPE_THETA ** (-(f.astype(np.float64)) / half)
    pos = np.where(axis_is_col[None, :], col[:, None], row[:, None]).astype(np.float64)
    ang = (pos.astype(np.float32) * inv_freq.astype(np.float32)[None, :]).astype(np.float32)
    live = inside[None, :] & (t < n_x)[:, None]
    cos = np.where(live, np.cos(ang), 1.0).astype(np.float32)
    sin = np.where(live, np.sin(ang), 0.0).astype(np.float32)
    first = (lane % (2 * half)) < half
    s_lo = np.where(first[None, :], -sin, 0.0).astype(np.float32)
    s_hi = np.where(first[None, :], 0.0, sin).astype(np.float32)
    return jnp.asarray(cos), jnp.asarray(s_lo), jnp.asarray(s_hi)


def _make_rope(tag, half):
    def call(x, cos, s_lo, s_hi, transpose, name):
        T, W = x.shape

        def body(x_ref, c_ref, lo_ref, hi_ref, o_ref):
            x = x_ref[...]
            if not transpose:
                o_ref[...] = (x * c_ref[...] + pltpu.roll(x, LANE - half, 1) * lo_ref[...]
                              + pltpu.roll(x, half, 1) * hi_ref[...])
            else:
                o_ref[...] = (x * c_ref[...] + pltpu.roll(x * lo_ref[...], half, 1)
                              + pltpu.roll(x * hi_ref[...], LANE - half, 1))

        blk = pl.BlockSpec((ROW_TILE, LANE), lambda i, c: (i, c))
        tab = pl.BlockSpec((ROW_TILE, LANE), lambda i, c: (i, 0))
        return pl.pallas_call(body, name=name, grid=(T // ROW_TILE, W // LANE), in_specs=[blk, tab, tab, tab],
                              out_specs=blk, out_shape=jax.ShapeDtypeStruct((T, W), F32),
                              compiler_params=_cparams(2))(x, cos, s_lo, s_hi)

    @jax.custom_vjp
    def rope(x, cos, s_lo, s_hi):
        return call(x, cos, s_lo, s_hi, False, tag + "_fwd")

    def fwd(x, cos, s_lo, s_hi):
        return call(x, cos, s_lo, s_hi, False, tag + "_fwd"), (cos, s_lo, s_hi)

    def bwd(res, dy):
        cos, s_lo, s_hi = res
        return call(dy, cos, s_lo, s_hi, True, tag + "_bwd"), None, None, None

    rope.defvjp(fwd, bwd)
    return rope


def _make_conv_gate(tag, n_x):
    def shifted(v, T):
        t = lax.broadcasted_iota(jnp.int32, v.shape, 0)
        prev = jnp.where((t == 0) | (t == n_x), 0.0, pltpu.roll(v, 1, 0))
        nxt = jnp.where((t == n_x - 1) | (t == T - 1), 0.0, pltpu.roll(v, T - 1, 0))
        return prev, nxt

    def call_fwd(a, u, cw, cb, riders):
        T, Fd = a.shape

        def body(a_ref, u_ref, cw_ref, cb_ref, z_ref):
            pre = a_ref[...].astype(F32)
            prev, nxt = shifted(pre, T)
            s = prev * cw_ref[0:1, :] + pre * cw_ref[1:2, :] + nxt * cw_ref[2:3, :] + cb_ref[...]
            z_ref[...] = (s * (1.0 / (1.0 + jnp.exp(-s))) * u_ref[...].astype(F32)).astype(z_ref.dtype)

        col = pl.BlockSpec((T, LANE), lambda j: (0, j))
        return _pcall(body, name=tag + "_fwd", grid=(Fd // LANE,),
                      in_specs=[col, col, pl.BlockSpec((3, LANE), lambda j: (0, j)),
                                pl.BlockSpec((1, LANE), lambda j: (0, j))],
                      out_specs=[col], out_shape=[jax.ShapeDtypeStruct((T, Fd), a.dtype)], args=[a, u, cw, cb],
                      riders=riders)

    def call_bwd(a, u, cw, cb, dz, riders):
        T, Fd = a.shape

        def body(a_ref, u_ref, cw_ref, cb_ref, dz_ref, da_ref, du_ref, dcw_ref, dcb_ref):
            pre = a_ref[...].astype(F32)
            prev, nxt = shifted(pre, T)
            s = prev * cw_ref[0:1, :] + pre * cw_ref[1:2, :] + nxt * cw_ref[2:3, :] + cb_ref[...]
            sig = 1.0 / (1.0 + jnp.exp(-s))
            dz = dz_ref[...].astype(F32)
            du_ref[...] = (dz * s * sig).astype(du_ref.dtype)
            ds = dz * u_ref[...].astype(F32) * (sig * (1.0 + s * (1.0 - sig)))
            ds_prev, ds_next = shifted(ds, T)
            da_ref[...] = (ds * cw_ref[1:2, :] + ds_next * cw_ref[0:1, :] + ds_prev * cw_ref[2:3, :]).astype(
                da_ref.dtype)
            dcw_ref[0:1, :] = jnp.sum(ds * prev, axis=0, keepdims=True)
            dcw_ref[1:2, :] = jnp.sum(ds * pre, axis=0, keepdims=True)
            dcw_ref[2:3, :] = jnp.sum(ds * nxt, axis=0, keepdims=True)
            dcb_ref[...] = jnp.sum(ds, axis=0, keepdims=True)

        col = pl.BlockSpec((T, LANE), lambda j: (0, j))
        w3 = pl.BlockSpec((3, LANE), lambda j: (0, j))
        w1 = pl.BlockSpec((1, LANE), lambda j: (0, j))
        big = jax.ShapeDtypeStruct((T, Fd), a.dtype)
        return _pcall(body, name=tag + "_bwd", grid=(Fd // LANE,), in_specs=[col, col, w3, w1, col],
                      out_specs=[col, col, w3, w1],
                      out_shape=[big, big, jax.ShapeDtypeStruct((3, Fd), F32), jax.ShapeDtypeStruct((1, Fd), F32)],
                      args=[a, u, cw, cb, dz], riders=riders)

    def run(a, u, cw, cb, gin, tok):
        z, *got = call_fwd(a, u, cw, cb, _riders(gin, False))
        return z, tuple(got), tok

    @jax.custom_vjp
    def conv_gate(a, u, cw, cb, gin, tok):
        return run(a, u, cw, cb, gin, tok)

    def fwd(a, u, cw, cb, gin, tok):
        return run(a, u, cw, cb, gin, tok), (a, u, cw, cb, gin)

    def bwd(res, cts):
        a, u, cw, cb, gin = res
        dz, _, dtok = cts
        da, du, dcw, dcb, *recv = call_bwd(a, u, cw, cb, dz, _riders(dtok, True))
        return da, du, dcw, dcb, _nones(gin), tuple(recv)

    conv_gate.defvjp(fwd, bwd)
    return conv_gate


def _make_loss_head(tag):
    def call(y, tgt):
        T, D = y.shape

        def body(y_ref, t_ref, l_ref, r_ref):
            @pl.when(pl.program_id(0) == 0)
            def _():
                l_ref[...] = jnp.zeros_like(l_ref)

            d = y_ref[...] - t_ref[...]
            r_ref[...] = d * (1.0 / D)
            l_ref[...] += jnp.sum(d * d) * (0.5 / D)

        row = pl.BlockSpec((ROW_TILE, D), lambda i: (i, 0))
        return pl.pallas_call(body, name=tag, grid=(T // ROW_TILE,), in_specs=[row, row],
                              out_specs=[pl.BlockSpec((8, LANE), lambda i: (0, 0)), row],
                              out_shape=[jax.ShapeDtypeStruct((8, LANE), F32), jax.ShapeDtypeStruct((T, D), F32)],
                              compiler_params=_cparams(1))(y, tgt)

    @jax.custom_vjp
    def loss_head(y, tgt):
        return call(y, tgt)[0][0, 0]

    def fwd(y, tgt):
        l, r = call(y, tgt)
        return l[0, 0], r

    def bwd(r, g):
        return r * g, None

    loss_head.defvjp(fwd, bwd)
    return loss_head


class _AttnCfg:
    def __init__(self, H, Hkv, dqk, dv, scale, tq, tk, has_sink=False):
        self.H, self.Hkv, self.dqk, self.dv, self.scale = H, Hkv, dqk, dv, scale
        self.tq, self.tk, self.has_sink = tq, tk, has_sink
        self.G = H // Hkv


class _WinCfg:
    def __init__(self, mode, H, Hkv, scale, nx, n_ctx):
        self.mode, self.H, self.Hkv, self.scale, self.nx, self.n_ctx = mode, H, Hkv, scale, nx, n_ctx
        self.G = H // Hkv
        self.has_sink = mode == "swa"
        self.nwin, self.back = (3, 1) if mode == "swa" else (5, 2)
        self.width = self.nwin * ATT_BLOCK

    def first_block(self, i):
        return jnp.clip(i - self.back, 0, self.nx - self.nwin)

    def slots(self, i, jj):
        d = 2 * (self.first_block(i) + jj - i)
        return jnp.clip(d + 8, 0, 15), jnp.clip(d + 7, 0, 15)

    def local_scores(self, s, i, tab_ref):
        sub = lax.broadcasted_iota(jnp.int32, s.shape, 0)
        lan = lax.broadcasted_iota(jnp.int32, s.shape, 1)
        first = self.first_block(i)
        if self.mode == "swa":
            diff = (i * ATT_BLOCK + sub) - (first * ATT_BLOCK + lan)
            return jnp.where(jnp.abs(diff) <= SWA_WINDOW, s, -jnp.inf)
        rows = 2 * self.nx
        qrow = 2 * i + jnp.where(sub >= GRID_W, 1, 0)
        krow = 2 * first + lan // GRID_W
        r0 = jnp.clip(qrow - NA_WIN_R // 2, 0, rows - NA_WIN_R)
        ok = (krow >= r0) & (krow < r0 + NA_WIN_R)
        tiles = []
        for jj in range(self.nwin):
            top, bot = self.slots(i, jj)
            tiles.append(jnp.concatenate([tab_ref[0, top], tab_ref[0, bot]], axis=0))
        return jnp.where(ok, s + jnp.concatenate(tiles, axis=1), -jnp.inf)


def _win_specs(cfg, Tk):
    G = cfg.G
    qmap = lambda kh, g, i: (i, kh * G + g)
    q_spec = pl.BlockSpec((ATT_BLOCK, HEAD_DIM), qmap)
    kv_spec = pl.BlockSpec((Tk, HEAD_DIM), lambda kh, g, i: (0, kh))
    lse_spec = pl.BlockSpec((1, ATT_BLOCK, LANE), lambda kh, g, i: (kh * G + g, i, 0))
    tab_spec = pl.BlockSpec((1, 16, GRID_W, LANE), lambda kh, g, i: (kh * G + g, 0, 0, 0))
    sink_spec = pl.BlockSpec((1, 8, LANE), lambda kh, g, i: (kh * G + g, 0, 0))
    return q_spec, kv_spec, lse_spec, tab_spec, sink_spec


def _win_scores(cfg, i, q_ref, k_ref, tab_ref):
    dn = (((1,), (1,)), ((), ()))
    n_x = cfg.nx * ATT_BLOCK
    win = pl.ds(pl.multiple_of(cfg.first_block(i) * ATT_BLOCK, ATT_BLOCK), cfg.width)
    ctx = pl.ds(n_x, cfg.n_ctx)
    qb = q_ref[...].astype(_MXU)
    s_loc = lax.dot_general(qb, k_ref[win, :].astype(_MXU), dn, preferred_element_type=F32)
    s_ctx = lax.dot_general(qb, k_ref[ctx, :].astype(_MXU), dn, preferred_element_type=F32)
    return cfg.local_scores(s_loc, i, tab_ref), s_ctx, win, ctx


def _win_fwd_call(cfg, name, q, k, v, tab, sink, riders=()):
    Tq, Tk = q.shape[0], k.shape[0]

    def body(*refs):
        q_ref, k_ref, v_ref = refs[:3]
        extra = refs[3] if (cfg.mode == "na" or cfg.has_sink) else None
        o_ref, lse_ref = refs[-2:]
        i = pl.program_id(2)
        s_loc, s_ctx, win, ctx = _win_scores(cfg, i, q_ref, k_ref, extra if cfg.mode == "na" else None)
        m = jnp.maximum(jnp.max(s_loc, axis=-1, keepdims=True), jnp.max(s_ctx, axis=-1, keepdims=True))
        if cfg.has_sink:
            m = jnp.maximum(m, extra[0, 0:1, 0:1])
        p_loc, p_ctx = jnp.exp2(s_loc - m), jnp.exp2(s_ctx - m)
        l = jnp.sum(p_loc, axis=-1, keepdims=True) + jnp.sum(p_ctx, axis=-1, keepdims=True)
        if cfg.has_sink:
            l = l + jnp.exp2(extra[0, 0:1, 0:1] - m)
        acc = (jnp.dot(p_loc.astype(_MXU), v_ref[win, :].astype(_MXU), preferred_element_type=F32)
               + jnp.dot(p_ctx.astype(_MXU), v_ref[ctx, :].astype(_MXU), preferred_element_type=F32))
        o_ref[...] = acc / l
        lse_ref[0] = jnp.broadcast_to(m + jnp.log2(l), (ATT_BLOCK, LANE))

    q_spec, kv_spec, lse_spec, tab_spec, sink_spec = _win_specs(cfg, Tk)
    in_specs, args = [q_spec, kv_spec, kv_spec], [q, k, v]
    if cfg.mode == "na":
        in_specs.append(tab_spec)
        args.append(tab)
    if cfg.has_sink:
        in_specs.append(sink_spec)
        args.append(sink)
    return _pcall(
        body, name=name, grid=(cfg.Hkv, cfg.G, Tq // ATT_BLOCK), in_specs=in_specs, out_specs=[q_spec, lse_spec],
        out_shape=[jax.ShapeDtypeStruct((Tq, cfg.H * HEAD_DIM), F32), jax.ShapeDtypeStruct((cfg.H, Tq, LANE), F32)],
        args=args, riders=riders)


def _win_bwd_call(cfg, name, q, qs, k, v, tab, sink, o, lse, do, riders=()):
    Tq, Tk = q.shape[0], k.shape[0]
    dn_nt = (((1,), (1,)), ((), ()))
    dn_tn = (((0,), (0,)), ((), ()))

    def body(*refs):
        q_ref, k_ref, v_ref = refs[:3]
        has_extra = cfg.mode == "na" or cfg.has_sink
        extra = refs[3] if has_extra else None
        n = 4 if has_extra else 3
        o_ref, lse_ref, do_ref, qs_ref, dq_ref, dk_ref, dv_ref = refs[n:n + 7]
        dextra = refs[n + 7] if has_extra else None
        g, i = pl.program_id(1), pl.program_id(2)

        @pl.when((g == 0) & (i == 0))
        def _():
            dk_ref[...] = jnp.zeros_like(dk_ref)
            dv_ref[...] = jnp.zeros_like(dv_ref)

        if has_extra:
            @pl.when(i == 0)
            def _():
                dextra[...] = jnp.zeros_like(dextra)

        s_loc, s_ctx, win, ctx = _win_scores(cfg, i, q_ref, k_ref, extra if cfg.mode == "na" else None)
        lse = lse_ref[0][:, 0:1]
        p_loc, p_ctx = jnp.exp2(s_loc - lse), jnp.exp2(s_ctx - lse)
        do_f = do_ref[...]
        do_b = do_f.astype(_MXU)
        delta = jnp.sum(do_f * o_ref[...], axis=-1, keepdims=True)
        dp_loc = lax.dot_general(do_b, v_ref[win, :].astype(_MXU), dn_nt, preferred_element_type=F32)
        dp_ctx = lax.dot_general(do_b, v_ref[ctx, :].astype(_MXU), dn_nt, preferred_element_type=F32)
        ds_loc, ds_ctx = p_loc * (dp_loc - delta), p_ctx * (dp_ctx - delta)
        if cfg.mode == "na":
            for jj in range(cfg.nwin):
                top, bot = cfg.slots(i, jj)
                tile = ds_loc[:, jj * ATT_BLOCK:(jj + 1) * ATT_BLOCK]
                dextra[0, top] += tile[0:GRID_W, :]
                dextra[0, bot] += tile[GRID_W:, :]
        if cfg.has_sink:
            p_sink = jnp.exp2(extra[0, 0:1, 0:1] - lse)
            lane0 = ((lax.broadcasted_iota(jnp.int32, (8, LANE), 0) == 0)
                     & (lax.broadcasted_iota(jnp.int32, (8, LANE), 1) == 0))
            dextra[0] += jnp.where(lane0, -jnp.sum(p_sink * delta), 0.0)
        dsb_loc, dsb_ctx = ds_loc.astype(_MXU), ds_ctx.astype(_MXU)
        qb = qs_ref[...].astype(_MXU)
        dq_ref[...] = cfg.scale * (jnp.dot(dsb_loc, k_ref[win, :].astype(_MXU), preferred_element_type=F32)
                                   + jnp.dot(dsb_ctx, k_ref[ctx, :].astype(_MXU), preferred_element_type=F32))
        dk_ref[win, :] += lax.dot_general(dsb_loc, qb, dn_tn, preferred_element_type=F32)
        dk_ref[ctx, :] += lax.dot_general(dsb_ctx, qb, dn_tn, preferred_element_type=F32)
        dv_ref[win, :] += lax.dot_general(p_loc.astype(_MXU), do_b, dn_tn, preferred_element_type=F32)
        dv_ref[ctx, :] += lax.dot_general(p_ctx.astype(_MXU), do_b, dn_tn, preferred_element_type=F32)

    q_spec, kv_spec, lse_spec, tab_spec, sink_spec = _win_specs(cfg, Tk)
    in_specs, args = [q_spec, kv_spec, kv_spec], [q, k, v]
    out_specs = [q_spec, kv_spec, kv_spec]
    out_shape = [jax.ShapeDtypeStruct(q.shape, F32), jax.ShapeDtypeStruct(k.shape, F32),
                 jax.ShapeDtypeStruct(v.shape, F32)]
    if cfg.mode == "na":
        in_specs.append(tab_spec)
        args.append(tab)
        out_specs.append(tab_spec)
        out_shape.append(jax.ShapeDtypeStruct(tab.shape, F32))
    if cfg.has_sink:
        in_specs.append(sink_spec)
        args.append(sink)
        out_specs.append(sink_spec)
        out_shape.append(jax.ShapeDtypeStruct(sink.shape, F32))
    in_specs += [q_spec, lse_spec, q_spec, q_spec]
    args += [o, lse, do, qs]
    return _pcall(body, name=name, grid=(cfg.Hkv, cfg.G, Tq // ATT_BLOCK), in_specs=in_specs, out_specs=out_specs,
                  out_shape=out_shape, args=args, riders=riders)


def _attn_specs(cfg, Tk):
    G, tq = cfg.G, cfg.tq
    qmap = lambda kh, g, i: (i, kh * G + g)
    q_spec = pl.BlockSpec((tq, cfg.dqk), qmap)
    k_spec = pl.BlockSpec((Tk, cfg.dqk), lambda kh, g, i: (0, kh))
    v_spec = pl.BlockSpec((Tk, cfg.dv), lambda kh, g, i: (0, kh))
    o_spec = pl.BlockSpec((tq, cfg.dv), qmap)
    lse_spec = pl.BlockSpec((1, tq, LANE), lambda kh, g, i: (kh * G + g, i, 0))
    sink_spec = pl.BlockSpec((1, 8, LANE), lambda kh, g, i: (kh * G + g, 0, 0))
    return q_spec, k_spec, v_spec, o_spec, lse_spec, sink_spec


def _attn_fwd_call(cfg, name, q, k, v, sink, riders=()):
    Tq, Tk = q.shape[0], k.shape[0]
    nq, nkv = Tq // cfg.tq, Tk // cfg.tk
    dn = (((1,), (1,)), ((), ()))

    def body(*refs):
        q_ref, k_ref, v_ref = refs[:3]
        sink_ref = refs[3] if cfg.has_sink else None
        o_ref, lse_ref = refs[-2:]
        qb = q_ref[...].astype(_MXU)
        m = jnp.full((cfg.tq, 1), NEG, F32)
        l = jnp.zeros((cfg.tq, 1), F32)
        acc = jnp.zeros((cfg.tq, cfg.dv), F32)
        for j in range(nkv):
            rows = pl.ds(j * cfg.tk, cfg.tk)
            s = lax.dot_general(qb, k_ref[rows, :].astype(_MXU), dn, preferred_element_type=F32)
            m_new = jnp.maximum(m, jnp.max(s, axis=-1, keepdims=True))
            alpha = jnp.exp2(m - m_new)
            p = jnp.exp2(s - m_new)
            l = alpha * l + jnp.sum(p, axis=-1, keepdims=True)
            acc = alpha * acc + jnp.dot(p.astype(_MXU), v_ref[rows, :].astype(_MXU), preferred_element_type=F32)
            m = m_new
        if cfg.has_sink:
            sk = sink_ref[0, 0:1, 0:1]
            m_new = jnp.maximum(m, sk)
            alpha = jnp.exp2(m - m_new)
            l = alpha * l + jnp.exp2(sk - m_new)
            acc = acc * alpha
            m = m_new
        o_ref[...] = acc / l
        lse_ref[0] = jnp.broadcast_to(m + jnp.log2(l), (cfg.tq, LANE))

    q_spec, k_spec, v_spec, o_spec, lse_spec, sink_spec = _attn_specs(cfg, Tk)
    in_specs, args = [q_spec, k_spec, v_spec], [q, k, v]
    if cfg.has_sink:
        in_specs.append(sink_spec)
        args.append(sink)
    return _pcall(
        body, name=name, grid=(cfg.Hkv, cfg.G, nq), in_specs=in_specs, out_specs=[o_spec, lse_spec],
        out_shape=[jax.ShapeDtypeStruct((Tq, cfg.H * cfg.dv), F32), jax.ShapeDtypeStruct((cfg.H, Tq, LANE), F32)],
        args=args, riders=riders)


def _attn_bwd_call(cfg, name, q, qs, k, v, sink, o, lse, do, riders=()):
    Tq, Tk = q.shape[0], k.shape[0]
    nq, nkv = Tq // cfg.tq, Tk // cfg.tk
    tq, tk = cfg.tq, cfg.tk
    dn_nt = (((1,), (1,)), ((), ()))
    dn_tn = (((0,), (0,)), ((), ()))

    def body(*refs):
        q_ref, k_ref, v_ref = refs[:3]
        n = 3
        sink_ref = dsink_ref = None
        if cfg.has_sink:
            sink_ref, n = refs[n], n + 1
        o_ref, lse_ref, do_ref, qs_ref, dq_ref, dk_ref, dv_ref = refs[n:n + 7]
        n += 7
        if cfg.has_sink:
            dsink_ref, n = refs[n], n + 1
        g, i = pl.program_id(1), pl.program_id(2)

        @pl.when((g == 0) & (i == 0))
        def _():
            dk_ref[...] = jnp.zeros_like(dk_ref)
            dv_ref[...] = jnp.zeros_like(dv_ref)

        if cfg.has_sink:
            @pl.when(i == 0)
            def _():
                dsink_ref[...] = jnp.zeros_like(dsink_ref)

        do_f = do_ref[...]
        do_b = do_f.astype(_MXU)
        qb, qsb = q_ref[...].astype(_MXU), qs_ref[...].astype(_MXU)
        lse = lse_ref[0][:, 0:1]
        delta = jnp.sum(do_f * o_ref[...], axis=-1, keepdims=True)
        if cfg.has_sink:
            p_sink = jnp.exp2(sink_ref[0, 0:1, 0:1] - lse)
            lane0 = ((lax.broadcasted_iota(jnp.int32, (8, LANE), 0) == 0)
                     & (lax.broadcasted_iota(jnp.int32, (8, LANE), 1) == 0))
            dsink_ref[0] += jnp.where(lane0, -jnp.sum(p_sink * delta), 0.0)
        dq = jnp.zeros((tq, cfg.dqk), F32)
        for j in range(nkv):
            rows = pl.ds(j * tk, tk)
            kb, vb = k_ref[rows, :].astype(_MXU), v_ref[rows, :].astype(_MXU)
            s = lax.dot_general(qb, kb, dn_nt, preferred_element_type=F32)
            p = jnp.exp2(s - lse)
            dp = lax.dot_general(do_b, vb, dn_nt, preferred_element_type=F32)
            dsb = (p * (dp - delta)).astype(_MXU)
            dq = dq + jnp.dot(dsb, kb, preferred_element_type=F32)
            dk_ref[rows, :] += lax.dot_general(dsb, qsb, dn_tn, preferred_element_type=F32)
            dv_ref[rows, :] += lax.dot_general(p.astype(_MXU), do_b, dn_tn, preferred_element_type=F32)
        dq_ref[...] = dq * cfg.scale

    q_spec, k_spec, v_spec, o_spec, lse_spec, sink_spec = _attn_specs(cfg, Tk)
    in_specs, args = [q_spec, k_spec, v_spec], [q, k, v]
    if cfg.has_sink:
        in_specs.append(sink_spec)
        args.append(sink)
    in_specs += [o_spec, lse_spec, o_spec, q_spec]
    args += [o, lse, do, qs]
    out_specs = [q_spec, k_spec, v_spec]
    out_shape = [jax.ShapeDtypeStruct(q.shape, F32), jax.ShapeDtypeStruct(k.shape, F32),
                 jax.ShapeDtypeStruct(v.shape, F32)]
    if cfg.has_sink:
        out_specs.append(sink_spec)
        out_shape.append(jax.ShapeDtypeStruct(sink.shape, F32))
    return _pcall(body, name=name, grid=(cfg.Hkv, cfg.G, nq), in_specs=in_specs, out_specs=out_specs,
                  out_shape=out_shape, args=args, riders=riders)


def _make_attention(tag, cfg):
    windowed = isinstance(cfg, _WinCfg)

    def run_fwd(q, k, v, tab, sink, gin):
        if windowed:
            return _win_fwd_call(cfg, tag + "_fwd", q, k, v, tab, sink, _riders(gin, False))
        return _attn_fwd_call(cfg, tag + "_fwd", q, k, v, sink, _riders(gin, False))

    def prepared(q, k, v, tab, sink):
        q2 = (q * (cfg.scale * LOG2E)).astype(_MXU)
        tab2 = None if tab is None else tab * LOG2E
        sink2 = None if sink is None else sink * LOG2E
        return q2, k.astype(_MXU), v.astype(_MXU), tab2, sink2

    @jax.custom_vjp
    def attn(q, k, v, tab, sink, gin, tok):
        o, _, *got = run_fwd(*prepared(q, k, v, tab, sink), gin)
        return o, tuple(got), tok

    def fwd(q, k, v, tab, sink, gin, tok):
        q2, kb, vb, tab2, sink2 = prepared(q, k, v, tab, sink)
        o, lse, *got = run_fwd(q2, kb, vb, tab2, sink2, gin)
        return (o, tuple(got), tok), (q2, (q * cfg.scale).astype(_MXU), kb, vb, tab2, sink2, gin, o, lse)

    def bwd(res, cts):
        q, qs, k, v, tab, sink, gin, o, lse = res
        do, _, dtok = cts
        if windowed:
            outs = _win_bwd_call(cfg, tag + "_bwd", q, qs, k, v, tab, sink, o, lse, do, _riders(dtok, True))
        else:
            outs = _attn_bwd_call(cfg, tag + "_bwd", q, qs, k, v, sink, o, lse, do, _riders(dtok, True))
        dq, dk, dv = outs[:3]
        rest = outs[3:]
        dtab = rest.pop(0) if tab is not None else None
        dsink = rest.pop(0) if sink is not None else None
        return dq, dk, dv, dtab, dsink, _nones(gin), tuple(rest)

    attn.defvjp(fwd, bwd)
    return attn


def _pick_block(n, prefs):
    for p in prefs:
        if n % p == 0:
            return p
    return n


def _na_table(rpb):
    cq = np.arange(GRID_W)
    dcol = np.clip(cq[None, :] - cq[:, None] + NA_WIN_C - 1, 0, 2 * NA_WIN_C - 2)
    onehot = (dcol[:, :, None] == np.arange(2 * NA_WIN_C - 1)[None, None, :]).astype(np.float32)
    c0 = np.clip(cq - NA_WIN_C // 2, 0, GRID_W - NA_WIN_C)
    col_in = (cq[None, :] >= c0[:, None]) & (cq[None, :] < c0[:, None] + NA_WIN_C)
    tz = jnp.einsum("hrj,qkj->hrqk", rpb, jnp.asarray(onehot), precision=lax.Precision.HIGHEST)
    tz = jnp.where(jnp.asarray(col_in)[None, None], tz, NEG)
    zero = jnp.zeros_like(tz[:, :1])
    tzp = jnp.concatenate([zero, tz, zero], axis=1)
    return jnp.concatenate([tzp[:, 0:16], tzp[:, 1:17]], axis=-1)


def _sink_block(sink):
    return jnp.broadcast_to(sink[:, None, None], (sink.shape[0], 8, LANE))


def _make_split(bounds, axis):
    @jax.custom_vjp
    def split(t):
        return tuple(lax.slice_in_dim(t, a, b, axis=axis) for a, b in bounds)

    def fwd(t):
        return split(t), None

    def bwd(_, cts):
        return (jnp.concatenate(cts, axis=axis),)

    split.defvjp(fwd, bwd)
    return split


_IN_GROUPS = ((0, 512), (512, 1024), (1024, 1536), (1536, 2048), (2048, 2304), (2304, 2560), (2560, 2944),
              (2944, 3072), (3072, 3200), (3200, 3712), (3712, 3968), (3968, 4224))


def _seg2(vx, vc, nseg):
    rows = [vx, vc][:nseg]
    return jnp.stack(rows)[:, None, :]


_BIG_COL = ("w_in", "mla_w_uq", "mla_w_ukv", "ffn_w_gate", "ffn_w_up")
_BIG_ROW = ("w_out", "ffn_w_down")
_IN_GROUP = ("w_in", "mla_w_uq", "mla_w_ukv")
_PIECES = {"w_in": 4, "ffn_w_down": 2}


def _n_pieces(name):
    return _PIECES.get(name, 1)


def _shard_view(name, w):
    return jnp.swapaxes(w, 1, 2) if name in _BIG_COL else w


def _cut(name, a):
    return tuple(jnp.split(a, _n_pieces(name), axis=-1))


def _full_weight(name, pieces):
    g = jnp.concatenate(pieces, axis=-1) if len(pieces) > 1 else pieces[0]
    full = g.reshape(N_DEV * g.shape[1], g.shape[2])
    if name in _BIG_ROW:
        return full
    full = full.T
    if name == "w_in":
        K = full.shape[0]
        full = jnp.concatenate([full[:, :KPE_END], jnp.zeros((K, 64), full.dtype), full[:, KPE_END:]], axis=-1)
    return full


def _grad_parts(name):
    def to_parts(dw):
        if name == "w_in":
            dw = jnp.concatenate([dw[:, :KPE_END], dw[:, KPE_END + 64:]], axis=-1)
        if name in _BIG_COL:
            dw = dw.T
        return _cut(name, dw.reshape(N_DEV, dw.shape[0] // N_DEV, dw.shape[1]).astype(BF16))

    return to_parts


_RIDE_FWD = {
    "attn_a": ((0, "w_out", 0),), "attn_b": ((1, "w_in", 0), (1, "w_in", 1)), "attn_c": ((0, "ffn_w_gate", 0),),
    "attn_d": ((0, "ffn_w_up", 0),), "gate": ((0, "ffn_w_down", 0),), "up": ((0, "ffn_w_down", 1),),
    "conv": ((1, "mla_w_uq", 0), (1, "mla_w_ukv", 0)), "down": ((1, "w_in", 2), (1, "w_in", 3)),
}
_RIDE_BWD = {
    "attn_a": ((0, "w_out", 0),), "attn_b": ((1, "mla_w_uq", 0), (1, "mla_w_ukv", 0)),
    "attn_c": ((0, "ffn_w_gate", 0),), "attn_d": ((0, "ffn_w_up", 0),), "gate": ((0, "ffn_w_down", 0),),
    "up": ((0, "ffn_w_down", 1),), "conv": ((1, "w_in", 0), (1, "w_in", 1)), "down": ((1, "w_in", 2), (1, "w_in", 3)),
}


def _local_loss(x, ctx, tgt, modx, modc, p, full, shards, toks):
    S, D = x.shape
    C = ctx.shape[0]
    depth = len(full)
    ride = shards is not None
    have = [dict(f) for f in full]
    landing = {}
    sinks = {(l, n): list(toks[l][n]) for l in range(depth) for n in toks[l]}
    alpha = (2 * depth) ** 0.25
    T0 = S + C
    nx = S // ATT_BLOCK

    def riding(table, unit, l):
        return [(l + dl, n, k) for dl, n, k in table[unit] if l + dl < depth] if ride else []

    def gin(unit, l):
        return tuple(shards[ll][n][k] for ll, n, k in riding(_RIDE_FWD, unit, l))

    def tok(unit, l):
        return tuple(sinks[(ll, n)][k] for ll, n, k in riding(_RIDE_BWD, unit, l))

    def landed(result, unit, l):
        out, got, tk = result
        for (ll, n, k), g in zip(riding(_RIDE_FWD, unit, l), got):
            landing.setdefault((ll, n), {})[k] = g
            if len(landing[(ll, n)]) == _n_pieces(n):
                have[ll][n] = _full_weight(n, [landing[(ll, n)][j] for j in range(_n_pieces(n))])
        for (ll, n, k), s in zip(riding(_RIDE_BWD, unit, l), tk):
            sinks[(ll, n)][k] = s
        return out

    sc = HEAD_DIM ** -0.5
    sc_mla = (MLA_NOPE + MLA_ROPE) ** -0.5
    tq_full = _pick_block(S, (256, 128))
    tk_all = _pick_block(T0, (2176, 640, 512, 256, 128))
    tctx = _pick_block(C, (256, 128))
    rope128 = _rope_tables(S, T0, 128, 32)
    rope64 = _rope_tables(S, T0, 64, 16)

    xs = jnp.concatenate([x, ctx], axis=0)
    for l in range(depth):
        last = l == depth - 1
        t = "l%d_" % l
        lin = lambda a, name, unit=None, out=F32: _make_linear(t + name, _grad_parts(name), out)(
            a, have[l][name], tuple(sinks[(l, name)]), gin(unit, l) if unit else (), tok(unit, l) if unit else ())
        mx, mc = modx[l], modc[l]

        h = _make_modulate(t + "mod1", S)(xs, _seg2(mx[0], mc[0], 2), _seg2(mx[1], mc[1], 2))
        pj = lin(h, "w_in")[0]
        qa, ka, va, qb, kb, vb, cq, ckv, kpe, qd, kd, vd = _make_split(_IN_GROUPS, 1)(pj)
        rows = lambda v_: _make_split(((0, S), (S, T0)), 0)(v_)
        att = lambda unit, cfg, q_, k_, v_, tab_, s_: landed(
            _make_attention(t + unit, cfg)(q_, k_, v_, tab_, s_, gin(unit, l), tok(unit, l)), unit, l)
        tab = _na_table(p["na_rpb"][l])
        cfg_a = _WinCfg("na", NA_HEADS, NA_HEADS, sc, nx, C)
        qa_x, qa_c = rows(qa)
        oa = att("attn_a", cfg_a, qa_x, ka, va, tab, None)
        qb = _make_rope(t + "rope_b", 32)(qb, *rope128)
        kb = _make_rope(t + "rope_bk", 32)(kb, *rope128)
        snk = _sink_block(p["swa_sink"][l])
        cfg_b = _WinCfg("swa", SWA_HEADS, SWA_KV_HEADS, sc, nx, C)
        qb_x, qb_c = rows(qb)
        ob = att("attn_b", cfg_b, qb_x, kb, vb, None, snk)
        cq = _make_rmsnorm(t + "rms_cq")(cq, p["mla_q_norm"][l][None, :])
        ckv = _make_rmsnorm(t + "rms_ckv")(ckv, p["mla_kv_norm"][l][None, :])
        qh = lin(cq, "mla_w_uq")[0].reshape(T0, MLA_HEADS, MLA_NOPE + MLA_ROPE)
        kvh = lin(ckv, "mla_w_ukv")[0].reshape(T0, MLA_HEADS, MLA_NOPE + MLA_V)
        qpe = jnp.pad(qh[:, :, MLA_NOPE:], ((0, 0), (0, 0), (0, LANE - MLA_ROPE))).reshape(T0, MLA_HEADS * LANE)
        qpe = _make_rope(t + "rope_cq", 16)(qpe, *rope64).reshape(T0, MLA_HEADS, LANE)
        kpe = _make_rope(t + "rope_ck", 16)(kpe, *rope64)
        qc = jnp.concatenate([qh[:, :, :MLA_NOPE], qpe], axis=-1).reshape(T0, MLA_HEADS * 2 * LANE)
        kc = jnp.concatenate([kvh[:, :, :MLA_NOPE], jnp.broadcast_to(kpe[:, None, :], (T0, MLA_HEADS, LANE))],
                             axis=-1).reshape(T0, MLA_HEADS * 2 * LANE)
        vc = kvh[:, :, MLA_NOPE:].reshape(T0, MLA_HEADS * MLA_V)
        cfg_c = _AttnCfg(MLA_HEADS, MLA_HEADS, 2 * LANE, MLA_V, sc_mla, tq_full, tk_all)
        qc_x, qc_c = rows(qc)
        oc = att("attn_c", cfg_c, qc_x, kc, vc, None, None)
        qd = _make_rmsnorm(t + "rms_dq")(qd, p["gqa_q_norm"][l][None, :])
        kd = _make_rmsnorm(t + "rms_dk")(kd, p["gqa_k_norm"][l][None, :])
        qd = _make_rope(t + "rope_dq", 32)(qd, *rope128)
        kd = _make_rope(t + "rope_dk", 32)(kd, *rope128)
        cfg_d = _AttnCfg(GQA_HEADS, GQA_KV_HEADS, 128, 128, sc, tq_full, tk_all)
        qd_x, qd_c = rows(qd)
        od = att("attn_d", cfg_d, qd_x, kd, vd, None, None)
        mix = jnp.concatenate([oa, ob, oc, od], axis=1)
        if not last:
            full_c = lambda H, Hkv, dqk, s, sink: _AttnCfg(H, Hkv, dqk, 128, s, tctx, tctx, has_sink=sink)
            ctx_att = lambda name, cfg, q_, k_, v_, s_: _make_attention(t + name, cfg)(
                q_, k_[S:], v_[S:], None, s_, (), ())[0]
            oa_c = ctx_att("ctx_a", full_c(4, 4, 128, sc, False), qa_c, ka, va, None)
            ob_c = ctx_att("ctx_b", full_c(4, 2, 128, sc, True), qb_c, kb, vb, snk)
            oc_c = ctx_att("ctx_c", full_c(4, 4, 256, sc_mla, False), qc_c, kc, vc, None)
            od_c = ctx_att("ctx_d", full_c(4, 2, 128, sc, False), qd_c, kd, vd, None)
            mix = jnp.concatenate([mix, jnp.concatenate([oa_c, ob_c, oc_c, od_c], axis=1)], axis=0)
            res, nseg = xs, 2
        else:
            res, nseg = xs[:S], 1
        y = lin(mix, "w_out")[0]
        x1 = _make_resid_ln(t + "ln1", S, alpha)(res, y, _seg2(mx[2], mc[2], nseg), p["ln1_g"][l][None, :],
                                                 p["ln1_b"][l][None, :])
        h2 = _make_modulate(t + "mod2", S)(x1, _seg2(mx[3], mc[3], nseg), _seg2(mx[4], mc[4], nseg))
        gp = landed(lin(h2, "ffn_w_gate", "gate", _ACT), "gate", l)
        up = landed(lin(h2, "ffn_w_up", "up", _ACT), "up", l)
        z = landed(_make_conv_gate(t + "conv", S)(gp, up, p["ffn_conv_w"][l], p["ffn_conv_b"][l][None, :],
                                                  gin("conv", l), tok("conv", l)), "conv", l)
        f = landed(lin(z, "ffn_w_down", "down"), "down", l)
        xs = _make_resid_ln(t + "ln2", S, alpha)(x1, f, _seg2(mx[5], mc[5], nseg), p["ln2_g"][l][None, :],
                                                 p["ln2_b"][l][None, :])
    return _make_loss_head("loss_head")(xs, tgt)


def _adamw(w, parts, m, v, name):
    R, C = w.shape
    P = parts.shape[0]
    c1 = 1.0 - ADAM_B1 ** ADAM_STEP
    c2 = 1.0 - ADAM_B2 ** ADAM_STEP
    per_row = C * (4 * 14 + 2 * P * parts.dtype.itemsize)
    cands = _divisors(R, 8, R)
    fitting = [d for d in cands if d * per_row <= 24 * 1024 * 1024]
    tr = max(fitting) if fitting else min(cands)

    def body(w_ref, p_ref, m_ref, v_ref, g_ref, d_ref, nm_ref, nv_ref):
        g = p_ref[0].astype(F32)
        for k in range(1, P):
            g = g + p_ref[k].astype(F32)
        mm = ADAM_B1 * m_ref[...] + (1.0 - ADAM_B1) * g
        vv = ADAM_B2 * v_ref[...] + (1.0 - ADAM_B2) * (g * g)
        g_ref[...] = g
        nm_ref[...] = mm
        nv_ref[...] = vv
        d_ref[...] = -ADAM_LR * ((mm / c1) / (jnp.sqrt(vv / c2) + ADAM_EPS) + ADAM_WD * w_ref[...])

    blk = pl.BlockSpec((tr, C), lambda i: (i, 0))
    pblk = pl.BlockSpec((P, tr, C), lambda i: (0, i, 0))
    sh = jax.ShapeDtypeStruct((R, C), F32)
    return pl.pallas_call(body, name=name, grid=(R // tr,), in_specs=[blk, pblk, blk, blk],
                          out_specs=[blk, blk, blk, blk], out_shape=[sh, sh, sh, sh],
                          compiler_params=_cparams(1))(w, parts, m, v)


def _adamw_nd(w, parts, m, v, name):
    shape = w.shape
    C = shape[-1]
    R = int(np.prod(shape[:-1])) if len(shape) > 1 else 1
    outs = _adamw(w.reshape(R, C), parts.reshape(parts.shape[0], R, C), m.reshape(R, C), v.reshape(R, C), name)
    return [o.reshape(shape) for o in outs]


def _adamw_layers(w, parts, m, v, name, riders=()):
    L, R, C = w.shape
    P = parts[0].shape[0]
    c1 = 1.0 - ADAM_B1 ** ADAM_STEP
    c2 = 1.0 - ADAM_B2 ** ADAM_STEP
    per_row = C * (4 * 14 + 2 * L * P * parts[0].dtype.itemsize)
    cands = _divisors(R, 8, R)
    fitting = [d for d in cands if d * per_row <= 24 * 1024 * 1024]
    tr = max(fitting) if fitting else min(cands)
    nt = R // tr

    def body(*refs):
        w_ref, p_refs, (m_ref, v_ref, g_ref, d_ref, nm_ref, nv_ref) = refs[0], refs[1:1 + L], refs[1 + L:]
        layer = pl.program_id(0)
        for k in range(L):
            @pl.when(layer == k)
            def _(p_ref=p_refs[k]):
                g = p_ref[0].astype(F32)
                for j in range(1, P):
                    g = g + p_ref[j].astype(F32)
                mm = ADAM_B1 * m_ref[0] + (1.0 - ADAM_B1) * g
                vv = ADAM_B2 * v_ref[0] + (1.0 - ADAM_B2) * (g * g)
                g_ref[0] = g
                nm_ref[0] = mm
                nv_ref[0] = vv
                d_ref[0] = -ADAM_LR * ((mm / c1) / (jnp.sqrt(vv / c2) + ADAM_EPS) + ADAM_WD * w_ref[0])

    blk = pl.BlockSpec((1, tr, C), lambda l, i: (l, i, 0))
    pblk = [pl.BlockSpec((P, tr, C), lambda l, i, k=k: (0, jnp.where(l == k, i, jnp.where(l < k, 0, nt - 1)), 0))
            for k in range(L)]
    sh = jax.ShapeDtypeStruct((L, R, C), F32)
    return _pcall(body, name=name, grid=(L, nt), in_specs=[blk] + pblk + [blk, blk],
                  out_specs=[blk, blk, blk, blk], out_shape=[sh, sh, sh, sh], args=[w, *parts, m, v], riders=riders)


def _silu(v):
    return v * (1.0 / (1.0 + jnp.exp(-v)))


def _ada_rows(c_all, c_ctx):
    return jnp.concatenate([c_all, jnp.broadcast_to(c_ctx[None, :], (N_DEV, c_ctx.shape[0]))], axis=0)


def _silu_rows(rows, name):
    def body(r_ref, o_ref):
        o_ref[...] = _silu(r_ref[...])

    return pl.pallas_call(body, name=name, out_shape=jax.ShapeDtypeStruct(rows.shape, F32))(rows)


_SMALL = ("c_ctx", "na_rpb", "swa_sink", "mla_q_norm", "mla_kv_norm", "gqa_q_norm", "gqa_k_norm",
          "ln1_g", "ln1_b", "ffn_conv_b", "ln2_g", "ln2_b")
_NAMES = ("c_ctx", "w_ada", "b_ada", "w_in", "na_rpb", "swa_sink", "mla_q_norm", "mla_kv_norm", "mla_w_uq",
          "mla_w_ukv", "gqa_q_norm", "gqa_k_norm", "w_out", "ln1_g", "ln1_b", "ffn_w_gate", "ffn_w_up",
          "ffn_conv_w", "ffn_conv_b", "ffn_w_down", "ln2_g", "ln2_b")


def _full_from_cols(g, pad_in=False):
    _, L, K, n = g.shape
    full = jnp.transpose(g, (1, 2, 0, 3)).reshape(L, K, N_DEV * n)
    if pad_in:
        full = jnp.concatenate([full[:, :, :KPE_END], jnp.zeros((L, K, 64), full.dtype), full[:, :, KPE_END:]], axis=-1)
    return full


def _pack_small(tree, extra=None):
    flat = [tree[n].reshape(-1) for n in _SMALL]
    flat.append(jnp.zeros((1,), F32) if extra is None else extra.reshape(1))
    v = jnp.concatenate(flat)
    n = v.shape[0]
    padded = -(-n // 1024) * 1024
    return jnp.pad(v, (0, padded - n)).reshape(padded // LANE, LANE)


def _unpack_small(mat, like):
    v = mat.reshape(-1)
    out, o = {}, 0
    for n in _SMALL:
        k = int(np.prod(like[n].shape))
        out[n] = v[o:o + k].reshape(like[n].shape)
        o += k
    return out, v[o]


def kernel(x, c, ctx, c_ctx, w_ada, b_ada, w_in, na_rpb, swa_sink, mla_q_norm, mla_kv_norm, mla_w_uq, mla_w_ukv, gqa_q_norm, gqa_k_norm, w_out, ln1_g, ln1_b, ffn_w_gate, ffn_w_up, ffn_conv_w, ffn_conv_b, ffn_w_down, ln2_g, ln2_b, loss_target, m_c_ctx, m_w_ada, m_b_ada, m_w_in, m_na_rpb, m_swa_sink, m_mla_q_norm, m_mla_kv_norm, m_mla_w_uq, m_mla_w_ukv, m_gqa_q_norm, m_gqa_k_norm, m_w_out, m_ln1_g, m_ln1_b, m_ffn_w_gate, m_ffn_w_up, m_ffn_conv_w, m_ffn_conv_b, m_ffn_w_down, m_ln2_g, m_ln2_b, v_c_ctx, v_w_ada, v_b_ada, v_w_in, v_na_rpb, v_swa_sink, v_mla_q_norm, v_mla_kv_norm, v_mla_w_uq, v_mla_w_ukv, v_gqa_q_norm, v_gqa_k_norm, v_w_out, v_ln1_g, v_ln1_b, v_ffn_w_gate, v_ffn_w_up, v_ffn_conv_w, v_ffn_conv_b, v_ffn_w_down, v_ln2_g, v_ln2_b):
    W = dict(c_ctx=c_ctx, w_ada=w_ada, b_ada=b_ada, w_in=w_in, na_rpb=na_rpb, swa_sink=swa_sink,
             mla_q_norm=mla_q_norm, mla_kv_norm=mla_kv_norm, mla_w_uq=mla_w_uq, mla_w_ukv=mla_w_ukv,
             gqa_q_norm=gqa_q_norm, gqa_k_norm=gqa_k_norm, w_out=w_out, ln1_g=ln1_g, ln1_b=ln1_b,
             ffn_w_gate=ffn_w_gate, ffn_w_up=ffn_w_up, ffn_conv_w=ffn_conv_w, ffn_conv_b=ffn_conv_b,
             ffn_w_down=ffn_w_down, ln2_g=ln2_g, ln2_b=ln2_b)
    M = dict(c_ctx=m_c_ctx, w_ada=m_w_ada, b_ada=m_b_ada, w_in=m_w_in, na_rpb=m_na_rpb, swa_sink=m_swa_sink,
             mla_q_norm=m_mla_q_norm, mla_kv_norm=m_mla_kv_norm, mla_w_uq=m_mla_w_uq, mla_w_ukv=m_mla_w_ukv,
             gqa_q_norm=m_gqa_q_norm, gqa_k_norm=m_gqa_k_norm, w_out=m_w_out, ln1_g=m_ln1_g, ln1_b=m_ln1_b,
             ffn_w_gate=m_ffn_w_gate, ffn_w_up=m_ffn_w_up, ffn_conv_w=m_ffn_conv_w, ffn_conv_b=m_ffn_conv_b,
             ffn_w_down=m_ffn_w_down, ln2_g=m_ln2_g, ln2_b=m_ln2_b)
    V = dict(c_ctx=v_c_ctx, w_ada=v_w_ada, b_ada=v_b_ada, w_in=v_w_in, na_rpb=v_na_rpb, swa_sink=v_swa_sink,
             mla_q_norm=v_mla_q_norm, mla_kv_norm=v_mla_kv_norm, mla_w_uq=v_mla_w_uq, mla_w_ukv=v_mla_w_ukv,
             gqa_q_norm=v_gqa_q_norm, gqa_k_norm=v_gqa_k_norm, w_out=v_w_out, ln1_g=v_ln1_g, ln1_b=v_ln1_b,
             ffn_w_gate=v_ffn_w_gate, ffn_w_up=v_ffn_w_up, ffn_conv_w=v_ffn_conv_w, ffn_conv_b=v_ffn_conv_b,
             ffn_w_down=v_ffn_w_down, ln2_g=v_ln2_g, ln2_b=v_ln2_b)
    L, D, n_ada = w_ada.shape
    me = 4 * lax.axis_index("x") + 2 * lax.axis_index("y") + lax.axis_index("c")
    xs, ctxs, tgt = x[0], ctx[0], loss_target[0]

    c_all = _exchange(c, False, "gather_c").reshape(N_DEV, D)
    a_rows = _silu_rows(_ada_rows(c_all, c_ctx), "ada_silu")
    b_mine = lax.dynamic_slice(b_ada, (0, me * n_ada), (L, n_ada))
    mod_mine = jnp.stack([_matmul(a_rows, w_ada[l], "nn", "ada_fwd_l%d" % l) + b_mine[l][None, :] for l in range(L)])
    mod_all = _exchange(mod_mine, False, "gather_mod")
    mod_all = jnp.transpose(mod_all, (1, 2, 0, 3)).reshape(L, 2 * N_DEV, N_DEV * n_ada)
    modx = lax.dynamic_slice(mod_all, (0, me, 0), (L, 1, 6 * D)).reshape(L, 6, D)
    modc = mod_all[:, N_DEV].reshape(L, 6, D)

    big = _BIG_COL + _BIG_ROW
    views = {n: _shard_view(n, W[n]) for n in big}
    shards = [{n: _cut(n, views[n][l].astype(BF16)) for n in big} for l in range(L)]
    full = [{n: _full_weight(n, [_exchange(views[n][0].astype(BF16), False, "gather_l0_" + n)]) for n in _IN_GROUP}]
    full += [{} for _ in range(1, L)]
    conv_w_full = _full_from_cols(_exchange(ffn_conv_w, False, "gather_conv_w"))

    toks = [{n: tuple(jnp.zeros((N_DEV,) + s.shape, BF16) for s in shards[l][n]) for n in big} for l in range(L)]
    small = {n: W[n] for n in _SMALL if n != "c_ctx"}
    small["ffn_conv_w"] = conv_w_full

    def loss_fn(xv, mxv, mcv, sm, tk):
        return _local_loss(xv, ctxs, tgt, mxv, mcv, sm, full, shards, tk)

    loss_local, (grad_x, dmodx, dmodc, dsmall, dtoks) = jax.value_and_grad(loss_fn, argnums=(0, 1, 2, 3, 4))(
        xs, modx, modc, small, toks)

    dmods = _exchange(jnp.stack([dmodx.reshape(L, 6 * D), dmodc.reshape(L, 6 * D)]), False, "gather_dmod")
    dm_rows = jnp.concatenate([dmods[:, 0], dmods[:, 1]], axis=0)
    dm_rows = jnp.transpose(dm_rows, (1, 0, 2))
    dm_mine = lax.dynamic_slice(dm_rows, (0, 0, me * n_ada), (L, 2 * N_DEV, n_ada))
    g_w_ada = [_matmul(a_rows, dm_mine[l], "tn", "ada_dw_l%d" % l) for l in range(L)]
    d_rows = sum(_matmul(dm_mine[l], w_ada[l], "nt", "ada_da_l%d" % l) for l in range(L))
    sig = 1.0 / (1.0 + jnp.exp(-c_ctx))
    d_c_ctx_part = jnp.sum(d_rows[N_DEV:], axis=0) * (sig * (1.0 + c_ctx * (1.0 - sig)))

    outs = {}
    pieces = [{n: list(dtoks[l][n]) for n in big} for l in range(L)]
    late = {"ffn_w_gate": (("w_in", 0),), "ffn_w_up": (("w_in", 1),), "ffn_w_down": (("w_in", 2),),
            "w_ada": (("w_in", 3),), "w_out": (("mla_w_uq", 0), ("mla_w_ukv", 0))}

    def update(n, w, g_parts, m, v):
        res = _adamw_layers(w, g_parts, m, v, "adamw_" + n, _riders([pieces[0][a][k] for a, k in late.get(n, ())], True))
        for (a, k), got in zip(late.get(n, ()), res[4:]):
            pieces[0][a][k] = got
        return res[:4]

    def whole(l, n):
        return jnp.concatenate(pieces[l][n], axis=-1) if len(pieces[l][n]) > 1 else pieces[l][n][0]

    outs["w_ada"] = update("w_ada", w_ada, [g[None] for g in g_w_ada], M["w_ada"], V["w_ada"])
    for n in [n for n in big if n not in _IN_GROUP] + list(_IN_GROUP):
        res = update(n, views[n], [whole(l, n) for l in range(L)], _shard_view(n, M[n]), _shard_view(n, V[n]))
        outs[n] = [_shard_view(n, r) for r in res]
    dcw = dsmall.pop("ffn_conv_w")
    cw_parts = _exchange(jnp.transpose(dcw.reshape(L, 3, N_DEV, -1), (2, 0, 1, 3)), True, "scatter_conv_w")
    outs["ffn_conv_w"] = _adamw_layers(ffn_conv_w, [cw_parts[:, l] for l in range(L)], M["ffn_conv_w"],
                                       V["ffn_conv_w"], "adamw_conv_w")
    outs["b_ada"] = _adamw_nd(b_ada, jnp.transpose(dm_rows, (1, 0, 2)), M["b_ada"], V["b_ada"], "adamw_b_ada")

    dsmall["c_ctx"] = d_c_ctx_part
    small_parts = _exchange(_pack_small(dsmall, extra=loss_local), False, "gather_small")
    w_small = _pack_small({n: W[n] for n in _SMALL})
    g_s, d_s, nm_s, nv_s = _adamw(w_small, small_parts, _pack_small({n: M[n] for n in _SMALL}),
                                  _pack_small({n: V[n] for n in _SMALL}), "adamw_small")
    like = {n: W[n] for n in _SMALL}
    g_small, loss = _unpack_small(g_s, like)
    unpacked = [g_small, _unpack_small(d_s, like)[0], _unpack_small(nm_s, like)[0], _unpack_small(nv_s, like)[0]]
    for n in _SMALL:
        outs[n] = [u[n] for u in unpacked]

    result = [loss, grad_x[None]]
    for k in range(4):
        result += [outs[n][k] for n in _NAMES]
    return tuple(result)
```

```python
import functools
import math

import numpy as np
import jax
import jax.numpy as jnp
from jax import lax
from jax.experimental import pallas as pl
from jax.experimental.pallas import tpu as pltpu

F32 = jnp.float32
BF16 = jnp.bfloat16
_MXU = jnp.bfloat16
_ACT = jnp.bfloat16

N_DEV = 8
GRID_W = 64
HEAD_DIM = 128
NA_HEADS, NA_WIN_R, NA_WIN_C = 4, 8, 16
SWA_HEADS, SWA_KV_HEADS, SWA_WINDOW = 4, 2, 128
MLA_HEADS, MLA_Q_LORA, MLA_KV_LORA, MLA_NOPE, MLA_ROPE, MLA_V = 4, 384, 128, 128, 64, 128
GQA_HEADS, GQA_KV_HEADS = 4, 2
ROPE_THETA = 10000.0
EPS = 1e-6
NEG = -1e30
LOG2E = 1.4426950408889634
IN_SIZES = (512, 512, 512, 512, 256, 256, MLA_Q_LORA, MLA_KV_LORA, MLA_ROPE, 512, 256, 256)
IN_COLS = sum(IN_SIZES)
KPE_END = sum(IN_SIZES[:9])
IN_COLS_PAD = IN_COLS + 64
ADAM_LR, ADAM_B1, ADAM_B2, ADAM_EPS, ADAM_WD, ADAM_STEP = 0.001, 0.9, 0.999, 1e-08, 0.01, 10

LANE = 128
ROW_TILE = 256
ATT_BLOCK = 128
VMEM_LIMIT = 56 * 1024 * 1024
MM_BUDGET = 36 * 1024 * 1024
HBM_BPS = 3.0e12
MXU_FLOPS = 9.0e14
VMEM_BPS = 4.0e12
STEP_S = 0.4e-6


def _cparams(n_axes):
    return pltpu.CompilerParams(dimension_semantics=("arbitrary",) * n_axes, vmem_limit_bytes=VMEM_LIMIT)


def _exchange_copies(in_ref, out_ref, send_sems, recv_sems, local_sem, scatter, rows=None):
    x, y, c = lax.axis_index("x"), lax.axis_index("y"), lax.axis_index("c")
    me = 4 * x + 2 * y + c
    cut = (lambda ref: ref) if rows is None else (lambda ref: ref.at[rows])
    copies = [pltpu.make_async_copy(cut(in_ref.at[me] if scatter else in_ref), cut(out_ref.at[me]), local_sem)]
    for k in range(1, N_DEV):
        px, py, pc = (x + (k >> 2)) % 2, (y + ((k >> 1) & 1)) % 2, (c + (k & 1)) % 2
        peer = 4 * px + 2 * py + pc
        copies.append(pltpu.make_async_remote_copy(
            src_ref=cut(in_ref.at[peer] if scatter else in_ref), dst_ref=cut(out_ref.at[me]),
            send_sem=send_sems.at[k - 1], recv_sem=recv_sems.at[k - 1],
            device_id=(px, py, pc), device_id_type=pl.DeviceIdType.MESH))
    return copies


PIECE_ROWS = 16
MAX_PIECES = 32


def _pieces(rows, steps):
    if rows % PIECE_ROWS:
        return 1
    units = rows // PIECE_ROWS
    return max(d for d in range(1, min(MAX_PIECES, steps) + 1) if units % d == 0)


def _exchange_out_shape(arr, scatter):
    return jax.ShapeDtypeStruct(arr.shape if scatter else (N_DEV,) + arr.shape, arr.dtype)


_EXCHANGE_SEMS = (pltpu.SemaphoreType.DMA((N_DEV - 1,)), pltpu.SemaphoreType.DMA((N_DEV - 1,)),
                  pltpu.SemaphoreType.DMA)


def _exchange(inp, scatter, name):
    def body(in_ref, out_ref, send_sems, recv_sems, local_sem):
        copies = _exchange_copies(in_ref, out_ref, send_sems, recv_sems, local_sem, scatter)
        for cp in copies:
            cp.start()
        for cp in copies:
            cp.wait()

    hbm = pl.BlockSpec(memory_space=pl.ANY)
    return pl.pallas_call(
        body, name=name, in_specs=[hbm], out_specs=hbm, out_shape=_exchange_out_shape(inp, scatter),
        scratch_shapes=list(_EXCHANGE_SEMS), compiler_params=pltpu.CompilerParams(has_side_effects=True),
    )(inp)


def _gather_by_chip(inp, name):
    def body(x_ref, out_ref, send_sems, recv_sems, local_sem):
        x, y, c = lax.axis_index("x"), lax.axis_index("y"), lax.axis_index("c")
        me, sibling = (x, y, c), (x, y, 1 - c)
        chips = [(1 - x, y), (x, 1 - y), (1 - x, 1 - y)]

        def slot(px, py, pc):
            return out_ref.at[4 * px + 2 * py + pc]

        def copy(k, block, to, src=None):
            return pltpu.make_async_remote_copy(
                src_ref=slot(*block) if src is None else src, dst_ref=slot(*block), send_sem=send_sems.at[k],
                recv_sem=recv_sems.at[k], device_id=to, device_id_type=pl.DeviceIdType.MESH)

        mine = pltpu.make_async_copy(x_ref, slot(*me), local_sem)
        mine.start()
        first = [copy(0, me, sibling, src=x_ref)]
        first += [copy(1 + j, me, (*chip, c), src=x_ref) for j, chip in enumerate(chips)]
        for cp in first:
            cp.start()
        passed = [copy(4 + j, (*chip, c), sibling) for j, chip in enumerate(chips)]
        for j, chip in enumerate(chips):
            copy(1 + j, (*chip, c), me).wait_recv()
            passed[j].start()
        copy(0, sibling, me).wait_recv()
        for j, chip in enumerate(chips):
            copy(4 + j, (*chip, 1 - c), me).wait_recv()
        for cp in first + passed:
            cp.wait_send()
        mine.wait()

    hbm = pl.BlockSpec(memory_space=pl.ANY)
    return pl.pallas_call(
        body, name=name, in_specs=[hbm], out_specs=hbm, out_shape=_exchange_out_shape(inp, False),
        scratch_shapes=list(_EXCHANGE_SEMS), compiler_params=pltpu.CompilerParams(has_side_effects=True),
    )(inp)


def _pcall(body, *, name, grid, in_specs, out_specs, out_shape, args, scratch_shapes=(), riders=()):
    in_specs, out_specs, out_shape = list(in_specs), list(out_specs), list(out_shape)
    scratch, args = list(scratch_shapes), list(args)
    n_in, n_out, n_scr, n_r = len(in_specs), len(out_specs), len(scratch), len(riders)
    hbm = pl.BlockSpec(memory_space=pl.ANY)
    for arr, scatter in riders:
        in_specs.append(hbm)
        args.append(arr)
        out_specs.append(hbm)
        out_shape.append(_exchange_out_shape(arr, scatter))
        scratch += list(_EXCHANGE_SEMS)

    def wrapped(*refs):
        ins, r_in = refs[:n_in], refs[n_in:n_in + n_r]
        o0 = n_in + n_r
        outs, r_out = refs[o0:o0 + n_out], refs[o0 + n_out:o0 + n_out + n_r]
        s0 = o0 + n_out + n_r
        scr, sems = refs[s0:s0 + n_scr], refs[s0 + n_scr:]

        def copies(r, rows=None):
            return _exchange_copies(r_in[r], r_out[r], sems[3 * r], sems[3 * r + 1], sems[3 * r + 2], riders[r][1],
                                    rows)

        if n_r:
            steps = int(np.prod(grid))
            step = functools.reduce(lambda u, w: u * w[1] + w[0],
                                    [(pl.program_id(a), n) for a, n in enumerate(grid)], 0)
            for r, (arr, scatter) in enumerate(riders):
                block_rows = arr.shape[1] if scatter else arr.shape[0]
                n_pieces = _pieces(block_rows, steps)
                every, piece = steps // n_pieces, block_rows // n_pieces

                @pl.when((step % every == 0) & (step // every < n_pieces))
                def _(r=r, every=every, piece=piece, whole=n_pieces == 1):
                    rows = None if whole else pl.ds(pl.multiple_of((step // every) * piece, PIECE_ROWS), piece)
                    for cp in copies(r, rows):
                        cp.start()

        body(*ins, *outs, *scr)

        if n_r:
            @pl.when(step == steps - 1)
            def _():
                for r in range(n_r):
                    for cp in copies(r):
                        cp.wait()

    params = pltpu.CompilerParams(dimension_semantics=("arbitrary",) * len(grid), vmem_limit_bytes=VMEM_LIMIT,
                                  has_side_effects=bool(riders))
    return list(pl.pallas_call(wrapped, name=name, grid=grid, in_specs=in_specs, out_specs=out_specs,
                               out_shape=out_shape, scratch_shapes=scratch, compiler_params=params)(*args))


def _riders(arrays, scatter):
    return [(a, scatter) for a in arrays]


def _nones(arrays):
    return tuple(None for _ in arrays)


def _divisors(n, step, cap):
    out = [d for d in range(step, min(n, cap) + 1, step) if n % d == 0]
    if n <= cap and n not in out:
        out.append(n)
    return out


def _mm_tiles(M, N, K, sa, sb, m_step, k_step):
    best, best_cost = None, None
    for tm in _divisors(M, m_step, 2304):
        for tn in _divisors(N, LANE, 2304):
            for tk in _divisors(K, k_step, 2304):
                vm = 2 * (tm * tk * sa + tk * tn * sb) + 3 * tm * tn * 4
                if vm > MM_BUDGET:
                    continue
                traffic = M * K * sa * (N // tn) + K * N * sb * (M // tm) + M * N * 4
                steps = (M // tm) * (N // tn) * (K // tk)
                passes = 1 if tk == K else 3
                busy = steps * (2 * tm * tn * tk / MXU_FLOPS + passes * tm * tn * 4 / VMEM_BPS)
                cost = max(traffic / HBM_BPS, busy) + steps * STEP_S
                if m_step == LANE:
                    cost = traffic / HBM_BPS + steps * STEP_S
                if tm % 256 or tn % 256:
                    cost *= 1.05
                if best_cost is None or cost < best_cost:
                    best, best_cost = (tm, tn, tk), cost
    assert best is not None, (M, N, K)
    return best


def _matmul(a, b, mode, name, riders=(), out_dtype=F32):
    if mode == "nn":
        (M, K), (K2, N) = a.shape, b.shape
    elif mode == "nt":
        (M, K), (N, K2) = a.shape, b.shape
    else:
        (K, M), (K2, N) = a.shape, b.shape
    assert K == K2, (a.shape, b.shape, mode)
    tm, tn, tk = _mm_tiles(M, N, K, a.dtype.itemsize, b.dtype.itemsize,
                           LANE if mode == "tn" else 8, 8 if mode == "tn" else LANE)
    nk = K // tk
    dims = {"nn": (((1,), (0,)), ((), ())), "nt": (((1,), (1,)), ((), ())), "tn": (((0,), (0,)), ((), ()))}[mode]

    own_acc = nk > 1 and out_dtype != F32

    def body(a_ref, b_ref, o_ref, *scr):
        prod = lax.dot_general(a_ref[...].astype(_MXU), b_ref[...].astype(_MXU), dims, preferred_element_type=F32)
        if nk == 1:
            o_ref[...] = prod.astype(out_dtype)
        else:
            acc_ref = scr[0] if own_acc else o_ref
            k = pl.program_id(2)

            @pl.when(k == 0)
            def _():
                acc_ref[...] = prod

            @pl.when(k > 0)
            def _():
                acc_ref[...] += prod

            if own_acc:
                @pl.when(k == nk - 1)
                def _():
                    o_ref[...] = acc_ref[...].astype(out_dtype)

    if mode == "nn":
        a_spec = pl.BlockSpec((tm, tk), lambda i, j, k: (i, k))
        b_spec = pl.BlockSpec((tk, tn), lambda i, j, k: (k, j))
    elif mode == "nt":
        a_spec = pl.BlockSpec((tm, tk), lambda i, j, k: (i, k))
        b_spec = pl.BlockSpec((tn, tk), lambda i, j, k: (j, k))
    else:
        a_spec = pl.BlockSpec((tk, tm), lambda i, j, k: (k, i))
        b_spec = pl.BlockSpec((tk, tn), lambda i, j, k: (k, j))
    outs = _pcall(body, name=name, grid=(M // tm, N // tn, nk), in_specs=[a_spec, b_spec],
                  out_specs=[pl.BlockSpec((tm, tn), lambda i, j, k: (i, j))],
                  out_shape=[jax.ShapeDtypeStruct((M, N), out_dtype)], args=[a, b],
                  scratch_shapes=[pltpu.VMEM((tm, tn), F32)] if own_acc else [], riders=riders)
    return outs if riders else outs[0]


def _make_linear(tag, to_parts, out_dtype=F32):
    def run(a, w, gin, tok):
        if gin:
            y, *got = _matmul(a, w, "nn", tag + "_fwd", riders=_riders(gin, False), out_dtype=out_dtype)
        else:
            y, got = _matmul(a, w, "nn", tag + "_fwd", out_dtype=out_dtype), []
        return y, tuple(got), tok

    @jax.custom_vjp
    def linear(a, w, sink, gin, tok):
        return run(a, w, gin, tok)

    def fwd(a, w, sink, gin, tok):
        return run(a, w, gin, tok), (a, w, gin)

    def bwd(res, cts):
        a, w, gin = res
        dy, _, dtok = cts
        dw = _matmul(a, dy, "tn", tag + "_bwd_dw")
        if dtok:
            da, *recv = _matmul(dy, w, "nt", tag + "_bwd_da", riders=_riders(dtok, True), out_dtype=a.dtype)
        else:
            da, recv = _matmul(dy, w, "nt", tag + "_bwd_da", out_dtype=a.dtype), []
        return da, jnp.zeros_like(w), to_parts(dw), _nones(gin), tuple(recv)

    linear.defvjp(fwd, bwd)
    return linear


def _seg_map(n_x_tiles):
    return lambda i: (jnp.where(i >= n_x_tiles, 1, 0), 0, 0)


def _ln_stats(x):
    mu = jnp.mean(x, axis=-1, keepdims=True)
    xc = x - mu
    var = jnp.mean(xc * xc, axis=-1, keepdims=True)
    rstd = lax.rsqrt(var + EPS)
    return xc * rstd, rstd


def _make_modulate(tag, n_x):
    def call_fwd(x, shift, scale):
        T, D = x.shape
        nxt = n_x // ROW_TILE

        def body(x_ref, sh_ref, sc_ref, o_ref):
            xhat, _ = _ln_stats(x_ref[...])
            o_ref[...] = xhat * (1.0 + sc_ref[0]) + sh_ref[0]

        row = pl.BlockSpec((ROW_TILE, D), lambda i: (i, 0))
        seg = pl.BlockSpec((1, 1, D), _seg_map(nxt))
        return pl.pallas_call(body, name=tag + "_fwd", grid=(T // ROW_TILE,), in_specs=[row, seg, seg],
                              out_specs=row, out_shape=jax.ShapeDtypeStruct((T, D), F32),
                              compiler_params=_cparams(1))(x, shift, scale)

    def call_bwd(x, scale, dh):
        T, D = x.shape
        nxt = n_x // ROW_TILE
        nseg = scale.shape[0]

        def body(x_ref, sc_ref, dh_ref, dx_ref, dsh_ref, dsc_ref):
            i = pl.program_id(0)

            @pl.when((i == 0) | (i == nxt))
            def _():
                dsh_ref[...] = jnp.zeros_like(dsh_ref)
                dsc_ref[...] = jnp.zeros_like(dsc_ref)

            xhat, rstd = _ln_stats(x_ref[...])
            dh = dh_ref[...]
            dsh_ref[0] += jnp.sum(dh, axis=0, keepdims=True)
            dsc_ref[0] += jnp.sum(dh * xhat, axis=0, keepdims=True)
            dxh = dh * (1.0 + sc_ref[0])
            m1 = jnp.mean(dxh, axis=-1, keepdims=True)
            m2 = jnp.mean(dxh * xhat, axis=-1, keepdims=True)
            dx_ref[...] = rstd * (dxh - m1 - xhat * m2)

        row = pl.BlockSpec((ROW_TILE, D), lambda i: (i, 0))
        seg = pl.BlockSpec((1, 1, D), _seg_map(nxt))
        segshape = jax.ShapeDtypeStruct((nseg, 1, D), F32)
        return pl.pallas_call(body, name=tag + "_bwd", grid=(T // ROW_TILE,), in_specs=[row, seg, row],
                              out_specs=[row, seg, seg],
                              out_shape=[jax.ShapeDtypeStruct((T, D), F32), segshape, segshape],
                              compiler_params=_cparams(1))(x, scale, dh)

    @jax.custom_vjp
    def modulate(x, shift, scale):
        return call_fwd(x, shift, scale)

    def fwd(x, shift, scale):
        return call_fwd(x, shift, scale), (x, scale)

    def bwd(res, dh):
        x, scale = res
        return tuple(call_bwd(x, scale, dh))

    modulate.defvjp(fwd, bwd)
    return modulate


def _make_resid_ln(tag, n_x, alpha):
    def call_fwd(x, y, gate, g, b):
        T, D = x.shape
        nxt = n_x // ROW_TILE

        def body(x_ref, y_ref, gt_ref, g_ref, b_ref, o_ref):
            u = alpha * x_ref[...] + gt_ref[0] * y_ref[...]
            uhat, _ = _ln_stats(u)
            o_ref[...] = uhat * g_ref[...] + b_ref[...]

        row = pl.BlockSpec((ROW_TILE, D), lambda i: (i, 0))
        seg = pl.BlockSpec((1, 1, D), _seg_map(nxt))
        vec = pl.BlockSpec((1, D), lambda i: (0, 0))
        return pl.pallas_call(body, name=tag + "_fwd", grid=(T // ROW_TILE,), in_specs=[row, row, seg, vec, vec],
                              out_specs=row, out_shape=jax.ShapeDtypeStruct((T, D), F32),
                              compiler_params=_cparams(1))(x, y, gate, g, b)

    def call_bwd(x, y, gate, g, do):
        T, D = x.shape
        nxt = n_x // ROW_TILE
        nseg = gate.shape[0]

        def body(x_ref, y_ref, gt_ref, g_ref, do_ref, dx_ref, dy_ref, dgt_ref, dg_ref, db_ref):
            i = pl.program_id(0)

            @pl.when(i == 0)
            def _():
                dg_ref[...] = jnp.zeros_like(dg_ref)
                db_ref[...] = jnp.zeros_like(db_ref)

            @pl.when((i == 0) | (i == nxt))
            def _():
                dgt_ref[...] = jnp.zeros_like(dgt_ref)

            y = y_ref[...]
            gate_v = gt_ref[0]
            uhat, rstd = _ln_stats(alpha * x_ref[...] + gate_v * y)
            do = do_ref[...]
            dg_ref[...] += jnp.sum(do * uhat, axis=0, keepdims=True)
            db_ref[...] += jnp.sum(do, axis=0, keepdims=True)
            duh = do * g_ref[...]
            m1 = jnp.mean(duh, axis=-1, keepdims=True)
            m2 = jnp.mean(duh * uhat, axis=-1, keepdims=True)
            du = rstd * (duh - m1 - uhat * m2)
            dx_ref[...] = alpha * du
            dy_ref[...] = gate_v * du
            dgt_ref[0] += jnp.sum(du * y, axis=0, keepdims=True)

        row = pl.BlockSpec((ROW_TILE, D), lambda i: (i, 0))
        seg = pl.BlockSpec((1, 1, D), _seg_map(nxt))
        vec = pl.BlockSpec((1, D), lambda i: (0, 0))
        rs = jax.ShapeDtypeStruct((T, D), F32)
        vs = jax.ShapeDtypeStruct((1, D), F32)
        return pl.pallas_call(body, name=tag + "_bwd", grid=(T // ROW_TILE,), in_specs=[row, row, seg, vec, row],
                              out_specs=[row, row, seg, vec, vec],
                              out_shape=[rs, rs, jax.ShapeDtypeStruct((nseg, 1, D), F32), vs, vs],
                              compiler_params=_cparams(1))(x, y, gate, g, do)

    @jax.custom_vjp
    def resid_ln(x, y, gate, g, b):
        return call_fwd(x, y, gate, g, b)

    def fwd(x, y, gate, g, b):
        return call_fwd(x, y, gate, g, b), (x, y, gate, g)

    def bwd(res, do):
        x, y, gate, g = res
        return tuple(call_bwd(x, y, gate, g, do))

    resid_ln.defvjp(fwd, bwd)
    return resid_ln


def _make_rmsnorm(tag):
    def call_fwd(x, g):
        T, W = x.shape
        gw = g.shape[1]

        def body(x_ref, g_ref, o_ref):
            x = x_ref[...]
            r = lax.rsqrt(jnp.mean(x * x, axis=-1, keepdims=True) + EPS)
            o_ref[...] = x * r * g_ref[...]

        blk = pl.BlockSpec((ROW_TILE, gw), lambda i, c: (i, c))
        vec = pl.BlockSpec((1, gw), lambda i, c: (0, 0))
        return pl.pallas_call(body, name=tag + "_fwd", grid=(T // ROW_TILE, W // gw), in_specs=[blk, vec],
                              out_specs=blk, out_shape=jax.ShapeDtypeStruct((T, W), F32),
                              compiler_params=_cparams(2))(x, g)

    def call_bwd(x, g, dy):
        T, W = x.shape
        gw = g.shape[1]

        def body(x_ref, g_ref, dy_ref, dx_ref, dg_ref):
            @pl.when((pl.program_id(0) == 0) & (pl.program_id(1) == 0))
            def _():
                dg_ref[...] = jnp.zeros_like(dg_ref)

            x = x_ref[...]
            r = lax.rsqrt(jnp.mean(x * x, axis=-1, keepdims=True) + EPS)
            xn = x * r
            dy = dy_ref[...]
            dg_ref[...] += jnp.sum(dy * xn, axis=0, keepdims=True)
            dxn = dy * g_ref[...]
            dx_ref[...] = r * (dxn - xn * jnp.mean(dxn * xn, axis=-1, keepdims=True))

        blk = pl.BlockSpec((ROW_TILE, gw), lambda i, c: (i, c))
        vec = pl.BlockSpec((1, gw), lambda i, c: (0, 0))
        return pl.pallas_call(body, name=tag + "_bwd", grid=(T // ROW_TILE, W // gw), in_specs=[blk, vec, blk],
                              out_specs=[blk, vec],
                              out_shape=[jax.ShapeDtypeStruct((T, W), F32), jax.ShapeDtypeStruct((1, gw), F32)],
                              compiler_params=_cparams(2))(x, g, dy)

    @jax.custom_vjp
    def rmsnorm(x, g):
        return call_fwd(x, g)

    def fwd(x, g):
        return call_fwd(x, g), (x, g)

    def bwd(res, dy):
        x, g = res
        return tuple(call_bwd(x, g, dy))

    rmsnorm.defvjp(fwd, bwd)
    return rmsnorm


def _rope_tables(n_x, n_all, width, half):
    t = np.arange(n_all)
    row, col = t // GRID_W, t % GRID_W
    lane = np.arange(LANE)
    inside = lane < width
    axis_is_col = (lane // (2 * half)) % 2 == 1
    f = (lane % (2 * half)) % half
    inv_freq = ROPE_THETA ** (-(f.astype(np.float64)) / half)
    pos = np.where(axis_is_col[None, :], col[:, None], row[:, None]).astype(np.float64)
    ang = (pos.astype(np.float32) * inv_freq.astype(np.float32)[None, :]).astype(np.float32)
    live = inside[None, :] & (t < n_x)[:, None]
    cos = np.where(live, np.cos(ang), 1.0).astype(np.float32)
    sin = np.where(live, np.sin(ang), 0.0).astype(np.float32)
    first = (lane % (2 * half)) < half
    s_lo = np.where(first[None, :], -sin, 0.0).astype(np.float32)
    s_hi = np.where(first[None, :], 0.0, sin).astype(np.float32)
    return jnp.asarray(cos), jnp.asarray(s_lo), jnp.asarray(s_hi)


def _make_rope(tag, half):
    def call(x, cos, s_lo, s_hi, transpose, name):
        T, W = x.shape

        def body(x_ref, c_ref, lo_ref, hi_ref, o_ref):
            x = x_ref[...]
            if not transpose:
                o_ref[...] = (x * c_ref[...] + pltpu.roll(x, LANE - half, 1) * lo_ref[...]
                              + pltpu.roll(x, half, 1) * hi_ref[...])
            else:
                o_ref[...] = (x * c_ref[...] + pltpu.roll(x * lo_ref[...], half, 1)
                              + pltpu.roll(x * hi_ref[...], LANE - half, 1))

        blk = pl.BlockSpec((ROW_TILE, LANE), lambda i, c: (i, c))
        tab = pl.BlockSpec((ROW_TILE, LANE), lambda i, c: (i, 0))
        return pl.pallas_call(body, name=name, grid=(T // ROW_TILE, W // LANE), in_specs=[blk, tab, tab, tab],
                              out_specs=blk, out_shape=jax.ShapeDtypeStruct((T, W), F32),
                              compiler_params=_cparams(2))(x, cos, s_lo, s_hi)

    @jax.custom_vjp
    def rope(x, cos, s_lo, s_hi):
        return call(x, cos, s_lo, s_hi, False, tag + "_fwd")

    def fwd(x, cos, s_lo, s_hi):
        return call(x, cos, s_lo, s_hi, False, tag + "_fwd"), (cos, s_lo, s_hi)

    def bwd(res, dy):
        cos, s_lo, s_hi = res
        return call(dy, cos, s_lo, s_hi, True, tag + "_bwd"), None, None, None

    rope.defvjp(fwd, bwd)
    return rope


def _make_conv_gate(tag, n_x):
    def shifted(v, T):
        t = lax.broadcasted_iota(jnp.int32, v.shape, 0)
        prev = jnp.where((t == 0) | (t == n_x), 0.0, pltpu.roll(v, 1, 0))
        nxt = jnp.where((t == n_x - 1) | (t == T - 1), 0.0, pltpu.roll(v, T - 1, 0))
        return prev, nxt

    def call_fwd(a, u, cw, cb, riders):
        T, Fd = a.shape

        def body(a_ref, u_ref, cw_ref, cb_ref, z_ref):
            pre = a_ref[...].astype(F32)
            prev, nxt = shifted(pre, T)
            s = prev * cw_ref[0:1, :] + pre * cw_ref[1:2, :] + nxt * cw_ref[2:3, :] + cb_ref[...]
            z_ref[...] = (s * (1.0 / (1.0 + jnp.exp(-s))) * u_ref[...].astype(F32)).astype(z_ref.dtype)

        col = pl.BlockSpec((T, LANE), lambda j: (0, j))
        return _pcall(body, name=tag + "_fwd", grid=(Fd // LANE,),
                      in_specs=[col, col, pl.BlockSpec((3, LANE), lambda j: (0, j)),
                                pl.BlockSpec((1, LANE), lambda j: (0, j))],
                      out_specs=[col], out_shape=[jax.ShapeDtypeStruct((T, Fd), a.dtype)], args=[a, u, cw, cb],
                      riders=riders)

    def call_bwd(a, u, cw, cb, dz, riders):
        T, Fd = a.shape

        def body(a_ref, u_ref, cw_ref, cb_ref, dz_ref, da_ref, du_ref, dcw_ref, dcb_ref):
            pre = a_ref[...].astype(F32)
            prev, nxt = shifted(pre, T)
            s = prev * cw_ref[0:1, :] + pre * cw_ref[1:2, :] + nxt * cw_ref[2:3, :] + cb_ref[...]
            sig = 1.0 / (1.0 + jnp.exp(-s))
            dz = dz_ref[...].astype(F32)
            du_ref[...] = (dz * s * sig).astype(du_ref.dtype)
            ds = dz * u_ref[...].astype(F32) * (sig * (1.0 + s * (1.0 - sig)))
            ds_prev, ds_next = shifted(ds, T)
            da_ref[...] = (ds * cw_ref[1:2, :] + ds_next * cw_ref[0:1, :] + ds_prev * cw_ref[2:3, :]).astype(
                da_ref.dtype)
            dcw_ref[0:1, :] = jnp.sum(ds * prev, axis=0, keepdims=True)
            dcw_ref[1:2, :] = jnp.sum(ds * pre, axis=0, keepdims=True)
            dcw_ref[2:3, :] = jnp.sum(ds * nxt, axis=0, keepdims=True)
            dcb_ref[...] = jnp.sum(ds, axis=0, keepdims=True)

        col = pl.BlockSpec((T, LANE), lambda j: (0, j))
        w3 = pl.BlockSpec((3, LANE), lambda j: (0, j))
        w1 = pl.BlockSpec((1, LANE), lambda j: (0, j))
        big = jax.ShapeDtypeStruct((T, Fd), a.dtype)
        return _pcall(body, name=tag + "_bwd", grid=(Fd // LANE,), in_specs=[col, col, w3, w1, col],
                      out_specs=[col, col, w3, w1],
                      out_shape=[big, big, jax.ShapeDtypeStruct((3, Fd), F32), jax.ShapeDtypeStruct((1, Fd), F32)],
                      args=[a, u, cw, cb, dz], riders=riders)

    def run(a, u, cw, cb, gin, tok):
        z, *got = call_fwd(a, u, cw, cb, _riders(gin, False))
        return z, tuple(got), tok

    @jax.custom_vjp
    def conv_gate(a, u, cw, cb, gin, tok):
        return run(a, u, cw, cb, gin, tok)

    def fwd(a, u, cw, cb, gin, tok):
        return run(a, u, cw, cb, gin, tok), (a, u, cw, cb, gin)

    def bwd(res, cts):
        a, u, cw, cb, gin = res
        dz, _, dtok = cts
        da, du, dcw, dcb, *recv = call_bwd(a, u, cw, cb, dz, _riders(dtok, True))
        return da, du, dcw, dcb, _nones(gin), tuple(recv)

    conv_gate.defvjp(fwd, bwd)
    return conv_gate


def _loss_and_grad(y, tgt, name):
    T, D = y.shape

    def body(y_ref, t_ref, l_ref, r_ref):
        @pl.when(pl.program_id(0) == 0)
        def _():
            l_ref[...] = jnp.zeros_like(l_ref)

        d = y_ref[...] - t_ref[...]
        r_ref[...] = d * (1.0 / D)
        l_ref[...] += jnp.sum(d * d) * (0.5 / D)

    row = pl.BlockSpec((ROW_TILE, D), lambda i: (i, 0))
    l, r = pl.pallas_call(body, name=name, grid=(T // ROW_TILE,), in_specs=[row, row],
                          out_specs=[pl.BlockSpec((8, LANE), lambda i: (0, 0)), row],
                          out_shape=[jax.ShapeDtypeStruct((8, LANE), F32), jax.ShapeDtypeStruct((T, D), F32)],
                          compiler_params=_cparams(1))(y, tgt)
    return l[0, 0], r


class _AttnCfg:
    def __init__(self, H, Hkv, dqk, dv, scale, tq, tk, has_sink=False):
        self.H, self.Hkv, self.dqk, self.dv, self.scale = H, Hkv, dqk, dv, scale
        self.tq, self.tk, self.has_sink = tq, tk, has_sink
        self.G = H // Hkv


class _WinCfg:
    def __init__(self, mode, H, Hkv, scale, nx, n_ctx):
        self.mode, self.H, self.Hkv, self.scale, self.nx, self.n_ctx = mode, H, Hkv, scale, nx, n_ctx
        self.G = H // Hkv
        self.has_sink = mode == "swa"
        self.nwin, self.back = (3, 1) if mode == "swa" else (5, 2)
        self.width = self.nwin * ATT_BLOCK

    def first_block(self, i):
        return jnp.clip(i - self.back, 0, self.nx - self.nwin)

    def slots(self, i, jj):
        d = 2 * (self.first_block(i) + jj - i)
        return jnp.clip(d + 8, 0, 15), jnp.clip(d + 7, 0, 15)

    def local_scores(self, s, i, tab_ref):
        sub = lax.broadcasted_iota(jnp.int32, s.shape, 0)
        lan = lax.broadcasted_iota(jnp.int32, s.shape, 1)
        first = self.first_block(i)
        if self.mode == "swa":
            diff = (i * ATT_BLOCK + sub) - (first * ATT_BLOCK + lan)
            return jnp.where(jnp.abs(diff) <= SWA_WINDOW, s, -jnp.inf)
        rows = 2 * self.nx
        qrow = 2 * i + jnp.where(sub >= GRID_W, 1, 0)
        krow = 2 * first + lan // GRID_W
        r0 = jnp.clip(qrow - NA_WIN_R // 2, 0, rows - NA_WIN_R)
        ok = (krow >= r0) & (krow < r0 + NA_WIN_R)
        tiles = []
        for jj in range(self.nwin):
            top, bot = self.slots(i, jj)
            tiles.append(jnp.concatenate([tab_ref[0, top], tab_ref[0, bot]], axis=0))
        return jnp.where(ok, s + jnp.concatenate(tiles, axis=1), -jnp.inf)


def _win_specs(cfg, Tk):
    G = cfg.G
    qmap = lambda kh, g, i: (i, kh * G + g)
    q_spec = pl.BlockSpec((ATT_BLOCK, HEAD_DIM), qmap)
    kv_spec = pl.BlockSpec((Tk, HEAD_DIM), lambda kh, g, i: (0, kh))
    lse_spec = pl.BlockSpec((1, ATT_BLOCK, LANE), lambda kh, g, i: (kh * G + g, i, 0))
    tab_spec = pl.BlockSpec((1, 16, GRID_W, LANE), lambda kh, g, i: (kh * G + g, 0, 0, 0))
    sink_spec = pl.BlockSpec((1, 8, LANE), lambda kh, g, i: (kh * G + g, 0, 0))
    return q_spec, kv_spec, lse_spec, tab_spec, sink_spec


def _win_scores(cfg, i, q_ref, k_ref, tab_ref):
    dn = (((1,), (1,)), ((), ()))
    n_x = cfg.nx * ATT_BLOCK
    win = pl.ds(pl.multiple_of(cfg.first_block(i) * ATT_BLOCK, ATT_BLOCK), cfg.width)
    ctx = pl.ds(n_x, cfg.n_ctx)
    qb = q_ref[...].astype(_MXU)
    s_loc = lax.dot_general(qb, k_ref[win, :].astype(_MXU), dn, preferred_element_type=F32)
    s_ctx = lax.dot_general(qb, k_ref[ctx, :].astype(_MXU), dn, preferred_element_type=F32)
    return cfg.local_scores(s_loc, i, tab_ref), s_ctx, win, ctx


def _win_fwd_call(cfg, name, q, k, v, tab, sink, riders=()):
    Tq, Tk = q.shape[0], k.shape[0]

    def body(*refs):
        q_ref, k_ref, v_ref = refs[:3]
        extra = refs[3] if (cfg.mode == "na" or cfg.has_sink) else None
        o_ref, lse_ref = refs[-2:]
        i = pl.program_id(2)
        s_loc, s_ctx, win, ctx = _win_scores(cfg, i, q_ref, k_ref, extra if cfg.mode == "na" else None)
        m = jnp.maximum(jnp.max(s_loc, axis=-1, keepdims=True), jnp.max(s_ctx, axis=-1, keepdims=True))
        if cfg.has_sink:
            m = jnp.maximum(m, extra[0, 0:1, 0:1])
        p_loc, p_ctx = jnp.exp2(s_loc - m), jnp.exp2(s_ctx - m)
        l = jnp.sum(p_loc, axis=-1, keepdims=True) + jnp.sum(p_ctx, axis=-1, keepdims=True)
        if cfg.has_sink:
            l = l + jnp.exp2(extra[0, 0:1, 0:1] - m)
        acc = (jnp.dot(p_loc.astype(_MXU), v_ref[win, :].astype(_MXU), preferred_element_type=F32)
               + jnp.dot(p_ctx.astype(_MXU), v_ref[ctx, :].astype(_MXU), preferred_element_type=F32))
        o_ref[...] = acc / l
        lse_ref[0] = jnp.broadcast_to(m + jnp.log2(l), (ATT_BLOCK, LANE))

    q_spec, kv_spec, lse_spec, tab_spec, sink_spec = _win_specs(cfg, Tk)
    in_specs, args = [q_spec, kv_spec, kv_spec], [q, k, v]
    if cfg.mode == "na":
        in_specs.append(tab_spec)
        args.append(tab)
    if cfg.has_sink:
        in_specs.append(sink_spec)
        args.append(sink)
    return _pcall(
        body, name=name, grid=(cfg.Hkv, cfg.G, Tq // ATT_BLOCK), in_specs=in_specs, out_specs=[q_spec, lse_spec],
        out_shape=[jax.ShapeDtypeStruct((Tq, cfg.H * HEAD_DIM), F32), jax.ShapeDtypeStruct((cfg.H, Tq, LANE), F32)],
        args=args, riders=riders)


def _win_bwd_call(cfg, name, q, k, v, tab, sink, o, lse, do, riders=()):
    Tq, Tk = q.shape[0], k.shape[0]
    nq = Tq // ATT_BLOCK
    dn_nt = (((1,), (1,)), ((), ()))
    dn_tn = (((0,), (0,)), ((), ()))

    def body(*refs):
        q_ref, k_ref, v_ref = refs[:3]
        has_extra = cfg.mode == "na" or cfg.has_sink
        extra = refs[3] if has_extra else None
        n = 4 if has_extra else 3
        o_ref, lse_ref, do_ref, dq_ref, dk_ref, dv_ref = refs[n:n + 6]
        dextra = refs[n + 6] if has_extra else None
        g, i = pl.program_id(1), pl.program_id(2)

        @pl.when((g == 0) & (i == 0))
        def _():
            dk_ref[...] = jnp.zeros_like(dk_ref)
            dv_ref[...] = jnp.zeros_like(dv_ref)

        if has_extra:
            @pl.when(i == 0)
            def _():
                dextra[...] = jnp.zeros_like(dextra)

        s_loc, s_ctx, win, ctx = _win_scores(cfg, i, q_ref, k_ref, extra if cfg.mode == "na" else None)
        lse = lse_ref[0][:, 0:1]
        p_loc, p_ctx = jnp.exp2(s_loc - lse), jnp.exp2(s_ctx - lse)
        do_f = do_ref[...]
        do_b = do_f.astype(_MXU)
        delta = jnp.sum(do_f * o_ref[...], axis=-1, keepdims=True)
        dp_loc = lax.dot_general(do_b, v_ref[win, :].astype(_MXU), dn_nt, preferred_element_type=F32)
        dp_ctx = lax.dot_general(do_b, v_ref[ctx, :].astype(_MXU), dn_nt, preferred_element_type=F32)
        ds_loc, ds_ctx = p_loc * (dp_loc - delta), p_ctx * (dp_ctx - delta)
        if cfg.mode == "na":
            for jj in range(cfg.nwin):
                top, bot = cfg.slots(i, jj)
                tile = ds_loc[:, jj * ATT_BLOCK:(jj + 1) * ATT_BLOCK]
                dextra[0, top] += tile[0:GRID_W, :]
                dextra[0, bot] += tile[GRID_W:, :]
        if cfg.has_sink:
            p_sink = jnp.exp2(extra[0, 0:1, 0:1] - lse)
            lane0 = ((lax.broadcasted_iota(jnp.int32, (8, LANE), 0) == 0)
                     & (lax.broadcasted_iota(jnp.int32, (8, LANE), 1) == 0))
            dextra[0] += jnp.where(lane0, -jnp.sum(p_sink * delta), 0.0)
        dsb_loc, dsb_ctx = ds_loc.astype(_MXU), ds_ctx.astype(_MXU)
        qb = q_ref[...].astype(_MXU)
        dq_ref[...] = cfg.scale * (jnp.dot(dsb_loc, k_ref[win, :].astype(_MXU), preferred_element_type=F32)
                                   + jnp.dot(dsb_ctx, k_ref[ctx, :].astype(_MXU), preferred_element_type=F32))
        dk_ref[win, :] += lax.dot_general(dsb_loc, qb, dn_tn, preferred_element_type=F32)
        dk_ref[ctx, :] += lax.dot_general(dsb_ctx, qb, dn_tn, preferred_element_type=F32)
        dv_ref[win, :] += lax.dot_general(p_loc.astype(_MXU), do_b, dn_tn, preferred_element_type=F32)
        dv_ref[ctx, :] += lax.dot_general(p_ctx.astype(_MXU), do_b, dn_tn, preferred_element_type=F32)

        @pl.when((g == cfg.G - 1) & (i == nq - 1))
        def _():
            dk_ref[...] = dk_ref[...] * (1.0 / LOG2E)

    q_spec, kv_spec, lse_spec, tab_spec, sink_spec = _win_specs(cfg, Tk)
    in_specs, args = [q_spec, kv_spec, kv_spec], [q, k, v]
    out_specs = [q_spec, kv_spec, kv_spec]
    out_shape = [jax.ShapeDtypeStruct(q.shape, F32), jax.ShapeDtypeStruct(k.shape, F32),
                 jax.ShapeDtypeStruct(v.shape, F32)]
    if cfg.mode == "na":
        in_specs.append(tab_spec)
        args.append(tab)
        out_specs.append(tab_spec)
        out_shape.append(jax.ShapeDtypeStruct(tab.shape, F32))
    if cfg.has_sink:
        in_specs.append(sink_spec)
        args.append(sink)
        out_specs.append(sink_spec)
        out_shape.append(jax.ShapeDtypeStruct(sink.shape, F32))
    in_specs += [q_spec, lse_spec, q_spec]
    args += [o, lse, do]
    return _pcall(body, name=name, grid=(cfg.Hkv, cfg.G, nq), in_specs=in_specs, out_specs=out_specs,
                  out_shape=out_shape, args=args, riders=riders)


def _attn_specs(cfg, Tk):
    G, tq = cfg.G, cfg.tq
    qmap = lambda kh, g, i: (i, kh * G + g)
    q_spec = pl.BlockSpec((tq, cfg.dqk), qmap)
    k_spec = pl.BlockSpec((Tk, cfg.dqk), lambda kh, g, i: (0, kh))
    v_spec = pl.BlockSpec((Tk, cfg.dv), lambda kh, g, i: (0, kh))
    o_spec = pl.BlockSpec((tq, cfg.dv), qmap)
    lse_spec = pl.BlockSpec((1, tq, LANE), lambda kh, g, i: (kh * G + g, i, 0))
    sink_spec = pl.BlockSpec((1, 8, LANE), lambda kh, g, i: (kh * G + g, 0, 0))
    return q_spec, k_spec, v_spec, o_spec, lse_spec, sink_spec


def _attn_fwd_call(cfg, name, q, k, v, sink, riders=()):
    Tq, Tk = q.shape[0], k.shape[0]
    nq, nkv = Tq // cfg.tq, Tk // cfg.tk
    dn = (((1,), (1,)), ((), ()))

    def body(*refs):
        q_ref, k_ref, v_ref = refs[:3]
        sink_ref = refs[3] if cfg.has_sink else None
        o_ref, lse_ref = refs[-2:]
        qb = q_ref[...].astype(_MXU)
        m = jnp.full((cfg.tq, 1), NEG, F32)
        l = jnp.zeros((cfg.tq, 1), F32)
        acc = jnp.zeros((cfg.tq, cfg.dv), F32)
        for j in range(nkv):
            rows = pl.ds(j * cfg.tk, cfg.tk)
            s = lax.dot_general(qb, k_ref[rows, :].astype(_MXU), dn, preferred_element_type=F32)
            m_new = jnp.maximum(m, jnp.max(s, axis=-1, keepdims=True))
            alpha = jnp.exp2(m - m_new)
            p = jnp.exp2(s - m_new)
            l = alpha * l + jnp.sum(p, axis=-1, keepdims=True)
            acc = alpha * acc + jnp.dot(p.astype(_MXU), v_ref[rows, :].astype(_MXU), preferred_element_type=F32)
            m = m_new
        if cfg.has_sink:
            sk = sink_ref[0, 0:1, 0:1]
            m_new = jnp.maximum(m, sk)
            alpha = jnp.exp2(m - m_new)
            l = alpha * l + jnp.exp2(sk - m_new)
            acc = acc * alpha
            m = m_new
        o_ref[...] = acc / l
        lse_ref[0] = jnp.broadcast_to(m + jnp.log2(l), (cfg.tq, LANE))

    q_spec, k_spec, v_spec, o_spec, lse_spec, sink_spec = _attn_specs(cfg, Tk)
    in_specs, args = [q_spec, k_spec, v_spec], [q, k, v]
    if cfg.has_sink:
        in_specs.append(sink_spec)
        args.append(sink)
    return _pcall(
        body, name=name, grid=(cfg.Hkv, cfg.G, nq), in_specs=in_specs, out_specs=[o_spec, lse_spec],
        out_shape=[jax.ShapeDtypeStruct((Tq, cfg.H * cfg.dv), F32), jax.ShapeDtypeStruct((cfg.H, Tq, LANE), F32)],
        args=args, riders=riders)


def _attn_bwd_call(cfg, name, q, k, v, sink, o, lse, do, riders=()):
    Tq, Tk = q.shape[0], k.shape[0]
    nq, nkv = Tq // cfg.tq, Tk // cfg.tk
    tq, tk = cfg.tq, cfg.tk
    dn_nt = (((1,), (1,)), ((), ()))
    dn_tn = (((0,), (0,)), ((), ()))

    def body(*refs):
        q_ref, k_ref, v_ref = refs[:3]
        n = 3
        sink_ref = dsink_ref = None
        if cfg.has_sink:
            sink_ref, n = refs[n], n + 1
        o_ref, lse_ref, do_ref, dq_ref, dk_ref, dv_ref = refs[n:n + 6]
        n += 6
        if cfg.has_sink:
            dsink_ref, n = refs[n], n + 1
        g, i = pl.program_id(1), pl.program_id(2)

        @pl.when((g == 0) & (i == 0))
        def _():
            dk_ref[...] = jnp.zeros_like(dk_ref)
            dv_ref[...] = jnp.zeros_like(dv_ref)

        if cfg.has_sink:
            @pl.when(i == 0)
            def _():
                dsink_ref[...] = jnp.zeros_like(dsink_ref)

        do_f = do_ref[...]
        do_b = do_f.astype(_MXU)
        qb = q_ref[...].astype(_MXU)
        lse = lse_ref[0][:, 0:1]
        delta = jnp.sum(do_f * o_ref[...], axis=-1, keepdims=True)
        if cfg.has_sink:
            p_sink = jnp.exp2(sink_ref[0, 0:1, 0:1] - lse)
            lane0 = ((lax.broadcasted_iota(jnp.int32, (8, LANE), 0) == 0)
                     & (lax.broadcasted_iota(jnp.int32, (8, LANE), 1) == 0))
            dsink_ref[0] += jnp.where(lane0, -jnp.sum(p_sink * delta), 0.0)
        dq = jnp.zeros((tq, cfg.dqk), F32)
        for j in range(nkv):
            rows = pl.ds(j * tk, tk)
            kb, vb = k_ref[rows, :].astype(_MXU), v_ref[rows, :].astype(_MXU)
            s = lax.dot_general(qb, kb, dn_nt, preferred_element_type=F32)
            p = jnp.exp2(s - lse)
            dp = lax.dot_general(do_b, vb, dn_nt, preferred_element_type=F32)
            dsb = (p * (dp - delta)).astype(_MXU)
            dq = dq + jnp.dot(dsb, kb, preferred_element_type=F32)
            dk_ref[rows, :] += lax.dot_general(dsb, qb, dn_tn, preferred_element_type=F32)
            dv_ref[rows, :] += lax.dot_general(p.astype(_MXU), do_b, dn_tn, preferred_element_type=F32)
        dq_ref[...] = dq * cfg.scale

        @pl.when((g == cfg.G - 1) & (i == nq - 1))
        def _():
            dk_ref[...] = dk_ref[...] * (1.0 / LOG2E)

    q_spec, k_spec, v_spec, o_spec, lse_spec, sink_spec = _attn_specs(cfg, Tk)
    in_specs, args = [q_spec, k_spec, v_spec], [q, k, v]
    if cfg.has_sink:
        in_specs.append(sink_spec)
        args.append(sink)
    in_specs += [o_spec, lse_spec, o_spec]
    args += [o, lse, do]
    out_specs = [q_spec, k_spec, v_spec]
    out_shape = [jax.ShapeDtypeStruct(q.shape, F32), jax.ShapeDtypeStruct(k.shape, F32),
                 jax.ShapeDtypeStruct(v.shape, F32)]
    if cfg.has_sink:
        out_specs.append(sink_spec)
        out_shape.append(jax.ShapeDtypeStruct(sink.shape, F32))
    return _pcall(body, name=name, grid=(cfg.Hkv, cfg.G, nq), in_specs=in_specs, out_specs=out_specs,
                  out_shape=out_shape, args=args, riders=riders)


def _make_attention(tag, cfg):
    windowed = isinstance(cfg, _WinCfg)

    def run_fwd(q, k, v, tab, sink, gin):
        if windowed:
            return _win_fwd_call(cfg, tag + "_fwd", q, k, v, tab, sink, _riders(gin, False))
        return _attn_fwd_call(cfg, tag + "_fwd", q, k, v, sink, _riders(gin, False))

    def prepared(q, k, v, tab, sink):
        q2 = (q * (cfg.scale * LOG2E)).astype(_MXU)
        tab2 = None if tab is None else tab * LOG2E
        sink2 = None if sink is None else sink * LOG2E
        return q2, k.astype(_MXU), v.astype(_MXU), tab2, sink2

    @jax.custom_vjp
    def attn(q, k, v, tab, sink, gin, tok):
        o, _, *got = run_fwd(*prepared(q, k, v, tab, sink), gin)
        return o, tuple(got), tok

    def fwd(q, k, v, tab, sink, gin, tok):
        q2, kb, vb, tab2, sink2 = prepared(q, k, v, tab, sink)
        o, lse, *got = run_fwd(q2, kb, vb, tab2, sink2, gin)
        return (o, tuple(got), tok), (q2, kb, vb, tab2, sink2, gin, o, lse)

    def bwd(res, cts):
        q, k, v, tab, sink, gin, o, lse = res
        do, _, dtok = cts
        if windowed:
            outs = _win_bwd_call(cfg, tag + "_bwd", q, k, v, tab, sink, o, lse, do, _riders(dtok, True))
        else:
            outs = _attn_bwd_call(cfg, tag + "_bwd", q, k, v, sink, o, lse, do, _riders(dtok, True))
        dq, dk, dv = outs[:3]
        rest = outs[3:]
        dtab = rest.pop(0) if tab is not None else None
        dsink = rest.pop(0) if sink is not None else None
        return dq, dk, dv, dtab, dsink, _nones(gin), tuple(rest)

    attn.defvjp(fwd, bwd)
    return attn


def _pick_block(n, prefs):
    for p in prefs:
        if n % p == 0:
            return p
    return n


def _na_table(rpb):
    cq = np.arange(GRID_W)
    dcol = np.clip(cq[None, :] - cq[:, None] + NA_WIN_C - 1, 0, 2 * NA_WIN_C - 2)
    onehot = (dcol[:, :, None] == np.arange(2 * NA_WIN_C - 1)[None, None, :]).astype(np.float32)
    c0 = np.clip(cq - NA_WIN_C // 2, 0, GRID_W - NA_WIN_C)
    col_in = (cq[None, :] >= c0[:, None]) & (cq[None, :] < c0[:, None] + NA_WIN_C)
    tz = jnp.einsum("hrj,qkj->hrqk", rpb, jnp.asarray(onehot), precision=lax.Precision.HIGHEST)
    tz = jnp.where(jnp.asarray(col_in)[None, None], tz, NEG)
    zero = jnp.zeros_like(tz[:, :1])
    tzp = jnp.concatenate([zero, tz, zero], axis=1)
    return jnp.concatenate([tzp[:, 0:16], tzp[:, 1:17]], axis=-1)


def _sink_block(sink):
    return jnp.broadcast_to(sink[:, None, None], (sink.shape[0], 8, LANE))


def _make_split(bounds, axis):
    @jax.custom_vjp
    def split(t):
        return tuple(lax.slice_in_dim(t, a, b, axis=axis) for a, b in bounds)

    def fwd(t):
        return split(t), None

    def bwd(_, cts):
        return (jnp.concatenate(cts, axis=axis),)

    split.defvjp(fwd, bwd)
    return split


_IN_GROUPS = ((0, 512), (512, 1024), (1024, 1536), (1536, 2048), (2048, 2304), (2304, 2560), (2560, 2944),
              (2944, 3072), (3072, 3200), (3200, 3712), (3712, 3968), (3968, 4224))


def _seg2(vx, vc, nseg):
    rows = [vx, vc][:nseg]
    return jnp.stack(rows)[:, None, :]


_BIG_COL = ("w_in", "mla_w_uq", "mla_w_ukv", "ffn_w_gate", "ffn_w_up")
_BIG_ROW = ("w_out", "ffn_w_down")
_IN_GROUP = ("w_in", "mla_w_uq", "mla_w_ukv")
_PIECES = {"w_in": 4, "ffn_w_down": 2}


def _n_pieces(name):
    return _PIECES.get(name, 1)


def _shard_view(name, w):
    return jnp.swapaxes(w, 1, 2) if name in _BIG_COL else w


def _cut(name, a):
    return tuple(jnp.split(a, _n_pieces(name), axis=-1))


def _full_weight(name, pieces):
    g = jnp.concatenate(pieces, axis=-1) if len(pieces) > 1 else pieces[0]
    full = g.reshape(N_DEV * g.shape[1], g.shape[2])
    if name in _BIG_ROW:
        return full
    full = full.T
    if name == "w_in":
        K = full.shape[0]
        full = jnp.concatenate([full[:, :KPE_END], jnp.zeros((K, 64), full.dtype), full[:, KPE_END:]], axis=-1)
    return full


def _grad_parts(name):
    def to_parts(dw):
        if name == "w_in":
            dw = jnp.concatenate([dw[:, :KPE_END], dw[:, KPE_END + 64:]], axis=-1)
        if name in _BIG_COL:
            dw = dw.T
        return _cut(name, dw.reshape(N_DEV, dw.shape[0] // N_DEV, dw.shape[1]).astype(BF16))

    return to_parts


_RIDE_FWD = {
    "attn_a": ((0, "w_out", 0),), "attn_b": ((1, "w_in", 0), (1, "w_in", 1)), "attn_c": ((0, "ffn_w_gate", 0),),
    "attn_d": ((0, "ffn_w_up", 0),), "gate": ((0, "ffn_w_down", 0),), "up": ((0, "ffn_w_down", 1),),
    "conv": ((1, "mla_w_uq", 0), (1, "mla_w_ukv", 0)), "down": ((1, "w_in", 2), (1, "w_in", 3)),
}
_RIDE_BWD = {
    "attn_a": ((0, "w_out", 0),), "attn_b": ((1, "mla_w_uq", 0), (1, "mla_w_ukv", 0)),
    "attn_c": ((0, "ffn_w_gate", 0),), "attn_d": ((0, "ffn_w_up", 0),), "gate": ((0, "ffn_w_down", 0),),
    "up": ((0, "ffn_w_down", 1),), "conv": ((1, "w_in", 0), (1, "w_in", 1)), "down": ((1, "w_in", 2), (1, "w_in", 3)),
}


def _local_forward(x, ctx, modx, modc, p, full, shards, toks):
    S, D = x.shape
    C = ctx.shape[0]
    depth = len(full)
    ride = shards is not None
    have = [dict(f) for f in full]
    landing = {}
    sinks = {(l, n): list(toks[l][n]) for l in range(depth) for n in toks[l]}
    alpha = (2 * depth) ** 0.25
    T0 = S + C
    nx = S // ATT_BLOCK

    def riding(table, unit, l):
        return [(l + dl, n, k) for dl, n, k in table[unit] if l + dl < depth] if ride else []

    def gin(unit, l):
        return tuple(shards[ll][n][k] for ll, n, k in riding(_RIDE_FWD, unit, l))

    def tok(unit, l):
        return tuple(sinks[(ll, n)][k] for ll, n, k in riding(_RIDE_BWD, unit, l))

    def landed(result, unit, l):
        out, got, tk = result
        for (ll, n, k), g in zip(riding(_RIDE_FWD, unit, l), got):
            landing.setdefault((ll, n), {})[k] = g
            if len(landing[(ll, n)]) == _n_pieces(n):
                have[ll][n] = _full_weight(n, [landing[(ll, n)][j] for j in range(_n_pieces(n))])
        for (ll, n, k), s in zip(riding(_RIDE_BWD, unit, l), tk):
            sinks[(ll, n)][k] = s
        return out

    sc = HEAD_DIM ** -0.5
    sc_mla = (MLA_NOPE + MLA_ROPE) ** -0.5
    tq_full = _pick_block(S, (256, 128))
    tk_all = _pick_block(T0, (2176, 640, 512, 256, 128))
    tctx = _pick_block(C, (256, 128))
    rope128 = _rope_tables(S, T0, 128, 32)
    rope64 = _rope_tables(S, T0, 64, 16)

    xs = jnp.concatenate([x, ctx], axis=0)
    for l in range(depth):
        last = l == depth - 1
        t = "l%d_" % l
        lin = lambda a, name, unit=None, out=F32: _make_linear(t + name, _grad_parts(name), out)(
            a, have[l][name], tuple(sinks[(l, name)]), gin(unit, l) if unit else (), tok(unit, l) if unit else ())
        mx, mc = modx[l], modc[l]

        h = _make_modulate(t + "mod1", S)(xs, _seg2(mx[0], mc[0], 2), _seg2(mx[1], mc[1], 2))
        pj = lin(h, "w_in")[0]
        qa, ka, va, qb, kb, vb, cq, ckv, kpe, qd, kd, vd = _make_split(_IN_GROUPS, 1)(pj)
        rows = lambda v_: _make_split(((0, S), (S, T0)), 0)(v_)
        att = lambda unit, cfg, q_, k_, v_, tab_, s_: landed(
            _make_attention(t + unit, cfg)(q_, k_, v_, tab_, s_, gin(unit, l), tok(unit, l)), unit, l)
        tab = _na_table(p["na_rpb"][l])
        cfg_a = _WinCfg("na", NA_HEADS, NA_HEADS, sc, nx, C)
        qa_x, qa_c = rows(qa)
        oa = att("attn_a", cfg_a, qa_x, ka, va, tab, None)
        qb = _make_rope(t + "rope_b", 32)(qb, *rope128)
        kb = _make_rope(t + "rope_bk", 32)(kb, *rope128)
        snk = _sink_block(p["swa_sink"][l])
        cfg_b = _WinCfg("swa", SWA_HEADS, SWA_KV_HEADS, sc, nx, C)
        qb_x, qb_c = rows(qb)
        ob = att("attn_b", cfg_b, qb_x, kb, vb, None, snk)
        cq = _make_rmsnorm(t + "rms_cq")(cq, p["mla_q_norm"][l][None, :])
        ckv = _make_rmsnorm(t + "rms_ckv")(ckv, p["mla_kv_norm"][l][None, :])
        qh = lin(cq, "mla_w_uq")[0].reshape(T0, MLA_HEADS, MLA_NOPE + MLA_ROPE)
        kvh = lin(ckv, "mla_w_ukv")[0].reshape(T0, MLA_HEADS, MLA_NOPE + MLA_V)
        qpe = jnp.pad(qh[:, :, MLA_NOPE:], ((0, 0), (0, 0), (0, LANE - MLA_ROPE))).reshape(T0, MLA_HEADS * LANE)
        qpe = _make_rope(t + "rope_cq", 16)(qpe, *rope64).reshape(T0, MLA_HEADS, LANE)
        kpe = _make_rope(t + "rope_ck", 16)(kpe, *rope64)
        qc = jnp.concatenate([qh[:, :, :MLA_NOPE], qpe], axis=-1).reshape(T0, MLA_HEADS * 2 * LANE)
        kc = jnp.concatenate([kvh[:, :, :MLA_NOPE], jnp.broadcast_to(kpe[:, None, :], (T0, MLA_HEADS, LANE))],
                             axis=-1).reshape(T0, MLA_HEADS * 2 * LANE)
        vc = kvh[:, :, MLA_NOPE:].reshape(T0, MLA_HEADS * MLA_V)
        cfg_c = _AttnCfg(MLA_HEADS, MLA_HEADS, 2 * LANE, MLA_V, sc_mla, tq_full, tk_all)
        qc_x, qc_c = rows(qc)
        oc = att("attn_c", cfg_c, qc_x, kc, vc, None, None)
        qd = _make_rmsnorm(t + "rms_dq")(qd, p["gqa_q_norm"][l][None, :])
        kd = _make_rmsnorm(t + "rms_dk")(kd, p["gqa_k_norm"][l][None, :])
        qd = _make_rope(t + "rope_dq", 32)(qd, *rope128)
        kd = _make_rope(t + "rope_dk", 32)(kd, *rope128)
        cfg_d = _AttnCfg(GQA_HEADS, GQA_KV_HEADS, 128, 128, sc, tq_full, tk_all)
        qd_x, qd_c = rows(qd)
        od = att("attn_d", cfg_d, qd_x, kd, vd, None, None)
        mix = jnp.concatenate([oa, ob, oc, od], axis=1)
        if not last:
            full_c = lambda H, Hkv, dqk, s, sink: _AttnCfg(H, Hkv, dqk, 128, s, tctx, tctx, has_sink=sink)
            ctx_att = lambda name, cfg, q_, k_, v_, s_: _make_attention(t + name, cfg)(
                q_, k_[S:], v_[S:], None, s_, (), ())[0]
            oa_c = ctx_att("ctx_a", full_c(4, 4, 128, sc, False), qa_c, ka, va, None)
            ob_c = ctx_att("ctx_b", full_c(4, 2, 128, sc, True), qb_c, kb, vb, snk)
            oc_c = ctx_att("ctx_c", full_c(4, 4, 256, sc_mla, False), qc_c, kc, vc, None)
            od_c = ctx_att("ctx_d", full_c(4, 2, 128, sc, False), qd_c, kd, vd, None)
            mix = jnp.concatenate([mix, jnp.concatenate([oa_c, ob_c, oc_c, od_c], axis=1)], axis=0)
            res, nseg = xs, 2
        else:
            res, nseg = xs[:S], 1
        y = lin(mix, "w_out")[0]
        x1 = _make_resid_ln(t + "ln1", S, alpha)(res, y, _seg2(mx[2], mc[2], nseg), p["ln1_g"][l][None, :],
                                                 p["ln1_b"][l][None, :])
        h2 = _make_modulate(t + "mod2", S)(x1, _seg2(mx[3], mc[3], nseg), _seg2(mx[4], mc[4], nseg))
        gp = landed(lin(h2, "ffn_w_gate", "gate", _ACT), "gate", l)
        up = landed(lin(h2, "ffn_w_up", "up", _ACT), "up", l)
        z = landed(_make_conv_gate(t + "conv", S)(gp, up, p["ffn_conv_w"][l], p["ffn_conv_b"][l][None, :],
                                                  gin("conv", l), tok("conv", l)), "conv", l)
        f = landed(lin(z, "ffn_w_down", "down"), "down", l)
        xs = _make_resid_ln(t + "ln2", S, alpha)(x1, f, _seg2(mx[5], mc[5], nseg), p["ln2_g"][l][None, :],
                                                 p["ln2_b"][l][None, :])
    return xs


def _adamw(w, parts, m, v, name):
    R, C = w.shape
    P = parts.shape[0]
    c1 = 1.0 - ADAM_B1 ** ADAM_STEP
    c2 = 1.0 - ADAM_B2 ** ADAM_STEP
    per_row = C * (4 * 14 + 2 * P * parts.dtype.itemsize)
    cands = _divisors(R, 8, R)
    fitting = [d for d in cands if d * per_row <= 24 * 1024 * 1024]
    tr = max(fitting) if fitting else min(cands)

    def body(w_ref, p_ref, m_ref, v_ref, g_ref, d_ref, nm_ref, nv_ref):
        g = p_ref[0].astype(F32)
        for k in range(1, P):
            g = g + p_ref[k].astype(F32)
        mm = ADAM_B1 * m_ref[...] + (1.0 - ADAM_B1) * g
        vv = ADAM_B2 * v_ref[...] + (1.0 - ADAM_B2) * (g * g)
        g_ref[...] = g
        nm_ref[...] = mm
        nv_ref[...] = vv
        d_ref[...] = -ADAM_LR * ((mm / c1) / (jnp.sqrt(vv / c2) + ADAM_EPS) + ADAM_WD * w_ref[...])

    blk = pl.BlockSpec((tr, C), lambda i: (i, 0))
    pblk = pl.BlockSpec((P, tr, C), lambda i: (0, i, 0))
    sh = jax.ShapeDtypeStruct((R, C), F32)
    return pl.pallas_call(body, name=name, grid=(R // tr,), in_specs=[blk, pblk, blk, blk],
                          out_specs=[blk, blk, blk, blk], out_shape=[sh, sh, sh, sh],
                          compiler_params=_cparams(1))(w, parts, m, v)


def _adamw_nd(w, parts, m, v, name):
    shape = w.shape
    C = shape[-1]
    R = int(np.prod(shape[:-1])) if len(shape) > 1 else 1
    outs = _adamw(w.reshape(R, C), parts.reshape(parts.shape[0], R, C), m.reshape(R, C), v.reshape(R, C), name)
    return [o.reshape(shape) for o in outs]


def _adamw_layers(w, parts, m, v, name, riders=()):
    L, R, C = w.shape
    P = parts[0].shape[0]
    c1 = 1.0 - ADAM_B1 ** ADAM_STEP
    c2 = 1.0 - ADAM_B2 ** ADAM_STEP
    per_row = C * (4 * 14 + 2 * L * P * parts[0].dtype.itemsize)
    cands = _divisors(R, 8, R)
    fitting = [d for d in cands if d * per_row <= 24 * 1024 * 1024]
    tr = max(fitting) if fitting else min(cands)
    nt = R // tr

    def body(*refs):
        w_ref, p_refs, (m_ref, v_ref, g_ref, d_ref, nm_ref, nv_ref) = refs[0], refs[1:1 + L], refs[1 + L:]
        layer = pl.program_id(0)
        for k in range(L):
            @pl.when(layer == k)
            def _(p_ref=p_refs[k]):
                g = p_ref[0].astype(F32)
                for j in range(1, P):
                    g = g + p_ref[j].astype(F32)
                mm = ADAM_B1 * m_ref[0] + (1.0 - ADAM_B1) * g
                vv = ADAM_B2 * v_ref[0] + (1.0 - ADAM_B2) * (g * g)
                g_ref[0] = g
                nm_ref[0] = mm
                nv_ref[0] = vv
                d_ref[0] = -ADAM_LR * ((mm / c1) / (jnp.sqrt(vv / c2) + ADAM_EPS) + ADAM_WD * w_ref[0])

    blk = pl.BlockSpec((1, tr, C), lambda l, i: (l, i, 0))
    pblk = [pl.BlockSpec((P, tr, C), lambda l, i, k=k: (0, jnp.where(l == k, i, jnp.where(l < k, 0, nt - 1)), 0))
            for k in range(L)]
    sh = jax.ShapeDtypeStruct((L, R, C), F32)
    return _pcall(body, name=name, grid=(L, nt), in_specs=[blk] + pblk + [blk, blk],
                  out_specs=[blk, blk, blk, blk], out_shape=[sh, sh, sh, sh], args=[w, *parts, m, v], riders=riders)


def _silu(v):
    return v * (1.0 / (1.0 + jnp.exp(-v)))


def _ada_rows(c_all, c_ctx):
    return jnp.concatenate([c_all, jnp.broadcast_to(c_ctx[None, :], (N_DEV, c_ctx.shape[0]))], axis=0)


def _silu_rows(rows, name):
    def body(r_ref, o_ref):
        o_ref[...] = _silu(r_ref[...])

    return pl.pallas_call(body, name=name, out_shape=jax.ShapeDtypeStruct(rows.shape, F32))(rows)


_SMALL = ("c_ctx", "na_rpb", "swa_sink", "mla_q_norm", "mla_kv_norm", "gqa_q_norm", "gqa_k_norm",
          "ln1_g", "ln1_b", "ffn_conv_b", "ln2_g", "ln2_b")
_NAMES = ("c_ctx", "w_ada", "b_ada", "w_in", "na_rpb", "swa_sink", "mla_q_norm", "mla_kv_norm", "mla_w_uq",
          "mla_w_ukv", "gqa_q_norm", "gqa_k_norm", "w_out", "ln1_g", "ln1_b", "ffn_w_gate", "ffn_w_up",
          "ffn_conv_w", "ffn_conv_b", "ffn_w_down", "ln2_g", "ln2_b")


def _full_from_cols(g, pad_in=False):
    _, L, K, n = g.shape
    full = jnp.transpose(g, (1, 2, 0, 3)).reshape(L, K, N_DEV * n)
    if pad_in:
        full = jnp.concatenate([full[:, :, :KPE_END], jnp.zeros((L, K, 64), full.dtype), full[:, :, KPE_END:]], axis=-1)
    return full


def _pack_small(tree, extra=None):
    flat = [tree[n].reshape(-1) for n in _SMALL]
    flat.append(jnp.zeros((1,), F32) if extra is None else extra.reshape(1))
    v = jnp.concatenate(flat)
    n = v.shape[0]
    padded = -(-n // 1024) * 1024
    return jnp.pad(v, (0, padded - n)).reshape(padded // LANE, LANE)


def _unpack_small(mat, like):
    v = mat.reshape(-1)
    out, o = {}, 0
    for n in _SMALL:
        k = int(np.prod(like[n].shape))
        out[n] = v[o:o + k].reshape(like[n].shape)
        o += k
    return out, v[o]


def kernel(x, c, ctx, c_ctx, w_ada, b_ada, w_in, na_rpb, swa_sink, mla_q_norm, mla_kv_norm, mla_w_uq, mla_w_ukv, gqa_q_norm, gqa_k_norm, w_out, ln1_g, ln1_b, ffn_w_gate, ffn_w_up, ffn_conv_w, ffn_conv_b, ffn_w_down, ln2_g, ln2_b, loss_target, m_c_ctx, m_w_ada, m_b_ada, m_w_in, m_na_rpb, m_swa_sink, m_mla_q_norm, m_mla_kv_norm, m_mla_w_uq, m_mla_w_ukv, m_gqa_q_norm, m_gqa_k_norm, m_w_out, m_ln1_g, m_ln1_b, m_ffn_w_gate, m_ffn_w_up, m_ffn_conv_w, m_ffn_conv_b, m_ffn_w_down, m_ln2_g, m_ln2_b, v_c_ctx, v_w_ada, v_b_ada, v_w_in, v_na_rpb, v_swa_sink, v_mla_q_norm, v_mla_kv_norm, v_mla_w_uq, v_mla_w_ukv, v_gqa_q_norm, v_gqa_k_norm, v_w_out, v_ln1_g, v_ln1_b, v_ffn_w_gate, v_ffn_w_up, v_ffn_conv_w, v_ffn_conv_b, v_ffn_w_down, v_ln2_g, v_ln2_b):
    W = dict(c_ctx=c_ctx, w_ada=w_ada, b_ada=b_ada, w_in=w_in, na_rpb=na_rpb, swa_sink=swa_sink,
             mla_q_norm=mla_q_norm, mla_kv_norm=mla_kv_norm, mla_w_uq=mla_w_uq, mla_w_ukv=mla_w_ukv,
             gqa_q_norm=gqa_q_norm, gqa_k_norm=gqa_k_norm, w_out=w_out, ln1_g=ln1_g, ln1_b=ln1_b,
             ffn_w_gate=ffn_w_gate, ffn_w_up=ffn_w_up, ffn_conv_w=ffn_conv_w, ffn_conv_b=ffn_conv_b,
             ffn_w_down=ffn_w_down, ln2_g=ln2_g, ln2_b=ln2_b)
    M = dict(c_ctx=m_c_ctx, w_ada=m_w_ada, b_ada=m_b_ada, w_in=m_w_in, na_rpb=m_na_rpb, swa_sink=m_swa_sink,
             mla_q_norm=m_mla_q_norm, mla_kv_norm=m_mla_kv_norm, mla_w_uq=m_mla_w_uq, mla_w_ukv=m_mla_w_ukv,
             gqa_q_norm=m_gqa_q_norm, gqa_k_norm=m_gqa_k_norm, w_out=m_w_out, ln1_g=m_ln1_g, ln1_b=m_ln1_b,
             ffn_w_gate=m_ffn_w_gate, ffn_w_up=m_ffn_w_up, ffn_conv_w=m_ffn_conv_w, ffn_conv_b=m_ffn_conv_b,
             ffn_w_down=m_ffn_w_down, ln2_g=m_ln2_g, ln2_b=m_ln2_b)
    V = dict(c_ctx=v_c_ctx, w_ada=v_w_ada, b_ada=v_b_ada, w_in=v_w_in, na_rpb=v_na_rpb, swa_sink=v_swa_sink,
             mla_q_norm=v_mla_q_norm, mla_kv_norm=v_mla_kv_norm, mla_w_uq=v_mla_w_uq, mla_w_ukv=v_mla_w_ukv,
             gqa_q_norm=v_gqa_q_norm, gqa_k_norm=v_gqa_k_norm, w_out=v_w_out, ln1_g=v_ln1_g, ln1_b=v_ln1_b,
             ffn_w_gate=v_ffn_w_gate, ffn_w_up=v_ffn_w_up, ffn_conv_w=v_ffn_conv_w, ffn_conv_b=v_ffn_conv_b,
             ffn_w_down=v_ffn_w_down, ln2_g=v_ln2_g, ln2_b=v_ln2_b)
    L, D, n_ada = w_ada.shape
    me = 4 * lax.axis_index("x") + 2 * lax.axis_index("y") + lax.axis_index("c")
    xs, ctxs, tgt = x[0], ctx[0], loss_target[0]

    c_all = _exchange(c, False, "gather_c").reshape(N_DEV, D)
    a_rows = _silu_rows(_ada_rows(c_all, c_ctx), "ada_silu")
    b_mine = lax.dynamic_slice(b_ada, (0, me * n_ada), (L, n_ada))
    mod_mine = jnp.stack([_matmul(a_rows, w_ada[l], "nn", "ada_fwd_l%d" % l) + b_mine[l][None, :] for l in range(L)])
    mod_all = _exchange(mod_mine, False, "gather_mod")
    mod_all = jnp.transpose(mod_all, (1, 2, 0, 3)).reshape(L, 2 * N_DEV, N_DEV * n_ada)
    modx = lax.dynamic_slice(mod_all, (0, me, 0), (L, 1, 6 * D)).reshape(L, 6, D)
    modc = mod_all[:, N_DEV].reshape(L, 6, D)

    big = _BIG_COL + _BIG_ROW
    views = {n: _shard_view(n, W[n]) for n in big}
    shards = [{n: _cut(n, views[n][l].astype(BF16)) for n in big} for l in range(L)]
    full = [{n: _full_weight(n, [_gather_by_chip(views[n][0].astype(BF16), "gather_l0_" + n)]) for n in _IN_GROUP}]
    full += [{} for _ in range(1, L)]
    conv_w_full = _full_from_cols(_exchange(ffn_conv_w, False, "gather_conv_w"))

    toks = [{n: tuple(jnp.zeros((N_DEV,) + s.shape, BF16) for s in shards[l][n]) for n in big} for l in range(L)]
    small = {n: W[n] for n in _SMALL if n != "c_ctx"}
    small["ffn_conv_w"] = conv_w_full

    def forward(xv, mxv, mcv, sm, tk):
        return _local_forward(xv, ctxs, mxv, mcv, sm, full, shards, tk)

    y, backward = jax.vjp(forward, xs, modx, modc, small, toks)
    loss_local, dy = _loss_and_grad(y, tgt, "loss_head")
    grad_x, dmodx, dmodc, dsmall, dtoks = backward(dy)

    dmods = _exchange(jnp.stack([dmodx.reshape(L, 6 * D), dmodc.reshape(L, 6 * D)]), False, "gather_dmod")
    dm_rows = jnp.concatenate([dmods[:, 0], dmods[:, 1]], axis=0)
    dm_rows = jnp.transpose(dm_rows, (1, 0, 2))
    dm_mine = lax.dynamic_slice(dm_rows, (0, 0, me * n_ada), (L, 2 * N_DEV, n_ada))
    g_w_ada = [_matmul(a_rows, dm_mine[l], "tn", "ada_dw_l%d" % l) for l in range(L)]
    d_rows = sum(_matmul(dm_mine[l], w_ada[l], "nt", "ada_da_l%d" % l) for l in range(L))
    sig = 1.0 / (1.0 + jnp.exp(-c_ctx))
    d_c_ctx_part = jnp.sum(d_rows[N_DEV:], axis=0) * (sig * (1.0 + c_ctx * (1.0 - sig)))

    outs = {}
    pieces = [{n: list(dtoks[l][n]) for n in big} for l in range(L)]
    late = {"ffn_w_gate": (("w_in", 0),), "ffn_w_up": (("w_in", 1),), "ffn_w_down": (("w_in", 2),),
            "w_ada": (("w_in", 3),), "w_out": (("mla_w_uq", 0), ("mla_w_ukv", 0))}

    def update(n, w, g_parts, m, v):
        res = _adamw_layers(w, g_parts, m, v, "adamw_" + n, _riders([pieces[0][a][k] for a, k in late.get(n, ())], True))
        for (a, k), got in zip(late.get(n, ()), res[4:]):
            pieces[0][a][k] = got
        return res[:4]

    def whole(l, n):
        return jnp.concatenate(pieces[l][n], axis=-1) if len(pieces[l][n]) > 1 else pieces[l][n][0]

    outs["w_ada"] = update("w_ada", w_ada, [g[None] for g in g_w_ada], M["w_ada"], V["w_ada"])
    for n in [n for n in big if n not in _IN_GROUP] + list(_IN_GROUP):
        res = update(n, views[n], [whole(l, n) for l in range(L)], _shard_view(n, M[n]), _shard_view(n, V[n]))
        outs[n] = [_shard_view(n, r) for r in res]
    dcw = dsmall.pop("ffn_conv_w")
    cw_parts = _exchange(jnp.transpose(dcw.reshape(L, 3, N_DEV, -1), (2, 0, 1, 3)), True, "scatter_conv_w")
    outs["ffn_conv_w"] = _adamw_layers(ffn_conv_w, [cw_parts[:, l] for l in range(L)], M["ffn_conv_w"],
                                       V["ffn_conv_w"], "adamw_conv_w")
    outs["b_ada"] = _adamw_nd(b_ada, jnp.transpose(dm_rows, (1, 0, 2)), M["b_ada"], V["b_ada"], "adamw_b_ada")

    dsmall["c_ctx"] = d_c_ctx_part
    small_parts = _exchange(_pack_small(dsmall, extra=loss_local), False, "gather_small")
    w_small = _pack_small({n: W[n] for n in _SMALL})
    g_s, d_s, nm_s, nv_s = _adamw(w_small, small_parts, _pack_small({n: M[n] for n in _SMALL}),
                                  _pack_small({n: V[n] for n in _SMALL}), "adamw_small")
    like = {n: W[n] for n in _SMALL}
    g_small, loss = _unpack_small(g_s, like)
    unpacked = [g_small, _unpack_small(d_s, like)[0], _unpack_small(nm_s, like)[0], _unpack_small(nv_s, like)[0]]
    for n in _SMALL:
        outs[n] = [u[n] for u in unpacked]

    result = [loss, grad_x[None]]
    for k in range(4):
        result += [outs[n][k] for n in _NAMES]
    return tuple(result)
```

```python
import functools
import math

import numpy as np
import jax
import jax.numpy as jnp
from jax import lax
from jax.experimental import pallas as pl
from jax.experimental.pallas import tpu as pltpu

F32 = jnp.float32
BF16 = jnp.bfloat16
_MXU = jnp.bfloat16
_ACT = jnp.bfloat16

N_DEV = 8
GRID_W = 64
HEAD_DIM = 128
NA_HEADS, NA_WIN_R, NA_WIN_C = 4, 8, 16
SWA_HEADS, SWA_KV_HEADS, SWA_WINDOW = 4, 2, 128
MLA_HEADS, MLA_Q_LORA, MLA_KV_LORA, MLA_NOPE, MLA_ROPE, MLA_V = 4, 384, 128, 128, 64, 128
GQA_HEADS, GQA_KV_HEADS = 4, 2
ROPE_THETA = 10000.0
EPS = 1e-6
NEG = -1e30
LOG2E = 1.4426950408889634
IN_SIZES = (512, 512, 512, 512, 256, 256, MLA_Q_LORA, MLA_KV_LORA, MLA_ROPE, 512, 256, 256)
IN_COLS = sum(IN_SIZES)
KPE_END = sum(IN_SIZES[:9])
IN_COLS_PAD = IN_COLS + 64
ADAM_LR, ADAM_B1, ADAM_B2, ADAM_EPS, ADAM_WD, ADAM_STEP = 0.001, 0.9, 0.999, 1e-08, 0.01, 10

LANE = 128
ROW_TILE = 256
WIDE_ROW_TILES = (1088, 1024, 640, 512, 256)
ATT_BLOCK = 128
VMEM_LIMIT = 56 * 1024 * 1024
MM_BUDGET = 36 * 1024 * 1024
HBM_BPS = 3.0e12
MXU_FLOPS = 9.0e14
VMEM_BPS = 4.0e12
STEP_S = 0.4e-6


def _cparams(n_axes):
    return pltpu.CompilerParams(dimension_semantics=("arbitrary",) * n_axes, vmem_limit_bytes=VMEM_LIMIT)


def _exchange_copies(in_ref, out_ref, send_sems, recv_sems, local_sem, scatter, rows=None):
    x, y, c = lax.axis_index("x"), lax.axis_index("y"), lax.axis_index("c")
    me = 4 * x + 2 * y + c
    cut = (lambda ref: ref) if rows is None else (lambda ref: ref.at[rows])
    copies = [pltpu.make_async_copy(cut(in_ref.at[me] if scatter else in_ref), cut(out_ref.at[me]), local_sem)]
    for k in range(1, N_DEV):
        px, py, pc = (x + (k >> 2)) % 2, (y + ((k >> 1) & 1)) % 2, (c + (k & 1)) % 2
        peer = 4 * px + 2 * py + pc
        copies.append(pltpu.make_async_remote_copy(
            src_ref=cut(in_ref.at[peer] if scatter else in_ref), dst_ref=cut(out_ref.at[me]),
            send_sem=send_sems.at[k - 1], recv_sem=recv_sems.at[k - 1],
            device_id=(px, py, pc), device_id_type=pl.DeviceIdType.MESH))
    return copies


PIECE_ROWS = 16
MAX_PIECES = 32


def _pieces(rows, steps):
    if rows % PIECE_ROWS:
        return 1
    units = rows // PIECE_ROWS
    return max(d for d in range(1, min(MAX_PIECES, steps) + 1) if units % d == 0)


def _exchange_out_shape(arr, scatter):
    return jax.ShapeDtypeStruct(arr.shape if scatter else (N_DEV,) + arr.shape, arr.dtype)


_EXCHANGE_SEMS = (pltpu.SemaphoreType.DMA((N_DEV - 1,)), pltpu.SemaphoreType.DMA((N_DEV - 1,)),
                  pltpu.SemaphoreType.DMA)


def _exchange(inp, scatter, name):
    def body(in_ref, out_ref, send_sems, recv_sems, local_sem):
        copies = _exchange_copies(in_ref, out_ref, send_sems, recv_sems, local_sem, scatter)
        for cp in copies:
            cp.start()
        for cp in copies:
            cp.wait()

    hbm = pl.BlockSpec(memory_space=pl.ANY)
    return pl.pallas_call(
        body, name=name, in_specs=[hbm], out_specs=hbm, out_shape=_exchange_out_shape(inp, scatter),
        scratch_shapes=list(_EXCHANGE_SEMS), compiler_params=pltpu.CompilerParams(has_side_effects=True),
    )(inp)


def _gather_by_chip(inp, name):
    def body(x_ref, out_ref, send_sems, recv_sems, local_sem):
        x, y, c = lax.axis_index("x"), lax.axis_index("y"), lax.axis_index("c")
        me, sibling = (x, y, c), (x, y, 1 - c)
        chips = [(1 - x, y), (x, 1 - y), (1 - x, 1 - y)]

        def slot(px, py, pc):
            return out_ref.at[4 * px + 2 * py + pc]

        def copy(k, block, to, src=None):
            return pltpu.make_async_remote_copy(
                src_ref=slot(*block) if src is None else src, dst_ref=slot(*block), send_sem=send_sems.at[k],
                recv_sem=recv_sems.at[k], device_id=to, device_id_type=pl.DeviceIdType.MESH)

        mine = pltpu.make_async_copy(x_ref, slot(*me), local_sem)
        mine.start()
        first = [copy(0, me, sibling, src=x_ref)]
        first += [copy(1 + j, me, (*chip, c), src=x_ref) for j, chip in enumerate(chips)]
        for cp in first:
            cp.start()
        passed = [copy(4 + j, (*chip, c), sibling) for j, chip in enumerate(chips)]
        for j, chip in enumerate(chips):
            copy(1 + j, (*chip, c), me).wait_recv()
            passed[j].start()
        copy(0, sibling, me).wait_recv()
        for j, chip in enumerate(chips):
            copy(4 + j, (*chip, 1 - c), me).wait_recv()
        for cp in first + passed:
            cp.wait_send()
        mine.wait()

    hbm = pl.BlockSpec(memory_space=pl.ANY)
    return pl.pallas_call(
        body, name=name, in_specs=[hbm], out_specs=hbm, out_shape=_exchange_out_shape(inp, False),
        scratch_shapes=list(_EXCHANGE_SEMS), compiler_params=pltpu.CompilerParams(has_side_effects=True),
    )(inp)


def _pcall(body, *, name, grid, in_specs, out_specs, out_shape, args, scratch_shapes=(), riders=()):
    in_specs, out_specs, out_shape = list(in_specs), list(out_specs), list(out_shape)
    scratch, args = list(scratch_shapes), list(args)
    n_in, n_out, n_scr, n_r = len(in_specs), len(out_specs), len(scratch), len(riders)
    hbm = pl.BlockSpec(memory_space=pl.ANY)
    for arr, scatter in riders:
        in_specs.append(hbm)
        args.append(arr)
        out_specs.append(hbm)
        out_shape.append(_exchange_out_shape(arr, scatter))
        scratch += list(_EXCHANGE_SEMS)

    def wrapped(*refs):
        ins, r_in = refs[:n_in], refs[n_in:n_in + n_r]
        o0 = n_in + n_r
        outs, r_out = refs[o0:o0 + n_out], refs[o0 + n_out:o0 + n_out + n_r]
        s0 = o0 + n_out + n_r
        scr, sems = refs[s0:s0 + n_scr], refs[s0 + n_scr:]

        def copies(r, rows=None):
            return _exchange_copies(r_in[r], r_out[r], sems[3 * r], sems[3 * r + 1], sems[3 * r + 2], riders[r][1],
                                    rows)

        if n_r:
            steps = int(np.prod(grid))
            step = functools.reduce(lambda u, w: u * w[1] + w[0],
                                    [(pl.program_id(a), n) for a, n in enumerate(grid)], 0)
            for r, (arr, scatter) in enumerate(riders):
                block_rows = arr.shape[1] if scatter else arr.shape[0]
                n_pieces = _pieces(block_rows, steps)
                every, piece = steps // n_pieces, block_rows // n_pieces

                @pl.when((step % every == 0) & (step // every < n_pieces))
                def _(r=r, every=every, piece=piece, whole=n_pieces == 1):
                    rows = None if whole else pl.ds(pl.multiple_of((step // every) * piece, PIECE_ROWS), piece)
                    for cp in copies(r, rows):
                        cp.start()

        body(*ins, *outs, *scr)

        if n_r:
            @pl.when(step == steps - 1)
            def _():
                for r in range(n_r):
                    for cp in copies(r):
                        cp.wait()

    params = pltpu.CompilerParams(dimension_semantics=("arbitrary",) * len(grid), vmem_limit_bytes=VMEM_LIMIT,
                                  has_side_effects=bool(riders))
    return list(pl.pallas_call(wrapped, name=name, grid=grid, in_specs=in_specs, out_specs=out_specs,
                               out_shape=out_shape, scratch_shapes=scratch, compiler_params=params)(*args))


def _riders(arrays, scatter):
    return [(a, scatter) for a in arrays]


def _nones(arrays):
    return tuple(None for _ in arrays)


def _divisors(n, step, cap):
    out = [d for d in range(step, min(n, cap) + 1, step) if n % d == 0]
    if n <= cap and n not in out:
        out.append(n)
    return out


def _mm_tiles(M, N, K, sa, sb, m_step, k_step):
    best, best_cost = None, None
    for tm in _divisors(M, m_step, 2304):
        for tn in _divisors(N, LANE, 2304):
            for tk in _divisors(K, k_step, 2304):
                vm = 2 * (tm * tk * sa + tk * tn * sb) + 3 * tm * tn * 4
                if vm > MM_BUDGET:
                    continue
                traffic = M * K * sa * (N // tn) + K * N * sb * (M // tm) + M * N * 4
                steps = (M // tm) * (N // tn) * (K // tk)
                passes = 1 if tk == K else 3
                busy = steps * (2 * tm * tn * tk / MXU_FLOPS + passes * tm * tn * 4 / VMEM_BPS)
                cost = max(traffic / HBM_BPS, busy) + steps * STEP_S
                if m_step == LANE:
                    cost = traffic / HBM_BPS + steps * STEP_S
                if tm % 256 or tn % 256:
                    cost *= 1.05
                if best_cost is None or cost < best_cost:
                    best, best_cost = (tm, tn, tk), cost
    assert best is not None, (M, N, K)
    return best


def _matmul(a, b, mode, name, riders=(), out_dtype=F32):
    if mode == "nn":
        (M, K), (K2, N) = a.shape, b.shape
    elif mode == "nt":
        (M, K), (N, K2) = a.shape, b.shape
    else:
        (K, M), (K2, N) = a.shape, b.shape
    assert K == K2, (a.shape, b.shape, mode)
    tm, tn, tk = _mm_tiles(M, N, K, a.dtype.itemsize, b.dtype.itemsize,
                           LANE if mode == "tn" else 8, 8 if mode == "tn" else LANE)
    nk = K // tk
    dims = {"nn": (((1,), (0,)), ((), ())), "nt": (((1,), (1,)), ((), ())), "tn": (((0,), (0,)), ((), ()))}[mode]

    own_acc = nk > 1 and out_dtype != F32

    def body(a_ref, b_ref, o_ref, *scr):
        prod = lax.dot_general(a_ref[...].astype(_MXU), b_ref[...].astype(_MXU), dims, preferred_element_type=F32)
        if nk == 1:
            o_ref[...] = prod.astype(out_dtype)
        else:
            acc_ref = scr[0] if own_acc else o_ref
            k = pl.program_id(2)

            @pl.when(k == 0)
            def _():
                acc_ref[...] = prod

            @pl.when(k > 0)
            def _():
                acc_ref[...] += prod

            if own_acc:
                @pl.when(k == nk - 1)
                def _():
                    o_ref[...] = acc_ref[...].astype(out_dtype)

    if mode == "nn":
        a_spec = pl.BlockSpec((tm, tk), lambda i, j, k: (i, k))
        b_spec = pl.BlockSpec((tk, tn), lambda i, j, k: (k, j))
    elif mode == "nt":
        a_spec = pl.BlockSpec((tm, tk), lambda i, j, k: (i, k))
        b_spec = pl.BlockSpec((tn, tk), lambda i, j, k: (j, k))
    else:
        a_spec = pl.BlockSpec((tk, tm), lambda i, j, k: (k, i))
        b_spec = pl.BlockSpec((tk, tn), lambda i, j, k: (k, j))
    outs = _pcall(body, name=name, grid=(M // tm, N // tn, nk), in_specs=[a_spec, b_spec],
                  out_specs=[pl.BlockSpec((tm, tn), lambda i, j, k: (i, j))],
                  out_shape=[jax.ShapeDtypeStruct((M, N), out_dtype)], args=[a, b],
                  scratch_shapes=[pltpu.VMEM((tm, tn), F32)] if own_acc else [], riders=riders)
    return outs if riders else outs[0]


def _make_linear(tag, to_parts, out_dtype=F32):
    def run(a, w, gin, tok):
        if gin:
            y, *got = _matmul(a, w, "nn", tag + "_fwd", riders=_riders(gin, False), out_dtype=out_dtype)
        else:
            y, got = _matmul(a, w, "nn", tag + "_fwd", out_dtype=out_dtype), []
        return y, tuple(got), tok

    @jax.custom_vjp
    def linear(a, w, sink, gin, tok):
        return run(a, w, gin, tok)

    def fwd(a, w, sink, gin, tok):
        return run(a, w, gin, tok), (a, w, gin)

    def bwd(res, cts):
        a, w, gin = res
        dy, _, dtok = cts
        dw = _matmul(a, dy, "tn", tag + "_bwd_dw")
        if dtok:
            da, *recv = _matmul(dy, w, "nt", tag + "_bwd_da", riders=_riders(dtok, True), out_dtype=a.dtype)
        else:
            da, recv = _matmul(dy, w, "nt", tag + "_bwd_da", out_dtype=a.dtype), []
        return da, jnp.zeros_like(w), to_parts(dw), _nones(gin), tuple(recv)

    linear.defvjp(fwd, bwd)
    return linear


def _seg_map(n_x_tiles):
    return lambda i: (jnp.where(i >= n_x_tiles, 1, 0), 0, 0)


def _ln_stats(x):
    mu = jnp.mean(x, axis=-1, keepdims=True)
    xc = x - mu
    var = jnp.mean(xc * xc, axis=-1, keepdims=True)
    rstd = lax.rsqrt(var + EPS)
    return xc * rstd, rstd


def _make_modulate(tag, n_x):
    def call_fwd(x, shift, scale):
        T, D = x.shape
        nxt = n_x // ROW_TILE

        def body(x_ref, sh_ref, sc_ref, o_ref):
            xhat, _ = _ln_stats(x_ref[...])
            o_ref[...] = xhat * (1.0 + sc_ref[0]) + sh_ref[0]

        row = pl.BlockSpec((ROW_TILE, D), lambda i: (i, 0))
        seg = pl.BlockSpec((1, 1, D), _seg_map(nxt))
        return pl.pallas_call(body, name=tag + "_fwd", grid=(T // ROW_TILE,), in_specs=[row, seg, seg],
                              out_specs=row, out_shape=jax.ShapeDtypeStruct((T, D), F32),
                              compiler_params=_cparams(1))(x, shift, scale)

    def call_bwd(x, scale, dh):
        T, D = x.shape
        nxt = n_x // ROW_TILE
        nseg = scale.shape[0]

        def body(x_ref, sc_ref, dh_ref, dx_ref, dsh_ref, dsc_ref):
            i = pl.program_id(0)

            @pl.when((i == 0) | (i == nxt))
            def _():
                dsh_ref[...] = jnp.zeros_like(dsh_ref)
                dsc_ref[...] = jnp.zeros_like(dsc_ref)

            xhat, rstd = _ln_stats(x_ref[...])
            dh = dh_ref[...]
            dsh_ref[0] += jnp.sum(dh, axis=0, keepdims=True)
            dsc_ref[0] += jnp.sum(dh * xhat, axis=0, keepdims=True)
            dxh = dh * (1.0 + sc_ref[0])
            m1 = jnp.mean(dxh, axis=-1, keepdims=True)
            m2 = jnp.mean(dxh * xhat, axis=-1, keepdims=True)
            dx_ref[...] = rstd * (dxh - m1 - xhat * m2)

        row = pl.BlockSpec((ROW_TILE, D), lambda i: (i, 0))
        seg = pl.BlockSpec((1, 1, D), _seg_map(nxt))
        segshape = jax.ShapeDtypeStruct((nseg, 1, D), F32)
        return pl.pallas_call(body, name=tag + "_bwd", grid=(T // ROW_TILE,), in_specs=[row, seg, row],
                              out_specs=[row, seg, seg],
                              out_shape=[jax.ShapeDtypeStruct((T, D), F32), segshape, segshape],
                              compiler_params=_cparams(1))(x, scale, dh)

    @jax.custom_vjp
    def modulate(x, shift, scale):
        return call_fwd(x, shift, scale)

    def fwd(x, shift, scale):
        return call_fwd(x, shift, scale), (x, scale)

    def bwd(res, dh):
        x, scale = res
        return tuple(call_bwd(x, scale, dh))

    modulate.defvjp(fwd, bwd)
    return modulate


def _make_resid_ln(tag, n_x, alpha):
    def call_fwd(x, y, gate, g, b):
        T, D = x.shape
        nxt = n_x // ROW_TILE

        def body(x_ref, y_ref, gt_ref, g_ref, b_ref, o_ref):
            u = alpha * x_ref[...] + gt_ref[0] * y_ref[...]
            uhat, _ = _ln_stats(u)
            o_ref[...] = uhat * g_ref[...] + b_ref[...]

        row = pl.BlockSpec((ROW_TILE, D), lambda i: (i, 0))
        seg = pl.BlockSpec((1, 1, D), _seg_map(nxt))
        vec = pl.BlockSpec((1, D), lambda i: (0, 0))
        return pl.pallas_call(body, name=tag + "_fwd", grid=(T // ROW_TILE,), in_specs=[row, row, seg, vec, vec],
                              out_specs=row, out_shape=jax.ShapeDtypeStruct((T, D), F32),
                              compiler_params=_cparams(1))(x, y, gate, g, b)

    def call_bwd(x, y, gate, g, do):
        T, D = x.shape
        nxt = n_x // ROW_TILE
        nseg = gate.shape[0]

        def body(x_ref, y_ref, gt_ref, g_ref, do_ref, dx_ref, dy_ref, dgt_ref, dg_ref, db_ref):
            i = pl.program_id(0)

            @pl.when(i == 0)
            def _():
                dg_ref[...] = jnp.zeros_like(dg_ref)
                db_ref[...] = jnp.zeros_like(db_ref)

            @pl.when((i == 0) | (i == nxt))
            def _():
                dgt_ref[...] = jnp.zeros_like(dgt_ref)

            y = y_ref[...]
            gate_v = gt_ref[0]
            uhat, rstd = _ln_stats(alpha * x_ref[...] + gate_v * y)
            do = do_ref[...]
            dg_ref[...] += jnp.sum(do * uhat, axis=0, keepdims=True)
            db_ref[...] += jnp.sum(do, axis=0, keepdims=True)
            duh = do * g_ref[...]
            m1 = jnp.mean(duh, axis=-1, keepdims=True)
            m2 = jnp.mean(duh * uhat, axis=-1, keepdims=True)
            du = rstd * (duh - m1 - uhat * m2)
            dx_ref[...] = alpha * du
            dy_ref[...] = gate_v * du
            dgt_ref[0] += jnp.sum(du * y, axis=0, keepdims=True)

        row = pl.BlockSpec((ROW_TILE, D), lambda i: (i, 0))
        seg = pl.BlockSpec((1, 1, D), _seg_map(nxt))
        vec = pl.BlockSpec((1, D), lambda i: (0, 0))
        rs = jax.ShapeDtypeStruct((T, D), F32)
        vs = jax.ShapeDtypeStruct((1, D), F32)
        return pl.pallas_call(body, name=tag + "_bwd", grid=(T // ROW_TILE,), in_specs=[row, row, seg, vec, row],
                              out_specs=[row, row, seg, vec, vec],
                              out_shape=[rs, rs, jax.ShapeDtypeStruct((nseg, 1, D), F32), vs, vs],
                              compiler_params=_cparams(1))(x, y, gate, g, do)

    @jax.custom_vjp
    def resid_ln(x, y, gate, g, b):
        return call_fwd(x, y, gate, g, b)

    def fwd(x, y, gate, g, b):
        return call_fwd(x, y, gate, g, b), (x, y, gate, g)

    def bwd(res, do):
        x, y, gate, g = res
        return tuple(call_bwd(x, y, gate, g, do))

    resid_ln.defvjp(fwd, bwd)
    return resid_ln


def _make_rmsnorm(tag):
    def call_fwd(x, g):
        T, W = x.shape
        gw = g.shape[1]

        tr = _pick_block(T, WIDE_ROW_TILES)

        def body(x_ref, g_ref, o_ref):
            for gi in range(W // gw):
                cols = slice(gi * gw, (gi + 1) * gw)
                x = x_ref[:, cols]
                r = lax.rsqrt(jnp.mean(x * x, axis=-1, keepdims=True) + EPS)
                o_ref[:, cols] = x * r * g_ref[...]

        blk = pl.BlockSpec((tr, W), lambda i: (i, 0))
        vec = pl.BlockSpec((1, gw), lambda i: (0, 0))
        return pl.pallas_call(body, name=tag + "_fwd", grid=(T // tr,), in_specs=[blk, vec],
                              out_specs=blk, out_shape=jax.ShapeDtypeStruct((T, W), F32),
                              compiler_params=_cparams(1))(x, g)

    def call_bwd(x, g, dy):
        T, W = x.shape
        gw = g.shape[1]
        tr = _pick_block(T, WIDE_ROW_TILES)

        def body(x_ref, g_ref, dy_ref, dx_ref, dg_ref):
            @pl.when(pl.program_id(0) == 0)
            def _():
                dg_ref[...] = jnp.zeros_like(dg_ref)

            for gi in range(W // gw):
                cols = slice(gi * gw, (gi + 1) * gw)
                x = x_ref[:, cols]
                r = lax.rsqrt(jnp.mean(x * x, axis=-1, keepdims=True) + EPS)
                xn = x * r
                dy = dy_ref[:, cols]
                dg_ref[...] += jnp.sum(dy * xn, axis=0, keepdims=True)
                dxn = dy * g_ref[...]
                dx_ref[:, cols] = r * (dxn - xn * jnp.mean(dxn * xn, axis=-1, keepdims=True))

        blk = pl.BlockSpec((tr, W), lambda i: (i, 0))
        vec = pl.BlockSpec((1, gw), lambda i: (0, 0))
        return pl.pallas_call(body, name=tag + "_bwd", grid=(T // tr,), in_specs=[blk, vec, blk],
                              out_specs=[blk, vec],
                              out_shape=[jax.ShapeDtypeStruct((T, W), F32), jax.ShapeDtypeStruct((1, gw), F32)],
                              compiler_params=_cparams(1))(x, g, dy)

    @jax.custom_vjp
    def rmsnorm(x, g):
        return call_fwd(x, g)

    def fwd(x, g):
        return call_fwd(x, g), (x, g)

    def bwd(res, dy):
        x, g = res
        return tuple(call_bwd(x, g, dy))

    rmsnorm.defvjp(fwd, bwd)
    return rmsnorm


def _rope_tables(n_x, n_all, width, half):
    t = np.arange(n_all)
    row, col = t // GRID_W, t % GRID_W
    lane = np.arange(LANE)
    inside = lane < width
    axis_is_col = (lane // (2 * half)) % 2 == 1
    f = (lane % (2 * half)) % half
    inv_freq = ROPE_THETA ** (-(f.astype(np.float64)) / half)
    pos = np.where(axis_is_col[None, :], col[:, None], row[:, None]).astype(np.float64)
    ang = (pos.astype(np.float32) * inv_freq.astype(np.float32)[None, :]).astype(np.float32)
    live = inside[None, :] & (t < n_x)[:, None]
    cos = np.where(live, np.cos(ang), 1.0).astype(np.float32)
    sin = np.where(live, np.sin(ang), 0.0).astype(np.float32)
    first = (lane % (2 * half)) < half
    s_lo = np.where(first[None, :], -sin, 0.0).astype(np.float32)
    s_hi = np.where(first[None, :], 0.0, sin).astype(np.float32)
    return jnp.asarray(cos), jnp.asarray(s_lo), jnp.asarray(s_hi)


def _make_rope(tag, half):
    def call(x, cos, s_lo, s_hi, transpose, name):
        T, W = x.shape

        tr = _pick_block(T, WIDE_ROW_TILES)

        def body(x_ref, c_ref, lo_ref, hi_ref, o_ref):
            c, lo, hi = c_ref[...], lo_ref[...], hi_ref[...]
            for cb in range(W // LANE):
                cols = slice(cb * LANE, (cb + 1) * LANE)
                x = x_ref[:, cols]
                if not transpose:
                    o_ref[:, cols] = x * c + pltpu.roll(x, LANE - half, 1) * lo + pltpu.roll(x, half, 1) * hi
                else:
                    o_ref[:, cols] = x * c + pltpu.roll(x * lo, half, 1) + pltpu.roll(x * hi, LANE - half, 1)

        blk = pl.BlockSpec((tr, W), lambda i: (i, 0))
        tab = pl.BlockSpec((tr, LANE), lambda i: (i, 0))
        return pl.pallas_call(body, name=name, grid=(T // tr,), in_specs=[blk, tab, tab, tab],
                              out_specs=blk, out_shape=jax.ShapeDtypeStruct((T, W), F32),
                              compiler_params=_cparams(1))(x, cos, s_lo, s_hi)

    @jax.custom_vjp
    def rope(x, cos, s_lo, s_hi):
        return call(x, cos, s_lo, s_hi, False, tag + "_fwd")

    def fwd(x, cos, s_lo, s_hi):
        return call(x, cos, s_lo, s_hi, False, tag + "_fwd"), (cos, s_lo, s_hi)

    def bwd(res, dy):
        cos, s_lo, s_hi = res
        return call(dy, cos, s_lo, s_hi, True, tag + "_bwd"), None, None, None

    rope.defvjp(fwd, bwd)
    return rope


def _make_conv_gate(tag, n_x):
    def shifted(v, T):
        t = lax.broadcasted_iota(jnp.int32, v.shape, 0)
        prev = jnp.where((t == 0) | (t == n_x), 0.0, pltpu.roll(v, 1, 0))
        nxt = jnp.where((t == n_x - 1) | (t == T - 1), 0.0, pltpu.roll(v, T - 1, 0))
        return prev, nxt

    def call_fwd(a, u, cw, cb, riders):
        T, Fd = a.shape

        def body(a_ref, u_ref, cw_ref, cb_ref, z_ref):
            pre = a_ref[...].astype(F32)
            prev, nxt = shifted(pre, T)
            s = prev * cw_ref[0:1, :] + pre * cw_ref[1:2, :] + nxt * cw_ref[2:3, :] + cb_ref[...]
            z_ref[...] = (s * (1.0 / (1.0 + jnp.exp(-s))) * u_ref[...].astype(F32)).astype(z_ref.dtype)

        col = pl.BlockSpec((T, LANE), lambda j: (0, j))
        return _pcall(body, name=tag + "_fwd", grid=(Fd // LANE,),
                      in_specs=[col, col, pl.BlockSpec((3, LANE), lambda j: (0, j)),
                                pl.BlockSpec((1, LANE), lambda j: (0, j))],
                      out_specs=[col], out_shape=[jax.ShapeDtypeStruct((T, Fd), a.dtype)], args=[a, u, cw, cb],
                      riders=riders)

    def call_bwd(a, u, cw, cb, dz, riders):
        T, Fd = a.shape

        def body(a_ref, u_ref, cw_ref, cb_ref, dz_ref, da_ref, du_ref, dcw_ref, dcb_ref):
            pre = a_ref[...].astype(F32)
            prev, nxt = shifted(pre, T)
            s = prev * cw_ref[0:1, :] + pre * cw_ref[1:2, :] + nxt * cw_ref[2:3, :] + cb_ref[...]
            sig = 1.0 / (1.0 + jnp.exp(-s))
            dz = dz_ref[...].astype(F32)
            du_ref[...] = (dz * s * sig).astype(du_ref.dtype)
            ds = dz * u_ref[...].astype(F32) * (sig * (1.0 + s * (1.0 - sig)))
            ds_prev, ds_next = shifted(ds, T)
            da_ref[...] = (ds * cw_ref[1:2, :] + ds_next * cw_ref[0:1, :] + ds_prev * cw_ref[2:3, :]).astype(
                da_ref.dtype)
            dcw_ref[0:1, :] = jnp.sum(ds * prev, axis=0, keepdims=True)
            dcw_ref[1:2, :] = jnp.sum(ds * pre, axis=0, keepdims=True)
            dcw_ref[2:3, :] = jnp.sum(ds * nxt, axis=0, keepdims=True)
            dcb_ref[...] = jnp.sum(ds, axis=0, keepdims=True)

        col = pl.BlockSpec((T, LANE), lambda j: (0, j))
        w3 = pl.BlockSpec((3, LANE), lambda j: (0, j))
        w1 = pl.BlockSpec((1, LANE), lambda j: (0, j))
        big = jax.ShapeDtypeStruct((T, Fd), a.dtype)
        return _pcall(body, name=tag + "_bwd", grid=(Fd // LANE,), in_specs=[col, col, w3, w1, col],
                      out_specs=[col, col, w3, w1],
                      out_shape=[big, big, jax.ShapeDtypeStruct((3, Fd), F32), jax.ShapeDtypeStruct((1, Fd), F32)],
                      args=[a, u, cw, cb, dz], riders=riders)

    def run(a, u, cw, cb, gin, tok):
        z, *got = call_fwd(a, u, cw, cb, _riders(gin, False))
        return z, tuple(got), tok

    @jax.custom_vjp
    def conv_gate(a, u, cw, cb, gin, tok):
        return run(a, u, cw, cb, gin, tok)

    def fwd(a, u, cw, cb, gin, tok):
        return run(a, u, cw, cb, gin, tok), (a, u, cw, cb, gin)

    def bwd(res, cts):
        a, u, cw, cb, gin = res
        dz, _, dtok = cts
        da, du, dcw, dcb, *recv = call_bwd(a, u, cw, cb, dz, _riders(dtok, True))
        return da, du, dcw, dcb, _nones(gin), tuple(recv)

    conv_gate.defvjp(fwd, bwd)
    return conv_gate


def _loss_and_grad(y, tgt, name):
    T, D = y.shape

    def body(y_ref, t_ref, l_ref, r_ref):
        @pl.when(pl.program_id(0) == 0)
        def _():
            l_ref[...] = jnp.zeros_like(l_ref)

        d = y_ref[...] - t_ref[...]
        r_ref[...] = d * (1.0 / D)
        l_ref[...] += jnp.sum(d * d) * (0.5 / D)

    row = pl.BlockSpec((ROW_TILE, D), lambda i: (i, 0))
    l, r = pl.pallas_call(body, name=name, grid=(T // ROW_TILE,), in_specs=[row, row],
                          out_specs=[pl.BlockSpec((8, LANE), lambda i: (0, 0)), row],
                          out_shape=[jax.ShapeDtypeStruct((8, LANE), F32), jax.ShapeDtypeStruct((T, D), F32)],
                          compiler_params=_cparams(1))(y, tgt)
    return l[0, 0], r


class _AttnCfg:
    def __init__(self, H, Hkv, dqk, dv, scale, tq, tk, has_sink=False):
        self.H, self.Hkv, self.dqk, self.dv, self.scale = H, Hkv, dqk, dv, scale
        self.tq, self.tk, self.has_sink = tq, tk, has_sink
        self.G = H // Hkv


class _WinCfg:
    def __init__(self, mode, H, Hkv, scale, nx, n_ctx):
        self.mode, self.H, self.Hkv, self.scale, self.nx, self.n_ctx = mode, H, Hkv, scale, nx, n_ctx
        self.G = H // Hkv
        self.has_sink = mode == "swa"
        self.nwin, self.back = (3, 1) if mode == "swa" else (5, 2)
        self.width = self.nwin * ATT_BLOCK

    def first_block(self, i):
        return jnp.clip(i - self.back, 0, self.nx - self.nwin)

    def slots(self, i, jj):
        d = 2 * (self.first_block(i) + jj - i)
        return jnp.clip(d + 8, 0, 15), jnp.clip(d + 7, 0, 15)

    def local_scores(self, s, i, tab_ref):
        sub = lax.broadcasted_iota(jnp.int32, s.shape, 0)
        lan = lax.broadcasted_iota(jnp.int32, s.shape, 1)
        first = self.first_block(i)
        if self.mode == "swa":
            diff = (i * ATT_BLOCK + sub) - (first * ATT_BLOCK + lan)
            return jnp.where(jnp.abs(diff) <= SWA_WINDOW, s, -jnp.inf)
        rows = 2 * self.nx
        qrow = 2 * i + jnp.where(sub >= GRID_W, 1, 0)
        krow = 2 * first + lan // GRID_W
        r0 = jnp.clip(qrow - NA_WIN_R // 2, 0, rows - NA_WIN_R)
        ok = (krow >= r0) & (krow < r0 + NA_WIN_R)
        tiles = []
        for jj in range(self.nwin):
            top, bot = self.slots(i, jj)
            tiles.append(jnp.concatenate([tab_ref[0, top], tab_ref[0, bot]], axis=0))
        return jnp.where(ok, s + jnp.concatenate(tiles, axis=1), -jnp.inf)


def _win_specs(cfg, Tk):
    G = cfg.G
    qmap = lambda kh, g, i: (i, kh * G + g)
    q_spec = pl.BlockSpec((ATT_BLOCK, HEAD_DIM), qmap)
    kv_spec = pl.BlockSpec((Tk, HEAD_DIM), lambda kh, g, i: (0, kh))
    lse_spec = pl.BlockSpec((1, ATT_BLOCK, LANE), lambda kh, g, i: (kh * G + g, i, 0))
    tab_spec = pl.BlockSpec((1, 16, GRID_W, LANE), lambda kh, g, i: (kh * G + g, 0, 0, 0))
    sink_spec = pl.BlockSpec((1, 8, LANE), lambda kh, g, i: (kh * G + g, 0, 0))
    return q_spec, kv_spec, lse_spec, tab_spec, sink_spec


def _win_scores(cfg, i, q_ref, k_ref, tab_ref):
    dn = (((1,), (1,)), ((), ()))
    n_x = cfg.nx * ATT_BLOCK
    win = pl.ds(pl.multiple_of(cfg.first_block(i) * ATT_BLOCK, ATT_BLOCK), cfg.width)
    ctx = pl.ds(n_x, cfg.n_ctx)
    qb = q_ref[...].astype(_MXU)
    s_loc = lax.dot_general(qb, k_ref[win, :].astype(_MXU), dn, preferred_element_type=F32)
    s_ctx = lax.dot_general(qb, k_ref[ctx, :].astype(_MXU), dn, preferred_element_type=F32)
    return cfg.local_scores(s_loc, i, tab_ref), s_ctx, win, ctx


def _win_fwd_call(cfg, name, q, k, v, tab, sink, riders=()):
    Tq, Tk = q.shape[0], k.shape[0]

    def body(*refs):
        q_ref, k_ref, v_ref = refs[:3]
        extra = refs[3] if (cfg.mode == "na" or cfg.has_sink) else None
        o_ref, lse_ref = refs[-2:]
        i = pl.program_id(2)
        s_loc, s_ctx, win, ctx = _win_scores(cfg, i, q_ref, k_ref, extra if cfg.mode == "na" else None)
        m = jnp.maximum(jnp.max(s_loc, axis=-1, keepdims=True), jnp.max(s_ctx, axis=-1, keepdims=True))
        if cfg.has_sink:
            m = jnp.maximum(m, extra[0, 0:1, 0:1])
        p_loc, p_ctx = jnp.exp2(s_loc - m), jnp.exp2(s_ctx - m)
        l = jnp.sum(p_loc, axis=-1, keepdims=True) + jnp.sum(p_ctx, axis=-1, keepdims=True)
        if cfg.has_sink:
            l = l + jnp.exp2(extra[0, 0:1, 0:1] - m)
        acc = (jnp.dot(p_loc.astype(_MXU), v_ref[win, :].astype(_MXU), preferred_element_type=F32)
               + jnp.dot(p_ctx.astype(_MXU), v_ref[ctx, :].astype(_MXU), preferred_element_type=F32))
        o_ref[...] = acc / l
        lse_ref[0] = jnp.broadcast_to(m + jnp.log2(l), (ATT_BLOCK, LANE))

    q_spec, kv_spec, lse_spec, tab_spec, sink_spec = _win_specs(cfg, Tk)
    in_specs, args = [q_spec, kv_spec, kv_spec], [q, k, v]
    if cfg.mode == "na":
        in_specs.append(tab_spec)
        args.append(tab)
    if cfg.has_sink:
        in_specs.append(sink_spec)
        args.append(sink)
    return _pcall(
        body, name=name, grid=(cfg.Hkv, cfg.G, Tq // ATT_BLOCK), in_specs=in_specs, out_specs=[q_spec, lse_spec],
        out_shape=[jax.ShapeDtypeStruct((Tq, cfg.H * HEAD_DIM), F32), jax.ShapeDtypeStruct((cfg.H, Tq, LANE), F32)],
        args=args, riders=riders)


def _win_bwd_call(cfg, name, q, k, v, tab, sink, o, lse, do, riders=()):
    Tq, Tk = q.shape[0], k.shape[0]
    nq = Tq // ATT_BLOCK
    dn_nt = (((1,), (1,)), ((), ()))
    dn_tn = (((0,), (0,)), ((), ()))

    def body(*refs):
        q_ref, k_ref, v_ref = refs[:3]
        has_extra = cfg.mode == "na" or cfg.has_sink
        extra = refs[3] if has_extra else None
        n = 4 if has_extra else 3
        o_ref, lse_ref, do_ref, dq_ref, dk_ref, dv_ref = refs[n:n + 6]
        dextra = refs[n + 6] if has_extra else None
        g, i = pl.program_id(1), pl.program_id(2)

        @pl.when((g == 0) & (i == 0))
        def _():
            dk_ref[...] = jnp.zeros_like(dk_ref)
            dv_ref[...] = jnp.zeros_like(dv_ref)

        if has_extra:
            @pl.when(i == 0)
            def _():
                dextra[...] = jnp.zeros_like(dextra)

        s_loc, s_ctx, win, ctx = _win_scores(cfg, i, q_ref, k_ref, extra if cfg.mode == "na" else None)
        lse = lse_ref[0][:, 0:1]
        p_loc, p_ctx = jnp.exp2(s_loc - lse), jnp.exp2(s_ctx - lse)
        do_f = do_ref[...]
        do_b = do_f.astype(_MXU)
        delta = jnp.sum(do_f * o_ref[...], axis=-1, keepdims=True)
        dp_loc = lax.dot_general(do_b, v_ref[win, :].astype(_MXU), dn_nt, preferred_element_type=F32)
        dp_ctx = lax.dot_general(do_b, v_ref[ctx, :].astype(_MXU), dn_nt, preferred_element_type=F32)
        ds_loc, ds_ctx = p_loc * (dp_loc - delta), p_ctx * (dp_ctx - delta)
        if cfg.mode == "na":
            for jj in range(cfg.nwin):
                top, bot = cfg.slots(i, jj)
                tile = ds_loc[:, jj * ATT_BLOCK:(jj + 1) * ATT_BLOCK]
                dextra[0, top] += tile[0:GRID_W, :]
                dextra[0, bot] += tile[GRID_W:, :]
        if cfg.has_sink:
            p_sink = jnp.exp2(extra[0, 0:1, 0:1] - lse)
            lane0 = ((lax.broadcasted_iota(jnp.int32, (8, LANE), 0) == 0)
                     & (lax.broadcasted_iota(jnp.int32, (8, LANE), 1) == 0))
            dextra[0] += jnp.where(lane0, -jnp.sum(p_sink * delta), 0.0)
        dsb_loc, dsb_ctx = ds_loc.astype(_MXU), ds_ctx.astype(_MXU)
        qb = q_ref[...].astype(_MXU)
        dq_ref[...] = cfg.scale * (jnp.dot(dsb_loc, k_ref[win, :].astype(_MXU), preferred_element_type=F32)
                                   + jnp.dot(dsb_ctx, k_ref[ctx, :].astype(_MXU), preferred_element_type=F32))
        dk_ref[win, :] += lax.dot_general(dsb_loc, qb, dn_tn, preferred_element_type=F32)
        dk_ref[ctx, :] += lax.dot_general(dsb_ctx, qb, dn_tn, preferred_element_type=F32)
        dv_ref[win, :] += lax.dot_general(p_loc.astype(_MXU), do_b, dn_tn, preferred_element_type=F32)
        dv_ref[ctx, :] += lax.dot_general(p_ctx.astype(_MXU), do_b, dn_tn, preferred_element_type=F32)

        @pl.when((g == cfg.G - 1) & (i == nq - 1))
        def _():
            dk_ref[...] = dk_ref[...] * (1.0 / LOG2E)

    q_spec, kv_spec, lse_spec, tab_spec, sink_spec = _win_specs(cfg, Tk)
    in_specs, args = [q_spec, kv_spec, kv_spec], [q, k, v]
    out_specs = [q_spec, kv_spec, kv_spec]
    out_shape = [jax.ShapeDtypeStruct(q.shape, F32), jax.ShapeDtypeStruct(k.shape, F32),
                 jax.ShapeDtypeStruct(v.shape, F32)]
    if cfg.mode == "na":
        in_specs.append(tab_spec)
        args.append(tab)
        out_specs.append(tab_spec)
        out_shape.append(jax.ShapeDtypeStruct(tab.shape, F32))
    if cfg.has_sink:
        in_specs.append(sink_spec)
        args.append(sink)
        out_specs.append(sink_spec)
        out_shape.append(jax.ShapeDtypeStruct(sink.shape, F32))
    in_specs += [q_spec, lse_spec, q_spec]
    args += [o, lse, do]
    return _pcall(body, name=name, grid=(cfg.Hkv, cfg.G, nq), in_specs=in_specs, out_specs=out_specs,
                  out_shape=out_shape, args=args, riders=riders)


def _attn_specs(cfg, Tk):
    G, tq = cfg.G, cfg.tq
    qmap = lambda kh, g, i: (i, kh * G + g)
    q_spec = pl.BlockSpec((tq, cfg.dqk), qmap)
    k_spec = pl.BlockSpec((Tk, cfg.dqk), lambda kh, g, i: (0, kh))
    v_spec = pl.BlockSpec((Tk, cfg.dv), lambda kh, g, i: (0, kh))
    o_spec = pl.BlockSpec((tq, cfg.dv), qmap)
    lse_spec = pl.BlockSpec((1, tq, LANE), lambda kh, g, i: (kh * G + g, i, 0))
    sink_spec = pl.BlockSpec((1, 8, LANE), lambda kh, g, i: (kh * G + g, 0, 0))
    return q_spec, k_spec, v_spec, o_spec, lse_spec, sink_spec


def _attn_fwd_call(cfg, name, q, k, v, sink, riders=()):
    Tq, Tk = q.shape[0], k.shape[0]
    nq, nkv = Tq // cfg.tq, Tk // cfg.tk
    dn = (((1,), (1,)), ((), ()))

    def body(*refs):
        q_ref, k_ref, v_ref = refs[:3]
        sink_ref = refs[3] if cfg.has_sink else None
        o_ref, lse_ref = refs[-2:]
        qb = q_ref[...].astype(_MXU)
        m = jnp.full((cfg.tq, 1), NEG, F32)
        l = jnp.zeros((cfg.tq, 1), F32)
        acc = jnp.zeros((cfg.tq, cfg.dv), F32)
        for j in range(nkv):
            rows = pl.ds(j * cfg.tk, cfg.tk)
            s = lax.dot_general(qb, k_ref[rows, :].astype(_MXU), dn, preferred_element_type=F32)
            m_new = jnp.maximum(m, jnp.max(s, axis=-1, keepdims=True))
            alpha = jnp.exp2(m - m_new)
            p = jnp.exp2(s - m_new)
            l = alpha * l + jnp.sum(p, axis=-1, keepdims=True)
            acc = alpha * acc + jnp.dot(p.astype(_MXU), v_ref[rows, :].astype(_MXU), preferred_element_type=F32)
            m = m_new
        if cfg.has_sink:
            sk = sink_ref[0, 0:1, 0:1]
            m_new = jnp.maximum(m, sk)
            alpha = jnp.exp2(m - m_new)
            l = alpha * l + jnp.exp2(sk - m_new)
            acc = acc * alpha
            m = m_new
        o_ref[...] = acc / l
        lse_ref[0] = jnp.broadcast_to(m + jnp.log2(l), (cfg.tq, LANE))

    q_spec, k_spec, v_spec, o_spec, lse_spec, sink_spec = _attn_specs(cfg, Tk)
    in_specs, args = [q_spec, k_spec, v_spec], [q, k, v]
    if cfg.has_sink:
        in_specs.append(sink_spec)
        args.append(sink)
    return _pcall(
        body, name=name, grid=(cfg.Hkv, cfg.G, nq), in_specs=in_specs, out_specs=[o_spec, lse_spec],
        out_shape=[jax.ShapeDtypeStruct((Tq, cfg.H * cfg.dv), F32), jax.ShapeDtypeStruct((cfg.H, Tq, LANE), F32)],
        args=args, riders=riders)


def _attn_bwd_call(cfg, name, q, k, v, sink, o, lse, do, riders=()):
    Tq, Tk = q.shape[0], k.shape[0]
    nq, nkv = Tq // cfg.tq, Tk // cfg.tk
    tq, tk = cfg.tq, cfg.tk
    dn_nt = (((1,), (1,)), ((), ()))
    dn_tn = (((0,), (0,)), ((), ()))

    def body(*refs):
        q_ref, k_ref, v_ref = refs[:3]
        n = 3
        sink_ref = dsink_ref = None
        if cfg.has_sink:
            sink_ref, n = refs[n], n + 1
        o_ref, lse_ref, do_ref, dq_ref, dk_ref, dv_ref = refs[n:n + 6]
        n += 6
        if cfg.has_sink:
            dsink_ref, n = refs[n], n + 1
        g, i = pl.program_id(1), pl.program_id(2)

        @pl.when((g == 0) & (i == 0))
        def _():
            dk_ref[...] = jnp.zeros_like(dk_ref)
            dv_ref[...] = jnp.zeros_like(dv_ref)

        if cfg.has_sink:
            @pl.when(i == 0)
            def _():
                dsink_ref[...] = jnp.zeros_like(dsink_ref)

        do_f = do_ref[...]
        do_b = do_f.astype(_MXU)
        qb = q_ref[...].astype(_MXU)
        lse = lse_ref[0][:, 0:1]
        delta = jnp.sum(do_f * o_ref[...], axis=-1, keepdims=True)
        if cfg.has_sink:
            p_sink = jnp.exp2(sink_ref[0, 0:1, 0:1] - lse)
            lane0 = ((lax.broadcasted_iota(jnp.int32, (8, LANE), 0) == 0)
                     & (lax.broadcasted_iota(jnp.int32, (8, LANE), 1) == 0))
            dsink_ref[0] += jnp.where(lane0, -jnp.sum(p_sink * delta), 0.0)
        dq = jnp.zeros((tq, cfg.dqk), F32)
        for j in range(nkv):
            rows = pl.ds(j * tk, tk)
            kb, vb = k_ref[rows, :].astype(_MXU), v_ref[rows, :].astype(_MXU)
            s = lax.dot_general(qb, kb, dn_nt, preferred_element_type=F32)
            p = jnp.exp2(s - lse)
            dp = lax.dot_general(do_b, vb, dn_nt, preferred_element_type=F32)
            dsb = (p * (dp - delta)).astype(_MXU)
            dq = dq + jnp.dot(dsb, kb, preferred_element_type=F32)
            dk_ref[rows, :] += lax.dot_general(dsb, qb, dn_tn, preferred_element_type=F32)
            dv_ref[rows, :] += lax.dot_general(p.astype(_MXU), do_b, dn_tn, preferred_element_type=F32)
        dq_ref[...] = dq * cfg.scale

        @pl.when((g == cfg.G - 1) & (i == nq - 1))
        def _():
            dk_ref[...] = dk_ref[...] * (1.0 / LOG2E)

    q_spec, k_spec, v_spec, o_spec, lse_spec, sink_spec = _attn_specs(cfg, Tk)
    in_specs, args = [q_spec, k_spec, v_spec], [q, k, v]
    if cfg.has_sink:
        in_specs.append(sink_spec)
        args.append(sink)
    in_specs += [o_spec, lse_spec, o_spec]
    args += [o, lse, do]
    out_specs = [q_spec, k_spec, v_spec]
    out_shape = [jax.ShapeDtypeStruct(q.shape, F32), jax.ShapeDtypeStruct(k.shape, F32),
                 jax.ShapeDtypeStruct(v.shape, F32)]
    if cfg.has_sink:
        out_specs.append(sink_spec)
        out_shape.append(jax.ShapeDtypeStruct(sink.shape, F32))
    return _pcall(body, name=name, grid=(cfg.Hkv, cfg.G, nq), in_specs=in_specs, out_specs=out_specs,
                  out_shape=out_shape, args=args, riders=riders)


def _make_attention(tag, cfg):
    windowed = isinstance(cfg, _WinCfg)

    def run_fwd(q, k, v, tab, sink, gin):
        if windowed:
            return _win_fwd_call(cfg, tag + "_fwd", q, k, v, tab, sink, _riders(gin, False))
        return _attn_fwd_call(cfg, tag + "_fwd", q, k, v, sink, _riders(gin, False))

    def prepared(q, k, v, tab, sink):
        q2 = (q * (cfg.scale * LOG2E)).astype(_MXU)
        tab2 = None if tab is None else tab * LOG2E
        sink2 = None if sink is None else sink * LOG2E
        return q2, k.astype(_MXU), v.astype(_MXU), tab2, sink2

    @jax.custom_vjp
    def attn(q, k, v, tab, sink, gin, tok):
        o, _, *got = run_fwd(*prepared(q, k, v, tab, sink), gin)
        return o, tuple(got), tok

    def fwd(q, k, v, tab, sink, gin, tok):
        q2, kb, vb, tab2, sink2 = prepared(q, k, v, tab, sink)
        o, lse, *got = run_fwd(q2, kb, vb, tab2, sink2, gin)
        return (o, tuple(got), tok), (q2, kb, vb, tab2, sink2, gin, o, lse)

    def bwd(res, cts):
        q, k, v, tab, sink, gin, o, lse = res
        do, _, dtok = cts
        if windowed:
            outs = _win_bwd_call(cfg, tag + "_bwd", q, k, v, tab, sink, o, lse, do, _riders(dtok, True))
        else:
            outs = _attn_bwd_call(cfg, tag + "_bwd", q, k, v, sink, o, lse, do, _riders(dtok, True))
        dq, dk, dv = outs[:3]
        rest = outs[3:]
        dtab = rest.pop(0) if tab is not None else None
        dsink = rest.pop(0) if sink is not None else None
        return dq, dk, dv, dtab, dsink, _nones(gin), tuple(rest)

    attn.defvjp(fwd, bwd)
    return attn


def _pick_block(n, prefs):
    for p in prefs:
        if n % p == 0:
            return p
    return n


def _na_table(rpb):
    cq = np.arange(GRID_W)
    dcol = np.clip(cq[None, :] - cq[:, None] + NA_WIN_C - 1, 0, 2 * NA_WIN_C - 2)
    onehot = (dcol[:, :, None] == np.arange(2 * NA_WIN_C - 1)[None, None, :]).astype(np.float32)
    c0 = np.clip(cq - NA_WIN_C // 2, 0, GRID_W - NA_WIN_C)
    col_in = (cq[None, :] >= c0[:, None]) & (cq[None, :] < c0[:, None] + NA_WIN_C)
    tz = jnp.einsum("hrj,qkj->hrqk", rpb, jnp.asarray(onehot), precision=lax.Precision.HIGHEST)
    tz = jnp.where(jnp.asarray(col_in)[None, None], tz, NEG)
    zero = jnp.zeros_like(tz[:, :1])
    tzp = jnp.concatenate([zero, tz, zero], axis=1)
    return jnp.concatenate([tzp[:, 0:16], tzp[:, 1:17]], axis=-1)


def _sink_block(sink):
    return jnp.broadcast_to(sink[:, None, None], (sink.shape[0], 8, LANE))


def _make_split(bounds, axis):
    @jax.custom_vjp
    def split(t):
        return tuple(lax.slice_in_dim(t, a, b, axis=axis) for a, b in bounds)

    def fwd(t):
        return split(t), None

    def bwd(_, cts):
        return (jnp.concatenate(cts, axis=axis),)

    split.defvjp(fwd, bwd)
    return split


_IN_GROUPS = ((0, 512), (512, 1024), (1024, 1536), (1536, 2048), (2048, 2304), (2304, 2560), (2560, 2944),
              (2944, 3072), (3072, 3200), (3200, 3712), (3712, 3968), (3968, 4224))


def _seg2(vx, vc, nseg):
    rows = [vx, vc][:nseg]
    return jnp.stack(rows)[:, None, :]


_BIG_COL = ("w_in", "mla_w_uq", "mla_w_ukv", "ffn_w_gate", "ffn_w_up")
_BIG_ROW = ("w_out", "ffn_w_down")
_IN_GROUP = ("w_in", "mla_w_uq", "mla_w_ukv")
_PIECES = {"w_in": 4, "ffn_w_down": 2}


def _n_pieces(name):
    return _PIECES.get(name, 1)


def _shard_view(name, w):
    return jnp.swapaxes(w, 1, 2) if name in _BIG_COL else w


def _cut(name, a):
    return tuple(jnp.split(a, _n_pieces(name), axis=-1))


def _full_weight(name, pieces):
    g = jnp.concatenate(pieces, axis=-1) if len(pieces) > 1 else pieces[0]
    full = g.reshape(N_DEV * g.shape[1], g.shape[2])
    if name in _BIG_ROW:
        return full
    full = full.T
    if name == "w_in":
        K = full.shape[0]
        full = jnp.concatenate([full[:, :KPE_END], jnp.zeros((K, 64), full.dtype), full[:, KPE_END:]], axis=-1)
    return full


def _grad_parts(name):
    def to_parts(dw):
        if name == "w_in":
            dw = jnp.concatenate([dw[:, :KPE_END], dw[:, KPE_END + 64:]], axis=-1)
        if name in _BIG_COL:
            dw = dw.T
        return _cut(name, dw.reshape(N_DEV, dw.shape[0] // N_DEV, dw.shape[1]).astype(BF16))

    return to_parts


_RIDE_FWD = {
    "attn_a": ((0, "w_out", 0),), "attn_b": ((1, "w_in", 0), (1, "w_in", 1)), "attn_c": ((0, "ffn_w_gate", 0),),
    "attn_d": ((0, "ffn_w_up", 0),), "gate": ((0, "ffn_w_down", 0),), "up": ((0, "ffn_w_down", 1),),
    "conv": ((1, "mla_w_uq", 0), (1, "mla_w_ukv", 0)), "down": ((1, "w_in", 2), (1, "w_in", 3)),
}
_RIDE_BWD = {
    "attn_a": ((0, "w_out", 0),), "attn_b": ((1, "mla_w_uq", 0), (1, "mla_w_ukv", 0)),
    "attn_c": ((0, "ffn_w_gate", 0),), "attn_d": ((0, "ffn_w_up", 0),), "gate": ((0, "ffn_w_down", 0),),
    "up": ((0, "ffn_w_down", 1),), "conv": ((1, "w_in", 0), (1, "w_in", 1)), "down": ((1, "w_in", 2), (1, "w_in", 3)),
}


def _local_forward(x, ctx, modx, modc, p, full, shards, toks):
    S, D = x.shape
    C = ctx.shape[0]
    depth = len(full)
    ride = shards is not None
    have = [dict(f) for f in full]
    landing = {}
    sinks = {(l, n): list(toks[l][n]) for l in range(depth) for n in toks[l]}
    alpha = (2 * depth) ** 0.25
    T0 = S + C
    nx = S // ATT_BLOCK

    def riding(table, unit, l):
        return [(l + dl, n, k) for dl, n, k in table[unit] if l + dl < depth] if ride else []

    def gin(unit, l):
        return tuple(shards[ll][n][k] for ll, n, k in riding(_RIDE_FWD, unit, l))

    def tok(unit, l):
        return tuple(sinks[(ll, n)][k] for ll, n, k in riding(_RIDE_BWD, unit, l))

    def landed(result, unit, l):
        out, got, tk = result
        for (ll, n, k), g in zip(riding(_RIDE_FWD, unit, l), got):
            landing.setdefault((ll, n), {})[k] = g
            if len(landing[(ll, n)]) == _n_pieces(n):
                have[ll][n] = _full_weight(n, [landing[(ll, n)][j] for j in range(_n_pieces(n))])
        for (ll, n, k), s in zip(riding(_RIDE_BWD, unit, l), tk):
            sinks[(ll, n)][k] = s
        return out

    sc = HEAD_DIM ** -0.5
    sc_mla = (MLA_NOPE + MLA_ROPE) ** -0.5
    tq_full = _pick_block(S, (256, 128))
    tk_all = _pick_block(T0, (2176, 640, 512, 256, 128))
    tctx = _pick_block(C, (256, 128))
    rope128 = _rope_tables(S, T0, 128, 32)
    rope64 = _rope_tables(S, T0, 64, 16)

    xs = jnp.concatenate([x, ctx], axis=0)
    for l in range(depth):
        last = l == depth - 1
        t = "l%d_" % l
        lin = lambda a, name, unit=None, out=F32: _make_linear(t + name, _grad_parts(name), out)(
            a, have[l][name], tuple(sinks[(l, name)]), gin(unit, l) if unit else (), tok(unit, l) if unit else ())
        mx, mc = modx[l], modc[l]

        h = _make_modulate(t + "mod1", S)(xs, _seg2(mx[0], mc[0], 2), _seg2(mx[1], mc[1], 2))
        pj = lin(h, "w_in")[0]
        qa, ka, va, qb, kb, vb, cq, ckv, kpe, qd, kd, vd = _make_split(_IN_GROUPS, 1)(pj)
        rows = lambda v_: _make_split(((0, S), (S, T0)), 0)(v_)
        att = lambda unit, cfg, q_, k_, v_, tab_, s_: landed(
            _make_attention(t + unit, cfg)(q_, k_, v_, tab_, s_, gin(unit, l), tok(unit, l)), unit, l)
        tab = _na_table(p["na_rpb"][l])
        cfg_a = _WinCfg("na", NA_HEADS, NA_HEADS, sc, nx, C)
        qa_x, qa_c = rows(qa)
        oa = att("attn_a", cfg_a, qa_x, ka, va, tab, None)
        qb = _make_rope(t + "rope_b", 32)(qb, *rope128)
        kb = _make_rope(t + "rope_bk", 32)(kb, *rope128)
        snk = _sink_block(p["swa_sink"][l])
        cfg_b = _WinCfg("swa", SWA_HEADS, SWA_KV_HEADS, sc, nx, C)
        qb_x, qb_c = rows(qb)
        ob = att("attn_b", cfg_b, qb_x, kb, vb, None, snk)
        cq = _make_rmsnorm(t + "rms_cq")(cq, p["mla_q_norm"][l][None, :])
        ckv = _make_rmsnorm(t + "rms_ckv")(ckv, p["mla_kv_norm"][l][None, :])
        qh = lin(cq, "mla_w_uq")[0].reshape(T0, MLA_HEADS, MLA_NOPE + MLA_ROPE)
        kvh = lin(ckv, "mla_w_ukv")[0].reshape(T0, MLA_HEADS, MLA_NOPE + MLA_V)
        qpe = jnp.pad(qh[:, :, MLA_NOPE:], ((0, 0), (0, 0), (0, LANE - MLA_ROPE))).reshape(T0, MLA_HEADS * LANE)
        qpe = _make_rope(t + "rope_cq", 16)(qpe, *rope64).reshape(T0, MLA_HEADS, LANE)
        kpe = _make_rope(t + "rope_ck", 16)(kpe, *rope64)
        qc = jnp.concatenate([qh[:, :, :MLA_NOPE], qpe], axis=-1).reshape(T0, MLA_HEADS * 2 * LANE)
        kc = jnp.concatenate([kvh[:, :, :MLA_NOPE], jnp.broadcast_to(kpe[:, None, :], (T0, MLA_HEADS, LANE))],
                             axis=-1).reshape(T0, MLA_HEADS * 2 * LANE)
        vc = kvh[:, :, MLA_NOPE:].reshape(T0, MLA_HEADS * MLA_V)
        cfg_c = _AttnCfg(MLA_HEADS, MLA_HEADS, 2 * LANE, MLA_V, sc_mla, tq_full, tk_all)
        qc_x, qc_c = rows(qc)
        oc = att("attn_c", cfg_c, qc_x, kc, vc, None, None)
        qd = _make_rmsnorm(t + "rms_dq")(qd, p["gqa_q_norm"][l][None, :])
        kd = _make_rmsnorm(t + "rms_dk")(kd, p["gqa_k_norm"][l][None, :])
        qd = _make_rope(t + "rope_dq", 32)(qd, *rope128)
        kd = _make_rope(t + "rope_dk", 32)(kd, *rope128)
        cfg_d = _AttnCfg(GQA_HEADS, GQA_KV_HEADS, 128, 128, sc, tq_full, tk_all)
        qd_x, qd_c = rows(qd)
        od = att("attn_d", cfg_d, qd_x, kd, vd, None, None)
        mix = jnp.concatenate([oa, ob, oc, od], axis=1)
        if not last:
            full_c = lambda H, Hkv, dqk, s, sink: _AttnCfg(H, Hkv, dqk, 128, s, tctx, tctx, has_sink=sink)
            ctx_att = lambda name, cfg, q_, k_, v_, s_: _make_attention(t + name, cfg)(
                q_, k_[S:], v_[S:], None, s_, (), ())[0]
            oa_c = ctx_att("ctx_a", full_c(4, 4, 128, sc, False), qa_c, ka, va, None)
            ob_c = ctx_att("ctx_b", full_c(4, 2, 128, sc, True), qb_c, kb, vb, snk)
            oc_c = ctx_att("ctx_c", full_c(4, 4, 256, sc_mla, False), qc_c, kc, vc, None)
            od_c = ctx_att("ctx_d", full_c(4, 2, 128, sc, False), qd_c, kd, vd, None)
            mix = jnp.concatenate([mix, jnp.concatenate([oa_c, ob_c, oc_c, od_c], axis=1)], axis=0)
            res, nseg = xs, 2
        else:
            res, nseg = xs[:S], 1
        y = lin(mix, "w_out")[0]
        x1 = _make_resid_ln(t + "ln1", S, alpha)(res, y, _seg2(mx[2], mc[2], nseg), p["ln1_g"][l][None, :],
                                                 p["ln1_b"][l][None, :])
        h2 = _make_modulate(t + "mod2", S)(x1, _seg2(mx[3], mc[3], nseg), _seg2(mx[4], mc[4], nseg))
        gp = landed(lin(h2, "ffn_w_gate", "gate", _ACT), "gate", l)
        up = landed(lin(h2, "ffn_w_up", "up", _ACT), "up", l)
        z = landed(_make_conv_gate(t + "conv", S)(gp, up, p["ffn_conv_w"][l], p["ffn_conv_b"][l][None, :],
                                                  gin("conv", l), tok("conv", l)), "conv", l)
        f = landed(lin(z, "ffn_w_down", "down"), "down", l)
        xs = _make_resid_ln(t + "ln2", S, alpha)(x1, f, _seg2(mx[5], mc[5], nseg), p["ln2_g"][l][None, :],
                                                 p["ln2_b"][l][None, :])
    return xs


def _adamw(w, parts, m, v, name):
    R, C = w.shape
    P = parts.shape[0]
    c1 = 1.0 - ADAM_B1 ** ADAM_STEP
    c2 = 1.0 - ADAM_B2 ** ADAM_STEP
    per_row = C * (4 * 14 + 2 * P * parts.dtype.itemsize)
    cands = _divisors(R, 8, R)
    fitting = [d for d in cands if d * per_row <= 24 * 1024 * 1024]
    tr = max(fitting) if fitting else min(cands)

    def body(w_ref, p_ref, m_ref, v_ref, g_ref, d_ref, nm_ref, nv_ref):
        g = p_ref[0].astype(F32)
        for k in range(1, P):
            g = g + p_ref[k].astype(F32)
        mm = ADAM_B1 * m_ref[...] + (1.0 - ADAM_B1) * g
        vv = ADAM_B2 * v_ref[...] + (1.0 - ADAM_B2) * (g * g)
        g_ref[...] = g
        nm_ref[...] = mm
        nv_ref[...] = vv
        d_ref[...] = -ADAM_LR * ((mm / c1) / (jnp.sqrt(vv / c2) + ADAM_EPS) + ADAM_WD * w_ref[...])

    blk = pl.BlockSpec((tr, C), lambda i: (i, 0))
    pblk = pl.BlockSpec((P, tr, C), lambda i: (0, i, 0))
    sh = jax.ShapeDtypeStruct((R, C), F32)
    return pl.pallas_call(body, name=name, grid=(R // tr,), in_specs=[blk, pblk, blk, blk],
                          out_specs=[blk, blk, blk, blk], out_shape=[sh, sh, sh, sh],
                          compiler_params=_cparams(1))(w, parts, m, v)


def _adamw_nd(w, parts, m, v, name):
    shape = w.shape
    C = shape[-1]
    R = int(np.prod(shape[:-1])) if len(shape) > 1 else 1
    outs = _adamw(w.reshape(R, C), parts.reshape(parts.shape[0], R, C), m.reshape(R, C), v.reshape(R, C), name)
    return [o.reshape(shape) for o in outs]


def _adamw_layers(w, parts, m, v, name, riders=()):
    L, R, C = w.shape
    P = parts[0].shape[0]
    c1 = 1.0 - ADAM_B1 ** ADAM_STEP
    c2 = 1.0 - ADAM_B2 ** ADAM_STEP
    per_row = C * (4 * 14 + 2 * L * P * parts[0].dtype.itemsize)
    cands = _divisors(R, 8, R)
    fitting = [d for d in cands if d * per_row <= 24 * 1024 * 1024]
    tr = max(fitting) if fitting else min(cands)
    nt = R // tr

    def body(*refs):
        w_ref, p_refs, (m_ref, v_ref, g_ref, d_ref, nm_ref, nv_ref) = refs[0], refs[1:1 + L], refs[1 + L:]
        layer = pl.program_id(0)
        for k in range(L):
            @pl.when(layer == k)
            def _(p_ref=p_refs[k]):
                g = p_ref[0].astype(F32)
                for j in range(1, P):
                    g = g + p_ref[j].astype(F32)
                mm = ADAM_B1 * m_ref[0] + (1.0 - ADAM_B1) * g
                vv = ADAM_B2 * v_ref[0] + (1.0 - ADAM_B2) * (g * g)
                g_ref[0] = g
                nm_ref[0] = mm
                nv_ref[0] = vv
                d_ref[0] = -ADAM_LR * ((mm / c1) / (jnp.sqrt(vv / c2) + ADAM_EPS) + ADAM_WD * w_ref[0])

    blk = pl.BlockSpec((1, tr, C), lambda l, i: (l, i, 0))
    pblk = [pl.BlockSpec((P, tr, C), lambda l, i, k=k: (0, jnp.where(l == k, i, jnp.where(l < k, 0, nt - 1)), 0))
            for k in range(L)]
    sh = jax.ShapeDtypeStruct((L, R, C), F32)
    return _pcall(body, name=name, grid=(L, nt), in_specs=[blk] + pblk + [blk, blk],
                  out_specs=[blk, blk, blk, blk], out_shape=[sh, sh, sh, sh], args=[w, *parts, m, v], riders=riders)


def _silu(v):
    return v * (1.0 / (1.0 + jnp.exp(-v)))


def _ada_rows(c_all, c_ctx):
    return jnp.concatenate([c_all, jnp.broadcast_to(c_ctx[None, :], (N_DEV, c_ctx.shape[0]))], axis=0)


def _silu_rows(rows, name):
    def body(r_ref, o_ref):
        o_ref[...] = _silu(r_ref[...])

    return pl.pallas_call(body, name=name, out_shape=jax.ShapeDtypeStruct(rows.shape, F32))(rows)


_SMALL = ("c_ctx", "na_rpb", "swa_sink", "mla_q_norm", "mla_kv_norm", "gqa_q_norm", "gqa_k_norm",
          "ln1_g", "ln1_b", "ffn_conv_b", "ln2_g", "ln2_b")
_NAMES = ("c_ctx", "w_ada", "b_ada", "w_in", "na_rpb", "swa_sink", "mla_q_norm", "mla_kv_norm", "mla_w_uq",
          "mla_w_ukv", "gqa_q_norm", "gqa_k_norm", "w_out", "ln1_g", "ln1_b", "ffn_w_gate", "ffn_w_up",
          "ffn_conv_w", "ffn_conv_b", "ffn_w_down", "ln2_g", "ln2_b")


def _full_from_cols(g, pad_in=False):
    _, L, K, n = g.shape
    full = jnp.transpose(g, (1, 2, 0, 3)).reshape(L, K, N_DEV * n)
    if pad_in:
        full = jnp.concatenate([full[:, :, :KPE_END], jnp.zeros((L, K, 64), full.dtype), full[:, :, KPE_END:]], axis=-1)
    return full


def _pack_small(tree, extra=None):
    flat = [tree[n].reshape(-1) for n in _SMALL]
    flat.append(jnp.zeros((1,), F32) if extra is None else extra.reshape(1))
    v = jnp.concatenate(flat)
    n = v.shape[0]
    padded = -(-n // 1024) * 1024
    return jnp.pad(v, (0, padded - n)).reshape(padded // LANE, LANE)


def _unpack_small(mat, like):
    v = mat.reshape(-1)
    out, o = {}, 0
    for n in _SMALL:
        k = int(np.prod(like[n].shape))
        out[n] = v[o:o + k].reshape(like[n].shape)
        o += k
    return out, v[o]


def kernel(x, c, ctx, c_ctx, w_ada, b_ada, w_in, na_rpb, swa_sink, mla_q_norm, mla_kv_norm, mla_w_uq, mla_w_ukv, gqa_q_norm, gqa_k_norm, w_out, ln1_g, ln1_b, ffn_w_gate, ffn_w_up, ffn_conv_w, ffn_conv_b, ffn_w_down, ln2_g, ln2_b, loss_target, m_c_ctx, m_w_ada, m_b_ada, m_w_in, m_na_rpb, m_swa_sink, m_mla_q_norm, m_mla_kv_norm, m_mla_w_uq, m_mla_w_ukv, m_gqa_q_norm, m_gqa_k_norm, m_w_out, m_ln1_g, m_ln1_b, m_ffn_w_gate, m_ffn_w_up, m_ffn_conv_w, m_ffn_conv_b, m_ffn_w_down, m_ln2_g, m_ln2_b, v_c_ctx, v_w_ada, v_b_ada, v_w_in, v_na_rpb, v_swa_sink, v_mla_q_norm, v_mla_kv_norm, v_mla_w_uq, v_mla_w_ukv, v_gqa_q_norm, v_gqa_k_norm, v_w_out, v_ln1_g, v_ln1_b, v_ffn_w_gate, v_ffn_w_up, v_ffn_conv_w, v_ffn_conv_b, v_ffn_w_down, v_ln2_g, v_ln2_b):
    W = dict(c_ctx=c_ctx, w_ada=w_ada, b_ada=b_ada, w_in=w_in, na_rpb=na_rpb, swa_sink=swa_sink,
             mla_q_norm=mla_q_norm, mla_kv_norm=mla_kv_norm, mla_w_uq=mla_w_uq, mla_w_ukv=mla_w_ukv,
             gqa_q_norm=gqa_q_norm, gqa_k_norm=gqa_k_norm, w_out=w_out, ln1_g=ln1_g, ln1_b=ln1_b,
             ffn_w_gate=ffn_w_gate, ffn_w_up=ffn_w_up, ffn_conv_w=ffn_conv_w, ffn_conv_b=ffn_conv_b,
             ffn_w_down=ffn_w_down, ln2_g=ln2_g, ln2_b=ln2_b)
    M = dict(c_ctx=m_c_ctx, w_ada=m_w_ada, b_ada=m_b_ada, w_in=m_w_in, na_rpb=m_na_rpb, swa_sink=m_swa_sink,
             mla_q_norm=m_mla_q_norm, mla_kv_norm=m_mla_kv_norm, mla_w_uq=m_mla_w_uq, mla_w_ukv=m_mla_w_ukv,
             gqa_q_norm=m_gqa_q_norm, gqa_k_norm=m_gqa_k_norm, w_out=m_w_out, ln1_g=m_ln1_g, ln1_b=m_ln1_b,
             ffn_w_gate=m_ffn_w_gate, ffn_w_up=m_ffn_w_up, ffn_conv_w=m_ffn_conv_w, ffn_conv_b=m_ffn_conv_b,
             ffn_w_down=m_ffn_w_down, ln2_g=m_ln2_g, ln2_b=m_ln2_b)
    V = dict(c_ctx=v_c_ctx, w_ada=v_w_ada, b_ada=v_b_ada, w_in=v_w_in, na_rpb=v_na_rpb, swa_sink=v_swa_sink,
             mla_q_norm=v_mla_q_norm, mla_kv_norm=v_mla_kv_norm, mla_w_uq=v_mla_w_uq, mla_w_ukv=v_mla_w_ukv,
             gqa_q_norm=v_gqa_q_norm, gqa_k_norm=v_gqa_k_norm, w_out=v_w_out, ln1_g=v_ln1_g, ln1_b=v_ln1_b,
             ffn_w_gate=v_ffn_w_gate, ffn_w_up=v_ffn_w_up, ffn_conv_w=v_ffn_conv_w, ffn_conv_b=v_ffn_conv_b,
             ffn_w_down=v_ffn_w_down, ln2_g=v_ln2_g, ln2_b=v_ln2_b)
    L, D, n_ada = w_ada.shape
    me = 4 * lax.axis_index("x") + 2 * lax.axis_index("y") + lax.axis_index("c")
    xs, ctxs, tgt = x[0], ctx[0], loss_target[0]

    c_all = _exchange(c, False, "gather_c").reshape(N_DEV, D)
    a_rows = _silu_rows(_ada_rows(c_all, c_ctx), "ada_silu")
    b_mine = lax.dynamic_slice(b_ada, (0, me * n_ada), (L, n_ada))
    mod_mine = jnp.stack([_matmul(a_rows, w_ada[l], "nn", "ada_fwd_l%d" % l) + b_mine[l][None, :] for l in range(L)])
    mod_all = _exchange(mod_mine, False, "gather_mod")
    mod_all = jnp.transpose(mod_all, (1, 2, 0, 3)).reshape(L, 2 * N_DEV, N_DEV * n_ada)
    modx = lax.dynamic_slice(mod_all, (0, me, 0), (L, 1, 6 * D)).reshape(L, 6, D)
    modc = mod_all[:, N_DEV].reshape(L, 6, D)

    big = _BIG_COL + _BIG_ROW
    views = {n: _shard_view(n, W[n]) for n in big}
    shards = [{n: _cut(n, views[n][l].astype(BF16)) for n in big} for l in range(L)]
    full = [{n: _full_weight(n, [_gather_by_chip(views[n][0].astype(BF16), "gather_l0_" + n)]) for n in _IN_GROUP}]
    full += [{} for _ in range(1, L)]
    conv_w_full = _full_from_cols(_exchange(ffn_conv_w, False, "gather_conv_w"))

    toks = [{n: tuple(jnp.zeros((N_DEV,) + s.shape, BF16) for s in shards[l][n]) for n in big} for l in range(L)]
    small = {n: W[n] for n in _SMALL if n != "c_ctx"}
    small["ffn_conv_w"] = conv_w_full

    def forward(xv, mxv, mcv, sm, tk):
        return _local_forward(xv, ctxs, mxv, mcv, sm, full, shards, tk)

    y, backward = jax.vjp(forward, xs, modx, modc, small, toks)
    loss_local, dy = _loss_and_grad(y, tgt, "loss_head")
    grad_x, dmodx, dmodc, dsmall, dtoks = backward(dy)

    dmods = _exchange(jnp.stack([dmodx.reshape(L, 6 * D), dmodc.reshape(L, 6 * D)]), False, "gather_dmod")
    dm_rows = jnp.concatenate([dmods[:, 0], dmods[:, 1]], axis=0)
    dm_rows = jnp.transpose(dm_rows, (1, 0, 2))
    dm_mine = lax.dynamic_slice(dm_rows, (0, 0, me * n_ada), (L, 2 * N_DEV, n_ada))
    g_w_ada = [_matmul(a_rows, dm_mine[l], "tn", "ada_dw_l%d" % l) for l in range(L)]
    d_rows = sum(_matmul(dm_mine[l], w_ada[l], "nt", "ada_da_l%d" % l) for l in range(L))
    sig = 1.0 / (1.0 + jnp.exp(-c_ctx))
    d_c_ctx_part = jnp.sum(d_rows[N_DEV:], axis=0) * (sig * (1.0 + c_ctx * (1.0 - sig)))

    outs = {}
    pieces = [{n: list(dtoks[l][n]) for n in big} for l in range(L)]
    late = {"ffn_w_gate": (("w_in", 0),), "ffn_w_up": (("w_in", 1),), "ffn_w_down": (("w_in", 2),),
            "w_ada": (("w_in", 3),), "w_out": (("mla_w_uq", 0), ("mla_w_ukv", 0))}

    def update(n, w, g_parts, m, v):
        res = _adamw_layers(w, g_parts, m, v, "adamw_" + n, _riders([pieces[0][a][k] for a, k in late.get(n, ())], True))
        for (a, k), got in zip(late.get(n, ()), res[4:]):
            pieces[0][a][k] = got
        return res[:4]

    def whole(l, n):
        return jnp.concatenate(pieces[l][n], axis=-1) if len(pieces[l][n]) > 1 else pieces[l][n][0]

    outs["w_ada"] = update("w_ada", w_ada, [g[None] for g in g_w_ada], M["w_ada"], V["w_ada"])
    for n in [n for n in big if n not in _IN_GROUP] + list(_IN_GROUP):
        res = update(n, views[n], [whole(l, n) for l in range(L)], _shard_view(n, M[n]), _shard_view(n, V[n]))
        outs[n] = [_shard_view(n, r) for r in res]
    dcw = dsmall.pop("ffn_conv_w")
    cw_parts = _exchange(jnp.transpose(dcw.reshape(L, 3, N_DEV, -1), (2, 0, 1, 3)), True, "scatter_conv_w")
    outs["ffn_conv_w"] = _adamw_layers(ffn_conv_w, [cw_parts[:, l] for l in range(L)], M["ffn_conv_w"],
                                       V["ffn_conv_w"], "adamw_conv_w")
    outs["b_ada"] = _adamw_nd(b_ada, jnp.transpose(dm_rows, (1, 0, 2)), M["b_ada"], V["b_ada"], "adamw_b_ada")

    dsmall["c_ctx"] = d_c_ctx_part
    small_parts = _exchange(_pack_small(dsmall, extra=loss_local), False, "gather_small")
    w_small = _pack_small({n: W[n] for n in _SMALL})
    g_s, d_s, nm_s, nv_s = _adamw(w_small, small_parts, _pack_small({n: M[n] for n in _SMALL}),
                                  _pack_small({n: V[n] for n in _SMALL}), "adamw_small")
    like = {n: W[n] for n in _SMALL}
    g_small, loss = _unpack_small(g_s, like)
    unpacked = [g_small, _unpack_small(d_s, like)[0], _unpack_small(nm_s, like)[0], _unpack_small(nv_s, like)[0]]
    for n in _SMALL:
        outs[n] = [u[n] for u in unpacked]

    result = [loss, grad_x[None]]
    for k in range(4):
        result += [outs[n][k] for n in _NAMES]
    return tuple(result)
```

```python
import functools

import numpy as np
import jax
import jax.numpy as jnp
from jax import lax
from jax.experimental import pallas as pl
from jax.experimental.pallas import tpu as pltpu

F32 = jnp.float32
BF16 = jnp.bfloat16
_MXU = jnp.bfloat16
_ACT = jnp.bfloat16

N_DEV = 8
GRID_W = 64
HEAD_DIM = 128
NA_HEADS, NA_WIN_R, NA_WIN_C = 4, 8, 16
SWA_HEADS, SWA_KV_HEADS, SWA_WINDOW = 4, 2, 128
MLA_HEADS, MLA_Q_LORA, MLA_KV_LORA, MLA_NOPE, MLA_ROPE, MLA_V = 4, 384, 128, 128, 64, 128
GQA_HEADS, GQA_KV_HEADS = 4, 2
ROPE_THETA = 10000.0
EPS = 1e-6
NEG = -1e30
LOG2E = 1.4426950408889634
IN_SIZES = (512, 512, 512, 512, 256, 256, MLA_Q_LORA, MLA_KV_LORA, MLA_ROPE, 512, 256, 256)
IN_COLS = sum(IN_SIZES)
KPE_END = sum(IN_SIZES[:9])
IN_COLS_PAD = IN_COLS + 64
ADAM_LR, ADAM_B1, ADAM_B2, ADAM_EPS, ADAM_WD, ADAM_STEP = 0.001, 0.9, 0.999, 1e-08, 0.01, 10

LANE = 128
ROW_TILE = 256
WIDE_ROW_TILES = (1088, 1024, 640, 512, 256)
ATT_BLOCK = 128
VMEM_LIMIT = 56 * 1024 * 1024
MM_BUDGET = 36 * 1024 * 1024
HBM_BPS = 3.0e12
MXU_FLOPS = 9.0e14
VMEM_BPS = 4.0e12
STEP_S = 0.4e-6


def _cparams(n_axes):
    return pltpu.CompilerParams(dimension_semantics=("arbitrary",) * n_axes, vmem_limit_bytes=VMEM_LIMIT)


def _exchange_copies(in_ref, out_ref, send_sems, recv_sems, local_sem, scatter, rows=None):
    x, y, c = lax.axis_index("x"), lax.axis_index("y"), lax.axis_index("c")
    me = 4 * x + 2 * y + c
    cut = (lambda ref: ref) if rows is None else (lambda ref: ref.at[rows])
    copies = [pltpu.make_async_copy(cut(in_ref.at[me] if scatter else in_ref), cut(out_ref.at[me]), local_sem)]
    for k in range(1, N_DEV):
        px, py, pc = (x + (k >> 2)) % 2, (y + ((k >> 1) & 1)) % 2, (c + (k & 1)) % 2
        peer = 4 * px + 2 * py + pc
        copies.append(pltpu.make_async_remote_copy(
            src_ref=cut(in_ref.at[peer] if scatter else in_ref), dst_ref=cut(out_ref.at[me]),
            send_sem=send_sems.at[k - 1], recv_sem=recv_sems.at[k - 1],
            device_id=(px, py, pc), device_id_type=pl.DeviceIdType.MESH))
    return copies


PIECE_ROWS = 16
MAX_PIECES = 32


def _pieces(rows, steps):
    if rows % PIECE_ROWS:
        return 1
    units = rows // PIECE_ROWS
    return max(d for d in range(1, min(MAX_PIECES, steps) + 1) if units % d == 0)


def _exchange_out_shape(arr, scatter):
    return jax.ShapeDtypeStruct(arr.shape if scatter else (N_DEV,) + arr.shape, arr.dtype)


_EXCHANGE_SEMS = (pltpu.SemaphoreType.DMA((N_DEV - 1,)), pltpu.SemaphoreType.DMA((N_DEV - 1,)),
                  pltpu.SemaphoreType.DMA)


def _exchange(inp, scatter, name):
    def body(in_ref, out_ref, send_sems, recv_sems, local_sem):
        copies = _exchange_copies(in_ref, out_ref, send_sems, recv_sems, local_sem, scatter)
        for cp in copies:
            cp.start()
        for cp in copies:
            cp.wait()

    hbm = pl.BlockSpec(memory_space=pl.ANY)
    return pl.pallas_call(
        body, name=name, in_specs=[hbm], out_specs=hbm, out_shape=_exchange_out_shape(inp, scatter),
        scratch_shapes=list(_EXCHANGE_SEMS), compiler_params=pltpu.CompilerParams(has_side_effects=True),
    )(inp)


def _gather_by_chip(inp, name):
    def body(x_ref, out_ref, send_sems, recv_sems, local_sem):
        x, y, c = lax.axis_index("x"), lax.axis_index("y"), lax.axis_index("c")
        me, sibling = (x, y, c), (x, y, 1 - c)
        chips = [(1 - x, y), (x, 1 - y), (1 - x, 1 - y)]

        def slot(px, py, pc):
            return out_ref.at[4 * px + 2 * py + pc]

        def copy(k, block, to, src=None):
            return pltpu.make_async_remote_copy(
                src_ref=slot(*block) if src is None else src, dst_ref=slot(*block), send_sem=send_sems.at[k],
                recv_sem=recv_sems.at[k], device_id=to, device_id_type=pl.DeviceIdType.MESH)

        mine = pltpu.make_async_copy(x_ref, slot(*me), local_sem)
        mine.start()
        first = [copy(0, me, sibling, src=x_ref)]
        first += [copy(1 + j, me, (*chip, c), src=x_ref) for j, chip in enumerate(chips)]
        for cp in first:
            cp.start()
        passed = [copy(4 + j, (*chip, c), sibling) for j, chip in enumerate(chips)]
        for j, chip in enumerate(chips):
            copy(1 + j, (*chip, c), me).wait_recv()
            passed[j].start()
        copy(0, sibling, me).wait_recv()
        for j, chip in enumerate(chips):
            copy(4 + j, (*chip, 1 - c), me).wait_recv()
        for cp in first + passed:
            cp.wait_send()
        mine.wait()

    hbm = pl.BlockSpec(memory_space=pl.ANY)
    return pl.pallas_call(
        body, name=name, in_specs=[hbm], out_specs=hbm, out_shape=_exchange_out_shape(inp, False),
        scratch_shapes=list(_EXCHANGE_SEMS), compiler_params=pltpu.CompilerParams(has_side_effects=True),
    )(inp)


def _pcall(body, *, name, grid, in_specs, out_specs, out_shape, args, scratch_shapes=(), riders=()):
    in_specs, out_specs, out_shape = list(in_specs), list(out_specs), list(out_shape)
    scratch, args = list(scratch_shapes), list(args)
    n_in, n_out, n_scr, n_r = len(in_specs), len(out_specs), len(scratch), len(riders)
    hbm = pl.BlockSpec(memory_space=pl.ANY)
    for arr, scatter in riders:
        in_specs.append(hbm)
        args.append(arr)
        out_specs.append(hbm)
        out_shape.append(_exchange_out_shape(arr, scatter))
        scratch += list(_EXCHANGE_SEMS)

    def wrapped(*refs):
        ins, r_in = refs[:n_in], refs[n_in:n_in + n_r]
        o0 = n_in + n_r
        outs, r_out = refs[o0:o0 + n_out], refs[o0 + n_out:o0 + n_out + n_r]
        s0 = o0 + n_out + n_r
        scr, sems = refs[s0:s0 + n_scr], refs[s0 + n_scr:]

        def copies(r, rows=None):
            return _exchange_copies(r_in[r], r_out[r], sems[3 * r], sems[3 * r + 1], sems[3 * r + 2], riders[r][1],
                                    rows)

        if n_r:
            steps = int(np.prod(grid))
            step = functools.reduce(lambda u, w: u * w[1] + w[0],
                                    [(pl.program_id(a), n) for a, n in enumerate(grid)], 0)
            for r, (arr, scatter) in enumerate(riders):
                block_rows = arr.shape[1] if scatter else arr.shape[0]
                n_pieces = _pieces(block_rows, steps)
                every, piece = steps // n_pieces, block_rows // n_pieces

                @pl.when((step % every == 0) & (step // every < n_pieces))
                def _(r=r, every=every, piece=piece, whole=n_pieces == 1):
                    rows = None if whole else pl.ds(pl.multiple_of((step // every) * piece, PIECE_ROWS), piece)
                    for cp in copies(r, rows):
                        cp.start()

        body(*ins, *outs, *scr)

        if n_r:
            @pl.when(step == steps - 1)
            def _():
                for r in range(n_r):
                    for cp in copies(r):
                        cp.wait()

    params = pltpu.CompilerParams(dimension_semantics=("arbitrary",) * len(grid), vmem_limit_bytes=VMEM_LIMIT,
                                  has_side_effects=bool(riders))
    return list(pl.pallas_call(wrapped, name=name, grid=grid, in_specs=in_specs, out_specs=out_specs,
                               out_shape=out_shape, scratch_shapes=scratch, compiler_params=params)(*args))


def _riders(arrays, scatter):
    return [(a, scatter) for a in arrays]


def _nones(arrays):
    return tuple(None for _ in arrays)


def _divisors(n, step, cap):
    out = [d for d in range(step, min(n, cap) + 1, step) if n % d == 0]
    if n <= cap and n not in out:
        out.append(n)
    return out


def _mm_tiles(M, N, K, sa, sb, m_step, k_step):
    best, best_cost = None, None
    for tm in _divisors(M, m_step, 2304):
        for tn in _divisors(N, LANE, 2304):
            for tk in _divisors(K, k_step, 2304):
                vm = 2 * (tm * tk * sa + tk * tn * sb) + 3 * tm * tn * 4
                if vm > MM_BUDGET:
                    continue
                traffic = M * K * sa * (N // tn) + K * N * sb * (M // tm) + M * N * 4
                steps = (M // tm) * (N // tn) * (K // tk)
                passes = 1 if tk == K else 3
                busy = steps * (2 * tm * tn * tk / MXU_FLOPS + passes * tm * tn * 4 / VMEM_BPS)
                cost = max(traffic / HBM_BPS, busy) + steps * STEP_S
                if m_step == LANE:
                    cost = traffic / HBM_BPS + steps * STEP_S
                if tm % 256 or tn % 256:
                    cost *= 1.05
                if best_cost is None or cost < best_cost:
                    best, best_cost = (tm, tn, tk), cost
    assert best is not None, (M, N, K)
    return best


def _matmul(a, b, mode, name, riders=(), out_dtype=F32):
    if mode == "nn":
        (M, K), (K2, N) = a.shape, b.shape
    elif mode == "nt":
        (M, K), (N, K2) = a.shape, b.shape
    else:
        (K, M), (K2, N) = a.shape, b.shape
    assert K == K2, (a.shape, b.shape, mode)
    tm, tn, tk = _mm_tiles(M, N, K, a.dtype.itemsize, b.dtype.itemsize,
                           LANE if mode == "tn" else 8, 8 if mode == "tn" else LANE)
    nk = K // tk
    dims = {"nn": (((1,), (0,)), ((), ())), "nt": (((1,), (1,)), ((), ())), "tn": (((0,), (0,)), ((), ()))}[mode]

    own_acc = nk > 1 and out_dtype != F32

    def body(a_ref, b_ref, o_ref, *scr):
        prod = lax.dot_general(a_ref[...].astype(_MXU), b_ref[...].astype(_MXU), dims, preferred_element_type=F32)
        if nk == 1:
            o_ref[...] = prod.astype(out_dtype)
        else:
            acc_ref = scr[0] if own_acc else o_ref
            k = pl.program_id(2)

            @pl.when(k == 0)
            def _():
                acc_ref[...] = prod

            @pl.when(k > 0)
            def _():
                acc_ref[...] += prod

            if own_acc:
                @pl.when(k == nk - 1)
                def _():
                    o_ref[...] = acc_ref[...].astype(out_dtype)

    if mode == "nn":
        a_spec = pl.BlockSpec((tm, tk), lambda i, j, k: (i, k))
        b_spec = pl.BlockSpec((tk, tn), lambda i, j, k: (k, j))
    elif mode == "nt":
        a_spec = pl.BlockSpec((tm, tk), lambda i, j, k: (i, k))
        b_spec = pl.BlockSpec((tn, tk), lambda i, j, k: (j, k))
    else:
        a_spec = pl.BlockSpec((tk, tm), lambda i, j, k: (k, i))
        b_spec = pl.BlockSpec((tk, tn), lambda i, j, k: (k, j))
    outs = _pcall(body, name=name, grid=(M // tm, N // tn, nk), in_specs=[a_spec, b_spec],
                  out_specs=[pl.BlockSpec((tm, tn), lambda i, j, k: (i, j))],
                  out_shape=[jax.ShapeDtypeStruct((M, N), out_dtype)], args=[a, b],
                  scratch_shapes=[pltpu.VMEM((tm, tn), F32)] if own_acc else [], riders=riders)
    return outs if riders else outs[0]


def _make_linear(tag, to_parts, out_dtype=F32):
    def run(a, w, gin, tok):
        if gin:
            y, *got = _matmul(a, w, "nn", tag + "_fwd", riders=_riders(gin, False), out_dtype=out_dtype)
        else:
            y, got = _matmul(a, w, "nn", tag + "_fwd", out_dtype=out_dtype), []
        return y, tuple(got), tok

    @jax.custom_vjp
    def linear(a, w, sink, gin, tok):
        return run(a, w, gin, tok)

    def fwd(a, w, sink, gin, tok):
        return run(a, w, gin, tok), (a, w, gin)

    def bwd(res, cts):
        a, w, gin = res
        dy, _, dtok = cts
        dw = _matmul(a, dy, "tn", tag + "_bwd_dw")
        if dtok:
            da, *recv = _matmul(dy, w, "nt", tag + "_bwd_da", riders=_riders(dtok, True), out_dtype=a.dtype)
        else:
            da, recv = _matmul(dy, w, "nt", tag + "_bwd_da", out_dtype=a.dtype), []
        return da, jnp.zeros_like(w), to_parts(dw), _nones(gin), tuple(recv)

    linear.defvjp(fwd, bwd)
    return linear


def _seg_map(n_x_tiles):
    return lambda i: (jnp.where(i >= n_x_tiles, 1, 0), 0, 0)


def _ln_stats(x):
    mu = jnp.mean(x, axis=-1, keepdims=True)
    xc = x - mu
    var = jnp.mean(xc * xc, axis=-1, keepdims=True)
    rstd = lax.rsqrt(var + EPS)
    return xc * rstd, rstd


def _make_modulate(tag, n_x):
    def call_fwd(x, shift, scale):
        T, D = x.shape
        nxt = n_x // ROW_TILE

        def body(x_ref, sh_ref, sc_ref, o_ref):
            xhat, _ = _ln_stats(x_ref[...])
            o_ref[...] = xhat * (1.0 + sc_ref[0]) + sh_ref[0]

        row = pl.BlockSpec((ROW_TILE, D), lambda i: (i, 0))
        seg = pl.BlockSpec((1, 1, D), _seg_map(nxt))
        return pl.pallas_call(body, name=tag + "_fwd", grid=(T // ROW_TILE,), in_specs=[row, seg, seg],
                              out_specs=row, out_shape=jax.ShapeDtypeStruct((T, D), F32),
                              compiler_params=_cparams(1))(x, shift, scale)

    def call_bwd(x, scale, dh):
        T, D = x.shape
        nxt = n_x // ROW_TILE
        nseg = scale.shape[0]

        def body(x_ref, sc_ref, dh_ref, dx_ref, dsh_ref, dsc_ref):
            i = pl.program_id(0)

            @pl.when((i == 0) | (i == nxt))
            def _():
                dsh_ref[...] = jnp.zeros_like(dsh_ref)
                dsc_ref[...] = jnp.zeros_like(dsc_ref)

            xhat, rstd = _ln_stats(x_ref[...])
            dh = dh_ref[...]
            dsh_ref[0] += jnp.sum(dh, axis=0, keepdims=True)
            dsc_ref[0] += jnp.sum(dh * xhat, axis=0, keepdims=True)
            dxh = dh * (1.0 + sc_ref[0])
            m1 = jnp.mean(dxh, axis=-1, keepdims=True)
            m2 = jnp.mean(dxh * xhat, axis=-1, keepdims=True)
            dx_ref[...] = rstd * (dxh - m1 - xhat * m2)

        row = pl.BlockSpec((ROW_TILE, D), lambda i: (i, 0))
        seg = pl.BlockSpec((1, 1, D), _seg_map(nxt))
        segshape = jax.ShapeDtypeStruct((nseg, 1, D), F32)
        return pl.pallas_call(body, name=tag + "_bwd", grid=(T // ROW_TILE,), in_specs=[row, seg, row],
                              out_specs=[row, seg, seg],
                              out_shape=[jax.ShapeDtypeStruct((T, D), F32), segshape, segshape],
                              compiler_params=_cparams(1))(x, scale, dh)

    @jax.custom_vjp
    def modulate(x, shift, scale):
        return call_fwd(x, shift, scale)

    def fwd(x, shift, scale):
        return call_fwd(x, shift, scale), (x, scale)

    def bwd(res, dh):
        x, scale = res
        return tuple(call_bwd(x, scale, dh))

    modulate.defvjp(fwd, bwd)
    return modulate


def _make_resid_ln(tag, n_x, alpha):
    def call_fwd(x, y, gate, g, b):
        T, D = x.shape
        nxt = n_x // ROW_TILE

        def body(x_ref, y_ref, gt_ref, g_ref, b_ref, o_ref):
            u = alpha * x_ref[...] + gt_ref[0] * y_ref[...]
            uhat, _ = _ln_stats(u)
            o_ref[...] = uhat * g_ref[...] + b_ref[...]

        row = pl.BlockSpec((ROW_TILE, D), lambda i: (i, 0))
        seg = pl.BlockSpec((1, 1, D), _seg_map(nxt))
        vec = pl.BlockSpec((1, D), lambda i: (0, 0))
        return pl.pallas_call(body, name=tag + "_fwd", grid=(T // ROW_TILE,), in_specs=[row, row, seg, vec, vec],
                              out_specs=row, out_shape=jax.ShapeDtypeStruct((T, D), F32),
                              compiler_params=_cparams(1))(x, y, gate, g, b)

    def call_bwd(x, y, gate, g, do):
        T, D = x.shape
        nxt = n_x // ROW_TILE
        nseg = gate.shape[0]

        def body(x_ref, y_ref, gt_ref, g_ref, do_ref, dx_ref, dy_ref, dgt_ref, dg_ref, db_ref):
            i = pl.program_id(0)

            @pl.when(i == 0)
            def _():
                dg_ref[...] = jnp.zeros_like(dg_ref)
                db_ref[...] = jnp.zeros_like(db_ref)

            @pl.when((i == 0) | (i == nxt))
            def _():
                dgt_ref[...] = jnp.zeros_like(dgt_ref)

            y = y_ref[...]
            gate_v = gt_ref[0]
            uhat, rstd = _ln_stats(alpha * x_ref[...] + gate_v * y)
            do = do_ref[...]
            dg_ref[...] += jnp.sum(do * uhat, axis=0, keepdims=True)
            db_ref[...] += jnp.sum(do, axis=0, keepdims=True)
            duh = do * g_ref[...]
            m1 = jnp.mean(duh, axis=-1, keepdims=True)
            m2 = jnp.mean(duh * uhat, axis=-1, keepdims=True)
            du = rstd * (duh - m1 - uhat * m2)
            dx_ref[...] = alpha * du
            dy_ref[...] = gate_v * du
            dgt_ref[0] += jnp.sum(du * y, axis=0, keepdims=True)

        row = pl.BlockSpec((ROW_TILE, D), lambda i: (i, 0))
        seg = pl.BlockSpec((1, 1, D), _seg_map(nxt))
        vec = pl.BlockSpec((1, D), lambda i: (0, 0))
        rs = jax.ShapeDtypeStruct((T, D), F32)
        vs = jax.ShapeDtypeStruct((1, D), F32)
        return pl.pallas_call(body, name=tag + "_bwd", grid=(T // ROW_TILE,), in_specs=[row, row, seg, vec, row],
                              out_specs=[row, row, seg, vec, vec],
                              out_shape=[rs, rs, jax.ShapeDtypeStruct((nseg, 1, D), F32), vs, vs],
                              compiler_params=_cparams(1))(x, y, gate, g, do)

    @jax.custom_vjp
    def resid_ln(x, y, gate, g, b):
        return call_fwd(x, y, gate, g, b)

    def fwd(x, y, gate, g, b):
        return call_fwd(x, y, gate, g, b), (x, y, gate, g)

    def bwd(res, do):
        x, y, gate, g = res
        return tuple(call_bwd(x, y, gate, g, do))

    resid_ln.defvjp(fwd, bwd)
    return resid_ln


def _make_rmsnorm(tag):
    def call_fwd(x, g):
        T, W = x.shape
        gw = g.shape[1]

        tr = _pick_block(T, WIDE_ROW_TILES)

        def body(x_ref, g_ref, o_ref):
            for gi in range(W // gw):
                cols = slice(gi * gw, (gi + 1) * gw)
                x = x_ref[:, cols]
                r = lax.rsqrt(jnp.mean(x * x, axis=-1, keepdims=True) + EPS)
                o_ref[:, cols] = x * r * g_ref[...]

        blk = pl.BlockSpec((tr, W), lambda i: (i, 0))
        vec = pl.BlockSpec((1, gw), lambda i: (0, 0))
        return pl.pallas_call(body, name=tag + "_fwd", grid=(T // tr,), in_specs=[blk, vec],
                              out_specs=blk, out_shape=jax.ShapeDtypeStruct((T, W), F32),
                              compiler_params=_cparams(1))(x, g)

    def call_bwd(x, g, dy):
        T, W = x.shape
        gw = g.shape[1]
        tr = _pick_block(T, WIDE_ROW_TILES)

        def body(x_ref, g_ref, dy_ref, dx_ref, dg_ref):
            @pl.when(pl.program_id(0) == 0)
            def _():
                dg_ref[...] = jnp.zeros_like(dg_ref)

            for gi in range(W // gw):
                cols = slice(gi * gw, (gi + 1) * gw)
                x = x_ref[:, cols]
                r = lax.rsqrt(jnp.mean(x * x, axis=-1, keepdims=True) + EPS)
                xn = x * r
                dy = dy_ref[:, cols]
                dg_ref[...] += jnp.sum(dy * xn, axis=0, keepdims=True)
                dxn = dy * g_ref[...]
                dx_ref[:, cols] = r * (dxn - xn * jnp.mean(dxn * xn, axis=-1, keepdims=True))

        blk = pl.BlockSpec((tr, W), lambda i: (i, 0))
        vec = pl.BlockSpec((1, gw), lambda i: (0, 0))
        return pl.pallas_call(body, name=tag + "_bwd", grid=(T // tr,), in_specs=[blk, vec, blk],
                              out_specs=[blk, vec],
                              out_shape=[jax.ShapeDtypeStruct((T, W), F32), jax.ShapeDtypeStruct((1, gw), F32)],
                              compiler_params=_cparams(1))(x, g, dy)

    @jax.custom_vjp
    def rmsnorm(x, g):
        return call_fwd(x, g)

    def fwd(x, g):
        return call_fwd(x, g), (x, g)

    def bwd(res, dy):
        x, g = res
        return tuple(call_bwd(x, g, dy))

    rmsnorm.defvjp(fwd, bwd)
    return rmsnorm


def _rope_tables(n_x, n_all, width, half):
    t = np.arange(n_all)
    row, col = t // GRID_W, t % GRID_W
    lane = np.arange(LANE)
    inside = lane < width
    axis_is_col = (lane // (2 * half)) % 2 == 1
    f = (lane % (2 * half)) % half
    inv_freq = ROPE_THETA ** (-(f.astype(np.float64)) / half)
    pos = np.where(axis_is_col[None, :], col[:, None], row[:, None]).astype(np.float64)
    ang = (pos.astype(np.float32) * inv_freq.astype(np.float32)[None, :]).astype(np.float32)
    live = inside[None, :] & (t < n_x)[:, None]
    cos = np.where(live, np.cos(ang), 1.0).astype(np.float32)
    sin = np.where(live, np.sin(ang), 0.0).astype(np.float32)
    first = (lane % (2 * half)) < half
    s_lo = np.where(first[None, :], -sin, 0.0).astype(np.float32)
    s_hi = np.where(first[None, :], 0.0, sin).astype(np.float32)
    return jnp.asarray(cos), jnp.asarray(s_lo), jnp.asarray(s_hi)


def _make_rope(tag, half):
    def call(x, cos, s_lo, s_hi, transpose, name):
        T, W = x.shape

        tr = _pick_block(T, WIDE_ROW_TILES)

        def body(x_ref, c_ref, lo_ref, hi_ref, o_ref):
            c, lo, hi = c_ref[...], lo_ref[...], hi_ref[...]
            for cb in range(W // LANE):
                cols = slice(cb * LANE, (cb + 1) * LANE)
                x = x_ref[:, cols]
                if not transpose:
                    o_ref[:, cols] = x * c + pltpu.roll(x, LANE - half, 1) * lo + pltpu.roll(x, half, 1) * hi
                else:
                    o_ref[:, cols] = x * c + pltpu.roll(x * lo, half, 1) + pltpu.roll(x * hi, LANE - half, 1)

        blk = pl.BlockSpec((tr, W), lambda i: (i, 0))
        tab = pl.BlockSpec((tr, LANE), lambda i: (i, 0))
        return pl.pallas_call(body, name=name, grid=(T // tr,), in_specs=[blk, tab, tab, tab],
                              out_specs=blk, out_shape=jax.ShapeDtypeStruct((T, W), F32),
                              compiler_params=_cparams(1))(x, cos, s_lo, s_hi)

    @jax.custom_vjp
    def rope(x, cos, s_lo, s_hi):
        return call(x, cos, s_lo, s_hi, False, tag + "_fwd")

    def fwd(x, cos, s_lo, s_hi):
        return call(x, cos, s_lo, s_hi, False, tag + "_fwd"), (cos, s_lo, s_hi)

    def bwd(res, dy):
        cos, s_lo, s_hi = res
        return call(dy, cos, s_lo, s_hi, True, tag + "_bwd"), None, None, None

    rope.defvjp(fwd, bwd)
    return rope


def _make_conv_gate(tag, n_x):
    def shifted(v, T):
        t = lax.broadcasted_iota(jnp.int32, v.shape, 0)
        prev = jnp.where((t == 0) | (t == n_x), 0.0, pltpu.roll(v, 1, 0))
        nxt = jnp.where((t == n_x - 1) | (t == T - 1), 0.0, pltpu.roll(v, T - 1, 0))
        return prev, nxt

    def call_fwd(a, u, cw, cb, riders):
        T, Fd = a.shape

        def body(a_ref, u_ref, cw_ref, cb_ref, z_ref):
            pre = a_ref[...].astype(F32)
            prev, nxt = shifted(pre, T)
            s = prev * cw_ref[0:1, :] + pre * cw_ref[1:2, :] + nxt * cw_ref[2:3, :] + cb_ref[...]
            z_ref[...] = (s * (1.0 / (1.0 + jnp.exp(-s))) * u_ref[...].astype(F32)).astype(z_ref.dtype)

        col = pl.BlockSpec((T, LANE), lambda j: (0, j))
        return _pcall(body, name=tag + "_fwd", grid=(Fd // LANE,),
                      in_specs=[col, col, pl.BlockSpec((3, LANE), lambda j: (0, j)),
                                pl.BlockSpec((1, LANE), lambda j: (0, j))],
                      out_specs=[col], out_shape=[jax.ShapeDtypeStruct((T, Fd), a.dtype)], args=[a, u, cw, cb],
                      riders=riders)

    def call_bwd(a, u, cw, cb, dz, riders):
        T, Fd = a.shape

        def body(a_ref, u_ref, cw_ref, cb_ref, dz_ref, da_ref, du_ref, dcw_ref, dcb_ref):
            pre = a_ref[...].astype(F32)
            prev, nxt = shifted(pre, T)
            s = prev * cw_ref[0:1, :] + pre * cw_ref[1:2, :] + nxt * cw_ref[2:3, :] + cb_ref[...]
            sig = 1.0 / (1.0 + jnp.exp(-s))
            dz = dz_ref[...].astype(F32)
            du_ref[...] = (dz * s * sig).astype(du_ref.dtype)
            ds = dz * u_ref[...].astype(F32) * (sig * (1.0 + s * (1.0 - sig)))
            ds_prev, ds_next = shifted(ds, T)
            da_ref[...] = (ds * cw_ref[1:2, :] + ds_next * cw_ref[0:1, :] + ds_prev * cw_ref[2:3, :]).astype(
                da_ref.dtype)
            dcw_ref[0:1, :] = jnp.sum(ds * prev, axis=0, keepdims=True)
            dcw_ref[1:2, :] = jnp.sum(ds * pre, axis=0, keepdims=True)
            dcw_ref[2:3, :] = jnp.sum(ds * nxt, axis=0, keepdims=True)
            dcb_ref[...] = jnp.sum(ds, axis=0, keepdims=True)

        col = pl.BlockSpec((T, LANE), lambda j: (0, j))
        w3 = pl.BlockSpec((3, LANE), lambda j: (0, j))
        w1 = pl.BlockSpec((1, LANE), lambda j: (0, j))
        big = jax.ShapeDtypeStruct((T, Fd), a.dtype)
        return _pcall(body, name=tag + "_bwd", grid=(Fd // LANE,), in_specs=[col, col, w3, w1, col],
                      out_specs=[col, col, w3, w1],
                      out_shape=[big, big, jax.ShapeDtypeStruct((3, Fd), F32), jax.ShapeDtypeStruct((1, Fd), F32)],
                      args=[a, u, cw, cb, dz], riders=riders)

    def run(a, u, cw, cb, gin, tok):
        z, *got = call_fwd(a, u, cw, cb, _riders(gin, False))
        return z, tuple(got), tok

    @jax.custom_vjp
    def conv_gate(a, u, cw, cb, gin, tok):
        return run(a, u, cw, cb, gin, tok)

    def fwd(a, u, cw, cb, gin, tok):
        return run(a, u, cw, cb, gin, tok), (a, u, cw, cb, gin)

    def bwd(res, cts):
        a, u, cw, cb, gin = res
        dz, _, dtok = cts
        da, du, dcw, dcb, *recv = call_bwd(a, u, cw, cb, dz, _riders(dtok, True))
        return da, du, dcw, dcb, _nones(gin), tuple(recv)

    conv_gate.defvjp(fwd, bwd)
    return conv_gate


def _loss_and_grad(y, tgt, name):
    T, D = y.shape

    def body(y_ref, t_ref, l_ref, r_ref):
        @pl.when(pl.program_id(0) == 0)
        def _():
            l_ref[...] = jnp.zeros_like(l_ref)

        d = y_ref[...] - t_ref[...]
        r_ref[...] = d * (1.0 / D)
        l_ref[...] += jnp.sum(d * d) * (0.5 / D)

    row = pl.BlockSpec((ROW_TILE, D), lambda i: (i, 0))
    l, r = pl.pallas_call(body, name=name, grid=(T // ROW_TILE,), in_specs=[row, row],
                          out_specs=[pl.BlockSpec((8, LANE), lambda i: (0, 0)), row],
                          out_shape=[jax.ShapeDtypeStruct((8, LANE), F32), jax.ShapeDtypeStruct((T, D), F32)],
                          compiler_params=_cparams(1))(y, tgt)
    return l[0, 0], r


class _AttnCfg:
    def __init__(self, H, Hkv, dqk, dv, scale, tq, tk, has_sink=False):
        self.H, self.Hkv, self.dqk, self.dv, self.scale = H, Hkv, dqk, dv, scale
        self.tq, self.tk, self.has_sink = tq, tk, has_sink
        self.G = H // Hkv


class _WinCfg:
    def __init__(self, mode, H, Hkv, scale, nx, n_ctx):
        self.mode, self.H, self.Hkv, self.scale, self.nx, self.n_ctx = mode, H, Hkv, scale, nx, n_ctx
        self.G = H // Hkv
        self.has_sink = mode == "swa"
        self.nwin, self.back = (3, 1) if mode == "swa" else (5, 2)
        self.width = self.nwin * ATT_BLOCK

    def first_block(self, i):
        return jnp.clip(i - self.back, 0, self.nx - self.nwin)

    def slots(self, i, jj):
        d = 2 * (self.first_block(i) + jj - i)
        return jnp.clip(d + 8, 0, 15), jnp.clip(d + 7, 0, 15)

    def local_scores(self, s, i, tab_ref):
        sub = lax.broadcasted_iota(jnp.int32, s.shape, 0)
        lan = lax.broadcasted_iota(jnp.int32, s.shape, 1)
        first = self.first_block(i)
        if self.mode == "swa":
            diff = (i * ATT_BLOCK + sub) - (first * ATT_BLOCK + lan)
            return jnp.where(jnp.abs(diff) <= SWA_WINDOW, s, -jnp.inf)
        rows = 2 * self.nx
        qrow = 2 * i + jnp.where(sub >= GRID_W, 1, 0)
        krow = 2 * first + lan // GRID_W
        r0 = jnp.clip(qrow - NA_WIN_R // 2, 0, rows - NA_WIN_R)
        ok = (krow >= r0) & (krow < r0 + NA_WIN_R)
        tiles = []
        for jj in range(self.nwin):
            top, bot = self.slots(i, jj)
            tiles.append(jnp.concatenate([tab_ref[0, top], tab_ref[0, bot]], axis=0))
        return jnp.where(ok, s + jnp.concatenate(tiles, axis=1), -jnp.inf)


def _win_specs(cfg, Tk):
    G = cfg.G
    qmap = lambda kh, g, i: (i, kh * G + g)
    q_spec = pl.BlockSpec((ATT_BLOCK, HEAD_DIM), qmap)
    kv_spec = pl.BlockSpec((Tk, HEAD_DIM), lambda kh, g, i: (0, kh))
    lse_spec = pl.BlockSpec((1, ATT_BLOCK, LANE), lambda kh, g, i: (kh * G + g, i, 0))
    tab_spec = pl.BlockSpec((1, 16, GRID_W, LANE), lambda kh, g, i: (kh * G + g, 0, 0, 0))
    sink_spec = pl.BlockSpec((1, 8, LANE), lambda kh, g, i: (kh * G + g, 0, 0))
    return q_spec, kv_spec, lse_spec, tab_spec, sink_spec


def _win_scores(cfg, i, q_ref, k_ref, tab_ref):
    dn = (((1,), (1,)), ((), ()))
    n_x = cfg.nx * ATT_BLOCK
    win = pl.ds(pl.multiple_of(cfg.first_block(i) * ATT_BLOCK, ATT_BLOCK), cfg.width)
    ctx = pl.ds(n_x, cfg.n_ctx)
    qb = q_ref[...].astype(_MXU)
    s_loc = lax.dot_general(qb, k_ref[win, :].astype(_MXU), dn, preferred_element_type=F32)
    s_ctx = lax.dot_general(qb, k_ref[ctx, :].astype(_MXU), dn, preferred_element_type=F32)
    return cfg.local_scores(s_loc, i, tab_ref), s_ctx, win, ctx


def _win_fwd_call(cfg, name, q, k, v, tab, sink, riders=()):
    Tq, Tk = q.shape[0], k.shape[0]

    def body(*refs):
        q_ref, k_ref, v_ref = refs[:3]
        extra = refs[3] if (cfg.mode == "na" or cfg.has_sink) else None
        o_ref, lse_ref = refs[-2:]
        i = pl.program_id(2)
        s_loc, s_ctx, win, ctx = _win_scores(cfg, i, q_ref, k_ref, extra if cfg.mode == "na" else None)
        m = jnp.maximum(jnp.max(s_loc, axis=-1, keepdims=True), jnp.max(s_ctx, axis=-1, keepdims=True))
        if cfg.has_sink:
            m = jnp.maximum(m, extra[0, 0:1, 0:1])
        p_loc, p_ctx = jnp.exp2(s_loc - m), jnp.exp2(s_ctx - m)
        l = jnp.sum(p_loc, axis=-1, keepdims=True) + jnp.sum(p_ctx, axis=-1, keepdims=True)
        if cfg.has_sink:
            l = l + jnp.exp2(extra[0, 0:1, 0:1] - m)
        acc = (jnp.dot(p_loc.astype(_MXU), v_ref[win, :].astype(_MXU), preferred_element_type=F32)
               + jnp.dot(p_ctx.astype(_MXU), v_ref[ctx, :].astype(_MXU), preferred_element_type=F32))
        o_ref[...] = acc / l
        lse_ref[0] = jnp.broadcast_to(m + jnp.log2(l), (ATT_BLOCK, LANE))

    q_spec, kv_spec, lse_spec, tab_spec, sink_spec = _win_specs(cfg, Tk)
    in_specs, args = [q_spec, kv_spec, kv_spec], [q, k, v]
    if cfg.mode == "na":
        in_specs.append(tab_spec)
        args.append(tab)
    if cfg.has_sink:
        in_specs.append(sink_spec)
        args.append(sink)
    return _pcall(
        body, name=name, grid=(cfg.Hkv, cfg.G, Tq // ATT_BLOCK), in_specs=in_specs, out_specs=[q_spec, lse_spec],
        out_shape=[jax.ShapeDtypeStruct((Tq, cfg.H * HEAD_DIM), F32), jax.ShapeDtypeStruct((cfg.H, Tq, LANE), F32)],
        args=args, riders=riders)


def _win_bwd_call(cfg, name, q, k, v, tab, sink, o, lse, do, riders=()):
    Tq, Tk = q.shape[0], k.shape[0]
    nq = Tq // ATT_BLOCK
    dn_nt = (((1,), (1,)), ((), ()))
    dn_tn = (((0,), (0,)), ((), ()))

    def body(*refs):
        q_ref, k_ref, v_ref = refs[:3]
        has_extra = cfg.mode == "na" or cfg.has_sink
        extra = refs[3] if has_extra else None
        n = 4 if has_extra else 3
        o_ref, lse_ref, do_ref, dq_ref, dk_ref, dv_ref = refs[n:n + 6]
        dextra = refs[n + 6] if has_extra else None
        g, i = pl.program_id(1), pl.program_id(2)

        @pl.when((g == 0) & (i == 0))
        def _():
            dk_ref[...] = jnp.zeros_like(dk_ref)
            dv_ref[...] = jnp.zeros_like(dv_ref)

        if has_extra:
            @pl.when(i == 0)
            def _():
                dextra[...] = jnp.zeros_like(dextra)

        s_loc, s_ctx, win, ctx = _win_scores(cfg, i, q_ref, k_ref, extra if cfg.mode == "na" else None)
        lse = lse_ref[0][:, 0:1]
        p_loc, p_ctx = jnp.exp2(s_loc - lse), jnp.exp2(s_ctx - lse)
        do_f = do_ref[...]
        do_b = do_f.astype(_MXU)
        delta = jnp.sum(do_f * o_ref[...], axis=-1, keepdims=True)
        dp_loc = lax.dot_general(do_b, v_ref[win, :].astype(_MXU), dn_nt, preferred_element_type=F32)
        dp_ctx = lax.dot_general(do_b, v_ref[ctx, :].astype(_MXU), dn_nt, preferred_element_type=F32)
        ds_loc, ds_ctx = p_loc * (dp_loc - delta), p_ctx * (dp_ctx - delta)
        if cfg.mode == "na":
            for jj in range(cfg.nwin):
                top, bot = cfg.slots(i, jj)
                tile = ds_loc[:, jj * ATT_BLOCK:(jj + 1) * ATT_BLOCK]
                dextra[0, top] += tile[0:GRID_W, :]
                dextra[0, bot] += tile[GRID_W:, :]
        if cfg.has_sink:
            p_sink = jnp.exp2(extra[0, 0:1, 0:1] - lse)
            lane0 = ((lax.broadcasted_iota(jnp.int32, (8, LANE), 0) == 0)
                     & (lax.broadcasted_iota(jnp.int32, (8, LANE), 1) == 0))
            dextra[0] += jnp.where(lane0, -jnp.sum(p_sink * delta), 0.0)
        dsb_loc, dsb_ctx = ds_loc.astype(_MXU), ds_ctx.astype(_MXU)
        qb = q_ref[...].astype(_MXU)
        dq_ref[...] = cfg.scale * (jnp.dot(dsb_loc, k_ref[win, :].astype(_MXU), preferred_element_type=F32)
                                   + jnp.dot(dsb_ctx, k_ref[ctx, :].astype(_MXU), preferred_element_type=F32))
        dk_ref[win, :] += lax.dot_general(dsb_loc, qb, dn_tn, preferred_element_type=F32)
        dk_ref[ctx, :] += lax.dot_general(dsb_ctx, qb, dn_tn, preferred_element_type=F32)
        dv_ref[win, :] += lax.dot_general(p_loc.astype(_MXU), do_b, dn_tn, preferred_element_type=F32)
        dv_ref[ctx, :] += lax.dot_general(p_ctx.astype(_MXU), do_b, dn_tn, preferred_element_type=F32)

        @pl.when((g == cfg.G - 1) & (i == nq - 1))
        def _():
            dk_ref[...] = dk_ref[...] * (1.0 / LOG2E)

    q_spec, kv_spec, lse_spec, tab_spec, sink_spec = _win_specs(cfg, Tk)
    in_specs, args = [q_spec, kv_spec, kv_spec], [q, k, v]
    out_specs = [q_spec, kv_spec, kv_spec]
    out_shape = [jax.ShapeDtypeStruct(q.shape, F32), jax.ShapeDtypeStruct(k.shape, F32),
                 jax.ShapeDtypeStruct(v.shape, F32)]
    if cfg.mode == "na":
        in_specs.append(tab_spec)
        args.append(tab)
        out_specs.append(tab_spec)
        out_shape.append(jax.ShapeDtypeStruct(tab.shape, F32))
    if cfg.has_sink:
        in_specs.append(sink_spec)
        args.append(sink)
        out_specs.append(sink_spec)
        out_shape.append(jax.ShapeDtypeStruct(sink.shape, F32))
    in_specs += [q_spec, lse_spec, q_spec]
    args += [o, lse, do]
    return _pcall(body, name=name, grid=(cfg.Hkv, cfg.G, nq), in_specs=in_specs, out_specs=out_specs,
                  out_shape=out_shape, args=args, riders=riders)


def _attn_specs(cfg, Tk):
    G, tq = cfg.G, cfg.tq
    qmap = lambda kh, g, i: (i, kh * G + g)
    q_spec = pl.BlockSpec((tq, cfg.dqk), qmap)
    k_spec = pl.BlockSpec((Tk, cfg.dqk), lambda kh, g, i: (0, kh))
    v_spec = pl.BlockSpec((Tk, cfg.dv), lambda kh, g, i: (0, kh))
    o_spec = pl.BlockSpec((tq, cfg.dv), qmap)
    lse_spec = pl.BlockSpec((1, tq, LANE), lambda kh, g, i: (kh * G + g, i, 0))
    sink_spec = pl.BlockSpec((1, 8, LANE), lambda kh, g, i: (kh * G + g, 0, 0))
    return q_spec, k_spec, v_spec, o_spec, lse_spec, sink_spec


def _attn_fwd_call(cfg, name, q, k, v, sink, riders=()):
    Tq, Tk = q.shape[0], k.shape[0]
    nq, nkv = Tq // cfg.tq, Tk // cfg.tk
    dn = (((1,), (1,)), ((), ()))

    def body(*refs):
        q_ref, k_ref, v_ref = refs[:3]
        sink_ref = refs[3] if cfg.has_sink else None
        o_ref, lse_ref = refs[-2:]
        qb = q_ref[...].astype(_MXU)
        m = jnp.full((cfg.tq, 1), NEG, F32)
        l = jnp.zeros((cfg.tq, 1), F32)
        acc = jnp.zeros((cfg.tq, cfg.dv), F32)
        for j in range(nkv):
            rows = pl.ds(j * cfg.tk, cfg.tk)
            s = lax.dot_general(qb, k_ref[rows, :].astype(_MXU), dn, preferred_element_type=F32)
            m_new = jnp.maximum(m, jnp.max(s, axis=-1, keepdims=True))
            alpha = jnp.exp2(m - m_new)
            p = jnp.exp2(s - m_new)
            l = alpha * l + jnp.sum(p, axis=-1, keepdims=True)
            acc = alpha * acc + jnp.dot(p.astype(_MXU), v_ref[rows, :].astype(_MXU), preferred_element_type=F32)
            m = m_new
        if cfg.has_sink:
            sk = sink_ref[0, 0:1, 0:1]
            m_new = jnp.maximum(m, sk)
            alpha = jnp.exp2(m - m_new)
            l = alpha * l + jnp.exp2(sk - m_new)
            acc = acc * alpha
            m = m_new
        o_ref[...] = acc / l
        lse_ref[0] = jnp.broadcast_to(m + jnp.log2(l), (cfg.tq, LANE))

    q_spec, k_spec, v_spec, o_spec, lse_spec, sink_spec = _attn_specs(cfg, Tk)
    in_specs, args = [q_spec, k_spec, v_spec], [q, k, v]
    if cfg.has_sink:
        in_specs.append(sink_spec)
        args.append(sink)
    return _pcall(
        body, name=name, grid=(cfg.Hkv, cfg.G, nq), in_specs=in_specs, out_specs=[o_spec, lse_spec],
        out_shape=[jax.ShapeDtypeStruct((Tq, cfg.H * cfg.dv), F32), jax.ShapeDtypeStruct((cfg.H, Tq, LANE), F32)],
        args=args, riders=riders)


def _attn_bwd_call(cfg, name, q, k, v, sink, o, lse, do, riders=()):
    Tq, Tk = q.shape[0], k.shape[0]
    nq, nkv = Tq // cfg.tq, Tk // cfg.tk
    tq, tk = cfg.tq, cfg.tk
    dn_nt = (((1,), (1,)), ((), ()))
    dn_tn = (((0,), (0,)), ((), ()))

    def body(*refs):
        q_ref, k_ref, v_ref = refs[:3]
        n = 3
        sink_ref = dsink_ref = None
        if cfg.has_sink:
            sink_ref, n = refs[n], n + 1
        o_ref, lse_ref, do_ref, dq_ref, dk_ref, dv_ref = refs[n:n + 6]
        n += 6
        if cfg.has_sink:
            dsink_ref, n = refs[n], n + 1
        g, i = pl.program_id(1), pl.program_id(2)

        @pl.when((g == 0) & (i == 0))
        def _():
            dk_ref[...] = jnp.zeros_like(dk_ref)
            dv_ref[...] = jnp.zeros_like(dv_ref)

        if cfg.has_sink:
            @pl.when(i == 0)
            def _():
                dsink_ref[...] = jnp.zeros_like(dsink_ref)

        do_f = do_ref[...]
        do_b = do_f.astype(_MXU)
        qb = q_ref[...].astype(_MXU)
        lse = lse_ref[0][:, 0:1]
        delta = jnp.sum(do_f * o_ref[...], axis=-1, keepdims=True)
        if cfg.has_sink:
            p_sink = jnp.exp2(sink_ref[0, 0:1, 0:1] - lse)
            lane0 = ((lax.broadcasted_iota(jnp.int32, (8, LANE), 0) == 0)
                     & (lax.broadcasted_iota(jnp.int32, (8, LANE), 1) == 0))
            dsink_ref[0] += jnp.where(lane0, -jnp.sum(p_sink * delta), 0.0)
        dq = jnp.zeros((tq, cfg.dqk), F32)
        for j in range(nkv):
            rows = pl.ds(j * tk, tk)
            kb, vb = k_ref[rows, :].astype(_MXU), v_ref[rows, :].astype(_MXU)
            s = lax.dot_general(qb, kb, dn_nt, preferred_element_type=F32)
            p = jnp.exp2(s - lse)
            dp = lax.dot_general(do_b, vb, dn_nt, preferred_element_type=F32)
            dsb = (p * (dp - delta)).astype(_MXU)
            dq = dq + jnp.dot(dsb, kb, preferred_element_type=F32)
            dk_ref[rows, :] += lax.dot_general(dsb, qb, dn_tn, preferred_element_type=F32)
            dv_ref[rows, :] += lax.dot_general(p.astype(_MXU), do_b, dn_tn, preferred_element_type=F32)
        dq_ref[...] = dq * cfg.scale

        @pl.when((g == cfg.G - 1) & (i == nq - 1))
        def _():
            dk_ref[...] = dk_ref[...] * (1.0 / LOG2E)

    q_spec, k_spec, v_spec, o_spec, lse_spec, sink_spec = _attn_specs(cfg, Tk)
    in_specs, args = [q_spec, k_spec, v_spec], [q, k, v]
    if cfg.has_sink:
        in_specs.append(sink_spec)
        args.append(sink)
    in_specs += [o_spec, lse_spec, o_spec]
    args += [o, lse, do]
    out_specs = [q_spec, k_spec, v_spec]
    out_shape = [jax.ShapeDtypeStruct(q.shape, F32), jax.ShapeDtypeStruct(k.shape, F32),
                 jax.ShapeDtypeStruct(v.shape, F32)]
    if cfg.has_sink:
        out_specs.append(sink_spec)
        out_shape.append(jax.ShapeDtypeStruct(sink.shape, F32))
    return _pcall(body, name=name, grid=(cfg.Hkv, cfg.G, nq), in_specs=in_specs, out_specs=out_specs,
                  out_shape=out_shape, args=args, riders=riders)


def _make_attention(tag, cfg):
    windowed = isinstance(cfg, _WinCfg)

    def run_fwd(q, k, v, tab, sink, gin):
        if windowed:
            return _win_fwd_call(cfg, tag + "_fwd", q, k, v, tab, sink, _riders(gin, False))
        return _attn_fwd_call(cfg, tag + "_fwd", q, k, v, sink, _riders(gin, False))

    def prepared(q, k, v, tab, sink):
        q2 = (q * (cfg.scale * LOG2E)).astype(_MXU)
        tab2 = None if tab is None else tab * LOG2E
        sink2 = None if sink is None else sink * LOG2E
        return q2, k.astype(_MXU), v.astype(_MXU), tab2, sink2

    @jax.custom_vjp
    def attn(q, k, v, tab, sink, gin, tok):
        o, _, *got = run_fwd(*prepared(q, k, v, tab, sink), gin)
        return o, tuple(got), tok

    def fwd(q, k, v, tab, sink, gin, tok):
        q2, kb, vb, tab2, sink2 = prepared(q, k, v, tab, sink)
        o, lse, *got = run_fwd(q2, kb, vb, tab2, sink2, gin)
        return (o, tuple(got), tok), (q2, kb, vb, tab2, sink2, gin, o, lse)

    def bwd(res, cts):
        q, k, v, tab, sink, gin, o, lse = res
        do, _, dtok = cts
        if windowed:
            outs = _win_bwd_call(cfg, tag + "_bwd", q, k, v, tab, sink, o, lse, do, _riders(dtok, True))
        else:
            outs = _attn_bwd_call(cfg, tag + "_bwd", q, k, v, sink, o, lse, do, _riders(dtok, True))
        dq, dk, dv = outs[:3]
        rest = outs[3:]
        dtab = rest.pop(0) if tab is not None else None
        dsink = rest.pop(0) if sink is not None else None
        return dq, dk, dv, dtab, dsink, _nones(gin), tuple(rest)

    attn.defvjp(fwd, bwd)
    return attn


def _pick_block(n, prefs):
    for p in prefs:
        if n % p == 0:
            return p
    return n


def _na_table(rpb):
    cq = np.arange(GRID_W)
    dcol = np.clip(cq[None, :] - cq[:, None] + NA_WIN_C - 1, 0, 2 * NA_WIN_C - 2)
    onehot = (dcol[:, :, None] == np.arange(2 * NA_WIN_C - 1)[None, None, :]).astype(np.float32)
    c0 = np.clip(cq - NA_WIN_C // 2, 0, GRID_W - NA_WIN_C)
    col_in = (cq[None, :] >= c0[:, None]) & (cq[None, :] < c0[:, None] + NA_WIN_C)
    tz = jnp.einsum("hrj,qkj->hrqk", rpb, jnp.asarray(onehot), precision=lax.Precision.HIGHEST)
    tz = jnp.where(jnp.asarray(col_in)[None, None], tz, NEG)
    zero = jnp.zeros_like(tz[:, :1])
    tzp = jnp.concatenate([zero, tz, zero], axis=1)
    return jnp.concatenate([tzp[:, 0:16], tzp[:, 1:17]], axis=-1)


def _sink_block(sink):
    return jnp.broadcast_to(sink[:, None, None], (sink.shape[0], 8, LANE))


def _make_split(bounds, axis):
    @jax.custom_vjp
    def split(t):
        return tuple(lax.slice_in_dim(t, a, b, axis=axis) for a, b in bounds)

    def fwd(t):
        return split(t), None

    def bwd(_, cts):
        return (jnp.concatenate(cts, axis=axis),)

    split.defvjp(fwd, bwd)
    return split


_IN_GROUPS = ((0, 512), (512, 1024), (1024, 1536), (1536, 2048), (2048, 2304), (2304, 2560), (2560, 2944),
              (2944, 3072), (3072, 3200), (3200, 3712), (3712, 3968), (3968, 4224))


def _seg2(vx, vc, nseg):
    rows = [vx, vc][:nseg]
    return jnp.stack(rows)[:, None, :]


_BIG_COL = ("w_in", "mla_w_uq", "mla_w_ukv", "ffn_w_gate", "ffn_w_up")
_BIG_ROW = ("w_out", "ffn_w_down")
_IN_GROUP = ("w_in", "mla_w_uq", "mla_w_ukv")
_PIECES = {"w_in": 4, "ffn_w_down": 2}


def _n_pieces(name):
    return _PIECES.get(name, 1)


def _shard_view(name, w):
    return jnp.swapaxes(w, 1, 2) if name in _BIG_COL else w


def _cut(name, a):
    return tuple(jnp.split(a, _n_pieces(name), axis=-1))


def _full_weight(name, pieces):
    g = jnp.concatenate(pieces, axis=-1) if len(pieces) > 1 else pieces[0]
    full = g.reshape(N_DEV * g.shape[1], g.shape[2])
    if name in _BIG_ROW:
        return full
    full = full.T
    if name == "w_in":
        K = full.shape[0]
        full = jnp.concatenate([full[:, :KPE_END], jnp.zeros((K, 64), full.dtype), full[:, KPE_END:]], axis=-1)
    return full


def _grad_parts(name):
    def to_parts(dw):
        if name == "w_in":
            dw = jnp.concatenate([dw[:, :KPE_END], dw[:, KPE_END + 64:]], axis=-1)
        if name in _BIG_COL:
            dw = dw.T
        return _cut(name, dw.reshape(N_DEV, dw.shape[0] // N_DEV, dw.shape[1]).astype(BF16))

    return to_parts


_RIDE_FWD = {
    "attn_a": ((0, "w_out", 0),), "attn_b": ((1, "w_in", 0), (1, "w_in", 1)), "attn_c": ((0, "ffn_w_gate", 0),),
    "attn_d": ((0, "ffn_w_up", 0),), "gate": ((0, "ffn_w_down", 0),), "up": ((0, "ffn_w_down", 1),),
    "conv": ((1, "mla_w_uq", 0), (1, "mla_w_ukv", 0)), "down": ((1, "w_in", 2), (1, "w_in", 3)),
}
_RIDE_BWD = {
    "attn_a": ((0, "w_out", 0),), "attn_b": ((1, "mla_w_uq", 0), (1, "mla_w_ukv", 0)),
    "attn_c": ((0, "ffn_w_gate", 0),), "attn_d": ((0, "ffn_w_up", 0),), "gate": ((0, "ffn_w_down", 0),),
    "up": ((0, "ffn_w_down", 1),), "conv": ((1, "w_in", 0), (1, "w_in", 1)), "down": ((1, "w_in", 2), (1, "w_in", 3)),
}


def _local_forward(x, ctx, modx, modc, p, full, shards, toks):
    S, D = x.shape
    C = ctx.shape[0]
    depth = len(full)
    ride = shards is not None
    have = [dict(f) for f in full]
    landing = {}
    sinks = {(l, n): list(toks[l][n]) for l in range(depth) for n in toks[l]}
    alpha = (2 * depth) ** 0.25
    T0 = S + C
    nx = S // ATT_BLOCK

    def riding(table, unit, l):
        return [(l + dl, n, k) for dl, n, k in table[unit] if l + dl < depth] if ride else []

    def gin(unit, l):
        return tuple(shards[ll][n][k] for ll, n, k in riding(_RIDE_FWD, unit, l))

    def tok(unit, l):
        return tuple(sinks[(ll, n)][k] for ll, n, k in riding(_RIDE_BWD, unit, l))

    def landed(result, unit, l):
        out, got, tk = result
        for (ll, n, k), g in zip(riding(_RIDE_FWD, unit, l), got):
            landing.setdefault((ll, n), {})[k] = g
            if len(landing[(ll, n)]) == _n_pieces(n):
                have[ll][n] = _full_weight(n, [landing[(ll, n)][j] for j in range(_n_pieces(n))])
        for (ll, n, k), s in zip(riding(_RIDE_BWD, unit, l), tk):
            sinks[(ll, n)][k] = s
        return out

    sc = HEAD_DIM ** -0.5
    sc_mla = (MLA_NOPE + MLA_ROPE) ** -0.5
    tq_full = _pick_block(S, (512, 256, 128))
    tk_all = _pick_block(T0, (2176, 640, 512, 256, 128))
    tctx = _pick_block(C, (256, 128))
    rope128 = _rope_tables(S, T0, 128, 32)
    rope64 = _rope_tables(S, T0, 64, 16)

    xs = jnp.concatenate([x, ctx], axis=0)
    for l in range(depth):
        last = l == depth - 1
        t = "l%d_" % l
        lin = lambda a, name, unit=None, out=F32: _make_linear(t + name, _grad_parts(name), out)(
            a, have[l][name], tuple(sinks[(l, name)]), gin(unit, l) if unit else (), tok(unit, l) if unit else ())
        mx, mc = modx[l], modc[l]

        h = _make_modulate(t + "mod1", S)(xs, _seg2(mx[0], mc[0], 2), _seg2(mx[1], mc[1], 2))
        pj = lin(h, "w_in")[0]
        qa, ka, va, qb, kb, vb, cq, ckv, kpe, qd, kd, vd = _make_split(_IN_GROUPS, 1)(pj)
        rows = lambda v_: _make_split(((0, S), (S, T0)), 0)(v_)
        att = lambda unit, cfg, q_, k_, v_, tab_, s_: landed(
            _make_attention(t + unit, cfg)(q_, k_, v_, tab_, s_, gin(unit, l), tok(unit, l)), unit, l)
        tab = _na_table(p["na_rpb"][l])
        cfg_a = _WinCfg("na", NA_HEADS, NA_HEADS, sc, nx, C)
        qa_x, qa_c = rows(qa)
        oa = att("attn_a", cfg_a, qa_x, ka, va, tab, None)
        qb = _make_rope(t + "rope_b", 32)(qb, *rope128)
        kb = _make_rope(t + "rope_bk", 32)(kb, *rope128)
        snk = _sink_block(p["swa_sink"][l])
        cfg_b = _WinCfg("swa", SWA_HEADS, SWA_KV_HEADS, sc, nx, C)
        qb_x, qb_c = rows(qb)
        ob = att("attn_b", cfg_b, qb_x, kb, vb, None, snk)
        cq = _make_rmsnorm(t + "rms_cq")(cq, p["mla_q_norm"][l][None, :])
        ckv = _make_rmsnorm(t + "rms_ckv")(ckv, p["mla_kv_norm"][l][None, :])
        qh = lin(cq, "mla_w_uq")[0].reshape(T0, MLA_HEADS, MLA_NOPE + MLA_ROPE)
        kvh = lin(ckv, "mla_w_ukv")[0].reshape(T0, MLA_HEADS, MLA_NOPE + MLA_V)
        qpe = jnp.pad(qh[:, :, MLA_NOPE:], ((0, 0), (0, 0), (0, LANE - MLA_ROPE))).reshape(T0, MLA_HEADS * LANE)
        qpe = _make_rope(t + "rope_cq", 16)(qpe, *rope64).reshape(T0, MLA_HEADS, LANE)
        kpe = _make_rope(t + "rope_ck", 16)(kpe, *rope64)
        qc = jnp.concatenate([qh[:, :, :MLA_NOPE], qpe], axis=-1).reshape(T0, MLA_HEADS * 2 * LANE)
        kc = jnp.concatenate([kvh[:, :, :MLA_NOPE], jnp.broadcast_to(kpe[:, None, :], (T0, MLA_HEADS, LANE))],
                             axis=-1).reshape(T0, MLA_HEADS * 2 * LANE)
        vc = kvh[:, :, MLA_NOPE:].reshape(T0, MLA_HEADS * MLA_V)
        cfg_c = _AttnCfg(MLA_HEADS, MLA_HEADS, 2 * LANE, MLA_V, sc_mla, tq_full, tk_all)
        qc_x, qc_c = rows(qc)
        oc = att("attn_c", cfg_c, qc_x, kc, vc, None, None)
        qd = _make_rmsnorm(t + "rms_dq")(qd, p["gqa_q_norm"][l][None, :])
        kd = _make_rmsnorm(t + "rms_dk")(kd, p["gqa_k_norm"][l][None, :])
        qd = _make_rope(t + "rope_dq", 32)(qd, *rope128)
        kd = _make_rope(t + "rope_dk", 32)(kd, *rope128)
        cfg_d = _AttnCfg(GQA_HEADS, GQA_KV_HEADS, 128, 128, sc, tq_full, tk_all)
        qd_x, qd_c = rows(qd)
        od = att("attn_d", cfg_d, qd_x, kd, vd, None, None)
        mix = jnp.concatenate([oa, ob, oc, od], axis=1)
        if not last:
            full_c = lambda H, Hkv, dqk, s, sink: _AttnCfg(H, Hkv, dqk, 128, s, tctx, tctx, has_sink=sink)
            ctx_att = lambda name, cfg, q_, k_, v_, s_: _make_attention(t + name, cfg)(
                q_, k_[S:], v_[S:], None, s_, (), ())[0]
            oa_c = ctx_att("ctx_a", full_c(4, 4, 128, sc, False), qa_c, ka, va, None)
            ob_c = ctx_att("ctx_b", full_c(4, 2, 128, sc, True), qb_c, kb, vb, snk)
            oc_c = ctx_att("ctx_c", full_c(4, 4, 256, sc_mla, False), qc_c, kc, vc, None)
            od_c = ctx_att("ctx_d", full_c(4, 2, 128, sc, False), qd_c, kd, vd, None)
            mix = jnp.concatenate([mix, jnp.concatenate([oa_c, ob_c, oc_c, od_c], axis=1)], axis=0)
            res, nseg = xs, 2
        else:
            res, nseg = xs[:S], 1
        y = lin(mix, "w_out")[0]
        x1 = _make_resid_ln(t + "ln1", S, alpha)(res, y, _seg2(mx[2], mc[2], nseg), p["ln1_g"][l][None, :],
                                                 p["ln1_b"][l][None, :])
        h2 = _make_modulate(t + "mod2", S)(x1, _seg2(mx[3], mc[3], nseg), _seg2(mx[4], mc[4], nseg))
        gp = landed(lin(h2, "ffn_w_gate", "gate", _ACT), "gate", l)
        up = landed(lin(h2, "ffn_w_up", "up", _ACT), "up", l)
        z = landed(_make_conv_gate(t + "conv", S)(gp, up, p["ffn_conv_w"][l], p["ffn_conv_b"][l][None, :],
                                                  gin("conv", l), tok("conv", l)), "conv", l)
        f = landed(lin(z, "ffn_w_down", "down"), "down", l)
        xs = _make_resid_ln(t + "ln2", S, alpha)(x1, f, _seg2(mx[5], mc[5], nseg), p["ln2_g"][l][None, :],
                                                 p["ln2_b"][l][None, :])
    return xs


def _adamw(w, parts, m, v, name):
    R, C = w.shape
    P = parts.shape[0]
    c1 = 1.0 - ADAM_B1 ** ADAM_STEP
    c2 = 1.0 - ADAM_B2 ** ADAM_STEP
    per_row = C * (4 * 14 + 2 * P * parts.dtype.itemsize)
    cands = _divisors(R, 8, R)
    fitting = [d for d in cands if d * per_row <= 24 * 1024 * 1024]
    tr = max(fitting) if fitting else min(cands)

    def body(w_ref, p_ref, m_ref, v_ref, g_ref, d_ref, nm_ref, nv_ref):
        g = p_ref[0].astype(F32)
        for k in range(1, P):
            g = g + p_ref[k].astype(F32)
        mm = ADAM_B1 * m_ref[...] + (1.0 - ADAM_B1) * g
        vv = ADAM_B2 * v_ref[...] + (1.0 - ADAM_B2) * (g * g)
        g_ref[...] = g
        nm_ref[...] = mm
        nv_ref[...] = vv
        d_ref[...] = -ADAM_LR * ((mm / c1) / (jnp.sqrt(vv / c2) + ADAM_EPS) + ADAM_WD * w_ref[...])

    blk = pl.BlockSpec((tr, C), lambda i: (i, 0))
    pblk = pl.BlockSpec((P, tr, C), lambda i: (0, i, 0))
    sh = jax.ShapeDtypeStruct((R, C), F32)
    return pl.pallas_call(body, name=name, grid=(R // tr,), in_specs=[blk, pblk, blk, blk],
                          out_specs=[blk, blk, blk, blk], out_shape=[sh, sh, sh, sh],
                          compiler_params=_cparams(1))(w, parts, m, v)


def _adamw_nd(w, parts, m, v, name):
    shape = w.shape
    C = shape[-1]
    R = int(np.prod(shape[:-1])) if len(shape) > 1 else 1
    outs = _adamw(w.reshape(R, C), parts.reshape(parts.shape[0], R, C), m.reshape(R, C), v.reshape(R, C), name)
    return [o.reshape(shape) for o in outs]


def _adamw_layers(w, parts, m, v, name, riders=()):
    L, R, C = w.shape
    P = parts[0].shape[0]
    c1 = 1.0 - ADAM_B1 ** ADAM_STEP
    c2 = 1.0 - ADAM_B2 ** ADAM_STEP
    per_row = C * (4 * 14 + 2 * L * P * parts[0].dtype.itemsize)
    cands = _divisors(R, 8, R)
    fitting = [d for d in cands if d * per_row <= 24 * 1024 * 1024]
    tr = max(fitting) if fitting else min(cands)
    nt = R // tr

    def body(*refs):
        w_ref, p_refs, (m_ref, v_ref, g_ref, d_ref, nm_ref, nv_ref) = refs[0], refs[1:1 + L], refs[1 + L:]
        layer = pl.program_id(0)
        for k in range(L):
            @pl.when(layer == k)
            def _(p_ref=p_refs[k]):
                g = p_ref[0].astype(F32)
                for j in range(1, P):
                    g = g + p_ref[j].astype(F32)
                mm = ADAM_B1 * m_ref[0] + (1.0 - ADAM_B1) * g
                vv = ADAM_B2 * v_ref[0] + (1.0 - ADAM_B2) * (g * g)
                g_ref[0] = g
                nm_ref[0] = mm
                nv_ref[0] = vv
                d_ref[0] = -ADAM_LR * ((mm / c1) / (jnp.sqrt(vv / c2) + ADAM_EPS) + ADAM_WD * w_ref[0])

    blk = pl.BlockSpec((1, tr, C), lambda l, i: (l, i, 0))
    pblk = [pl.BlockSpec((P, tr, C), lambda l, i, k=k: (0, jnp.where(l == k, i, jnp.where(l < k, 0, nt - 1)), 0))
            for k in range(L)]
    sh = jax.ShapeDtypeStruct((L, R, C), F32)
    return _pcall(body, name=name, grid=(L, nt), in_specs=[blk] + pblk + [blk, blk],
                  out_specs=[blk, blk, blk, blk], out_shape=[sh, sh, sh, sh], args=[w, *parts, m, v], riders=riders)


def _silu(v):
    return v * (1.0 / (1.0 + jnp.exp(-v)))


def _ada_rows(c_all, c_ctx):
    return jnp.concatenate([c_all, jnp.broadcast_to(c_ctx[None, :], (N_DEV, c_ctx.shape[0]))], axis=0)


def _silu_rows(rows, name):
    def body(r_ref, o_ref):
        o_ref[...] = _silu(r_ref[...])

    return pl.pallas_call(body, name=name, out_shape=jax.ShapeDtypeStruct(rows.shape, F32))(rows)


_SMALL = ("c_ctx", "na_rpb", "swa_sink", "mla_q_norm", "mla_kv_norm", "gqa_q_norm", "gqa_k_norm",
          "ln1_g", "ln1_b", "ffn_conv_b", "ln2_g", "ln2_b")
_NAMES = ("c_ctx", "w_ada", "b_ada", "w_in", "na_rpb", "swa_sink", "mla_q_norm", "mla_kv_norm", "mla_w_uq",
          "mla_w_ukv", "gqa_q_norm", "gqa_k_norm", "w_out", "ln1_g", "ln1_b", "ffn_w_gate", "ffn_w_up",
          "ffn_conv_w", "ffn_conv_b", "ffn_w_down", "ln2_g", "ln2_b")


def _full_from_cols(g, pad_in=False):
    _, L, K, n = g.shape
    full = jnp.transpose(g, (1, 2, 0, 3)).reshape(L, K, N_DEV * n)
    if pad_in:
        full = jnp.concatenate([full[:, :, :KPE_END], jnp.zeros((L, K, 64), full.dtype), full[:, :, KPE_END:]], axis=-1)
    return full


def _pack_small(tree, extra=None):
    flat = [tree[n].reshape(-1) for n in _SMALL]
    flat.append(jnp.zeros((1,), F32) if extra is None else extra.reshape(1))
    v = jnp.concatenate(flat)
    n = v.shape[0]
    padded = -(-n // 1024) * 1024
    return jnp.pad(v, (0, padded - n)).reshape(padded // LANE, LANE)


def _unpack_small(mat, like):
    v = mat.reshape(-1)
    out, o = {}, 0
    for n in _SMALL:
        k = int(np.prod(like[n].shape))
        out[n] = v[o:o + k].reshape(like[n].shape)
        o += k
    return out, v[o]


def kernel(x, c, ctx, c_ctx, w_ada, b_ada, w_in, na_rpb, swa_sink, mla_q_norm, mla_kv_norm, mla_w_uq, mla_w_ukv, gqa_q_norm, gqa_k_norm, w_out, ln1_g, ln1_b, ffn_w_gate, ffn_w_up, ffn_conv_w, ffn_conv_b, ffn_w_down, ln2_g, ln2_b, loss_target, m_c_ctx, m_w_ada, m_b_ada, m_w_in, m_na_rpb, m_swa_sink, m_mla_q_norm, m_mla_kv_norm, m_mla_w_uq, m_mla_w_ukv, m_gqa_q_norm, m_gqa_k_norm, m_w_out, m_ln1_g, m_ln1_b, m_ffn_w_gate, m_ffn_w_up, m_ffn_conv_w, m_ffn_conv_b, m_ffn_w_down, m_ln2_g, m_ln2_b, v_c_ctx, v_w_ada, v_b_ada, v_w_in, v_na_rpb, v_swa_sink, v_mla_q_norm, v_mla_kv_norm, v_mla_w_uq, v_mla_w_ukv, v_gqa_q_norm, v_gqa_k_norm, v_w_out, v_ln1_g, v_ln1_b, v_ffn_w_gate, v_ffn_w_up, v_ffn_conv_w, v_ffn_conv_b, v_ffn_w_down, v_ln2_g, v_ln2_b):
    W = dict(c_ctx=c_ctx, w_ada=w_ada, b_ada=b_ada, w_in=w_in, na_rpb=na_rpb, swa_sink=swa_sink,
             mla_q_norm=mla_q_norm, mla_kv_norm=mla_kv_norm, mla_w_uq=mla_w_uq, mla_w_ukv=mla_w_ukv,
             gqa_q_norm=gqa_q_norm, gqa_k_norm=gqa_k_norm, w_out=w_out, ln1_g=ln1_g, ln1_b=ln1_b,
             ffn_w_gate=ffn_w_gate, ffn_w_up=ffn_w_up, ffn_conv_w=ffn_conv_w, ffn_conv_b=ffn_conv_b,
             ffn_w_down=ffn_w_down, ln2_g=ln2_g, ln2_b=ln2_b)
    M = dict(c_ctx=m_c_ctx, w_ada=m_w_ada, b_ada=m_b_ada, w_in=m_w_in, na_rpb=m_na_rpb, swa_sink=m_swa_sink,
             mla_q_norm=m_mla_q_norm, mla_kv_norm=m_mla_kv_norm, mla_w_uq=m_mla_w_uq, mla_w_ukv=m_mla_w_ukv,
             gqa_q_norm=m_gqa_q_norm, gqa_k_norm=m_gqa_k_norm, w_out=m_w_out, ln1_g=m_ln1_g, ln1_b=m_ln1_b,
             ffn_w_gate=m_ffn_w_gate, ffn_w_up=m_ffn_w_up, ffn_conv_w=m_ffn_conv_w, ffn_conv_b=m_ffn_conv_b,
             ffn_w_down=m_ffn_w_down, ln2_g=m_ln2_g, ln2_b=m_ln2_b)
    V = dict(c_ctx=v_c_ctx, w_ada=v_w_ada, b_ada=v_b_ada, w_in=v_w_in, na_rpb=v_na_rpb, swa_sink=v_swa_sink,
             mla_q_norm=v_mla_q_norm, mla_kv_norm=v_mla_kv_norm, mla_w_uq=v_mla_w_uq, mla_w_ukv=v_mla_w_ukv,
             gqa_q_norm=v_gqa_q_norm, gqa_k_norm=v_gqa_k_norm, w_out=v_w_out, ln1_g=v_ln1_g, ln1_b=v_ln1_b,
             ffn_w_gate=v_ffn_w_gate, ffn_w_up=v_ffn_w_up, ffn_conv_w=v_ffn_conv_w, ffn_conv_b=v_ffn_conv_b,
             ffn_w_down=v_ffn_w_down, ln2_g=v_ln2_g, ln2_b=v_ln2_b)
    L, D, n_ada = w_ada.shape
    me = 4 * lax.axis_index("x") + 2 * lax.axis_index("y") + lax.axis_index("c")
    xs, ctxs, tgt = x[0], ctx[0], loss_target[0]

    c_all = _exchange(c, False, "gather_c").reshape(N_DEV, D)
    a_rows = _silu_rows(_ada_rows(c_all, c_ctx), "ada_silu")
    b_mine = lax.dynamic_slice(b_ada, (0, me * n_ada), (L, n_ada))
    mod_mine = jnp.stack([_matmul(a_rows, w_ada[l], "nn", "ada_fwd_l%d" % l) + b_mine[l][None, :] for l in range(L)])
    mod_all = _exchange(mod_mine, False, "gather_mod")
    mod_all = jnp.transpose(mod_all, (1, 2, 0, 3)).reshape(L, 2 * N_DEV, N_DEV * n_ada)
    modx = lax.dynamic_slice(mod_all, (0, me, 0), (L, 1, 6 * D)).reshape(L, 6, D)
    modc = mod_all[:, N_DEV].reshape(L, 6, D)

    big = _BIG_COL + _BIG_ROW
    views = {n: _shard_view(n, W[n]) for n in big}
    shards = [{n: _cut(n, views[n][l].astype(BF16)) for n in big} for l in range(L)]
    full = [{n: _full_weight(n, [_gather_by_chip(views[n][0].astype(BF16), "gather_l0_" + n)]) for n in _IN_GROUP}]
    full += [{} for _ in range(1, L)]
    conv_w_full = _full_from_cols(_exchange(ffn_conv_w, False, "gather_conv_w"))

    toks = [{n: tuple(jnp.zeros((N_DEV,) + s.shape, BF16) for s in shards[l][n]) for n in big} for l in range(L)]
    small = {n: W[n] for n in _SMALL if n != "c_ctx"}
    small["ffn_conv_w"] = conv_w_full

    def forward(xv, mxv, mcv, sm, tk):
        return _local_forward(xv, ctxs, mxv, mcv, sm, full, shards, tk)

    y, backward = jax.vjp(forward, xs, modx, modc, small, toks)
    loss_local, dy = _loss_and_grad(y, tgt, "loss_head")
    grad_x, dmodx, dmodc, dsmall, dtoks = backward(dy)

    dmods = _exchange(jnp.stack([dmodx.reshape(L, 6 * D), dmodc.reshape(L, 6 * D)]), False, "gather_dmod")
    dm_rows = jnp.concatenate([dmods[:, 0], dmods[:, 1]], axis=0)
    dm_rows = jnp.transpose(dm_rows, (1, 0, 2))
    dm_mine = lax.dynamic_slice(dm_rows, (0, 0, me * n_ada), (L, 2 * N_DEV, n_ada))
    g_w_ada = [_matmul(a_rows, dm_mine[l], "tn", "ada_dw_l%d" % l) for l in range(L)]
    d_rows = sum(_matmul(dm_mine[l], w_ada[l], "nt", "ada_da_l%d" % l) for l in range(L))
    sig = 1.0 / (1.0 + jnp.exp(-c_ctx))
    d_c_ctx_part = jnp.sum(d_rows[N_DEV:], axis=0) * (sig * (1.0 + c_ctx * (1.0 - sig)))

    outs = {}
    pieces = [{n: list(dtoks[l][n]) for n in big} for l in range(L)]
    late = {"ffn_w_gate": (("w_in", 0),), "ffn_w_up": (("w_in", 1),), "ffn_w_down": (("w_in", 2),),
            "w_ada": (("w_in", 3),), "w_out": (("mla_w_uq", 0), ("mla_w_ukv", 0))}

    def update(n, w, g_parts, m, v):
        res = _adamw_layers(w, g_parts, m, v, "adamw_" + n, _riders([pieces[0][a][k] for a, k in late.get(n, ())], True))
        for (a, k), got in zip(late.get(n, ()), res[4:]):
            pieces[0][a][k] = got
        return res[:4]

    def whole(l, n):
        return jnp.concatenate(pieces[l][n], axis=-1) if len(pieces[l][n]) > 1 else pieces[l][n][0]

    outs["w_ada"] = update("w_ada", w_ada, [g[None] for g in g_w_ada], M["w_ada"], V["w_ada"])
    for n in [n for n in big if n not in _IN_GROUP] + list(_IN_GROUP):
        res = update(n, views[n], [whole(l, n) for l in range(L)], _shard_view(n, M[n]), _shard_view(n, V[n]))
        outs[n] = [_shard_view(n, r) for r in res]
    dcw = dsmall.pop("ffn_conv_w")
    cw_parts = _exchange(jnp.transpose(dcw.reshape(L, 3, N_DEV, -1), (2, 0, 1, 3)), True, "scatter_conv_w")
    outs["ffn_conv_w"] = _adamw_layers(ffn_conv_w, [cw_parts[:, l] for l in range(L)], M["ffn_conv_w"],
                                       V["ffn_conv_w"], "adamw_conv_w")
    outs["b_ada"] = _adamw_nd(b_ada, jnp.transpose(dm_rows, (1, 0, 2)), M["b_ada"], V["b_ada"], "adamw_b_ada")

    dsmall["c_ctx"] = d_c_ctx_part
    small_parts = _exchange(_pack_small(dsmall, extra=loss_local), False, "gather_small")
    w_small = _pack_small({n: W[n] for n in _SMALL})
    g_s, d_s, nm_s, nv_s = _adamw(w_small, small_parts, _pack_small({n: M[n] for n in _SMALL}),
                                  _pack_small({n: V[n] for n in _SMALL}), "adamw_small")
    like = {n: W[n] for n in _SMALL}
    g_small, loss = _unpack_small(g_s, like)
    unpacked = [g_small, _unpack_small(d_s, like)[0], _unpack_small(nm_s, like)[0], _unpack_small(nv_s, like)[0]]
    for n in _SMALL:
        outs[n] = [u[n] for u in unpacked]

    result = [loss, grad_x[None]]
    for k in range(4):
        result += [outs[n][k] for n in _NAMES]
    return tuple(result)
```

```python
import functools
import math

import numpy as np
import jax
import jax.numpy as jnp
from jax import lax
from jax.experimental import pallas as pl
from jax.experimental.pallas import tpu as pltpu

F32 = jnp.float32
BF16 = jnp.bfloat16
_MXU = jnp.bfloat16
_ACT = jnp.bfloat16

N_DEV = 8
GRID_W = 64
HEAD_DIM = 128
NA_HEADS, NA_WIN_R, NA_WIN_C = 4, 8, 16
SWA_HEADS, SWA_KV_HEADS, SWA_WINDOW = 4, 2, 128
MLA_HEADS, MLA_Q_LORA, MLA_KV_LORA, MLA_NOPE, MLA_ROPE, MLA_V = 4, 384, 128, 128, 64, 128
GQA_HEADS, GQA_KV_HEADS = 4, 2
ROPE_THETA = 10000.0
EPS = 1e-6
NEG = -1e30
LOG2E = 1.4426950408889634
IN_SIZES = (512, 512, 512, 512, 256, 256, MLA_Q_LORA, MLA_KV_LORA, MLA_ROPE, 512, 256, 256)
IN_COLS = sum(IN_SIZES)
KPE_END = sum(IN_SIZES[:9])
IN_COLS_PAD = IN_COLS + 64
ADAM_LR, ADAM_B1, ADAM_B2, ADAM_EPS, ADAM_WD, ADAM_STEP = 0.001, 0.9, 0.999, 1e-08, 0.01, 10

LANE = 128
ROW_TILE = 256
WIDE_ROW_TILES = (1088, 1024, 640, 512, 256)
ATT_BLOCK = 128
VMEM_LIMIT = 56 * 1024 * 1024
MM_BUDGET = 36 * 1024 * 1024
HBM_BPS = 3.0e12
MXU_FLOPS = 9.0e14
VMEM_BPS = 4.0e12
STEP_S = 0.4e-6


def _cparams(n_axes):
    return pltpu.CompilerParams(dimension_semantics=("arbitrary",) * n_axes, vmem_limit_bytes=VMEM_LIMIT)


def _exchange_copies(in_ref, out_ref, send_sems, recv_sems, local_sem, scatter, rows=None):
    x, y, c = lax.axis_index("x"), lax.axis_index("y"), lax.axis_index("c")
    me = 4 * x + 2 * y + c
    cut = (lambda ref: ref) if rows is None else (lambda ref: ref.at[rows])
    copies = [pltpu.make_async_copy(cut(in_ref.at[me] if scatter else in_ref), cut(out_ref.at[me]), local_sem)]
    for k in range(1, N_DEV):
        px, py, pc = (x + (k >> 2)) % 2, (y + ((k >> 1) & 1)) % 2, (c + (k & 1)) % 2
        peer = 4 * px + 2 * py + pc
        copies.append(pltpu.make_async_remote_copy(
            src_ref=cut(in_ref.at[peer] if scatter else in_ref), dst_ref=cut(out_ref.at[me]),
            send_sem=send_sems.at[k - 1], recv_sem=recv_sems.at[k - 1],
            device_id=(px, py, pc), device_id_type=pl.DeviceIdType.MESH))
    return copies


PIECE_ROWS = 16
MAX_PIECES = 32


def _pieces(rows, steps):
    if rows % PIECE_ROWS:
        return 1
    units = rows // PIECE_ROWS
    return max(d for d in range(1, min(MAX_PIECES, steps) + 1) if units % d == 0)


def _exchange_out_shape(arr, scatter):
    return jax.ShapeDtypeStruct(arr.shape if scatter else (N_DEV,) + arr.shape, arr.dtype)


_EXCHANGE_SEMS = (pltpu.SemaphoreType.DMA((N_DEV - 1,)), pltpu.SemaphoreType.DMA((N_DEV - 1,)),
                  pltpu.SemaphoreType.DMA)


def _exchange(inp, scatter, name):
    def body(in_ref, out_ref, send_sems, recv_sems, local_sem):
        copies = _exchange_copies(in_ref, out_ref, send_sems, recv_sems, local_sem, scatter)
        for cp in copies:
            cp.start()
        for cp in copies:
            cp.wait()

    hbm = pl.BlockSpec(memory_space=pl.ANY)
    return pl.pallas_call(
        body, name=name, in_specs=[hbm], out_specs=hbm, out_shape=_exchange_out_shape(inp, scatter),
        scratch_shapes=list(_EXCHANGE_SEMS), compiler_params=pltpu.CompilerParams(has_side_effects=True),
    )(inp)


def _gather_by_chip(inp, name):
    def body(x_ref, out_ref, send_sems, recv_sems, local_sem):
        x, y, c = lax.axis_index("x"), lax.axis_index("y"), lax.axis_index("c")
        me, sibling = (x, y, c), (x, y, 1 - c)
        chips = [(1 - x, y), (x, 1 - y), (1 - x, 1 - y)]

        def slot(px, py, pc):
            return out_ref.at[4 * px + 2 * py + pc]

        def copy(k, block, to, src=None):
            return pltpu.make_async_remote_copy(
                src_ref=slot(*block) if src is None else src, dst_ref=slot(*block), send_sem=send_sems.at[k],
                recv_sem=recv_sems.at[k], device_id=to, device_id_type=pl.DeviceIdType.MESH)

        mine = pltpu.make_async_copy(x_ref, slot(*me), local_sem)
        mine.start()
        first = [copy(0, me, sibling, src=x_ref)]
        first += [copy(1 + j, me, (*chip, c), src=x_ref) for j, chip in enumerate(chips)]
        for cp in first:
            cp.start()
        passed = [copy(4 + j, (*chip, c), sibling) for j, chip in enumerate(chips)]
        for j, chip in enumerate(chips):
            copy(1 + j, (*chip, c), me).wait_recv()
            passed[j].start()
        copy(0, sibling, me).wait_recv()
        for j, chip in enumerate(chips):
            copy(4 + j, (*chip, 1 - c), me).wait_recv()
        for cp in first + passed:
            cp.wait_send()
        mine.wait()

    hbm = pl.BlockSpec(memory_space=pl.ANY)
    return pl.pallas_call(
        body, name=name, in_specs=[hbm], out_specs=hbm, out_shape=_exchange_out_shape(inp, False),
        scratch_shapes=list(_EXCHANGE_SEMS), compiler_params=pltpu.CompilerParams(has_side_effects=True),
    )(inp)


def _pcall(body, *, name, grid, in_specs, out_specs, out_shape, args, scratch_shapes=(), riders=()):
    in_specs, out_specs, out_shape = list(in_specs), list(out_specs), list(out_shape)
    scratch, args = list(scratch_shapes), list(args)
    n_in, n_out, n_scr, n_r = len(in_specs), len(out_specs), len(scratch), len(riders)
    hbm = pl.BlockSpec(memory_space=pl.ANY)
    for arr, scatter in riders:
        in_specs.append(hbm)
        args.append(arr)
        out_specs.append(hbm)
        out_shape.append(_exchange_out_shape(arr, scatter))
        scratch += list(_EXCHANGE_SEMS)

    def wrapped(*refs):
        ins, r_in = refs[:n_in], refs[n_in:n_in + n_r]
        o0 = n_in + n_r
        outs, r_out = refs[o0:o0 + n_out], refs[o0 + n_out:o0 + n_out + n_r]
        s0 = o0 + n_out + n_r
        scr, sems = refs[s0:s0 + n_scr], refs[s0 + n_scr:]

        def copies(r, rows=None):
            return _exchange_copies(r_in[r], r_out[r], sems[3 * r], sems[3 * r + 1], sems[3 * r + 2], riders[r][1],
                                    rows)

        if n_r:
            steps = int(np.prod(grid))
            step = functools.reduce(lambda u, w: u * w[1] + w[0],
                                    [(pl.program_id(a), n) for a, n in enumerate(grid)], 0)
            for r, (arr, scatter) in enumerate(riders):
                block_rows = arr.shape[1] if scatter else arr.shape[0]
                n_pieces = _pieces(block_rows, steps)
                every, piece = steps // n_pieces, block_rows // n_pieces

                @pl.when((step % every == 0) & (step // every < n_pieces))
                def _(r=r, every=every, piece=piece, whole=n_pieces == 1):
                    rows = None if whole else pl.ds(pl.multiple_of((step // every) * piece, PIECE_ROWS), piece)
                    for cp in copies(r, rows):
                        cp.start()

        body(*ins, *outs, *scr)

        if n_r:
            @pl.when(step == steps - 1)
            def _():
                for r in range(n_r):
                    for cp in copies(r):
                        cp.wait()

    params = pltpu.CompilerParams(dimension_semantics=("arbitrary",) * len(grid), vmem_limit_bytes=VMEM_LIMIT,
                                  has_side_effects=bool(riders))
    return list(pl.pallas_call(wrapped, name=name, grid=grid, in_specs=in_specs, out_specs=out_specs,
                               out_shape=out_shape, scratch_shapes=scratch, compiler_params=params)(*args))


def _riders(arrays, scatter):
    return [(a, scatter) for a in arrays]


def _nones(arrays):
    return tuple(None for _ in arrays)


def _divisors(n, step, cap):
    out = [d for d in range(step, min(n, cap) + 1, step) if n % d == 0]
    if n <= cap and n not in out:
        out.append(n)
    return out


def _mm_tiles(M, N, K, sa, sb, m_step, k_step):
    best, best_cost = None, None
    for tm in _divisors(M, m_step, 2304):
        for tn in _divisors(N, LANE, 2304):
            for tk in _divisors(K, k_step, 2304):
                vm = 2 * (tm * tk * sa + tk * tn * sb) + 3 * tm * tn * 4
                if vm > MM_BUDGET:
                    continue
                traffic = M * K * sa * (N // tn) + K * N * sb * (M // tm) + M * N * 4
                steps = (M // tm) * (N // tn) * (K // tk)
                passes = 1 if tk == K else 3
                busy = steps * (2 * tm * tn * tk / MXU_FLOPS + passes * tm * tn * 4 / VMEM_BPS)
                cost = max(traffic / HBM_BPS, busy) + steps * STEP_S
                if m_step == LANE:
                    cost = traffic / HBM_BPS + steps * STEP_S
                if tm % 256 or tn % 256:
                    cost *= 1.05
                if best_cost is None or cost < best_cost:
                    best, best_cost = (tm, tn, tk), cost
    assert best is not None, (M, N, K)
    return best


def _matmul(a, b, mode, name, riders=(), out_dtype=F32):
    if mode == "nn":
        (M, K), (K2, N) = a.shape, b.shape
    elif mode == "nt":
        (M, K), (N, K2) = a.shape, b.shape
    else:
        (K, M), (K2, N) = a.shape, b.shape
    assert K == K2, (a.shape, b.shape, mode)
    tm, tn, tk = _mm_tiles(M, N, K, a.dtype.itemsize, b.dtype.itemsize,
                           LANE if mode == "tn" else 8, 8 if mode == "tn" else LANE)
    nk = K // tk
    dims = {"nn": (((1,), (0,)), ((), ())), "nt": (((1,), (1,)), ((), ())), "tn": (((0,), (0,)), ((), ()))}[mode]

    own_acc = nk > 1 and out_dtype != F32

    def body(a_ref, b_ref, o_ref, *scr):
        prod = lax.dot_general(a_ref[...].astype(_MXU), b_ref[...].astype(_MXU), dims, preferred_element_type=F32)
        if nk == 1:
            o_ref[...] = prod.astype(out_dtype)
        else:
            acc_ref = scr[0] if own_acc else o_ref
            k = pl.program_id(2)

            @pl.when(k == 0)
            def _():
                acc_ref[...] = prod

            @pl.when(k > 0)
            def _():
                acc_ref[...] += prod

            if own_acc:
                @pl.when(k == nk - 1)
                def _():
                    o_ref[...] = acc_ref[...].astype(out_dtype)

    if mode == "nn":
        a_spec = pl.BlockSpec((tm, tk), lambda i, j, k: (i, k))
        b_spec = pl.BlockSpec((tk, tn), lambda i, j, k: (k, j))
    elif mode == "nt":
        a_spec = pl.BlockSpec((tm, tk), lambda i, j, k: (i, k))
        b_spec = pl.BlockSpec((tn, tk), lambda i, j, k: (j, k))
    else:
        a_spec = pl.BlockSpec((tk, tm), lambda i, j, k: (k, i))
        b_spec = pl.BlockSpec((tk, tn), lambda i, j, k: (k, j))
    outs = _pcall(body, name=name, grid=(M // tm, N // tn, nk), in_specs=[a_spec, b_spec],
                  out_specs=[pl.BlockSpec((tm, tn), lambda i, j, k: (i, j))],
                  out_shape=[jax.ShapeDtypeStruct((M, N), out_dtype)], args=[a, b],
                  scratch_shapes=[pltpu.VMEM((tm, tn), F32)] if own_acc else [], riders=riders)
    return outs if riders else outs[0]


def _make_linear(tag, to_parts, out_dtype=F32):
    def run(a, w, gin, tok):
        if gin:
            y, *got = _matmul(a, w, "nn", tag + "_fwd", riders=_riders(gin, False), out_dtype=out_dtype)
        else:
            y, got = _matmul(a, w, "nn", tag + "_fwd", out_dtype=out_dtype), []
        return y, tuple(got), tok

    @jax.custom_vjp
    def linear(a, w, sink, gin, tok):
        return run(a, w, gin, tok)

    def fwd(a, w, sink, gin, tok):
        return run(a, w, gin, tok), (a, w, gin)

    def bwd(res, cts):
        a, w, gin = res
        dy, _, dtok = cts
        dw = _matmul(a, dy, "tn", tag + "_bwd_dw")
        if dtok:
            da, *recv = _matmul(dy, w, "nt", tag + "_bwd_da", riders=_riders(dtok, True), out_dtype=a.dtype)
        else:
            da, recv = _matmul(dy, w, "nt", tag + "_bwd_da", out_dtype=a.dtype), []
        return da, jnp.zeros_like(w), to_parts(dw), _nones(gin), tuple(recv)

    linear.defvjp(fwd, bwd)
    return linear


def _seg_map(n_x_tiles):
    return lambda i: (jnp.where(i >= n_x_tiles, 1, 0), 0, 0)


def _ln_stats(x):
    mu = jnp.mean(x, axis=-1, keepdims=True)
    xc = x - mu
    var = jnp.mean(xc * xc, axis=-1, keepdims=True)
    rstd = lax.rsqrt(var + EPS)
    return xc * rstd, rstd


def _make_modulate(tag, n_x):
    def call_fwd(x, shift, scale):
        T, D = x.shape
        nxt = n_x // ROW_TILE

        def body(x_ref, sh_ref, sc_ref, o_ref):
            xhat, _ = _ln_stats(x_ref[...])
            o_ref[...] = xhat * (1.0 + sc_ref[0]) + sh_ref[0]

        row = pl.BlockSpec((ROW_TILE, D), lambda i: (i, 0))
        seg = pl.BlockSpec((1, 1, D), _seg_map(nxt))
        return pl.pallas_call(body, name=tag + "_fwd", grid=(T // ROW_TILE,), in_specs=[row, seg, seg],
                              out_specs=row, out_shape=jax.ShapeDtypeStruct((T, D), F32),
                              compiler_params=_cparams(1))(x, shift, scale)

    def call_bwd(x, scale, dh):
        T, D = x.shape
        nxt = n_x // ROW_TILE
        nseg = scale.shape[0]

        def body(x_ref, sc_ref, dh_ref, dx_ref, dsh_ref, dsc_ref):
            i = pl.program_id(0)

            @pl.when((i == 0) | (i == nxt))
            def _():
                dsh_ref[...] = jnp.zeros_like(dsh_ref)
                dsc_ref[...] = jnp.zeros_like(dsc_ref)

            xhat, rstd = _ln_stats(x_ref[...])
            dh = dh_ref[...]
            dsh_ref[0] += jnp.sum(dh, axis=0, keepdims=True)
            dsc_ref[0] += jnp.sum(dh * xhat, axis=0, keepdims=True)
            dxh = dh * (1.0 + sc_ref[0])
            m1 = jnp.mean(dxh, axis=-1, keepdims=True)
            m2 = jnp.mean(dxh * xhat, axis=-1, keepdims=True)
            dx_ref[...] = rstd * (dxh - m1 - xhat * m2)

        row = pl.BlockSpec((ROW_TILE, D), lambda i: (i, 0))
        seg = pl.BlockSpec((1, 1, D), _seg_map(nxt))
        segshape = jax.ShapeDtypeStruct((nseg, 1, D), F32)
        return pl.pallas_call(body, name=tag + "_bwd", grid=(T // ROW_TILE,), in_specs=[row, seg, row],
                              out_specs=[row, seg, seg],
                              out_shape=[jax.ShapeDtypeStruct((T, D), F32), segshape, segshape],
                              compiler_params=_cparams(1))(x, scale, dh)

    @jax.custom_vjp
    def modulate(x, shift, scale):
        return call_fwd(x, shift, scale)

    def fwd(x, shift, scale):
        return call_fwd(x, shift, scale), (x, scale)

    def bwd(res, dh):
        x, scale = res
        return tuple(call_bwd(x, scale, dh))

    modulate.defvjp(fwd, bwd)
    return modulate


def _make_resid_ln(tag, n_x, alpha):
    def call_fwd(x, y, gate, g, b):
        T, D = x.shape
        nxt = n_x // ROW_TILE

        def body(x_ref, y_ref, gt_ref, g_ref, b_ref, o_ref):
            u = alpha * x_ref[...] + gt_ref[0] * y_ref[...]
            uhat, _ = _ln_stats(u)
            o_ref[...] = uhat * g_ref[...] + b_ref[...]

        row = pl.BlockSpec((ROW_TILE, D), lambda i: (i, 0))
        seg = pl.BlockSpec((1, 1, D), _seg_map(nxt))
        vec = pl.BlockSpec((1, D), lambda i: (0, 0))
        return pl.pallas_call(body, name=tag + "_fwd", grid=(T // ROW_TILE,), in_specs=[row, row, seg, vec, vec],
                              out_specs=row, out_shape=jax.ShapeDtypeStruct((T, D), F32),
                              compiler_params=_cparams(1))(x, y, gate, g, b)

    def call_bwd(x, y, gate, g, do):
        T, D = x.shape
        nxt = n_x // ROW_TILE
        nseg = gate.shape[0]

        def body(x_ref, y_ref, gt_ref, g_ref, do_ref, dx_ref, dy_ref, dgt_ref, dg_ref, db_ref):
            i = pl.program_id(0)

            @pl.when(i == 0)
            def _():
                dg_ref[...] = jnp.zeros_like(dg_ref)
                db_ref[...] = jnp.zeros_like(db_ref)

            @pl.when((i == 0) | (i == nxt))
            def _():
                dgt_ref[...] = jnp.zeros_like(dgt_ref)

            y = y_ref[...]
            gate_v = gt_ref[0]
            uhat, rstd = _ln_stats(alpha * x_ref[...] + gate_v * y)
            do = do_ref[...]
            dg_ref[...] += jnp.sum(do * uhat, axis=0, keepdims=True)
            db_ref[...] += jnp.sum(do, axis=0, keepdims=True)
            duh = do * g_ref[...]
            m1 = jnp.mean(duh, axis=-1, keepdims=True)
            m2 = jnp.mean(duh * uhat, axis=-1, keepdims=True)
            du = rstd * (duh - m1 - uhat * m2)
            dx_ref[...] = alpha * du
            dy_ref[...] = gate_v * du
            dgt_ref[0] += jnp.sum(du * y, axis=0, keepdims=True)

        row = pl.BlockSpec((ROW_TILE, D), lambda i: (i, 0))
        seg = pl.BlockSpec((1, 1, D), _seg_map(nxt))
        vec = pl.BlockSpec((1, D), lambda i: (0, 0))
        rs = jax.ShapeDtypeStruct((T, D), F32)
        vs = jax.ShapeDtypeStruct((1, D), F32)
        return pl.pallas_call(body, name=tag + "_bwd", grid=(T // ROW_TILE,), in_specs=[row, row, seg, vec, row],
                              out_specs=[row, row, seg, vec, vec],
                              out_shape=[rs, rs, jax.ShapeDtypeStruct((nseg, 1, D), F32), vs, vs],
                              compiler_params=_cparams(1))(x, y, gate, g, do)

    @jax.custom_vjp
    def resid_ln(x, y, gate, g, b):
        return call_fwd(x, y, gate, g, b)

    def fwd(x, y, gate, g, b):
        return call_fwd(x, y, gate, g, b), (x, y, gate, g)

    def bwd(res, do):
        x, y, gate, g = res
        return tuple(call_bwd(x, y, gate, g, do))

    resid_ln.defvjp(fwd, bwd)
    return resid_ln


def _make_rmsnorm(tag):
    def call_fwd(x, g):
        T, W = x.shape
        gw = g.shape[1]

        tr = _pick_block(T, WIDE_ROW_TILES)

        def body(x_ref, g_ref, o_ref):
            for gi in range(W // gw):
                cols = slice(gi * gw, (gi + 1) * gw)
                x = x_ref[:, cols]
                r = lax.rsqrt(jnp.mean(x * x, axis=-1, keepdims=True) + EPS)
                o_ref[:, cols] = x * r * g_ref[...]

        blk = pl.BlockSpec((tr, W), lambda i: (i, 0))
        vec = pl.BlockSpec((1, gw), lambda i: (0, 0))
        return pl.pallas_call(body, name=tag + "_fwd", grid=(T // tr,), in_specs=[blk, vec],
                              out_specs=blk, out_shape=jax.ShapeDtypeStruct((T, W), F32),
                              compiler_params=_cparams(1))(x, g)

    def call_bwd(x, g, dy):
        T, W = x.shape
        gw = g.shape[1]
        tr = _pick_block(T, WIDE_ROW_TILES)

        def body(x_ref, g_ref, dy_ref, dx_ref, dg_ref):
            @pl.when(pl.program_id(0) == 0)
            def _():
                dg_ref[...] = jnp.zeros_like(dg_ref)

            for gi in range(W // gw):
                cols = slice(gi * gw, (gi + 1) * gw)
                x = x_ref[:, cols]
                r = lax.rsqrt(jnp.mean(x * x, axis=-1, keepdims=True) + EPS)
                xn = x * r
                dy = dy_ref[:, cols]
                dg_ref[...] += jnp.sum(dy * xn, axis=0, keepdims=True)
                dxn = dy * g_ref[...]
                dx_ref[:, cols] = r * (dxn - xn * jnp.mean(dxn * xn, axis=-1, keepdims=True))

        blk = pl.BlockSpec((tr, W), lambda i: (i, 0))
        vec = pl.BlockSpec((1, gw), lambda i: (0, 0))
        return pl.pallas_call(body, name=tag + "_bwd", grid=(T // tr,), in_specs=[blk, vec, blk],
                              out_specs=[blk, vec],
                              out_shape=[jax.ShapeDtypeStruct((T, W), F32), jax.ShapeDtypeStruct((1, gw), F32)],
                              compiler_params=_cparams(1))(x, g, dy)

    @jax.custom_vjp
    def rmsnorm(x, g):
        return call_fwd(x, g)

    def fwd(x, g):
        return call_fwd(x, g), (x, g)

    def bwd(res, dy):
        x, g = res
        return tuple(call_bwd(x, g, dy))

    rmsnorm.defvjp(fwd, bwd)
    return rmsnorm


def _rope_tables(n_x, n_all, width, half):
    t = np.arange(n_all)
    row, col = t // GRID_W, t % GRID_W
    lane = np.arange(LANE)
    inside = lane < width
    axis_is_col = (lane // (2 * half)) % 2 == 1
    f = (lane % (2 * half)) % half
    inv_freq = ROPE_THETA ** (-(f.astype(np.float64)) / half)
    pos = np.where(axis_is_col[None, :], col[:, None], row[:, None]).astype(np.float64)
    ang = (pos.astype(np.float32) * inv_freq.astype(np.float32)[None, :]).astype(np.float32)
    live = inside[None, :] & (t < n_x)[:, None]
    cos = np.where(live, np.cos(ang), 1.0).astype(np.float32)
    sin = np.where(live, np.sin(ang), 0.0).astype(np.float32)
    first = (lane % (2 * half)) < half
    s_lo = np.where(first[None, :], -sin, 0.0).astype(np.float32)
    s_hi = np.where(first[None, :], 0.0, sin).astype(np.float32)
    return jnp.asarray(cos), jnp.asarray(s_lo), jnp.asarray(s_hi)


def _make_rope(tag, half):
    def call(x, cos, s_lo, s_hi, transpose, name):
        T, W = x.shape

        tr = _pick_block(T, WIDE_ROW_TILES)

        def body(x_ref, c_ref, lo_ref, hi_ref, o_ref):
            c, lo, hi = c_ref[...], lo_ref[...], hi_ref[...]
            for cb in range(W // LANE):
                cols = slice(cb * LANE, (cb + 1) * LANE)
                x = x_ref[:, cols]
                if not transpose:
                    o_ref[:, cols] = x * c + pltpu.roll(x, LANE - half, 1) * lo + pltpu.roll(x, half, 1) * hi
                else:
                    o_ref[:, cols] = x * c + pltpu.roll(x * lo, half, 1) + pltpu.roll(x * hi, LANE - half, 1)

        blk = pl.BlockSpec((tr, W), lambda i: (i, 0))
        tab = pl.BlockSpec((tr, LANE), lambda i: (i, 0))
        return pl.pallas_call(body, name=name, grid=(T // tr,), in_specs=[blk, tab, tab, tab],
                              out_specs=blk, out_shape=jax.ShapeDtypeStruct((T, W), F32),
                              compiler_params=_cparams(1))(x, cos, s_lo, s_hi)

    @jax.custom_vjp
    def rope(x, cos, s_lo, s_hi):
        return call(x, cos, s_lo, s_hi, False, tag + "_fwd")

    def fwd(x, cos, s_lo, s_hi):
        return call(x, cos, s_lo, s_hi, False, tag + "_fwd"), (cos, s_lo, s_hi)

    def bwd(res, dy):
        cos, s_lo, s_hi = res
        return call(dy, cos, s_lo, s_hi, True, tag + "_bwd"), None, None, None

    rope.defvjp(fwd, bwd)
    return rope


def _make_conv_gate(tag, n_x):
    def shifted(v, T):
        t = lax.broadcasted_iota(jnp.int32, v.shape, 0)
        prev = jnp.where((t == 0) | (t == n_x), 0.0, pltpu.roll(v, 1, 0))
        nxt = jnp.where((t == n_x - 1) | (t == T - 1), 0.0, pltpu.roll(v, T - 1, 0))
        return prev, nxt

    def call_fwd(a, u, cw, cb, riders):
        T, Fd = a.shape

        def body(a_ref, u_ref, cw_ref, cb_ref, z_ref):
            pre = a_ref[...].astype(F32)
            prev, nxt = shifted(pre, T)
            s = prev * cw_ref[0:1, :] + pre * cw_ref[1:2, :] + nxt * cw_ref[2:3, :] + cb_ref[...]
            z_ref[...] = (s * (1.0 / (1.0 + jnp.exp(-s))) * u_ref[...].astype(F32)).astype(z_ref.dtype)

        col = pl.BlockSpec((T, LANE), lambda j: (0, j))
        return _pcall(body, name=tag + "_fwd", grid=(Fd // LANE,),
                      in_specs=[col, col, pl.BlockSpec((3, LANE), lambda j: (0, j)),
                                pl.BlockSpec((1, LANE), lambda j: (0, j))],
                      out_specs=[col], out_shape=[jax.ShapeDtypeStruct((T, Fd), a.dtype)], args=[a, u, cw, cb],
                      riders=riders)

    def call_bwd(a, u, cw, cb, dz, riders):
        T, Fd = a.shape

        def body(a_ref, u_ref, cw_ref, cb_ref, dz_ref, da_ref, du_ref, dcw_ref, dcb_ref):
            pre = a_ref[...].astype(F32)
            prev, nxt = shifted(pre, T)
            s = prev * cw_ref[0:1, :] + pre * cw_ref[1:2, :] + nxt * cw_ref[2:3, :] + cb_ref[...]
            sig = 1.0 / (1.0 + jnp.exp(-s))
            dz = dz_ref[...].astype(F32)
            du_ref[...] = (dz * s * sig).astype(du_ref.dtype)
            ds = dz * u_ref[...].astype(F32) * (sig * (1.0 + s * (1.0 - sig)))
            ds_prev, ds_next = shifted(ds, T)
            da_ref[...] = (ds * cw_ref[1:2, :] + ds_next * cw_ref[0:1, :] + ds_prev * cw_ref[2:3, :]).astype(
                da_ref.dtype)
            dcw_ref[0:1, :] = jnp.sum(ds * prev, axis=0, keepdims=True)
            dcw_ref[1:2, :] = jnp.sum(ds * pre, axis=0, keepdims=True)
            dcw_ref[2:3, :] = jnp.sum(ds * nxt, axis=0, keepdims=True)
            dcb_ref[...] = jnp.sum(ds, axis=0, keepdims=True)

        col = pl.BlockSpec((T, LANE), lambda j: (0, j))
        w3 = pl.BlockSpec((3, LANE), lambda j: (0, j))
        w1 = pl.BlockSpec((1, LANE), lambda j: (0, j))
        big = jax.ShapeDtypeStruct((T, Fd), a.dtype)
        return _pcall(body, name=tag + "_bwd", grid=(Fd // LANE,), in_specs=[col, col, w3, w1, col],
                      out_specs=[col, col, w3, w1],
                      out_shape=[big, big, jax.ShapeDtypeStruct((3, Fd), F32), jax.ShapeDtypeStruct((1, Fd), F32)],
                      args=[a, u, cw, cb, dz], riders=riders)

    def run(a, u, cw, cb, gin, tok):
        z, *got = call_fwd(a, u, cw, cb, _riders(gin, False))
        return z, tuple(got), tok

    @jax.custom_vjp
    def conv_gate(a, u, cw, cb, gin, tok):
        return run(a, u, cw, cb, gin, tok)

    def fwd(a, u, cw, cb, gin, tok):
        return run(a, u, cw, cb, gin, tok), (a, u, cw, cb, gin)

    def bwd(res, cts):
        a, u, cw, cb, gin = res
        dz, _, dtok = cts
        da, du, dcw, dcb, *recv = call_bwd(a, u, cw, cb, dz, _riders(dtok, True))
        return da, du, dcw, dcb, _nones(gin), tuple(recv)

    conv_gate.defvjp(fwd, bwd)
    return conv_gate


def _loss_and_grad(y, tgt, name):
    T, D = y.shape

    def body(y_ref, t_ref, l_ref, r_ref):
        @pl.when(pl.program_id(0) == 0)
        def _():
            l_ref[...] = jnp.zeros_like(l_ref)

        d = y_ref[...] - t_ref[...]
        r_ref[...] = d * (1.0 / D)
        l_ref[...] += jnp.sum(d * d) * (0.5 / D)

    row = pl.BlockSpec((ROW_TILE, D), lambda i: (i, 0))
    l, r = pl.pallas_call(body, name=name, grid=(T // ROW_TILE,), in_specs=[row, row],
                          out_specs=[pl.BlockSpec((8, LANE), lambda i: (0, 0)), row],
                          out_shape=[jax.ShapeDtypeStruct((8, LANE), F32), jax.ShapeDtypeStruct((T, D), F32)],
                          compiler_params=_cparams(1))(y, tgt)
    return l[0, 0], r


class _AttnCfg:
    def __init__(self, H, Hkv, dqk, dv, scale, tq, tk, has_sink=False):
        self.H, self.Hkv, self.dqk, self.dv, self.scale = H, Hkv, dqk, dv, scale
        self.tq, self.tk, self.has_sink = tq, tk, has_sink
        self.G = H // Hkv


class _WinCfg:
    def __init__(self, mode, H, Hkv, scale, nx, n_ctx):
        self.mode, self.H, self.Hkv, self.scale, self.nx, self.n_ctx = mode, H, Hkv, scale, nx, n_ctx
        self.G = H // Hkv
        self.has_sink = mode == "swa"
        self.nwin, self.back = (3, 1) if mode == "swa" else (5, 2)
        self.width = self.nwin * ATT_BLOCK

    def first_block(self, i):
        return jnp.clip(i - self.back, 0, self.nx - self.nwin)

    def slots(self, i, jj):
        d = 2 * (self.first_block(i) + jj - i)
        return jnp.clip(d + 8, 0, 15), jnp.clip(d + 7, 0, 15)

    def local_scores(self, s, i, tab_ref):
        sub = lax.broadcasted_iota(jnp.int32, s.shape, 0)
        lan = lax.broadcasted_iota(jnp.int32, s.shape, 1)
        first = self.first_block(i)
        if self.mode == "swa":
            diff = (i * ATT_BLOCK + sub) - (first * ATT_BLOCK + lan)
            return jnp.where(jnp.abs(diff) <= SWA_WINDOW, s, -jnp.inf)
        rows = 2 * self.nx
        qrow = 2 * i + jnp.where(sub >= GRID_W, 1, 0)
        krow = 2 * first + lan // GRID_W
        r0 = jnp.clip(qrow - NA_WIN_R // 2, 0, rows - NA_WIN_R)
        ok = (krow >= r0) & (krow < r0 + NA_WIN_R)
        tiles = []
        for jj in range(self.nwin):
            top, bot = self.slots(i, jj)
            tiles.append(jnp.concatenate([tab_ref[0, top], tab_ref[0, bot]], axis=0))
        return jnp.where(ok, s + jnp.concatenate(tiles, axis=1), -jnp.inf)


def _win_specs(cfg, Tk):
    G = cfg.G
    qmap = lambda kh, g, i: (i, kh * G + g)
    q_spec = pl.BlockSpec((ATT_BLOCK, HEAD_DIM), qmap)
    kv_spec = pl.BlockSpec((Tk, HEAD_DIM), lambda kh, g, i: (0, kh))
    lse_spec = pl.BlockSpec((1, ATT_BLOCK, LANE), lambda kh, g, i: (kh * G + g, i, 0))
    tab_spec = pl.BlockSpec((1, 16, GRID_W, LANE), lambda kh, g, i: (kh * G + g, 0, 0, 0))
    sink_spec = pl.BlockSpec((1, 8, LANE), lambda kh, g, i: (kh * G + g, 0, 0))
    return q_spec, kv_spec, lse_spec, tab_spec, sink_spec


def _win_scores(cfg, i, q_ref, k_ref, tab_ref):
    dn = (((1,), (1,)), ((), ()))
    n_x = cfg.nx * ATT_BLOCK
    win = pl.ds(pl.multiple_of(cfg.first_block(i) * ATT_BLOCK, ATT_BLOCK), cfg.width)
    ctx = pl.ds(n_x, cfg.n_ctx)
    qb = q_ref[...].astype(_MXU)
    s_loc = lax.dot_general(qb, k_ref[win, :].astype(_MXU), dn, preferred_element_type=F32)
    s_ctx = lax.dot_general(qb, k_ref[ctx, :].astype(_MXU), dn, preferred_element_type=F32)
    return cfg.local_scores(s_loc, i, tab_ref), s_ctx, win, ctx


def _win_fwd_call(cfg, name, q, k, v, tab, sink, riders=()):
    Tq, Tk = q.shape[0], k.shape[0]

    def body(*refs):
        q_ref, k_ref, v_ref = refs[:3]
        extra = refs[3] if (cfg.mode == "na" or cfg.has_sink) else None
        o_ref, lse_ref = refs[-2:]
        i = pl.program_id(2)
        s_loc, s_ctx, win, ctx = _win_scores(cfg, i, q_ref, k_ref, extra if cfg.mode == "na" else None)
        m = jnp.maximum(jnp.max(s_loc, axis=-1, keepdims=True), jnp.max(s_ctx, axis=-1, keepdims=True))
        if cfg.has_sink:
            m = jnp.maximum(m, extra[0, 0:1, 0:1])
        p_loc, p_ctx = jnp.exp2(s_loc - m), jnp.exp2(s_ctx - m)
        l = jnp.sum(p_loc, axis=-1, keepdims=True) + jnp.sum(p_ctx, axis=-1, keepdims=True)
        if cfg.has_sink:
            l = l + jnp.exp2(extra[0, 0:1, 0:1] - m)
        acc = (jnp.dot(p_loc.astype(_MXU), v_ref[win, :].astype(_MXU), preferred_element_type=F32)
               + jnp.dot(p_ctx.astype(_MXU), v_ref[ctx, :].astype(_MXU), preferred_element_type=F32))
        o_ref[...] = acc / l
        lse_ref[0] = jnp.broadcast_to(m + jnp.log2(l), (ATT_BLOCK, LANE))

    q_spec, kv_spec, lse_spec, tab_spec, sink_spec = _win_specs(cfg, Tk)
    in_specs, args = [q_spec, kv_spec, kv_spec], [q, k, v]
    if cfg.mode == "na":
        in_specs.append(tab_spec)
        args.append(tab)
    if cfg.has_sink:
        in_specs.append(sink_spec)
        args.append(sink)
    return _pcall(
        body, name=name, grid=(cfg.Hkv, cfg.G, Tq // ATT_BLOCK), in_specs=in_specs, out_specs=[q_spec, lse_spec],
        out_shape=[jax.ShapeDtypeStruct((Tq, cfg.H * HEAD_DIM), F32), jax.ShapeDtypeStruct((cfg.H, Tq, LANE), F32)],
        args=args, riders=riders)


def _win_bwd_call(cfg, name, q, k, v, tab, sink, o, lse, do, riders=()):
    Tq, Tk = q.shape[0], k.shape[0]
    nq = Tq // ATT_BLOCK
    dn_nt = (((1,), (1,)), ((), ()))
    dn_tn = (((0,), (0,)), ((), ()))

    def body(*refs):
        q_ref, k_ref, v_ref = refs[:3]
        has_extra = cfg.mode == "na" or cfg.has_sink
        extra = refs[3] if has_extra else None
        n = 4 if has_extra else 3
        o_ref, lse_ref, do_ref, dq_ref, dk_ref, dv_ref = refs[n:n + 6]
        dextra = refs[n + 6] if has_extra else None
        g, i = pl.program_id(1), pl.program_id(2)

        @pl.when((g == 0) & (i == 0))
        def _():
            dk_ref[...] = jnp.zeros_like(dk_ref)
            dv_ref[...] = jnp.zeros_like(dv_ref)

        if has_extra:
            @pl.when(i == 0)
            def _():
                dextra[...] = jnp.zeros_like(dextra)

        s_loc, s_ctx, win, ctx = _win_scores(cfg, i, q_ref, k_ref, extra if cfg.mode == "na" else None)
        lse = lse_ref[0][:, 0:1]
        p_loc, p_ctx = jnp.exp2(s_loc - lse), jnp.exp2(s_ctx - lse)
        do_f = do_ref[...]
        do_b = do_f.astype(_MXU)
        delta = jnp.sum(do_f * o_ref[...], axis=-1, keepdims=True)
        dp_loc = lax.dot_general(do_b, v_ref[win, :].astype(_MXU), dn_nt, preferred_element_type=F32)
        dp_ctx = lax.dot_general(do_b, v_ref[ctx, :].astype(_MXU), dn_nt, preferred_element_type=F32)
        ds_loc, ds_ctx = p_loc * (dp_loc - delta), p_ctx * (dp_ctx - delta)
        if cfg.mode == "na":
            for jj in range(cfg.nwin):
                top, bot = cfg.slots(i, jj)
                tile = ds_loc[:, jj * ATT_BLOCK:(jj + 1) * ATT_BLOCK]
                dextra[0, top] += tile[0:GRID_W, :]
                dextra[0, bot] += tile[GRID_W:, :]
        if cfg.has_sink:
            p_sink = jnp.exp2(extra[0, 0:1, 0:1] - lse)
            lane0 = ((lax.broadcasted_iota(jnp.int32, (8, LANE), 0) == 0)
                     & (lax.broadcasted_iota(jnp.int32, (8, LANE), 1) == 0))
            dextra[0] += jnp.where(lane0, -jnp.sum(p_sink * delta), 0.0)
        dsb_loc, dsb_ctx = ds_loc.astype(_MXU), ds_ctx.astype(_MXU)
        qb = q_ref[...].astype(_MXU)
        dq_ref[...] = cfg.scale * (jnp.dot(dsb_loc, k_ref[win, :].astype(_MXU), preferred_element_type=F32)
                                   + jnp.dot(dsb_ctx, k_ref[ctx, :].astype(_MXU), preferred_element_type=F32))
        dk_ref[win, :] += lax.dot_general(dsb_loc, qb, dn_tn, preferred_element_type=F32)
        dk_ref[ctx, :] += lax.dot_general(dsb_ctx, qb, dn_tn, preferred_element_type=F32)
        dv_ref[win, :] += lax.dot_general(p_loc.astype(_MXU), do_b, dn_tn, preferred_element_type=F32)
        dv_ref[ctx, :] += lax.dot_general(p_ctx.astype(_MXU), do_b, dn_tn, preferred_element_type=F32)

        @pl.when((g == cfg.G - 1) & (i == nq - 1))
        def _():
            dk_ref[...] = dk_ref[...] * (1.0 / LOG2E)

    q_spec, kv_spec, lse_spec, tab_spec, sink_spec = _win_specs(cfg, Tk)
    in_specs, args = [q_spec, kv_spec, kv_spec], [q, k, v]
    out_specs = [q_spec, kv_spec, kv_spec]
    out_shape = [jax.ShapeDtypeStruct(q.shape, F32), jax.ShapeDtypeStruct(k.shape, F32),
                 jax.ShapeDtypeStruct(v.shape, F32)]
    if cfg.mode == "na":
        in_specs.append(tab_spec)
        args.append(tab)
        out_specs.append(tab_spec)
        out_shape.append(jax.ShapeDtypeStruct(tab.shape, F32))
    if cfg.has_sink:
        in_specs.append(sink_spec)
        args.append(sink)
        out_specs.append(sink_spec)
        out_shape.append(jax.ShapeDtypeStruct(sink.shape, F32))
    in_specs += [q_spec, lse_spec, q_spec]
    args += [o, lse, do]
    return _pcall(body, name=name, grid=(cfg.Hkv, cfg.G, nq), in_specs=in_specs, out_specs=out_specs,
                  out_shape=out_shape, args=args, riders=riders)


def _attn_specs(cfg, Tk):
    G, tq = cfg.G, cfg.tq
    qmap = lambda kh, g, i: (i, kh * G + g)
    q_spec = pl.BlockSpec((tq, cfg.dqk), qmap)
    k_spec = pl.BlockSpec((Tk, cfg.dqk), lambda kh, g, i: (0, kh))
    v_spec = pl.BlockSpec((Tk, cfg.dv), lambda kh, g, i: (0, kh))
    o_spec = pl.BlockSpec((tq, cfg.dv), qmap)
    lse_spec = pl.BlockSpec((1, tq, LANE), lambda kh, g, i: (kh * G + g, i, 0))
    sink_spec = pl.BlockSpec((1, 8, LANE), lambda kh, g, i: (kh * G + g, 0, 0))
    return q_spec, k_spec, v_spec, o_spec, lse_spec, sink_spec


def _attn_fwd_call(cfg, name, q, k, v, sink, riders=()):
    Tq, Tk = q.shape[0], k.shape[0]
    nq, nkv = Tq // cfg.tq, Tk // cfg.tk
    dn = (((1,), (1,)), ((), ()))

    def body(*refs):
        q_ref, k_ref, v_ref = refs[:3]
        sink_ref = refs[3] if cfg.has_sink else None
        o_ref, lse_ref = refs[-2:]
        qb = q_ref[...].astype(_MXU)
        m = jnp.full((cfg.tq, 1), NEG, F32)
        l = jnp.zeros((cfg.tq, 1), F32)
        acc = jnp.zeros((cfg.tq, cfg.dv), F32)
        for j in range(nkv):
            rows = pl.ds(j * cfg.tk, cfg.tk)
            s = lax.dot_general(qb, k_ref[rows, :].astype(_MXU), dn, preferred_element_type=F32)
            m_new = jnp.maximum(m, jnp.max(s, axis=-1, keepdims=True))
            alpha = jnp.exp2(m - m_new)
            p = jnp.exp2(s - m_new)
            l = alpha * l + jnp.sum(p, axis=-1, keepdims=True)
            acc = alpha * acc + jnp.dot(p.astype(_MXU), v_ref[rows, :].astype(_MXU), preferred_element_type=F32)
            m = m_new
        if cfg.has_sink:
            sk = sink_ref[0, 0:1, 0:1]
            m_new = jnp.maximum(m, sk)
            alpha = jnp.exp2(m - m_new)
            l = alpha * l + jnp.exp2(sk - m_new)
            acc = acc * alpha
            m = m_new
        o_ref[...] = acc / l
        lse_ref[0] = jnp.broadcast_to(m + jnp.log2(l), (cfg.tq, LANE))

    q_spec, k_spec, v_spec, o_spec, lse_spec, sink_spec = _attn_specs(cfg, Tk)
    in_specs, args = [q_spec, k_spec, v_spec], [q, k, v]
    if cfg.has_sink:
        in_specs.append(sink_spec)
        args.append(sink)
    return _pcall(
        body, name=name, grid=(cfg.Hkv, cfg.G, nq), in_specs=in_specs, out_specs=[o_spec, lse_spec],
        out_shape=[jax.ShapeDtypeStruct((Tq, cfg.H * cfg.dv), F32), jax.ShapeDtypeStruct((cfg.H, Tq, LANE), F32)],
        args=args, riders=riders)


def _attn_bwd_call(cfg, name, q, k, v, sink, o, lse, do, riders=()):
    Tq, Tk = q.shape[0], k.shape[0]
    nq, nkv = Tq // cfg.tq, Tk // cfg.tk
    tq, tk = cfg.tq, cfg.tk
    dn_nt = (((1,), (1,)), ((), ()))
    dn_tn = (((0,), (0,)), ((), ()))

    def body(*refs):
        q_ref, k_ref, v_ref = refs[:3]
        n = 3
        sink_ref = dsink_ref = None
        if cfg.has_sink:
            sink_ref, n = refs[n], n + 1
        o_ref, lse_ref, do_ref, dq_ref, dk_ref, dv_ref = refs[n:n + 6]
        n += 6
        if cfg.has_sink:
            dsink_ref, n = refs[n], n + 1
        g, i = pl.program_id(1), pl.program_id(2)

        @pl.when((g == 0) & (i == 0))
        def _():
            dk_ref[...] = jnp.zeros_like(dk_ref)
            dv_ref[...] = jnp.zeros_like(dv_ref)

        if cfg.has_sink:
            @pl.when(i == 0)
            def _():
                dsink_ref[...] = jnp.zeros_like(dsink_ref)

        do_f = do_ref[...]
        do_b = do_f.astype(_MXU)
        qb = q_ref[...].astype(_MXU)
        lse = lse_ref[0][:, 0:1]
        delta = jnp.sum(do_f * o_ref[...], axis=-1, keepdims=True)
        if cfg.has_sink:
            p_sink = jnp.exp2(sink_ref[0, 0:1, 0:1] - lse)
            lane0 = ((lax.broadcasted_iota(jnp.int32, (8, LANE), 0) == 0)
                     & (lax.broadcasted_iota(jnp.int32, (8, LANE), 1) == 0))
            dsink_ref[0] += jnp.where(lane0, -jnp.sum(p_sink * delta), 0.0)
        dq = jnp.zeros((tq, cfg.dqk), F32)
        for j in range(nkv):
            rows = pl.ds(j * tk, tk)
            kb, vb = k_ref[rows, :].astype(_MXU), v_ref[rows, :].astype(_MXU)
            s = lax.dot_general(qb, kb, dn_nt, preferred_element_type=F32)
            p = jnp.exp2(s - lse)
            dp = lax.dot_general(do_b, vb, dn_nt, preferred_element_type=F32)
            dsb = (p * (dp - delta)).astype(_MXU)
            dq = dq + jnp.dot(dsb, kb, preferred_element_type=F32)
            dk_ref[rows, :] += lax.dot_general(dsb, qb, dn_tn, preferred_element_type=F32)
            dv_ref[rows, :] += lax.dot_general(p.astype(_MXU), do_b, dn_tn, preferred_element_type=F32)
        dq_ref[...] = dq * cfg.scale

        @pl.when((g == cfg.G - 1) & (i == nq - 1))
        def _():
            dk_ref[...] = dk_ref[...] * (1.0 / LOG2E)

    q_spec, k_spec, v_spec, o_spec, lse_spec, sink_spec = _attn_specs(cfg, Tk)
    in_specs, args = [q_spec, k_spec, v_spec], [q, k, v]
    if cfg.has_sink:
        in_specs.append(sink_spec)
        args.append(sink)
    in_specs += [o_spec, lse_spec, o_spec]
    args += [o, lse, do]
    out_specs = [q_spec, k_spec, v_spec]
    out_shape = [jax.ShapeDtypeStruct(q.shape, F32), jax.ShapeDtypeStruct(k.shape, F32),
                 jax.ShapeDtypeStruct(v.shape, F32)]
    if cfg.has_sink:
        out_specs.append(sink_spec)
        out_shape.append(jax.ShapeDtypeStruct(sink.shape, F32))
    return _pcall(body, name=name, grid=(cfg.Hkv, cfg.G, nq), in_specs=in_specs, out_specs=out_specs,
                  out_shape=out_shape, args=args, riders=riders)


def _make_attention(tag, cfg):
    windowed = isinstance(cfg, _WinCfg)

    def run_fwd(q, k, v, tab, sink, gin):
        if windowed:
            return _win_fwd_call(cfg, tag + "_fwd", q, k, v, tab, sink, _riders(gin, False))
        return _attn_fwd_call(cfg, tag + "_fwd", q, k, v, sink, _riders(gin, False))

    def prepared(q, k, v, tab, sink):
        q2 = (q * (cfg.scale * LOG2E)).astype(_MXU)
        tab2 = None if tab is None else tab * LOG2E
        sink2 = None if sink is None else sink * LOG2E
        return q2, k.astype(_MXU), v.astype(_MXU), tab2, sink2

    @jax.custom_vjp
    def attn(q, k, v, tab, sink, gin, tok):
        o, _, *got = run_fwd(*prepared(q, k, v, tab, sink), gin)
        return o, tuple(got), tok

    def fwd(q, k, v, tab, sink, gin, tok):
        q2, kb, vb, tab2, sink2 = prepared(q, k, v, tab, sink)
        o, lse, *got = run_fwd(q2, kb, vb, tab2, sink2, gin)
        return (o, tuple(got), tok), (q2, kb, vb, tab2, sink2, gin, o, lse)

    def bwd(res, cts):
        q, k, v, tab, sink, gin, o, lse = res
        do, _, dtok = cts
        if windowed:
            outs = _win_bwd_call(cfg, tag + "_bwd", q, k, v, tab, sink, o, lse, do, _riders(dtok, True))
        else:
            outs = _attn_bwd_call(cfg, tag + "_bwd", q, k, v, sink, o, lse, do, _riders(dtok, True))
        dq, dk, dv = outs[:3]
        rest = outs[3:]
        dtab = rest.pop(0) if tab is not None else None
        dsink = rest.pop(0) if sink is not None else None
        return dq, dk, dv, dtab, dsink, _nones(gin), tuple(rest)

    attn.defvjp(fwd, bwd)
    return attn


def _pick_block(n, prefs):
    for p in prefs:
        if n % p == 0:
            return p
    return n


def _na_table(rpb):
    cq = np.arange(GRID_W)
    dcol = np.clip(cq[None, :] - cq[:, None] + NA_WIN_C - 1, 0, 2 * NA_WIN_C - 2)
    onehot = (dcol[:, :, None] == np.arange(2 * NA_WIN_C - 1)[None, None, :]).astype(np.float32)
    c0 = np.clip(cq - NA_WIN_C // 2, 0, GRID_W - NA_WIN_C)
    col_in = (cq[None, :] >= c0[:, None]) & (cq[None, :] < c0[:, None] + NA_WIN_C)
    tz = jnp.einsum("hrj,qkj->hrqk", rpb, jnp.asarray(onehot), precision=lax.Precision.HIGHEST)
    tz = jnp.where(jnp.asarray(col_in)[None, None], tz, NEG)
    zero = jnp.zeros_like(tz[:, :1])
    tzp = jnp.concatenate([zero, tz, zero], axis=1)
    return jnp.concatenate([tzp[:, 0:16], tzp[:, 1:17]], axis=-1)


def _sink_block(sink):
    return jnp.broadcast_to(sink[:, None, None], (sink.shape[0], 8, LANE))


def _make_split(bounds, axis):
    @jax.custom_vjp
    def split(t):
        return tuple(lax.slice_in_dim(t, a, b, axis=axis) for a, b in bounds)

    def fwd(t):
        return split(t), None

    def bwd(_, cts):
        return (jnp.concatenate(cts, axis=axis),)

    split.defvjp(fwd, bwd)
    return split


_IN_GROUPS = ((0, 512), (512, 1024), (1024, 1536), (1536, 2048), (2048, 2304), (2304, 2560), (2560, 2944),
              (2944, 3072), (3072, 3200), (3200, 3712), (3712, 3968), (3968, 4224))


def _seg2(vx, vc, nseg):
    rows = [vx, vc][:nseg]
    return jnp.stack(rows)[:, None, :]


_BIG_COL = ("w_in", "mla_w_uq", "mla_w_ukv", "ffn_w_gate", "ffn_w_up")
_BIG_ROW = ("w_out", "ffn_w_down")
_IN_GROUP = ("w_in", "mla_w_uq", "mla_w_ukv")
_PIECES = {"w_in": 4, "ffn_w_down": 2}


def _n_pieces(name):
    return _PIECES.get(name, 1)


def _shard_view(name, w):
    return jnp.swapaxes(w, 1, 2) if name in _BIG_COL else w


def _cut(name, a):
    return tuple(jnp.split(a, _n_pieces(name), axis=-1))


def _full_weight(name, pieces):
    g = jnp.concatenate(pieces, axis=-1) if len(pieces) > 1 else pieces[0]
    full = g.reshape(N_DEV * g.shape[1], g.shape[2])
    if name in _BIG_ROW:
        return full
    full = full.T
    if name == "w_in":
        K = full.shape[0]
        full = jnp.concatenate([full[:, :KPE_END], jnp.zeros((K, 64), full.dtype), full[:, KPE_END:]], axis=-1)
    return full


def _grad_parts(name):
    def to_parts(dw):
        if name == "w_in":
            dw = jnp.concatenate([dw[:, :KPE_END], dw[:, KPE_END + 64:]], axis=-1)
        if name in _BIG_COL:
            dw = dw.T
        return _cut(name, dw.reshape(N_DEV, dw.shape[0] // N_DEV, dw.shape[1]).astype(BF16))

    return to_parts


_RIDE_FWD = {
    "attn_a": ((0, "w_out", 0),), "attn_b": ((1, "w_in", 0), (1, "w_in", 1)), "attn_c": ((0, "ffn_w_gate", 0),),
    "attn_d": ((0, "ffn_w_up", 0),), "gate": ((0, "ffn_w_down", 0),), "up": ((0, "ffn_w_down", 1),),
    "conv": ((1, "mla_w_uq", 0), (1, "mla_w_ukv", 0)), "down": ((1, "w_in", 2), (1, "w_in", 3)),
}
_RIDE_BWD = {
    "attn_a": ((0, "w_out", 0),), "attn_b": ((1, "mla_w_uq", 0), (1, "mla_w_ukv", 0)),
    "attn_c": ((0, "ffn_w_gate", 0),), "attn_d": ((0, "ffn_w_up", 0),), "gate": ((0, "ffn_w_down", 0),),
    "up": ((0, "ffn_w_down", 1),), "conv": ((1, "w_in", 0), (1, "w_in", 1)), "down": ((1, "w_in", 2), (1, "w_in", 3)),
}


def _local_forward(x, ctx, modx, modc, p, full, shards, toks):
    S, D = x.shape
    C = ctx.shape[0]
    depth = len(full)
    ride = shards is not None
    have = [dict(f) for f in full]
    landing = {}
    sinks = {(l, n): list(toks[l][n]) for l in range(depth) for n in toks[l]}
    alpha = (2 * depth) ** 0.25
    T0 = S + C
    nx = S // ATT_BLOCK

    def riding(table, unit, l):
        return [(l + dl, n, k) for dl, n, k in table[unit] if l + dl < depth] if ride else []

    def gin(unit, l):
        return tuple(shards[ll][n][k] for ll, n, k in riding(_RIDE_FWD, unit, l))

    def tok(unit, l):
        return tuple(sinks[(ll, n)][k] for ll, n, k in riding(_RIDE_BWD, unit, l))

    def landed(result, unit, l):
        out, got, tk = result
        for (ll, n, k), g in zip(riding(_RIDE_FWD, unit, l), got):
            landing.setdefault((ll, n), {})[k] = g
            if len(landing[(ll, n)]) == _n_pieces(n):
                have[ll][n] = _full_weight(n, [landing[(ll, n)][j] for j in range(_n_pieces(n))])
        for (ll, n, k), s in zip(riding(_RIDE_BWD, unit, l), tk):
            sinks[(ll, n)][k] = s
        return out

    sc = HEAD_DIM ** -0.5
    sc_mla = (MLA_NOPE + MLA_ROPE) ** -0.5
    tq_full = _pick_block(S, (256, 128))
    tk_all = _pick_block(T0, (4352, 2176, 640, 512, 256, 128))
    tctx = _pick_block(C, (256, 128))
    rope128 = _rope_tables(S, T0, 128, 32)
    rope64 = _rope_tables(S, T0, 64, 16)

    xs = jnp.concatenate([x, ctx], axis=0)
    for l in range(depth):
        last = l == depth - 1
        t = "l%d_" % l
        lin = lambda a, name, unit=None, out=F32: _make_linear(t + name, _grad_parts(name), out)(
            a, have[l][name], tuple(sinks[(l, name)]), gin(unit, l) if unit else (), tok(unit, l) if unit else ())
        mx, mc = modx[l], modc[l]

        h = _make_modulate(t + "mod1", S)(xs, _seg2(mx[0], mc[0], 2), _seg2(mx[1], mc[1], 2))
        pj = lin(h, "w_in")[0]
        qa, ka, va, qb, kb, vb, cq, ckv, kpe, qd, kd, vd = _make_split(_IN_GROUPS, 1)(pj)
        rows = lambda v_: _make_split(((0, S), (S, T0)), 0)(v_)
        att = lambda unit, cfg, q_, k_, v_, tab_, s_: landed(
            _make_attention(t + unit, cfg)(q_, k_, v_, tab_, s_, gin(unit, l), tok(unit, l)), unit, l)
        tab = _na_table(p["na_rpb"][l])
        cfg_a = _WinCfg("na", NA_HEADS, NA_HEADS, sc, nx, C)
        qa_x, qa_c = rows(qa)
        oa = att("attn_a", cfg_a, qa_x, ka, va, tab, None)
        qb = _make_rope(t + "rope_b", 32)(qb, *rope128)
        kb = _make_rope(t + "rope_bk", 32)(kb, *rope128)
        snk = _sink_block(p["swa_sink"][l])
        cfg_b = _WinCfg("swa", SWA_HEADS, SWA_KV_HEADS, sc, nx, C)
        qb_x, qb_c = rows(qb)
        ob = att("attn_b", cfg_b, qb_x, kb, vb, None, snk)
        cq = _make_rmsnorm(t + "rms_cq")(cq, p["mla_q_norm"][l][None, :])
        ckv = _make_rmsnorm(t + "rms_ckv")(ckv, p["mla_kv_norm"][l][None, :])
        qh = lin(cq, "mla_w_uq")[0].reshape(T0, MLA_HEADS, MLA_NOPE + MLA_ROPE)
        kvh = lin(ckv, "mla_w_ukv")[0].reshape(T0, MLA_HEADS, MLA_NOPE + MLA_V)
        qpe = jnp.pad(qh[:, :, MLA_NOPE:], ((0, 0), (0, 0), (0, LANE - MLA_ROPE))).reshape(T0, MLA_HEADS * LANE)
        qpe = _make_rope(t + "rope_cq", 16)(qpe, *rope64).reshape(T0, MLA_HEADS, LANE)
        kpe = _make_rope(t + "rope_ck", 16)(kpe, *rope64)
        qc = jnp.concatenate([qh[:, :, :MLA_NOPE], qpe], axis=-1).reshape(T0, MLA_HEADS * 2 * LANE)
        kc = jnp.concatenate([kvh[:, :, :MLA_NOPE], jnp.broadcast_to(kpe[:, None, :], (T0, MLA_HEADS, LANE))],
                             axis=-1).reshape(T0, MLA_HEADS * 2 * LANE)
        vc = kvh[:, :, MLA_NOPE:].reshape(T0, MLA_HEADS * MLA_V)
        cfg_c = _AttnCfg(MLA_HEADS, MLA_HEADS, 2 * LANE, MLA_V, sc_mla, tq_full, tk_all)
        qc_x, qc_c = rows(qc)
        oc = att("attn_c", cfg_c, qc_x, kc, vc, None, None)
        qd = _make_rmsnorm(t + "rms_dq")(qd, p["gqa_q_norm"][l][None, :])
        kd = _make_rmsnorm(t + "rms_dk")(kd, p["gqa_k_norm"][l][None, :])
        qd = _make_rope(t + "rope_dq", 32)(qd, *rope128)
        kd = _make_rope(t + "rope_dk", 32)(kd, *rope128)
        cfg_d = _AttnCfg(GQA_HEADS, GQA_KV_HEADS, 128, 128, sc, tq_full, tk_all)
        qd_x, qd_c = rows(qd)
        od = att("attn_d", cfg_d, qd_x, kd, vd, None, None)
        mix = jnp.concatenate([oa, ob, oc, od], axis=1)
        if not last:
            full_c = lambda H, Hkv, dqk, s, sink: _AttnCfg(H, Hkv, dqk, 128, s, tctx, tctx, has_sink=sink)
            ctx_att = lambda name, cfg, q_, k_, v_, s_: _make_attention(t + name, cfg)(
                q_, k_[S:], v_[S:], None, s_, (), ())[0]
            oa_c = ctx_att("ctx_a", full_c(4, 4, 128, sc, False), qa_c, ka, va, None)
            ob_c = ctx_att("ctx_b", full_c(4, 2, 128, sc, True), qb_c, kb, vb, snk)
            oc_c = ctx_att("ctx_c", full_c(4, 4, 256, sc_mla, False), qc_c, kc, vc, None)
            od_c = ctx_att("ctx_d", full_c(4, 2, 128, sc, False), qd_c, kd, vd, None)
            mix = jnp.concatenate([mix, jnp.concatenate([oa_c, ob_c, oc_c, od_c], axis=1)], axis=0)
            res, nseg = xs, 2
        else:
            res, nseg = xs[:S], 1
        y = lin(mix, "w_out")[0]
        x1 = _make_resid_ln(t + "ln1", S, alpha)(res, y, _seg2(mx[2], mc[2], nseg), p["ln1_g"][l][None, :],
                                                 p["ln1_b"][l][None, :])
        h2 = _make_modulate(t + "mod2", S)(x1, _seg2(mx[3], mc[3], nseg), _seg2(mx[4], mc[4], nseg))
        gp = landed(lin(h2, "ffn_w_gate", "gate", _ACT), "gate", l)
        up = landed(lin(h2, "ffn_w_up", "up", _ACT), "up", l)
        z = landed(_make_conv_gate(t + "conv", S)(gp, up, p["ffn_conv_w"][l], p["ffn_conv_b"][l][None, :],
                                                  gin("conv", l), tok("conv", l)), "conv", l)
        f = landed(lin(z, "ffn_w_down", "down"), "down", l)
        xs = _make_resid_ln(t + "ln2", S, alpha)(x1, f, _seg2(mx[5], mc[5], nseg), p["ln2_g"][l][None, :],
                                                 p["ln2_b"][l][None, :])
    return xs


def _adamw(w, parts, m, v, name):
    R, C = w.shape
    P = parts.shape[0]
    c1 = 1.0 - ADAM_B1 ** ADAM_STEP
    c2 = 1.0 - ADAM_B2 ** ADAM_STEP
    per_row = C * (4 * 14 + 2 * P * parts.dtype.itemsize)
    cands = _divisors(R, 8, R)
    fitting = [d for d in cands if d * per_row <= 24 * 1024 * 1024]
    tr = max(fitting) if fitting else min(cands)

    def body(w_ref, p_ref, m_ref, v_ref, g_ref, d_ref, nm_ref, nv_ref):
        g = p_ref[0].astype(F32)
        for k in range(1, P):
            g = g + p_ref[k].astype(F32)
        mm = ADAM_B1 * m_ref[...] + (1.0 - ADAM_B1) * g
        vv = ADAM_B2 * v_ref[...] + (1.0 - ADAM_B2) * (g * g)
        g_ref[...] = g
        nm_ref[...] = mm
        nv_ref[...] = vv
        d_ref[...] = -ADAM_LR * ((mm / c1) / (jnp.sqrt(vv / c2) + ADAM_EPS) + ADAM_WD * w_ref[...])

    blk = pl.BlockSpec((tr, C), lambda i: (i, 0))
    pblk = pl.BlockSpec((P, tr, C), lambda i: (0, i, 0))
    sh = jax.ShapeDtypeStruct((R, C), F32)
    return pl.pallas_call(body, name=name, grid=(R // tr,), in_specs=[blk, pblk, blk, blk],
                          out_specs=[blk, blk, blk, blk], out_shape=[sh, sh, sh, sh],
                          compiler_params=_cparams(1))(w, parts, m, v)


def _adamw_nd(w, parts, m, v, name):
    shape = w.shape
    C = shape[-1]
    R = int(np.prod(shape[:-1])) if len(shape) > 1 else 1
    outs = _adamw(w.reshape(R, C), parts.reshape(parts.shape[0], R, C), m.reshape(R, C), v.reshape(R, C), name)
    return [o.reshape(shape) for o in outs]


def _adamw_layers(w, parts, m, v, name, riders=()):
    L, R, C = w.shape
    P = parts[0].shape[0]
    c1 = 1.0 - ADAM_B1 ** ADAM_STEP
    c2 = 1.0 - ADAM_B2 ** ADAM_STEP
    per_row = C * (4 * 14 + 2 * L * P * parts[0].dtype.itemsize)
    cands = _divisors(R, 8, R)
    fitting = [d for d in cands if d * per_row <= 24 * 1024 * 1024]
    tr = max(fitting) if fitting else min(cands)
    nt = R // tr

    def body(*refs):
        w_ref, p_refs, (m_ref, v_ref, g_ref, d_ref, nm_ref, nv_ref) = refs[0], refs[1:1 + L], refs[1 + L:]
        layer = pl.program_id(0)
        for k in range(L):
            @pl.when(layer == k)
            def _(p_ref=p_refs[k]):
                g = p_ref[0].astype(F32)
                for j in range(1, P):
                    g = g + p_ref[j].astype(F32)
                mm = ADAM_B1 * m_ref[0] + (1.0 - ADAM_B1) * g
                vv = ADAM_B2 * v_ref[0] + (1.0 - ADAM_B2) * (g * g)
                g_ref[0] = g
                nm_ref[0] = mm
                nv_ref[0] = vv
                d_ref[0] = -ADAM_LR * ((mm / c1) / (jnp.sqrt(vv / c2) + ADAM_EPS) + ADAM_WD * w_ref[0])

    blk = pl.BlockSpec((1, tr, C), lambda l, i: (l, i, 0))
    pblk = [pl.BlockSpec((P, tr, C), lambda l, i, k=k: (0, jnp.where(l == k, i, jnp.where(l < k, 0, nt - 1)), 0))
            for k in range(L)]
    sh = jax.ShapeDtypeStruct((L, R, C), F32)
    return _pcall(body, name=name, grid=(L, nt), in_specs=[blk] + pblk + [blk, blk],
                  out_specs=[blk, blk, blk, blk], out_shape=[sh, sh, sh, sh], args=[w, *parts, m, v], riders=riders)


def _silu(v):
    return v * (1.0 / (1.0 + jnp.exp(-v)))


def _ada_rows(c_all, c_ctx):
    return jnp.concatenate([c_all, jnp.broadcast_to(c_ctx[None, :], (N_DEV, c_ctx.shape[0]))], axis=0)


def _silu_rows(rows, name):
    def body(r_ref, o_ref):
        o_ref[...] = _silu(r_ref[...])

    return pl.pallas_call(body, name=name, out_shape=jax.ShapeDtypeStruct(rows.shape, F32))(rows)


_SMALL = ("c_ctx", "na_rpb", "swa_sink", "mla_q_norm", "mla_kv_norm", "gqa_q_norm", "gqa_k_norm",
          "ln1_g", "ln1_b", "ffn_conv_b", "ln2_g", "ln2_b")
_NAMES = ("c_ctx", "w_ada", "b_ada", "w_in", "na_rpb", "swa_sink", "mla_q_norm", "mla_kv_norm", "mla_w_uq",
          "mla_w_ukv", "gqa_q_norm", "gqa_k_norm", "w_out", "ln1_g", "ln1_b", "ffn_w_gate", "ffn_w_up",
          "ffn_conv_w", "ffn_conv_b", "ffn_w_down", "ln2_g", "ln2_b")


def _full_from_cols(g, pad_in=False):
    _, L, K, n = g.shape
    full = jnp.transpose(g, (1, 2, 0, 3)).reshape(L, K, N_DEV * n)
    if pad_in:
        full = jnp.concatenate([full[:, :, :KPE_END], jnp.zeros((L, K, 64), full.dtype), full[:, :, KPE_END:]], axis=-1)
    return full


def _pack_small(tree, extra=None):
    flat = [tree[n].reshape(-1) for n in _SMALL]
    flat.append(jnp.zeros((1,), F32) if extra is None else extra.reshape(1))
    v = jnp.concatenate(flat)
    n = v.shape[0]
    padded = -(-n // 1024) * 1024
    return jnp.pad(v, (0, padded - n)).reshape(padded // LANE, LANE)


def _unpack_small(mat, like):
    v = mat.reshape(-1)
    out, o = {}, 0
    for n in _SMALL:
        k = int(np.prod(like[n].shape))
        out[n] = v[o:o + k].reshape(like[n].shape)
        o += k
    return out, v[o]


def kernel(x, c, ctx, c_ctx, w_ada, b_ada, w_in, na_rpb, swa_sink, mla_q_norm, mla_kv_norm, mla_w_uq, mla_w_ukv, gqa_q_norm, gqa_k_norm, w_out, ln1_g, ln1_b, ffn_w_gate, ffn_w_up, ffn_conv_w, ffn_conv_b, ffn_w_down, ln2_g, ln2_b, loss_target, m_c_ctx, m_w_ada, m_b_ada, m_w_in, m_na_rpb, m_swa_sink, m_mla_q_norm, m_mla_kv_norm, m_mla_w_uq, m_mla_w_ukv, m_gqa_q_norm, m_gqa_k_norm, m_w_out, m_ln1_g, m_ln1_b, m_ffn_w_gate, m_ffn_w_up, m_ffn_conv_w, m_ffn_conv_b, m_ffn_w_down, m_ln2_g, m_ln2_b, v_c_ctx, v_w_ada, v_b_ada, v_w_in, v_na_rpb, v_swa_sink, v_mla_q_norm, v_mla_kv_norm, v_mla_w_uq, v_mla_w_ukv, v_gqa_q_norm, v_gqa_k_norm, v_w_out, v_ln1_g, v_ln1_b, v_ffn_w_gate, v_ffn_w_up, v_ffn_conv_w, v_ffn_conv_b, v_ffn_w_down, v_ln2_g, v_ln2_b):
    W = dict(c_ctx=c_ctx, w_ada=w_ada, b_ada=b_ada, w_in=w_in, na_rpb=na_rpb, swa_sink=swa_sink,
             mla_q_norm=mla_q_norm, mla_kv_norm=mla_kv_norm, mla_w_uq=mla_w_uq, mla_w_ukv=mla_w_ukv,
             gqa_q_norm=gqa_q_norm, gqa_k_norm=gqa_k_norm, w_out=w_out, ln1_g=ln1_g, ln1_b=ln1_b,
             ffn_w_gate=ffn_w_gate, ffn_w_up=ffn_w_up, ffn_conv_w=ffn_conv_w, ffn_conv_b=ffn_conv_b,
             ffn_w_down=ffn_w_down, ln2_g=ln2_g, ln2_b=ln2_b)
    M = dict(c_ctx=m_c_ctx, w_ada=m_w_ada, b_ada=m_b_ada, w_in=m_w_in, na_rpb=m_na_rpb, swa_sink=m_swa_sink,
             mla_q_norm=m_mla_q_norm, mla_kv_norm=m_mla_kv_norm, mla_w_uq=m_mla_w_uq, mla_w_ukv=m_mla_w_ukv,
             gqa_q_norm=m_gqa_q_norm, gqa_k_norm=m_gqa_k_norm, w_out=m_w_out, ln1_g=m_ln1_g, ln1_b=m_ln1_b,
             ffn_w_gate=m_ffn_w_gate, ffn_w_up=m_ffn_w_up, ffn_conv_w=m_ffn_conv_w, ffn_conv_b=m_ffn_conv_b,
             ffn_w_down=m_ffn_w_down, ln2_g=m_ln2_g, ln2_b=m_ln2_b)
    V = dict(c_ctx=v_c_ctx, w_ada=v_w_ada, b_ada=v_b_ada, w_in=v_w_in, na_rpb=v_na_rpb, swa_sink=v_swa_sink,
             mla_q_norm=v_mla_q_norm, mla_kv_norm=v_mla_kv_norm, mla_w_uq=v_mla_w_uq, mla_w_ukv=v_mla_w_ukv,
             gqa_q_norm=v_gqa_q_norm, gqa_k_norm=v_gqa_k_norm, w_out=v_w_out, ln1_g=v_ln1_g, ln1_b=v_ln1_b,
             ffn_w_gate=v_ffn_w_gate, ffn_w_up=v_ffn_w_up, ffn_conv_w=v_ffn_conv_w, ffn_conv_b=v_ffn_conv_b,
             ffn_w_down=v_ffn_w_down, ln2_g=v_ln2_g, ln2_b=v_ln2_b)
    L, D, n_ada = w_ada.shape
    me = 4 * lax.axis_index("x") + 2 * lax.axis_index("y") + lax.axis_index("c")
    xs, ctxs, tgt = x[0], ctx[0], loss_target[0]

    c_all = _exchange(c, False, "gather_c").reshape(N_DEV, D)
    a_rows = _silu_rows(_ada_rows(c_all, c_ctx), "ada_silu")
    b_mine = lax.dynamic_slice(b_ada, (0, me * n_ada), (L, n_ada))
    mod_mine = jnp.stack([_matmul(a_rows, w_ada[l], "nn", "ada_fwd_l%d" % l) + b_mine[l][None, :] for l in range(L)])
    mod_all = _exchange(mod_mine, False, "gather_mod")
    mod_all = jnp.transpose(mod_all, (1, 2, 0, 3)).reshape(L, 2 * N_DEV, N_DEV * n_ada)
    modx = lax.dynamic_slice(mod_all, (0, me, 0), (L, 1, 6 * D)).reshape(L, 6, D)
    modc = mod_all[:, N_DEV].reshape(L, 6, D)

    big = _BIG_COL + _BIG_ROW
    views = {n: _shard_view(n, W[n]) for n in big}
    shards = [{n: _cut(n, views[n][l].astype(BF16)) for n in big} for l in range(L)]
    full = [{n: _full_weight(n, [_gather_by_chip(views[n][0].astype(BF16), "gather_l0_" + n)]) for n in _IN_GROUP}]
    full += [{} for _ in range(1, L)]
    conv_w_full = _full_from_cols(_exchange(ffn_conv_w, False, "gather_conv_w"))

    toks = [{n: tuple(jnp.zeros((N_DEV,) + s.shape, BF16) for s in shards[l][n]) for n in big} for l in range(L)]
    small = {n: W[n] for n in _SMALL if n != "c_ctx"}
    small["ffn_conv_w"] = conv_w_full

    def forward(xv, mxv, mcv, sm, tk):
        return _local_forward(xv, ctxs, mxv, mcv, sm, full, shards, tk)

    y, backward = jax.vjp(forward, xs, modx, modc, small, toks)
    loss_local, dy = _loss_and_grad(y, tgt, "loss_head")
    grad_x, dmodx, dmodc, dsmall, dtoks = backward(dy)

    dmods = _exchange(jnp.stack([dmodx.reshape(L, 6 * D), dmodc.reshape(L, 6 * D)]), False, "gather_dmod")
    dm_rows = jnp.concatenate([dmods[:, 0], dmods[:, 1]], axis=0)
    dm_rows = jnp.transpose(dm_rows, (1, 0, 2))
    dm_mine = lax.dynamic_slice(dm_rows, (0, 0, me * n_ada), (L, 2 * N_DEV, n_ada))
    g_w_ada = [_matmul(a_rows, dm_mine[l], "tn", "ada_dw_l%d" % l) for l in range(L)]
    d_rows = sum(_matmul(dm_mine[l], w_ada[l], "nt", "ada_da_l%d" % l) for l in range(L))
    sig = 1.0 / (1.0 + jnp.exp(-c_ctx))
    d_c_ctx_part = jnp.sum(d_rows[N_DEV:], axis=0) * (sig * (1.0 + c_ctx * (1.0 - sig)))

    outs = {}
    pieces = [{n: list(dtoks[l][n]) for n in big} for l in range(L)]
    late = {"ffn_w_gate": (("w_in", 0),), "ffn_w_up": (("w_in", 1),), "ffn_w_down": (("w_in", 2),),
            "w_ada": (("w_in", 3),), "w_out": (("mla_w_uq", 0), ("mla_w_ukv", 0))}

    def update(n, w, g_parts, m, v):
        res = _adamw_layers(w, g_parts, m, v, "adamw_" + n, _riders([pieces[0][a][k] for a, k in late.get(n, ())], True))
        for (a, k), got in zip(late.get(n, ()), res[4:]):
            pieces[0][a][k] = got
        return res[:4]

    def whole(l, n):
        return jnp.concatenate(pieces[l][n], axis=-1) if len(pieces[l][n]) > 1 else pieces[l][n][0]

    outs["w_ada"] = update("w_ada", w_ada, [g[None] for g in g_w_ada], M["w_ada"], V["w_ada"])
    for n in [n for n in big if n not in _IN_GROUP] + list(_IN_GROUP):
        res = update(n, views[n], [whole(l, n) for l in range(L)], _shard_view(n, M[n]), _shard_view(n, V[n]))
        outs[n] = [_shard_view(n, r) for r in res]
    dcw = dsmall.pop("ffn_conv_w")
    cw_parts = _exchange(jnp.transpose(dcw.reshape(L, 3, N_DEV, -1), (2, 0, 1, 3)), True, "scatter_conv_w")
    outs["ffn_conv_w"] = _adamw_layers(ffn_conv_w, [cw_parts[:, l] for l in range(L)], M["ffn_conv_w"],
                                       V["ffn_conv_w"], "adamw_conv_w")
    outs["b_ada"] = _adamw_nd(b_ada, jnp.transpose(dm_rows, (1, 0, 2)), M["b_ada"], V["b_ada"], "adamw_b_ada")

    dsmall["c_ctx"] = d_c_ctx_part
    small_parts = _exchange(_pack_small(dsmall, extra=loss_local), False, "gather_small")
    w_small = _pack_small({n: W[n] for n in _SMALL})
    g_s, d_s, nm_s, nv_s = _adamw(w_small, small_parts, _pack_small({n: M[n] for n in _SMALL}),
                                  _pack_small({n: V[n] for n in _SMALL}), "adamw_small")
    like = {n: W[n] for n in _SMALL}
    g_small, loss = _unpack_small(g_s, like)
    unpacked = [g_small, _unpack_small(d_s, like)[0], _unpack_small(nm_s, like)[0], _unpack_small(nv_s, like)[0]]
    for n in _SMALL:
        outs[n] = [u[n] for u in unpacked]

    result = [loss, grad_x[None]]
    for k in range(4):
        result += [outs[n][k] for n in _NAMES]
    return tuple(result)
```
